```python
import math
import jax, jax.numpy as jnp
from jax import lax
import numpy as np

D_MODEL = 1024
BATCH = 16
SEQ = 2048
DEPTH = 1

MEM_LEN = 256
HEAD_DIM = 64
A_HEADS = 8
A_KV_HEADS = 2
A_HALF_WIN = 128
B_HEADS = 8
B_BRANCHES = ((128, 1), (512, 4), (2048, 16))
MIX_WIDTH = (A_HEADS + B_HEADS) * HEAD_DIM
IN_SIZES = (A_HEADS * HEAD_DIM, A_KV_HEADS * HEAD_DIM, A_KV_HEADS * HEAD_DIM,
            B_HEADS * HEAD_DIM, B_HEADS * HEAD_DIM, B_HEADS * HEAD_DIM)
N_BUCKETS = 32
MAX_DISTANCE = 1024
X_HEADS = 4
X_HEAD_DIM = D_MODEL // X_HEADS
N_GROUPS = 4
EXPERTS_PER_GROUP = 8
TOP_K = 2
D_EXPERT = 512
ALPHA = (2.0 * DEPTH) ** 0.25
BETA = (8.0 * DEPTH) ** -0.25
LN_EPS = 1e-5
NEG = -1e30

kernel_name = "hybrid_dilated_window_moe_encoder"


def layer_norm(x, g, b):
    xf = x.astype(jnp.float32)
    mu = jnp.mean(xf, -1, keepdims=True)
    var = jnp.mean(jnp.square(xf - mu), -1, keepdims=True)
    return ((xf - mu) * lax.rsqrt(var + LN_EPS) * g.astype(jnp.float32) + b.astype(jnp.float32)).astype(x.dtype)


def rms_norm(x, g):
    xf = x.astype(jnp.float32)
    return (xf * lax.rsqrt(jnp.mean(jnp.square(xf), -1, keepdims=True) + LN_EPS) * g.astype(jnp.float32)).astype(x.dtype)


def t5_bucket(rel):
    nb = N_BUCKETS // 2
    max_exact = nb // 2
    ret = (rel > 0).astype(np.int32) * nb
    n = np.abs(rel)
    n_safe = np.maximum(n, 1).astype(np.float64)
    large = max_exact + (np.log(n_safe / max_exact) / np.log(MAX_DISTANCE / max_exact)
                         * (nb - max_exact)).astype(np.int32)
    large = np.minimum(large, nb - 1)
    return (ret + np.where(n < max_exact, n, large)).astype(np.int32)


def banded_attention(q, k, v, half, dil, rel_bias, sink=None):
    B, S, H, Dh = q.shape
    Hk = k.shape[2]
    G = H // Hk
    n = S // dil
    nb = -(-n // half)
    n_pad = nb * half

    def to_sub(t):
        h = t.shape[2]
        t = t.reshape(B, n, dil, h, Dh).transpose(0, 2, 1, 3, 4)
        return t.reshape(B * dil, n, h, Dh)

    qs, ks, vs = to_sub(q), to_sub(k), to_sub(v)
    qs = jnp.pad(qs, ((0, 0), (0, n_pad - n), (0, 0), (0, 0)))
    kpad = ((0, 0), (half, n_pad - n + half), (0, 0), (0, 0))
    ks = jnp.pad(ks, kpad)
    vs = jnp.pad(vs, kpad)
    qb = qs.reshape(B * dil, nb, half, Hk, G, Dh)
    kb = ks.reshape(B * dil, nb + 2, half, Hk, Dh)
    vb = vs.reshape(B * dil, nb + 2, half, Hk, Dh)
    kw = jnp.concatenate([kb[:, :-2], kb[:, 1:-1], kb[:, 2:]], axis=2)
    vw = jnp.concatenate([vb[:, :-2], vb[:, 1:-1], vb[:, 2:]], axis=2)

    scores = jnp.einsum('bnqhgd,bnchd->bnhgqc', qb, kw,
                        preferred_element_type=jnp.float32) * (Dh ** -0.5)
    rel_sub = np.arange(3 * half)[None, :] - half - np.arange(half)[:, None]
    band = np.abs(rel_sub) <= half
    key_pos = np.arange(nb)[:, None] * half - half + np.arange(3 * half)[None, :]
    key_ok = (key_pos >= 0) & (key_pos < n)
    mask = band[None] & key_ok[:, None, :]
    bias = rel_bias[t5_bucket(dil * rel_sub)].astype(jnp.float32)
    bias = bias.reshape(half, 3 * half, Hk, G).transpose(2, 3, 0, 1)
    logits = jnp.where(mask[None, :, None, None], scores + bias[None, None], NEG)
    if sink is not None:
        sink_col = jnp.broadcast_to(sink.astype(jnp.float32).reshape(1, 1, Hk, G, 1, 1),
                                    logits.shape[:-1] + (1,))
        lse = jax.nn.logsumexp(jnp.concatenate([logits, sink_col], -1), axis=-1)
    else:
        lse = jax.nn.logsumexp(logits, axis=-1)
    p = jnp.exp(logits - lse[..., None])
    o = jnp.einsum('bnhgqc,bnchd->bnqhgd', p.astype(v.dtype), vw)
    o = o.reshape(B * dil, n_pad, H, Dh)[:, :n].reshape(B, dil, n, H, Dh)
    o = o.transpose(0, 2, 1, 3, 4).reshape(B, S, H, Dh)
    lse = lse.transpose(0, 1, 4, 2, 3).reshape(B * dil, n_pad, H)[:, :n]
    lse = lse.reshape(B, dil, n, H).transpose(0, 2, 1, 3).reshape(B, S, H)
    return o, lse


def hybrid_mixer(h, w_in, rel_bias, sink_a, norm_a_g, norm_b_g, w_out):
    B, S, _ = h.shape
    proj = h @ w_in
    splits = [int(s) for s in np.cumsum(IN_SIZES)[:-1]]
    qa, ka, va, qb, kb, vb = jnp.split(proj, splits, axis=-1)
    qa = qa.reshape(B, S, A_HEADS, HEAD_DIM)
    ka = ka.reshape(B, S, A_KV_HEADS, HEAD_DIM)
    va = va.reshape(B, S, A_KV_HEADS, HEAD_DIM)
    qb = qb.reshape(B, S, B_HEADS, HEAD_DIM)
    kb = kb.reshape(B, S, B_HEADS, HEAD_DIM)
    vb = vb.reshape(B, S, B_HEADS, HEAD_DIM)
    oa, _ = banded_attention(qa, ka, va, A_HALF_WIN, 1, rel_bias[:, :A_HEADS], sink_a)
    outs, lses = [], []
    for win, dil in B_BRANCHES:
        o, l = banded_attention(qb, kb, vb, (win // 2) // dil, dil, rel_bias[:, A_HEADS:])
        outs.append(o)
        lses.append(l)
    wts = jax.nn.softmax(jnp.stack(lses), axis=0)
    ob = jnp.sum(wts[..., None] * jnp.stack(outs).astype(jnp.float32), axis=0).astype(h.dtype)
    ya = rms_norm(oa.reshape(B, S, A_HEADS * HEAD_DIM), norm_a_g)
    yb = rms_norm(ob.reshape(B, S, B_HEADS * HEAD_DIM), norm_b_g)
    return jnp.concatenate([ya, yb], axis=-1) @ w_out


def memory_cross_attention(h, mem, wq, wkv, wo):
    B, S, _ = h.shape
    M = mem.shape[1]
    q = (h @ wq).reshape(B, S, X_HEADS, X_HEAD_DIM)
    k, v = jnp.split(mem @ wkv, 2, axis=-1)
    k = k.reshape(B, M, X_HEADS, X_HEAD_DIM)
    v = v.reshape(B, M, X_HEADS, X_HEAD_DIM)
    s = jnp.einsum('bshd,bmhd->bhsm', q, k, preferred_element_type=jnp.float32) * (X_HEAD_DIM ** -0.5)
    p = jax.nn.softmax(s, axis=-1)
    o = jnp.einsum('bhsm,bmhd->bshd', p.astype(v.dtype), v).reshape(B, S, D_MODEL)
    return o @ wo


def hierarchical_moe(h, w_group, b_group, w_router, b_router, w_gate, w_up, w_down):
    B, S, D = h.shape
    xt = h.reshape(B * S, D)
    T = xt.shape[0]
    g_logits = (xt @ w_group).astype(jnp.float32) + b_group.astype(jnp.float32)
    g_prob = jax.nn.softmax(g_logits, axis=-1)
    _, g_idx = lax.top_k(g_logits, 1)
    g_p = jnp.take_along_axis(g_prob, g_idx, axis=-1)
    e_logits = ((xt @ w_router).astype(jnp.float32) + b_router.astype(jnp.float32))
    e_logits = e_logits.reshape(T, N_GROUPS, EXPERTS_PER_GROUP)
    e_sel = jnp.take_along_axis(e_logits, g_idx[:, :, None], axis=1)[:, 0]
    top_v, top_i = lax.top_k(e_sel, TOP_K)
    top_w = jax.nn.softmax(top_v, axis=-1)
    e_gate = jnp.sum(jax.nn.one_hot(top_i, EXPERTS_PER_GROUP, dtype=jnp.float32) * top_w[..., None], axis=1)
    gate = jax.nn.one_hot(g_idx[:, 0], N_GROUPS, dtype=jnp.float32)[:, :, None] * (g_p[:, :, None] * e_gate[:, None, :])
    y = jnp.zeros((T, D), jnp.float32)
    for g in range(N_GROUPS):
        for e in range(EXPERTS_PER_GROUP):
            hid = jax.nn.silu(xt @ w_gate[g, e]) * (xt @ w_up[g, e])
            y = y + ((gate[:, g, e:e + 1].astype(hid.dtype) * hid) @ w_down[g, e]).astype(jnp.float32)
    return y.astype(h.dtype).reshape(B, S, D)


def setup_inputs(seed: int = 0) -> dict:
    key = jax.random.key(seed)
    ks = jax.random.split(key, 32)
    f32 = jnp.float32
    L, D = DEPTH, D_MODEL
    nrm = lambda k, shape, scale: jax.random.normal(k, shape, f32) * scale
    gain = lambda k, shape: 1.0 + 0.02 * jax.random.normal(k, shape, f32)
    col_scale = np.concatenate([np.full(s, BETA if i in (2, 5) else 1.0, np.float32)
                                for i, s in enumerate(IN_SIZES)])
    xkv_scale = np.concatenate([np.ones(D, np.float32), np.full(D, BETA, np.float32)])
    return {
        "x": jax.random.normal(ks[0], (BATCH, SEQ, D), f32),
        "mem": jax.random.normal(ks[1], (BATCH, MEM_LEN, D), f32),
        "ln_in_g": gain(ks[2], (D,)),
        "ln_in_b": nrm(ks[3], (D,), 0.02),
        "w_in": nrm(ks[4], (L, D, sum(IN_SIZES)), D ** -0.5) * jnp.asarray(col_scale),
        "rel_bias": nrm(ks[5], (N_BUCKETS, A_HEADS + B_HEADS), 0.5),
        "sink_a": nrm(ks[6], (L, A_HEADS), 0.5),
        "norm_a_g": gain(ks[7], (L, A_HEADS * HEAD_DIM)),
        "norm_b_g": gain(ks[8], (L, B_HEADS * HEAD_DIM)),
        "w_out": nrm(ks[9], (L, MIX_WIDTH, D), MIX_WIDTH ** -0.5 * BETA),
        "ln1_g": gain(ks[10], (L, D)),
        "ln1_b": nrm(ks[11], (L, D), 0.02),
        "xq": nrm(ks[12], (L, D, D), D ** -0.5),
        "xkv": nrm(ks[13], (L, D, 2 * D), D ** -0.5) * jnp.asarray(xkv_scale),
        "xo": nrm(ks[14], (L, D, D), D ** -0.5 * BETA),
        "ln2_g": gain(ks[15], (L, D)),
        "ln2_b": nrm(ks[16], (L, D), 0.02),
        "w_group": nrm(ks[17], (L, D, N_GROUPS), D ** -0.5),
        "b_group": nrm(ks[18], (L, N_GROUPS), 0.01),
        "w_router": nrm(ks[19], (L, D, N_GROUPS * EXPERTS_PER_GROUP), D ** -0.5),
        "b_router": nrm(ks[20], (L, N_GROUPS * EXPERTS_PER_GROUP), 0.01),
        "w_gate": nrm(ks[21], (L, N_GROUPS, EXPERTS_PER_GROUP, D, D_EXPERT), D ** -0.5 * BETA),
        "w_up": nrm(ks[22], (L, N_GROUPS, EXPERTS_PER_GROUP, D, D_EXPERT), D ** -0.5 * BETA),
        "w_down": nrm(ks[23], (L, N_GROUPS, EXPERTS_PER_GROUP, D_EXPERT, D), D_EXPERT ** -0.5 * BETA),
        "ln3_g": gain(ks[24], (L, D)),
        "ln3_b": nrm(ks[25], (L, D), 0.02),
    }


def reference(x, mem, ln_in_g, ln_in_b, w_in, rel_bias, sink_a, norm_a_g, norm_b_g, w_out,
              ln1_g, ln1_b, xq, xkv, xo, ln2_g, ln2_b, w_group, b_group, w_router, b_router,
              w_gate, w_up, w_down, ln3_g, ln3_b):
    h = layer_norm(x, ln_in_g, ln_in_b)
    for l in range(DEPTH):
        mix = hybrid_mixer(h, w_in[l], rel_bias, sink_a[l], norm_a_g[l], norm_b_g[l], w_out[l])
        h = layer_norm(ALPHA * h + mix, ln1_g[l], ln1_b[l])
        xa = memory_cross_attention(h, mem, xq[l], xkv[l], xo[l])
        h = layer_norm(ALPHA * h + xa, ln2_g[l], ln2_b[l])
        ff = hierarchical_moe(h, w_group[l], b_group[l], w_router[l], b_router[l],
                              w_gate[l], w_up[l], w_down[l])
        h = layer_norm(ALPHA * h + ff, ln3_g[l], ln3_b[l])
    return h
```

```python
import functools

import numpy as np
import jax
import jax.numpy as jnp
from jax import lax
from jax.experimental import pallas as pl
from jax.experimental.pallas import tpu as pltpu

F32 = jnp.float32
BF16 = jnp.bfloat16

D_MODEL = 1024
SEQ = 2048
MEM_LEN = 256
HEAD_DIM = 64
A_HEADS = 8
A_KV_HEADS = 2
A_HALF_WIN = 128
B_HEADS = 8
B_BRANCHES = ((128, 1), (512, 4), (2048, 16))
N_BUCKETS = 32
MAX_DISTANCE = 1024
X_HEADS = 4
X_HEAD_DIM = D_MODEL // X_HEADS
N_GROUPS = 4
EXPERTS_PER_GROUP = 8
N_EXPERTS = N_GROUPS * EXPERTS_PER_GROUP
D_EXPERT = 512
DEPTH = 1
ALPHA = (2.0 * DEPTH) ** 0.25
LN_EPS = 1e-5
NEG = -1e30

A_WIDTH = A_HEADS * HEAD_DIM
A_KV_WIDTH = A_KV_HEADS * HEAD_DIM
B_WIDTH = B_HEADS * HEAD_DIM
QKV_A = A_WIDTH + 2 * A_KV_WIDTH
QKV_B = 3 * B_WIDTH
LANES = 128
B_OUT = B_WIDTH + LANES

PAIRS_PER_GROUP = EXPERTS_PER_GROUP * (EXPERTS_PER_GROUP - 1) // 2
N_CLASSES = N_GROUPS * PAIRS_PER_GROUP
ROUTE_LANES = LANES
XR_WIDTH = D_MODEL + ROUTE_LANES

TM_IN = 512
TM_MERGE = 256
TM_X = 256
TM_MOE = 128
VMEM_LIMIT = 56 * 1024 * 1024


def _cparams(*sem):
    return pltpu.CompilerParams(dimension_semantics=sem, vmem_limit_bytes=VMEM_LIMIT)


def _layer_norm(x, g, b):
    mu = jnp.mean(x, axis=-1, keepdims=True)
    xc = x - mu
    var = jnp.mean(xc * xc, axis=-1, keepdims=True)
    return xc * lax.rsqrt(var + LN_EPS) * g + b


def _rms_norm(x, g):
    return x * lax.rsqrt(jnp.mean(x * x, axis=-1, keepdims=True) + LN_EPS) * g


def _t5_bucket(rel):
    nb = N_BUCKETS // 2
    max_exact = nb // 2
    ret = (rel > 0).astype(np.int32) * nb
    n = np.abs(rel)
    n_safe = np.maximum(n, 1).astype(np.float64)
    large = max_exact + (np.log(n_safe / max_exact) / np.log(MAX_DISTANCE / max_exact)
                         * (nb - max_exact)).astype(np.int32)
    large = np.minimum(large, nb - 1)
    return (ret + np.where(n < max_exact, n, large)).astype(np.int32)


def _inproj_kernel(x_ref, g_ref, b_ref, w_ref, h0_ref, qa_ref, qb1_ref, qb4_ref, qb16_ref, pb_scr):
    h = _layer_norm(x_ref[...], g_ref[...], b_ref[...])
    h0_ref[...] = h
    proj = jnp.dot(h.astype(BF16), w_ref[...], preferred_element_type=F32)
    qa_ref[...] = proj[:, :QKV_A].astype(BF16)
    pb = proj[:, QKV_A:]
    qb1_ref[...] = pb.astype(BF16)
    for c in range(QKV_B // LANES):
        cs = slice(c * LANES, (c + 1) * LANES)
        pb_scr[c] = pb[:, cs]
        for dil, ref in ((4, qb4_ref), (16, qb16_ref)):
            for r in range(dil):
                ref[0, r, :, cs] = pb_scr[c, pl.ds(r, TM_IN // dil, stride=dil), :].astype(BF16)


def _input_projection(x2, ln_g, ln_b, w_in_b, batch):
    T = x2.shape[0]
    tiles_per_seq = SEQ // TM_IN
    row = lambda i: (i, 0)
    const = lambda i: (0, 0)
    deint = lambda i: (i // tiles_per_seq, 0, i % tiles_per_seq, 0)
    return pl.pallas_call(
        _inproj_kernel,
        grid=(T // TM_IN,),
        in_specs=[
            pl.BlockSpec((TM_IN, D_MODEL), row),
            pl.BlockSpec((1, D_MODEL), const),
            pl.BlockSpec((1, D_MODEL), const),
            pl.BlockSpec((D_MODEL, QKV_A + QKV_B), const),
        ],
        out_specs=[
            pl.BlockSpec((TM_IN, D_MODEL), row),
            pl.BlockSpec((TM_IN, QKV_A), row),
            pl.BlockSpec((TM_IN, QKV_B), row),
            pl.BlockSpec((1, 4, TM_IN // 4, QKV_B), deint),
            pl.BlockSpec((1, 16, TM_IN // 16, QKV_B), deint),
        ],
        out_shape=[
            jax.ShapeDtypeStruct((T, D_MODEL), F32),
            jax.ShapeDtypeStruct((T, QKV_A), BF16),
            jax.ShapeDtypeStruct((T, QKV_B), BF16),
            jax.ShapeDtypeStruct((batch, 4, SEQ // 4, QKV_B), BF16),
            jax.ShapeDtypeStruct((batch, 16, SEQ // 16, QKV_B), BF16),
        ],
        scratch_shapes=[pltpu.VMEM((QKV_B // LANES, TM_IN, LANES), F32)],
        compiler_params=_cparams("parallel"),
        name="ln_in_proj",
    )(x2, ln_g, ln_b, w_in_b)


def _band_attn_kernel(*refs, half, n_kv, group, with_sink):
    if with_sink:
        sink_ref, gain_ref = refs[0], refs[1]
        refs = refs[2:]
    q_ref, km_ref, kc_ref, kp_ref, vm_ref, vc_ref, vp_ref, bias_ref, o_ref = refs
    outs, lses = [], []
    for hk in range(n_kv):
        ksl = slice(hk * HEAD_DIM, (hk + 1) * HEAD_DIM)
        k_cat = jnp.concatenate([km_ref[0, :, ksl], kc_ref[0, :, ksl], kp_ref[0, :, ksl]], axis=0)
        v_cat = jnp.concatenate([vm_ref[0, :, ksl], vc_ref[0, :, ksl], vp_ref[0, :, ksl]], axis=0)
        heads = [hk * group + g for g in range(group)]
        q_st = jnp.concatenate([q_ref[0, :, h * HEAD_DIM:(h + 1) * HEAD_DIM] for h in heads], axis=0)
        s = lax.dot_general(q_st, k_cat, (((1,), (1,)), ((), ())), preferred_element_type=F32)
        s = s + bias_ref[0, hk]
        m = jnp.max(s, axis=-1, keepdims=True)
        if with_sink:
            sink_col = jnp.concatenate([jnp.full((half, 1), sink_ref[h], F32) for h in heads], axis=0)
            m = jnp.maximum(m, sink_col)
        p = jnp.exp(s - m)
        l = jnp.sum(p, axis=-1, keepdims=True)
        if with_sink:
            l = l + jnp.exp(sink_col - m)
        o = jnp.dot(p.astype(BF16), v_cat, preferred_element_type=F32) / l
        for g in range(group):
            outs.append(o[g * half:(g + 1) * half])
        if not with_sink:
            lses.append(m + jnp.log(l))
    o_all = jnp.concatenate(outs, axis=1)
    if with_sink:
        o_ref[0] = _rms_norm(o_all, gain_ref[...]).astype(o_ref.dtype)
    else:
        lane = lax.broadcasted_iota(jnp.int32, (half, LANES), 1)
        lse_tile = jnp.zeros((half, LANES), F32)
        for h, lse in enumerate(lses):
            lse_tile = jnp.where(lane == h, lse, lse_tile)
        o_ref[0] = jnp.concatenate([o_all, lse_tile], axis=1)


def _band_bias(rel_bias_h, half, dil, n_kv, group):
    rel_sub = np.arange(3 * half)[None, :] - half - np.arange(half)[:, None]
    band = np.abs(rel_sub) <= half
    bias = rel_bias_h.astype(F32)[_t5_bucket(dil * rel_sub)]
    bias = jnp.transpose(bias, (2, 0, 1))
    c = np.arange(3 * half)
    masks = np.stack([band & (c >= half)[None], band, band & (c < 2 * half)[None]])
    out = jnp.where(masks[:, None], bias[None], NEG)
    return out.reshape(3, n_kv, group * half, 3 * half)


def _band_attention(qkv, bias, *, half, n_heads, n_kv, q_width, kv_width, out_width, out_dtype,
                    sink=None, gain=None):
    Bd, n, _ = qkv.shape
    nb = n // half
    assert nb * half == n and nb >= 2
    group = n_heads // n_kv
    kcol = q_width // kv_width
    with_sink = sink is not None
    prev = lambda b, j: jnp.maximum(j - 1, 0)
    nxt = lambda b, j: jnp.minimum(j + 1, nb - 1)
    variant = lambda b, j: jnp.where(j == 0, 0, jnp.where(j == nb - 1, 2, 1))
    in_specs = [
        pl.BlockSpec((1, half, q_width), lambda b, j: (b, j, 0)),
        pl.BlockSpec((1, half, kv_width), lambda b, j: (b, prev(b, j), kcol)),
        pl.BlockSpec((1, half, kv_width), lambda b, j: (b, j, kcol)),
        pl.BlockSpec((1, half, kv_width), lambda b, j: (b, nxt(b, j), kcol)),
        pl.BlockSpec((1, half, kv_width), lambda b, j: (b, prev(b, j), kcol + 1)),
        pl.BlockSpec((1, half, kv_width), lambda b, j: (b, j, kcol + 1)),
        pl.BlockSpec((1, half, kv_width), lambda b, j: (b, nxt(b, j), kcol + 1)),
        pl.BlockSpec((1, n_kv, group * half, 3 * half), lambda b, j: (variant(b, j), 0, 0, 0)),
    ]
    args = [qkv] * 7 + [bias]
    if with_sink:
        in_specs = [pl.BlockSpec(memory_space=pltpu.SMEM),
                    pl.BlockSpec((1, q_width), lambda b, j: (0, 0))] + in_specs
        args = [sink, gain] + args
    return pl.pallas_call(
        functools.partial(_band_attn_kernel, half=half, n_kv=n_kv, group=group, with_sink=with_sink),
        grid=(Bd, nb),
        in_specs=in_specs,
        out_specs=pl.BlockSpec((1, half, out_width), lambda b, j: (b, j, 0)),
        out_shape=jax.ShapeDtypeStruct((Bd, n, out_width), out_dtype),
        compiler_params=_cparams("parallel", "parallel"),
        name="band_attn_sink" if with_sink else f"band_attn_n{n}",
    )(*args)


def _merge_kernel(ya_ref, o1_ref, o4_ref, o16_ref, h0_ref, gb_ref, w_ref, g1_ref, b1_ref, h1_ref,
                  s4_scr, s16_scr):
    for c in range(B_OUT // LANES):
        cs = slice(c * LANES, (c + 1) * LANES)
        for dil, src, dst in ((4, o4_ref, s4_scr), (16, o16_ref, s16_scr)):
            for r in range(dil):
                dst[c, pl.ds(r, TM_MERGE // dil, stride=dil), :] = src[0, r, :, cs]
    lse_c = B_WIDTH // LANES
    lse = [o1_ref[:, B_WIDTH:], s4_scr[lse_c], s16_scr[lse_c]]
    mx = jnp.maximum(jnp.maximum(lse[0], lse[1]), lse[2])
    ex = [jnp.exp(l - mx) for l in lse]
    inv = 1.0 / (ex[0] + ex[1] + ex[2])
    wts = [e * inv for e in ex]
    lane = lax.broadcasted_iota(jnp.int32, (TM_MERGE, LANES), 1)
    pieces = []
    for c in range(lse_c):
        tiles = (o1_ref[:, c * LANES:(c + 1) * LANES], s4_scr[c], s16_scr[c])
        acc = jnp.zeros((TM_MERGE, LANES), F32)
        for w, o in zip(wts, tiles):
            acc = acc + jnp.where(lane < HEAD_DIM, w[:, 2 * c:2 * c + 1], w[:, 2 * c + 1:2 * c + 2]) * o
        pieces.append(acc)
    yb = _rms_norm(jnp.concatenate(pieces, axis=1), gb_ref[...])
    y = jnp.concatenate([ya_ref[...], yb.astype(BF16)], axis=1)
    mix = jnp.dot(y, w_ref[...], preferred_element_type=F32)
    h1_ref[...] = _layer_norm(ALPHA * h0_ref[...] + mix, g1_ref[...], b1_ref[...])


def _merge_project(ya, o1, o4, o16, h0, gain_b, w_out_b, ln_g, ln_b):
    T = h0.shape[0]
    tiles_per_seq = SEQ // TM_MERGE
    row = lambda i: (i, 0)
    const = lambda i: (0, 0)
    deint = lambda i: (i // tiles_per_seq, 0, i % tiles_per_seq, 0)
    return pl.pallas_call(
        _merge_kernel,
        grid=(T // TM_MERGE,),
        in_specs=[
            pl.BlockSpec((TM_MERGE, A_WIDTH), row),
            pl.BlockSpec((TM_MERGE, B_OUT), row),
            pl.BlockSpec((1, 4, TM_MERGE // 4, B_OUT), deint),
            pl.BlockSpec((1, 16, TM_MERGE // 16, B_OUT), deint),
            pl.BlockSpec((TM_MERGE, D_MODEL), row),
            pl.BlockSpec((1, B_WIDTH), const),
            pl.BlockSpec((D_MODEL, D_MODEL), const),
            pl.BlockSpec((1, D_MODEL), const),
            pl.BlockSpec((1, D_MODEL), const),
        ],
        out_specs=pl.BlockSpec((TM_MERGE, D_MODEL), row),
        out_shape=jax.ShapeDtypeStruct((T, D_MODEL), F32),
        scratch_shapes=[pltpu.VMEM((B_OUT // LANES, TM_MERGE, LANES), F32)] * 2,
        compiler_params=_cparams("parallel"),
        name="merge_out_proj",
    )(ya, o1, o4, o16, h0, gain_b, w_out_b, ln_g, ln_b)


def _mem_kv_kernel(mem_ref, w_ref, k_ref, v_ref):
    kv = jnp.dot(mem_ref[0].astype(BF16), w_ref[...], preferred_element_type=F32)
    k_ref[0] = kv[:, :D_MODEL].astype(BF16)
    v_ref[0] = kv[:, D_MODEL:].astype(BF16)


def _mem_kv(mem, xkv_b):
    B = mem.shape[0]
    blk = pl.BlockSpec((1, MEM_LEN, D_MODEL), lambda b: (b, 0, 0))
    return pl.pallas_call(
        _mem_kv_kernel,
        grid=(B,),
        in_specs=[blk, pl.BlockSpec((D_MODEL, 2 * D_MODEL), lambda b: (0, 0))],
        out_specs=[blk, blk],
        out_shape=[jax.ShapeDtypeStruct((B, MEM_LEN, D_MODEL), BF16)] * 2,
        compiler_params=_cparams("parallel"),
        name="mem_kv_proj",
    )(mem, xkv_b)


def _route(logits):
    rows = logits.shape[0]
    lane = lax.broadcasted_iota(jnp.int32, (rows, LANES), 1).astype(F32)
    big = float(LANES)
    ninf = -jnp.inf
    gl = jnp.where(lane < N_GROUPS, logits, ninf)
    gmax = jnp.max(gl, axis=-1, keepdims=True)
    gidx = jnp.min(jnp.where(gl == gmax, lane, big), axis=-1, keepdims=True)
    g_p = 1.0 / jnp.sum(jnp.exp(gl - gmax), axis=-1, keepdims=True)
    lo_lane = N_GROUPS + EXPERTS_PER_GROUP * gidx
    el = jnp.where((lane >= lo_lane) & (lane < lo_lane + EXPERTS_PER_GROUP), logits, ninf)
    v1 = jnp.max(el, axis=-1, keepdims=True)
    i1 = jnp.min(jnp.where(el == v1, lane, big), axis=-1, keepdims=True)
    el2 = jnp.where(lane == i1, ninf, el)
    v2 = jnp.max(el2, axis=-1, keepdims=True)
    i2 = jnp.min(jnp.where(el2 == v2, lane, big), axis=-1, keepdims=True)
    t = jnp.exp(v2 - v1)
    w1 = g_p / (1.0 + t)
    w2 = g_p * t / (1.0 + t)
    a = jnp.minimum(i1, i2) - lo_lane
    b = jnp.maximum(i1, i2) - lo_lane
    pair = a * (2 * EXPERTS_PER_GROUP - 1 - a) * 0.5 + (b - a - 1.0)
    cls = gidx * PAIRS_PER_GROUP + pair
    w_lo = jnp.where(i1 < i2, w1, w2)
    w_hi = jnp.where(i1 < i2, w2, w1)
    return jnp.where(lane == 0, cls, jnp.where(lane == 1, w_lo, jnp.where(lane == 2, w_hi, 0.0)))


def _xattn_kernel(h1_ref, k_ref, v_ref, wq_ref, wo_ref, g2_ref, b2_ref, wr_hi_ref, wr_lo_ref, br_ref, xr_ref):
    h1 = h1_ref[...]
    q = jnp.dot(h1.astype(BF16), wq_ref[...], preferred_element_type=F32).astype(BF16)
    outs = []
    for hd in range(X_HEADS):
        sl = slice(hd * X_HEAD_DIM, (hd + 1) * X_HEAD_DIM)
        s = lax.dot_general(q[:, sl], k_ref[0, :, sl], (((1,), (1,)), ((), ())), preferred_element_type=F32)
        m = jnp.max(s, axis=-1, keepdims=True)
        p = jnp.exp(s - m)
        l = jnp.sum(p, axis=-1, keepdims=True)
        o = jnp.dot(p.astype(BF16), v_ref[0, :, sl], preferred_element_type=F32) / l
        outs.append(o.astype(BF16))
    xa = jnp.dot(jnp.concatenate(outs, axis=1), wo_ref[...], preferred_element_type=F32)
    h2 = _layer_norm(ALPHA * h1 + xa, g2_ref[...], b2_ref[...])
    h_hi = h2.astype(BF16)
    h_lo = (h2 - h_hi.astype(F32)).astype(BF16)
    logits = (jnp.dot(h_hi, wr_hi_ref[...], preferred_element_type=F32)
              + jnp.dot(h_hi, wr_lo_ref[...], preferred_element_type=F32)
              + jnp.dot(h_lo, wr_hi_ref[...], preferred_element_type=F32)) + br_ref[...]
    xr_ref[:, :D_MODEL] = h2
    xr_ref[:, D_MODEL:] = _route(logits)


def _cross_attention_route(h1, k, v, xq_b, xo_b, ln_g, ln_b, wr_hi, wr_lo, br):
    T = h1.shape[0]
    tiles_per_seq = SEQ // TM_X
    row = lambda i: (i, 0)
    const = lambda i: (0, 0)
    kv_blk = pl.BlockSpec((1, MEM_LEN, D_MODEL), lambda i: (i // tiles_per_seq, 0, 0))
    return pl.pallas_call(
        _xattn_kernel,
        grid=(T // TM_X,),
        in_specs=[
            pl.BlockSpec((TM_X, D_MODEL), row), kv_blk, kv_blk,
            pl.BlockSpec((D_MODEL, D_MODEL), const),
            pl.BlockSpec((D_MODEL, D_MODEL), const),
            pl.BlockSpec((1, D_MODEL), const),
            pl.BlockSpec((1, D_MODEL), const),
            pl.BlockSpec((D_MODEL, ROUTE_LANES), const),
            pl.BlockSpec((D_MODEL, ROUTE_LANES), const),
            pl.BlockSpec((1, ROUTE_LANES), const),
        ],
        out_specs=pl.BlockSpec((TM_X, XR_WIDTH), row),
        out_shape=jax.ShapeDtypeStruct((T, XR_WIDTH), F32),
        compiler_params=_cparams("parallel"),
        name="xattn_ln_route",
    )(h1, k, v, xq_b, xo_b, ln_g, ln_b, wr_hi, wr_lo, br)


def _row_copy(src_hbm, dst_hbm, src_row, dst_row, sem):
    return pltpu.make_async_copy(src_hbm.at[pl.ds(src_row, 1)], dst_hbm.at[pl.ds(dst_row, 1)], sem)


def _gather_kernel(src_ref, ntile_ref, x_hbm, z_hbm, o_hbm, sems):
    j = pl.program_id(0)
    last = ntile_ref[0] - 1
    n_steps = pl.num_programs(0)

    def wait_tile(t):
        def body(i, c):
            _row_copy(x_hbm, o_hbm, 0, t * TM_MOE + i, sems.at[t % 2]).wait()
            return c
        lax.fori_loop(0, TM_MOE, body, 0)

    def zero_copy(t):
        return pltpu.make_async_copy(z_hbm, o_hbm.at[pl.ds(t * TM_MOE, TM_MOE)], sems.at[2])

    @pl.when(j <= last)
    def _():
        def body(i, c):
            p = j * TM_MOE + i
            _row_copy(x_hbm, o_hbm, src_ref[p], p, sems.at[j % 2]).start()
            return c
        lax.fori_loop(0, TM_MOE, body, 0)

    @pl.when((j >= 1) & (j <= last))
    def _():
        wait_tile(j - 1)

    @pl.when(j == last)
    def _():
        wait_tile(j)

    @pl.when(j > last)
    def _():
        zero_copy(j).start()

    @pl.when(j == n_steps - 1)
    def _():
        def body(t, c):
            zero_copy(t).wait()
            return c
        lax.fori_loop(last + 1, n_steps, body, 0)


def _gather_rows(src_rows, n_tiles, x, n_tiles_max):
    any_spec = pl.BlockSpec(memory_space=pl.ANY)
    return pl.pallas_call(
        _gather_kernel,
        grid_spec=pltpu.PrefetchScalarGridSpec(
            num_scalar_prefetch=2,
            grid=(n_tiles_max,),
            in_specs=[any_spec, any_spec],
            out_specs=any_spec,
            scratch_shapes=[pltpu.SemaphoreType.DMA((3,))],
        ),
        out_shape=jax.ShapeDtypeStruct((n_tiles_max * TM_MOE, x.shape[1]), x.dtype),
        compiler_params=_cparams("arbitrary"),
        name="moe_gather",
    )(src_rows, n_tiles, x, jnp.zeros((TM_MOE, x.shape[1]), x.dtype))


def _scatter_kernel(dst_ref, cnt_ref, ntile_ref, y_hbm, o_hbm, sems):
    j = pl.program_id(0)
    last = ntile_ref[0] - 1

    def wait_tile(t):
        def body(i, c):
            _row_copy(y_hbm, o_hbm, t * TM_MOE + i, 0, sems.at[t % 2]).wait()
            return c
        lax.fori_loop(0, cnt_ref[t], body, 0)

    @pl.when(j <= last)
    def _():
        def body(i, c):
            p = j * TM_MOE + i
            _row_copy(y_hbm, o_hbm, p, dst_ref[p], sems.at[j % 2]).start()
            return c
        lax.fori_loop(0, cnt_ref[j], body, 0)

    @pl.when((j >= 1) & (j <= last))
    def _():
        wait_tile(j - 1)

    @pl.when(j == last)
    def _():
        wait_tile(j)


def _scatter_rows(dst_rows, tile_cnt, n_tiles, y, n_out):
    n_tiles_max = tile_cnt.shape[0]
    return pl.pallas_call(
        _scatter_kernel,
        grid_spec=pltpu.PrefetchScalarGridSpec(
            num_scalar_prefetch=3,
            grid=(n_tiles_max,),
            in_specs=[pl.BlockSpec(memory_space=pl.ANY)],
            out_specs=pl.BlockSpec(memory_space=pl.ANY),
            scratch_shapes=[pltpu.SemaphoreType.DMA((2,))],
        ),
        out_shape=jax.ShapeDtypeStruct((n_out, y.shape[1]), y.dtype),
        compiler_params=_cparams("arbitrary"),
        name="moe_scatter",
    )(dst_rows, tile_cnt, n_tiles, y)


def _expert_kernel(blk_ref, elo_ref, ehi_ref, ntile_ref, xr_ref,
                   wg_lo, wu_lo, wd_lo, wg_hi, wu_hi, wd_hi, g3_ref, b3_ref, o_ref):
    j = pl.program_id(0)

    @pl.when(j < ntile_ref[0])
    def _():
        x = xr_ref[:, :D_MODEL]
        xb = x.astype(BF16)
        y = jnp.zeros_like(x)
        for lane, wg, wu, wd in ((1, wg_lo, wu_lo, wd_lo), (2, wg_hi, wu_hi, wd_hi)):
            gate = xr_ref[:, D_MODEL + lane:D_MODEL + lane + 1]
            a = jnp.dot(xb, wg[0], preferred_element_type=F32)
            u = jnp.dot(xb, wu[0], preferred_element_type=F32)
            hid = a * jax.nn.sigmoid(a) * u
            y = y + jnp.dot((gate * hid).astype(BF16), wd[0], preferred_element_type=F32)
        o_ref[...] = _layer_norm(ALPHA * x + y, g3_ref[...], b3_ref[...])

    @pl.when(j >= ntile_ref[0])
    def _():
        o_ref[...] = jnp.zeros_like(o_ref)


def _expert_mlp(tile_blk, tile_elo, tile_ehi, n_tiles, xr_sorted, wg, wu, wd, ln_g, ln_b):
    n_tiles_max = tile_blk.shape[0]
    rows = lambda j, blk, elo, ehi, nt: (blk[j], 0)
    const = lambda j, blk, elo, ehi, nt: (0, 0)
    lo = lambda j, blk, elo, ehi, nt: (elo[j], 0, 0)
    hi = lambda j, blk, elo, ehi, nt: (ehi[j], 0, 0)
    up = (1, D_MODEL, D_EXPERT)
    down = (1, D_EXPERT, D_MODEL)
    return pl.pallas_call(
        _expert_kernel,
        grid_spec=pltpu.PrefetchScalarGridSpec(
            num_scalar_prefetch=4,
            grid=(n_tiles_max,),
            in_specs=[
                pl.BlockSpec((TM_MOE, XR_WIDTH), rows),
                pl.BlockSpec(up, lo), pl.BlockSpec(up, lo), pl.BlockSpec(down, lo),
                pl.BlockSpec(up, hi), pl.BlockSpec(up, hi), pl.BlockSpec(down, hi),
                pl.BlockSpec((1, D_MODEL), const),
                pl.BlockSpec((1, D_MODEL), const),
            ],
            out_specs=pl.BlockSpec((TM_MOE, D_MODEL), lambda j, blk, elo, ehi, nt: (j, 0)),
        ),
        out_shape=jax.ShapeDtypeStruct((n_tiles_max * TM_MOE, D_MODEL), F32),
        compiler_params=_cparams("arbitrary"),
        name="moe_experts",
    )(tile_blk, tile_elo, tile_ehi, n_tiles, xr_sorted, wg, wu, wd, wg, wu, wd, ln_g, ln_b)


def _class_experts():
    lo, hi = [], []
    for g in range(N_GROUPS):
        for a in range(EXPERTS_PER_GROUP):
            for b in range(a + 1, EXPERTS_PER_GROUP):
                lo.append(g * EXPERTS_PER_GROUP + a)
                hi.append(g * EXPERTS_PER_GROUP + b)
    return np.asarray(lo, np.int32), np.asarray(hi, np.int32)


def _moe_plan(cls, n_tiles_max):
    T = cls.shape[0]
    onehot = (cls[:, None] == jnp.arange(N_CLASSES, dtype=jnp.int32)[None, :]).astype(jnp.int32)
    csum = jnp.cumsum(onehot, axis=0)
    counts = csum[-1]
    rank = jnp.sum((csum - onehot) * onehot, axis=1)
    tiles_per_class = (counts + TM_MOE - 1) // TM_MOE
    tile_end = jnp.cumsum(tiles_per_class)
    tile_start = tile_end - tiles_per_class
    n_tiles = tile_end[-1]
    pos = tile_start[cls] * TM_MOE + rank
    tok = jnp.arange(T, dtype=jnp.int32)
    src_rows = jnp.zeros((n_tiles_max * TM_MOE,), jnp.int32).at[pos].set(tok)
    tile_ids = jnp.arange(n_tiles_max, dtype=jnp.int32)
    tile_cls = jnp.searchsorted(tile_end, jnp.minimum(tile_ids, n_tiles - 1), side="right").astype(jnp.int32)
    tile_cls = jnp.minimum(tile_cls, N_CLASSES - 1)
    cls_lo, cls_hi = _class_experts()
    tile_elo = jnp.asarray(cls_lo)[tile_cls]
    tile_ehi = jnp.asarray(cls_hi)[tile_cls]
    tile_blk = jnp.minimum(tile_ids, n_tiles - 1)
    within = tile_ids - tile_start[tile_cls]
    tile_cnt = jnp.clip(counts[tile_cls] - within * TM_MOE, 0, TM_MOE)
    tile_cnt = jnp.where(tile_ids < n_tiles, tile_cnt, 0).astype(jnp.int32)
    return src_rows, tile_blk, tile_elo, tile_ehi, tile_cnt, n_tiles.reshape(1).astype(jnp.int32)


def _vec(a):
    return a.reshape(1, -1).astype(F32)


def _mixer_and_cross_attention(x, mem, ln_in_g, ln_in_b, w_in, rel_bias, sink_a, norm_a_g, norm_b_g, w_out,
                               ln1_g, ln1_b, xq, xkv, xo, ln2_g, ln2_b, w_group, b_group, w_router, b_router):
    B, S, D = x.shape
    assert S == SEQ and D == D_MODEL and mem.shape[1:] == (MEM_LEN, D_MODEL)
    T = B * S
    vec = _vec

    col_scale = np.ones((QKV_A + QKV_B,), np.float32)
    col_scale[:A_WIDTH] = HEAD_DIM ** -0.5
    col_scale[QKV_A:QKV_A + B_WIDTH] = HEAD_DIM ** -0.5
    w_in_b = (w_in[0] * col_scale).astype(BF16)
    w_out_b = w_out[0].astype(BF16)
    xq_b = (xq[0] * (X_HEAD_DIM ** -0.5)).astype(BF16)
    xkv_b = xkv[0].astype(BF16)
    xo_b = xo[0].astype(BF16)
    wr = jnp.concatenate([w_group[0], w_router[0]], axis=1).astype(F32)
    wr = jnp.pad(wr, ((0, 0), (0, ROUTE_LANES - wr.shape[1])))
    wr_hi = wr.astype(BF16)
    wr_lo = (wr - wr_hi.astype(F32)).astype(BF16)
    br = jnp.concatenate([b_group[0], b_router[0]]).astype(F32)
    br = jnp.pad(br, (0, ROUTE_LANES - br.shape[0])).reshape(1, ROUTE_LANES)

    h0, qkv_a, qkv_b1, qkv_b4, qkv_b16 = _input_projection(
        x.reshape(T, D), vec(ln_in_g), vec(ln_in_b), w_in_b, B)

    bias_a = _band_bias(rel_bias[:, :A_HEADS], A_HALF_WIN, 1, A_KV_HEADS, A_HEADS // A_KV_HEADS)
    ya = _band_attention(qkv_a.reshape(B, S, QKV_A), bias_a, half=A_HALF_WIN, n_heads=A_HEADS,
                         n_kv=A_KV_HEADS, q_width=A_WIDTH, kv_width=A_KV_WIDTH, out_width=A_WIDTH,
                         out_dtype=BF16, sink=sink_a[0].astype(F32), gain=vec(norm_a_g[0]))
    branch_out = []
    for (win, dil), qkv in zip(B_BRANCHES, (qkv_b1.reshape(B, S, QKV_B),
                                             qkv_b4.reshape(B * 4, S // 4, QKV_B),
                                             qkv_b16.reshape(B * 16, S // 16, QKV_B))):
        half = (win // 2) // dil
        bias_b = _band_bias(rel_bias[:, A_HEADS:], half, dil, B_HEADS, 1)
        branch_out.append(_band_attention(qkv, bias_b, half=half, n_heads=B_HEADS, n_kv=B_HEADS,
                                          q_width=B_WIDTH, kv_width=B_WIDTH, out_width=B_OUT,
                                          out_dtype=F32))
    o1 = branch_out[0].reshape(T, B_OUT)
    o4 = branch_out[1].reshape(B, 4, S // 4, B_OUT)
    o16 = branch_out[2].reshape(B, 16, S // 16, B_OUT)

    h1 = _merge_project(ya.reshape(T, A_WIDTH), o1, o4, o16, h0, vec(norm_b_g[0]), w_out_b,
                        vec(ln1_g[0]), vec(ln1_b[0]))

    k_mem, v_mem = _mem_kv(mem, xkv_b)
    xr = _cross_attention_route(h1, k_mem, v_mem, xq_b, xo_b, vec(ln2_g[0]), vec(ln2_b[0]), wr_hi, wr_lo, br)
    return h0, h1, xr


def _moe(xr, w_gate, w_up, w_down, ln3_g, ln3_b):
    T = xr.shape[0]
    wg_b = w_gate[0].reshape(N_EXPERTS, D_MODEL, D_EXPERT).astype(BF16)
    wu_b = w_up[0].reshape(N_EXPERTS, D_MODEL, D_EXPERT).astype(BF16)
    wd_b = w_down[0].reshape(N_EXPERTS, D_EXPERT, D_MODEL).astype(BF16)
    n_tiles_max = T // TM_MOE + N_CLASSES
    cls = xr[:, D_MODEL].astype(jnp.int32)
    src_rows, tile_blk, tile_elo, tile_ehi, tile_cnt, n_tiles = _moe_plan(cls, n_tiles_max)
    xr_sorted = _gather_rows(src_rows, n_tiles, xr, n_tiles_max)
    y_sorted = _expert_mlp(tile_blk, tile_elo, tile_ehi, n_tiles, xr_sorted, wg_b, wu_b, wd_b,
                           _vec(ln3_g[0]), _vec(ln3_b[0]))
    return _scatter_rows(src_rows, tile_cnt, n_tiles, y_sorted, T)


def kernel(x, mem, ln_in_g, ln_in_b, w_in, rel_bias, sink_a, norm_a_g, norm_b_g, w_out,
           ln1_g, ln1_b, xq, xkv, xo, ln2_g, ln2_b, w_group, b_group, w_router, b_router,
           w_gate, w_up, w_down, ln3_g, ln3_b):
    _, _, xr = _mixer_and_cross_attention(
        x, mem, ln_in_g, ln_in_b, w_in, rel_bias, sink_a, norm_a_g, norm_b_g, w_out,
        ln1_g, ln1_b, xq, xkv, xo, ln2_g, ln2_b, w_group, b_group, w_router, b_router)
    return _moe(xr, w_gate, w_up, w_down, ln3_g, ln3_b).reshape(x.shape)
```

```python
import functools

import numpy as np
import jax
import jax.numpy as jnp
from jax import lax
from jax.experimental import pallas as pl
from jax.experimental.pallas import tpu as pltpu

F32 = jnp.float32
BF16 = jnp.bfloat16

D_MODEL = 1024
SEQ = 2048
MEM_LEN = 256
HEAD_DIM = 64
A_HEADS = 8
A_KV_HEADS = 2
A_HALF_WIN = 128
B_HEADS = 8
B_BRANCHES = ((128, 1), (512, 4), (2048, 16))
N_BUCKETS = 32
MAX_DISTANCE = 1024
X_HEADS = 4
X_HEAD_DIM = D_MODEL // X_HEADS
N_GROUPS = 4
EXPERTS_PER_GROUP = 8
N_EXPERTS = N_GROUPS * EXPERTS_PER_GROUP
D_EXPERT = 512
DEPTH = 1
ALPHA = (2.0 * DEPTH) ** 0.25
LN_EPS = 1e-5
NEG = -1e30

LANES = 128
SUBLANES = 8
SLAB = D_MODEL // LANES
A_WIDTH = A_HEADS * HEAD_DIM
B_WIDTH = B_HEADS * HEAD_DIM
A_KV_TILES = A_KV_HEADS
QKV_A = A_WIDTH + 2 * A_KV_TILES * LANES
QKV_B = 3 * B_WIDTH
B_OUT = B_WIDTH + LANES

PAIRS_PER_GROUP = EXPERTS_PER_GROUP * (EXPERTS_PER_GROUP - 1) // 2
N_CLASSES = N_GROUPS * PAIRS_PER_GROUP

TM_IN = 512
TM_ATT = 128
TM_MERGE = 256
TM_X = 256
TM_MOE = 128
VMEM_LIMIT = 56 * 1024 * 1024


def _cparams(*sem):
    return pltpu.CompilerParams(dimension_semantics=sem, vmem_limit_bytes=VMEM_LIMIT)


def _layer_norm(x, g, b):
    mu = jnp.mean(x, axis=-1, keepdims=True)
    xc = x - mu
    var = jnp.mean(xc * xc, axis=-1, keepdims=True)
    return xc * lax.rsqrt(var + LN_EPS) * g + b


def _rms_norm(x, g):
    return x * lax.rsqrt(jnp.mean(x * x, axis=-1, keepdims=True) + LN_EPS) * g


def _t5_bucket(rel):
    nb = N_BUCKETS // 2
    max_exact = nb // 2
    ret = (rel > 0).astype(np.int32) * nb
    n = np.abs(rel)
    n_safe = np.maximum(n, 1).astype(np.float64)
    large = max_exact + (np.log(n_safe / max_exact) / np.log(MAX_DISTANCE / max_exact)
                         * (nb - max_exact)).astype(np.int32)
    large = np.minimum(large, nb - 1)
    return (ret + np.where(n < max_exact, n, large)).astype(np.int32)


def _inproj_kernel(x_ref, g_ref, b_ref, w_ref, h0_ref, qa_ref, qb1_ref, qb4_ref, qb16_ref, pb_scr):
    h = _layer_norm(x_ref[...], g_ref[...], b_ref[...])
    h0_ref[...] = h
    proj = jnp.dot(h.astype(BF16), w_ref[...], preferred_element_type=F32)
    qa_ref[...] = proj[:, :QKV_A].astype(BF16)
    pb = proj[:, QKV_A:]
    qb1_ref[...] = pb.astype(BF16)
    for c in range(QKV_B // LANES):
        cs = slice(c * LANES, (c + 1) * LANES)
        pb_scr[c] = pb[:, cs]
        for dil, ref in ((4, qb4_ref), (16, qb16_ref)):
            for r in range(dil):
                ref[0, r, :, cs] = pb_scr[c, pl.ds(r, TM_IN // dil, stride=dil), :].astype(BF16)


def _input_projection(x2, ln_g, ln_b, w_in_b, batch):
    T = x2.shape[0]
    tiles_per_seq = SEQ // TM_IN
    row = lambda i: (i, 0)
    const = lambda i: (0, 0)
    deint = lambda i: (i // tiles_per_seq, 0, i % tiles_per_seq, 0)
    return pl.pallas_call(
        _inproj_kernel,
        grid=(T // TM_IN,),
        in_specs=[
            pl.BlockSpec((TM_IN, D_MODEL), row),
            pl.BlockSpec((1, D_MODEL), const),
            pl.BlockSpec((1, D_MODEL), const),
            pl.BlockSpec((D_MODEL, QKV_A + QKV_B), const),
        ],
        out_specs=[
            pl.BlockSpec((TM_IN, D_MODEL), row),
            pl.BlockSpec((TM_IN, QKV_A), row),
            pl.BlockSpec((TM_IN, QKV_B), row),
            pl.BlockSpec((1, 4, TM_IN // 4, QKV_B), deint),
            pl.BlockSpec((1, 16, TM_IN // 16, QKV_B), deint),
        ],
        out_shape=[
            jax.ShapeDtypeStruct((T, D_MODEL), F32),
            jax.ShapeDtypeStruct((T, QKV_A), BF16),
            jax.ShapeDtypeStruct((T, QKV_B), BF16),
            jax.ShapeDtypeStruct((batch, 4, SEQ // 4, QKV_B), BF16),
            jax.ShapeDtypeStruct((batch, 16, SEQ // 16, QKV_B), BF16),
        ],
        scratch_shapes=[pltpu.VMEM((QKV_B // LANES, TM_IN, LANES), F32)],
        compiler_params=_cparams("parallel"),
        name="ln_in_proj",
    )(x2, ln_g, ln_b, w_in_b)


def _band_attn_kernel(*refs, m, nk, n, sub, q_tiles, with_sink):
    if with_sink:
        sink_ref, gain_ref = refs[0], refs[1]
        refs = refs[2:]
    q_ref, k_ref, v_ref, bias_ref, o_ref = refs
    if nk == n:
        start = 0
    else:
        j = pl.program_id(1)
        start = pl.multiple_of(jnp.clip(j * m - (nk - m) // 2, 0, n - nk), HEAD_DIM)
    lane_row = lax.broadcasted_iota(jnp.int32, (1, LANES), 1)
    keep_lo = jnp.where(lane_row < HEAD_DIM, 1.0, 0.0).astype(BF16)
    keep_hi = jnp.where(lane_row < HEAD_DIM, 0.0, 1.0).astype(BF16)
    lane = lax.broadcasted_iota(jnp.int32, (m, LANES), 1)
    units = [(s, t) for s in range(sub) for t in range(len(q_tiles))]

    scores = []
    for s, t in units:
        k_t = k_ref[s, pl.ds(start, nk), t * LANES:(t + 1) * LANES]
        parts = []
        for qt in q_tiles[t]:
            q2 = q_ref[s, :, qt * LANES:(qt + 1) * LANES]
            parts += [q2 * keep_lo, q2 * keep_hi]
        lhs = jnp.concatenate(parts, axis=0)
        sc = lax.dot_general(lhs, k_t, (((1,), (1,)), ((), ())), preferred_element_type=F32)
        scores.append(sc + bias_ref[0, t])

    probs, inv_l, lse = [], [], []
    for (s, t), sc in zip(units, scores):
        mx = jnp.max(sc, axis=-1, keepdims=True)
        if with_sink:
            sink_col = jnp.concatenate(
                [jnp.full((m, 1), sink_ref[2 * qt + h], F32) for qt in q_tiles[t] for h in (0, 1)], axis=0)
            mx = jnp.maximum(mx, sink_col)
        p = jnp.exp(sc - mx)
        l = jnp.sum(p, axis=-1, keepdims=True)
        if with_sink:
            l = l + jnp.exp(sink_col - mx)
        else:
            lse.append(mx + jnp.log(l))
        probs.append(p.astype(BF16))
        inv_l.append(1.0 / l)

    n_q_tiles = sum(len(ts) for ts in q_tiles)
    for s in range(sub):
        pairs = [None] * n_q_tiles
        lse_tile = jnp.zeros((m, LANES), F32)
        for t, tiles in enumerate(q_tiles):
            u = s * len(q_tiles) + t
            v_t = v_ref[s, pl.ds(start, nk), t * LANES:(t + 1) * LANES]
            o = jnp.dot(probs[u], v_t, preferred_element_type=F32) * inv_l[u]
            for i, qt in enumerate(tiles):
                top = o[(2 * i) * m:(2 * i + 1) * m]
                bot = o[(2 * i + 1) * m:(2 * i + 2) * m]
                pairs[qt] = jnp.where(lane < HEAD_DIM, top, bot)
                if not with_sink:
                    lse_tile = jnp.where(lane == 2 * qt, lse[u][(2 * i) * m:(2 * i + 1) * m], lse_tile)
                    lse_tile = jnp.where(lane == 2 * qt + 1, lse[u][(2 * i + 1) * m:(2 * i + 2) * m], lse_tile)
        o_all = jnp.concatenate(pairs, axis=1)
        if with_sink:
            o_ref[s] = _rms_norm(o_all, gain_ref[...]).astype(o_ref.dtype)
        else:
            o_ref[s] = jnp.concatenate([o_all, lse_tile], axis=1)


def _band_window(m, nk, n):
    nb = n // m
    starts = np.clip(np.arange(nb) * m - (nk - m) // 2, 0, n - nk)
    offs = [int(o) for o in starts - np.arange(nb) * m]
    uniq = sorted(set(offs), reverse=True)
    var = [uniq.index(o) for o in offs]
    assert all(v == var[1] for v in var[1:-1])
    return uniq, var


def _band_bias(rel_bias_h, half, dil, m, nk, n, q_tiles):
    uniq, _ = _band_window(m, nk, n)
    i = np.arange(m)[:, None]
    c = np.arange(nk)[None, :]
    out = []
    for off in uniq:
        rel = off + c - i
        b = rel_bias_h.astype(F32)[_t5_bucket(dil * rel)]
        b = jnp.where((np.abs(rel) <= half)[:, :, None], b, NEG)
        b = jnp.transpose(b, (2, 0, 1))
        out.append(jnp.stack([jnp.concatenate([b[2 * qt + h] for qt in tiles for h in (0, 1)], axis=0)
                              for tiles in q_tiles]))
    return jnp.stack(out)


def _band_attention(qkv, bias, *, nk, sub, q_tiles, q_width, kv_width, out_width, out_dtype,
                    sink=None, gain=None):
    Bd, n, _ = qkv.shape
    m = TM_ATT
    nb = n // m
    assert nb * m == n and Bd % sub == 0
    _, var = _band_window(m, nk, n)
    kcol = q_width // kv_width
    rows = 2 * m * len(q_tiles[0])
    with_sink = sink is not None

    def variant(b, j):
        v = jnp.where(j == nb - 1, var[-1], var[min(1, nb - 1)])
        return jnp.where(j == 0, var[0], v)

    in_specs = [
        pl.BlockSpec((sub, m, q_width), lambda b, j: (b, j, 0)),
        pl.BlockSpec((sub, n, kv_width), lambda b, j: (b, 0, kcol)),
        pl.BlockSpec((sub, n, kv_width), lambda b, j: (b, 0, kcol + 1)),
        pl.BlockSpec((1, len(q_tiles), rows, nk), lambda b, j: (variant(b, j), 0, 0, 0)),
    ]
    args = [qkv, qkv, qkv, bias]
    if with_sink:
        in_specs = [pl.BlockSpec(memory_space=pltpu.SMEM),
                    pl.BlockSpec((1, q_width), lambda b, j: (0, 0))] + in_specs
        args = [sink, gain] + args
    return pl.pallas_call(
        functools.partial(_band_attn_kernel, m=m, nk=nk, n=n, sub=sub, q_tiles=q_tiles, with_sink=with_sink),
        grid=(Bd // sub, nb),
        in_specs=in_specs,
        out_specs=pl.BlockSpec((sub, m, out_width), lambda b, j: (b, j, 0)),
        out_shape=jax.ShapeDtypeStruct((Bd, n, out_width), out_dtype),
        compiler_params=_cparams("parallel", "arbitrary"),
        name="band_attn_sink" if with_sink else f"band_attn_n{n}",
    )(*args)


def _merge_kernel(ya_ref, o1_ref, o4_ref, o16_ref, h0_ref, gb_ref, w_ref, g1_ref, b1_ref, h1_ref,
                  s4_scr, s16_scr):
    for c in range(B_OUT // LANES):
        cs = slice(c * LANES, (c + 1) * LANES)
        for dil, src, dst in ((4, o4_ref, s4_scr), (16, o16_ref, s16_scr)):
            for r in range(dil):
                dst[c, pl.ds(r, TM_MERGE // dil, stride=dil), :] = src[0, r, :, cs]
    lse_c = B_WIDTH // LANES
    lse = [o1_ref[:, B_WIDTH:], s4_scr[lse_c], s16_scr[lse_c]]
    mx = jnp.maximum(jnp.maximum(lse[0], lse[1]), lse[2])
    ex = [jnp.exp(l - mx) for l in lse]
    inv = 1.0 / (ex[0] + ex[1] + ex[2])
    wts = [e * inv for e in ex]
    lane = lax.broadcasted_iota(jnp.int32, (TM_MERGE, LANES), 1)
    pieces = []
    for c in range(lse_c):
        tiles = (o1_ref[:, c * LANES:(c + 1) * LANES], s4_scr[c], s16_scr[c])
        acc = jnp.zeros((TM_MERGE, LANES), F32)
        for w, o in zip(wts, tiles):
            acc = acc + jnp.where(lane < HEAD_DIM, w[:, 2 * c:2 * c + 1], w[:, 2 * c + 1:2 * c + 2]) * o
        pieces.append(acc)
    yb = _rms_norm(jnp.concatenate(pieces, axis=1), gb_ref[...])
    y = jnp.concatenate([ya_ref[...], yb.astype(BF16)], axis=1)
    mix = jnp.dot(y, w_ref[...], preferred_element_type=F32)
    h1_ref[...] = _layer_norm(ALPHA * h0_ref[...] + mix, g1_ref[...], b1_ref[...])


def _merge_project(ya, o1, o4, o16, h0, gain_b, w_out_b, ln_g, ln_b):
    T = h0.shape[0]
    tiles_per_seq = SEQ // TM_MERGE
    row = lambda i: (i, 0)
    const = lambda i: (0, 0)
    deint = lambda i: (i // tiles_per_seq, 0, i % tiles_per_seq, 0)
    return pl.pallas_call(
        _merge_kernel,
        grid=(T // TM_MERGE,),
        in_specs=[
            pl.BlockSpec((TM_MERGE, A_WIDTH), row),
            pl.BlockSpec((TM_MERGE, B_OUT), row),
            pl.BlockSpec((1, 4, TM_MERGE // 4, B_OUT), deint),
            pl.BlockSpec((1, 16, TM_MERGE // 16, B_OUT), deint),
            pl.BlockSpec((TM_MERGE, D_MODEL), row),
            pl.BlockSpec((1, B_WIDTH), const),
            pl.BlockSpec((D_MODEL, D_MODEL), const),
            pl.BlockSpec((1, D_MODEL), const),
            pl.BlockSpec((1, D_MODEL), const),
        ],
        out_specs=pl.BlockSpec((TM_MERGE, D_MODEL), row),
        out_shape=jax.ShapeDtypeStruct((T, D_MODEL), F32),
        scratch_shapes=[pltpu.VMEM((B_OUT // LANES, TM_MERGE, LANES), F32)] * 2,
        compiler_params=_cparams("parallel"),
        name="merge_out_proj",
    )(ya, o1, o4, o16, h0, gain_b, w_out_b, ln_g, ln_b)


def _mem_kv_kernel(mem_ref, w_ref, k_ref, v_ref):
    kv = jnp.dot(mem_ref[0].astype(BF16), w_ref[...], preferred_element_type=F32)
    k_ref[0] = kv[:, :D_MODEL].astype(BF16)
    v_ref[0] = kv[:, D_MODEL:].astype(BF16)


def _mem_kv(mem, xkv_b):
    B = mem.shape[0]
    blk = pl.BlockSpec((1, MEM_LEN, D_MODEL), lambda b: (b, 0, 0))
    return pl.pallas_call(
        _mem_kv_kernel,
        grid=(B,),
        in_specs=[blk, pl.BlockSpec((D_MODEL, 2 * D_MODEL), lambda b: (0, 0))],
        out_specs=[blk, blk],
        out_shape=[jax.ShapeDtypeStruct((B, MEM_LEN, D_MODEL), BF16)] * 2,
        compiler_params=_cparams("parallel"),
        name="mem_kv_proj",
    )(mem, xkv_b)


def _route(logits):
    rows = logits.shape[0]
    lane = lax.broadcasted_iota(jnp.int32, (rows, LANES), 1).astype(F32)
    big = float(LANES)
    ninf = -jnp.inf
    gl = jnp.where(lane < N_GROUPS, logits, ninf)
    gmax = jnp.max(gl, axis=-1, keepdims=True)
    gidx = jnp.min(jnp.where(gl == gmax, lane, big), axis=-1, keepdims=True)
    g_p = 1.0 / jnp.sum(jnp.exp(gl - gmax), axis=-1, keepdims=True)
    lo_lane = N_GROUPS + EXPERTS_PER_GROUP * gidx
    el = jnp.where((lane >= lo_lane) & (lane < lo_lane + EXPERTS_PER_GROUP), logits, ninf)
    v1 = jnp.max(el, axis=-1, keepdims=True)
    i1 = jnp.min(jnp.where(el == v1, lane, big), axis=-1, keepdims=True)
    el2 = jnp.where(lane == i1, ninf, el)
    v2 = jnp.max(el2, axis=-1, keepdims=True)
    i2 = jnp.min(jnp.where(el2 == v2, lane, big), axis=-1, keepdims=True)
    t = jnp.exp(v2 - v1)
    w1 = g_p / (1.0 + t)
    w2 = g_p * t / (1.0 + t)
    a = jnp.minimum(i1, i2) - lo_lane
    b = jnp.maximum(i1, i2) - lo_lane
    pair = a * (2 * EXPERTS_PER_GROUP - 1 - a) * 0.5 + (b - a - 1.0)
    cls = gidx * PAIRS_PER_GROUP + pair
    w_lo = jnp.where(i1 < i2, w1, w2)
    w_hi = jnp.where(i1 < i2, w2, w1)
    return jnp.where(lane == 0, cls, jnp.where(lane == 1, w_lo, jnp.where(lane == 2, w_hi, 0.0)))


def _xattn_kernel(h1_ref, k_ref, v_ref, wq_ref, wo_ref, g2_ref, b2_ref, wr_hi_ref, wr_lo_ref, br_ref,
                  h2_slab_ref, route_ref):
    h1 = h1_ref[...]
    q = jnp.dot(h1.astype(BF16), wq_ref[...], preferred_element_type=F32).astype(BF16)
    outs = []
    for hd in range(X_HEADS):
        sl = slice(hd * X_HEAD_DIM, (hd + 1) * X_HEAD_DIM)
        s = lax.dot_general(q[:, sl], k_ref[0, :, sl], (((1,), (1,)), ((), ())), preferred_element_type=F32)
        m = jnp.max(s, axis=-1, keepdims=True)
        p = jnp.exp(s - m)
        l = jnp.sum(p, axis=-1, keepdims=True)
        o = jnp.dot(p.astype(BF16), v_ref[0, :, sl], preferred_element_type=F32) / l
        outs.append(o.astype(BF16))
    xa = jnp.dot(jnp.concatenate(outs, axis=1), wo_ref[...], preferred_element_type=F32)
    h2 = _layer_norm(ALPHA * h1 + xa, g2_ref[...], b2_ref[...])
    h_hi = h2.astype(BF16)
    h_lo = (h2 - h_hi.astype(F32)).astype(BF16)
    logits = (jnp.dot(h_hi, wr_hi_ref[...], preferred_element_type=F32)
              + jnp.dot(h_hi, wr_lo_ref[...], preferred_element_type=F32)
              + jnp.dot(h_lo, wr_hi_ref[...], preferred_element_type=F32)) + br_ref[...]
    for c in range(SLAB):
        h2_slab_ref[pl.ds(c, TM_X, stride=SLAB), :] = h2[:, c * LANES:(c + 1) * LANES]
    route_ref[...] = _route(logits)


def _cross_attention_route(h1, k, v, xq_b, xo_b, ln_g, ln_b, wr_hi, wr_lo, br):
    T = h1.shape[0]
    tiles_per_seq = SEQ // TM_X
    row = lambda i: (i, 0)
    const = lambda i: (0, 0)
    kv_blk = pl.BlockSpec((1, MEM_LEN, D_MODEL), lambda i: (i // tiles_per_seq, 0, 0))
    return pl.pallas_call(
        _xattn_kernel,
        grid=(T // TM_X,),
        in_specs=[
            pl.BlockSpec((TM_X, D_MODEL), row), kv_blk, kv_blk,
            pl.BlockSpec((D_MODEL, D_MODEL), const),
            pl.BlockSpec((D_MODEL, D_MODEL), const),
            pl.BlockSpec((1, D_MODEL), const),
            pl.BlockSpec((1, D_MODEL), const),
            pl.BlockSpec((D_MODEL, LANES), const),
            pl.BlockSpec((D_MODEL, LANES), const),
            pl.BlockSpec((1, LANES), const),
        ],
        out_specs=[pl.BlockSpec((TM_X * SLAB, LANES), row), pl.BlockSpec((TM_X, LANES), row)],
        out_shape=[jax.ShapeDtypeStruct((T * SLAB, LANES), F32), jax.ShapeDtypeStruct((T, LANES), F32)],
        compiler_params=_cparams("parallel"),
        name="xattn_ln_route",
    )(h1, k, v, xq_b, xo_b, ln_g, ln_b, wr_hi, wr_lo, br)


def _slab_copy(src_hbm, dst_hbm, src_row, dst_row, sem):
    src = pl.ds(pl.multiple_of(src_row * SLAB, SLAB), SLAB)
    dst = pl.ds(pl.multiple_of(dst_row * SLAB, SLAB), SLAB)
    return pltpu.make_async_copy(src_hbm.at[src], dst_hbm.at[dst], sem)


def _gather_kernel(src_ref, ntile_ref, x_hbm, z_hbm, o_hbm, sems):
    j = pl.program_id(0)
    last = ntile_ref[0] - 1
    n_steps = pl.num_programs(0)

    def wait_tile(t):
        def body(i, c):
            _slab_copy(x_hbm, o_hbm, 0, t * TM_MOE + i, sems.at[t % 2]).wait()
            return c
        lax.fori_loop(0, TM_MOE, body, 0)

    def zero_copy(t):
        dst = pl.ds(pl.multiple_of(t * (TM_MOE * SLAB), TM_MOE * SLAB), TM_MOE * SLAB)
        return pltpu.make_async_copy(z_hbm, o_hbm.at[dst], sems.at[2])

    @pl.when(j <= last)
    def _():
        def body(i, c):
            p = j * TM_MOE + i
            _slab_copy(x_hbm, o_hbm, src_ref[p], p, sems.at[j % 2]).start()
            return c
        lax.fori_loop(0, TM_MOE, body, 0)

    @pl.when((j >= 1) & (j <= last))
    def _():
        wait_tile(j - 1)

    @pl.when(j == last)
    def _():
        wait_tile(j)

    @pl.when(j > last)
    def _():
        zero_copy(j).start()

    @pl.when(j == n_steps - 1)
    def _():
        def body(t, c):
            zero_copy(t).wait()
            return c
        lax.fori_loop(last + 1, n_steps, body, 0)


def _gather_rows(src_rows, n_tiles, x_slab, n_tiles_max):
    any_spec = pl.BlockSpec(memory_space=pl.ANY)
    return pl.pallas_call(
        _gather_kernel,
        grid_spec=pltpu.PrefetchScalarGridSpec(
            num_scalar_prefetch=2,
            grid=(n_tiles_max,),
            in_specs=[any_spec, any_spec],
            out_specs=any_spec,
            scratch_shapes=[pltpu.SemaphoreType.DMA((3,))],
        ),
        out_shape=jax.ShapeDtypeStruct((n_tiles_max * TM_MOE * SLAB, LANES), x_slab.dtype),
        compiler_params=_cparams("arbitrary"),
        name="moe_gather",
    )(src_rows, n_tiles, x_slab, jnp.zeros((TM_MOE * SLAB, LANES), x_slab.dtype))


def _scatter_kernel(dst_ref, cnt_ref, ntile_ref, y_hbm, o_hbm, sems):
    j = pl.program_id(0)
    last = ntile_ref[0] - 1

    def wait_tile(t):
        def body(i, c):
            _slab_copy(y_hbm, o_hbm, t * TM_MOE + i, 0, sems.at[t % 2]).wait()
            return c
        lax.fori_loop(0, cnt_ref[t], body, 0)

    @pl.when(j <= last)
    def _():
        def body(i, c):
            p = j * TM_MOE + i
            _slab_copy(y_hbm, o_hbm, p, dst_ref[p], sems.at[j % 2]).start()
            return c
        lax.fori_loop(0, cnt_ref[j], body, 0)

    @pl.when((j >= 1) & (j <= last))
    def _():
        wait_tile(j - 1)

    @pl.when(j == last)
    def _():
        wait_tile(j)


def _scatter_rows(dst_rows, tile_cnt, n_tiles, y_slab, n_out):
    n_tiles_max = tile_cnt.shape[0]
    return pl.pallas_call(
        _scatter_kernel,
        grid_spec=pltpu.PrefetchScalarGridSpec(
            num_scalar_prefetch=3,
            grid=(n_tiles_max,),
            in_specs=[pl.BlockSpec(memory_space=pl.ANY)],
            out_specs=pl.BlockSpec(memory_space=pl.ANY),
            scratch_shapes=[pltpu.SemaphoreType.DMA((2,))],
        ),
        out_shape=jax.ShapeDtypeStruct((n_out * SLAB, LANES), y_slab.dtype),
        compiler_params=_cparams("arbitrary"),
        name="moe_scatter",
    )(dst_rows, tile_cnt, n_tiles, y_slab)


def _expert_kernel(blk_ref, elo_ref, ehi_ref, ntile_ref, xs_ref, gate_ref,
                   wg_lo, wu_lo, wd_lo, wg_hi, wu_hi, wd_hi, g3_ref, b3_ref, o_ref):
    j = pl.program_id(0)

    @pl.when(j < ntile_ref[0])
    def _():
        x = jnp.concatenate([xs_ref[pl.ds(c, TM_MOE, stride=SLAB), :] for c in range(SLAB)], axis=1)
        xb = x.astype(BF16)
        eye = (lax.broadcasted_iota(jnp.int32, (TM_MOE, TM_MOE), 0)
               == lax.broadcasted_iota(jnp.int32, (TM_MOE, TM_MOE), 1))
        y = jnp.zeros_like(x)
        for e, (wg, wu, wd) in enumerate(((wg_lo, wu_lo, wd_lo), (wg_hi, wu_hi, wd_hi))):
            gate = jnp.sum(jnp.where(eye, gate_ref[0, e:e + 1, :], 0.0), axis=1, keepdims=True)
            a = jnp.dot(xb, wg[0], preferred_element_type=F32)
            u = jnp.dot(xb, wu[0], preferred_element_type=F32)
            hid = a * jax.nn.sigmoid(a) * u
            y = y + jnp.dot((gate * hid).astype(BF16), wd[0], preferred_element_type=F32)
        out = _layer_norm(ALPHA * x + y, g3_ref[...], b3_ref[...])
        for c in range(SLAB):
            o_ref[pl.ds(c, TM_MOE, stride=SLAB), :] = out[:, c * LANES:(c + 1) * LANES]

    @pl.when(j >= ntile_ref[0])
    def _():
        o_ref[...] = jnp.zeros_like(o_ref)


def _expert_mlp(tile_blk, tile_elo, tile_ehi, n_tiles, xs_sorted, gates, wg, wu, wd, ln_g, ln_b):
    n_tiles_max = tile_blk.shape[0]
    const = lambda j, blk, elo, ehi, nt: (0, 0)
    lo = lambda j, blk, elo, ehi, nt: (elo[j], 0, 0)
    hi = lambda j, blk, elo, ehi, nt: (ehi[j], 0, 0)
    up = (1, D_MODEL, D_EXPERT)
    down = (1, D_EXPERT, D_MODEL)
    return pl.pallas_call(
        _expert_kernel,
        grid_spec=pltpu.PrefetchScalarGridSpec(
            num_scalar_prefetch=4,
            grid=(n_tiles_max,),
            in_specs=[
                pl.BlockSpec((TM_MOE * SLAB, LANES), lambda j, blk, elo, ehi, nt: (blk[j], 0)),
                pl.BlockSpec((1, 2, TM_MOE), lambda j, blk, elo, ehi, nt: (blk[j], 0, 0)),
                pl.BlockSpec(up, lo), pl.BlockSpec(up, lo), pl.BlockSpec(down, lo),
                pl.BlockSpec(up, hi), pl.BlockSpec(up, hi), pl.BlockSpec(down, hi),
                pl.BlockSpec((1, D_MODEL), const),
                pl.BlockSpec((1, D_MODEL), const),
            ],
            out_specs=pl.BlockSpec((TM_MOE * SLAB, LANES), lambda j, blk, elo, ehi, nt: (j, 0)),
        ),
        out_shape=jax.ShapeDtypeStruct((n_tiles_max * TM_MOE * SLAB, LANES), F32),
        compiler_params=_cparams("arbitrary"),
        name="moe_experts",
    )(tile_blk, tile_elo, tile_ehi, n_tiles, xs_sorted, gates, wg, wu, wd, wg, wu, wd, ln_g, ln_b)


def _class_experts():
    lo, hi = [], []
    for g in range(N_GROUPS):
        for a in range(EXPERTS_PER_GROUP):
            for b in range(a + 1, EXPERTS_PER_GROUP):
                lo.append(g * EXPERTS_PER_GROUP + a)
                hi.append(g * EXPERTS_PER_GROUP + b)
    return np.asarray(lo, np.int32), np.asarray(hi, np.int32)


def _moe_plan(cls, w_lo, w_hi, n_tiles_max):
    T = cls.shape[0]
    onehot = (cls[:, None] == jnp.arange(N_CLASSES, dtype=jnp.int32)[None, :]).astype(jnp.int32)
    csum = jnp.cumsum(onehot, axis=0)
    counts = csum[-1]
    rank = jnp.sum((csum - onehot) * onehot, axis=1)
    tiles_per_class = (counts + TM_MOE - 1) // TM_MOE
    tile_end = jnp.cumsum(tiles_per_class)
    tile_start = tile_end - tiles_per_class
    n_tiles = tile_end[-1]
    pos = tile_start[cls] * TM_MOE + rank
    tok = jnp.arange(T, dtype=jnp.int32)
    src_rows = jnp.zeros((n_tiles_max * TM_MOE,), jnp.int32).at[pos].set(tok)
    gates = jnp.stack([w_lo[src_rows], w_hi[src_rows]]).reshape(2, n_tiles_max, TM_MOE).transpose(1, 0, 2)
    tile_ids = jnp.arange(n_tiles_max, dtype=jnp.int32)
    tile_cls = jnp.searchsorted(tile_end, jnp.minimum(tile_ids, n_tiles - 1), side="right").astype(jnp.int32)
    tile_cls = jnp.minimum(tile_cls, N_CLASSES - 1)
    cls_lo, cls_hi = _class_experts()
    tile_elo = jnp.asarray(cls_lo)[tile_cls]
    tile_ehi = jnp.asarray(cls_hi)[tile_cls]
    tile_blk = jnp.minimum(tile_ids, n_tiles - 1)
    within = tile_ids - tile_start[tile_cls]
    tile_cnt = jnp.clip(counts[tile_cls] - within * TM_MOE, 0, TM_MOE)
    tile_cnt = jnp.where(tile_ids < n_tiles, tile_cnt, 0).astype(jnp.int32)
    return src_rows, gates, tile_blk, tile_elo, tile_ehi, tile_cnt, n_tiles.reshape(1).astype(jnp.int32)


def _vec(a):
    return a.reshape(1, -1).astype(F32)


def _mixer_and_cross_attention(x, mem, ln_in_g, ln_in_b, w_in, rel_bias, sink_a, norm_a_g, norm_b_g, w_out,
                               ln1_g, ln1_b, xq, xkv, xo, ln2_g, ln2_b, w_group, b_group, w_router, b_router):
    B, S, D = x.shape
    assert S == SEQ and D == D_MODEL and mem.shape[1:] == (MEM_LEN, D_MODEL)
    T = B * S
    vec = _vec

    w = w_in[0]
    edges = np.cumsum((0, A_WIDTH, A_KV_HEADS * HEAD_DIM, A_KV_HEADS * HEAD_DIM, B_WIDTH, B_WIDTH, B_WIDTH))
    qa, ka, va, qb, kb, vb = [w[:, a:b] for a, b in zip(edges[:-1], edges[1:])]
    dup = lambda t: jnp.repeat(t.reshape(D, A_KV_HEADS, 1, HEAD_DIM), 2, axis=2).reshape(D, A_KV_TILES * LANES)
    scale = HEAD_DIM ** -0.5
    w_in_b = jnp.concatenate([qa * scale, dup(ka), dup(va), qb * scale, kb, vb], axis=1).astype(BF16)
    w_out_b = w_out[0].astype(BF16)
    xq_b = (xq[0] * (X_HEAD_DIM ** -0.5)).astype(BF16)
    xkv_b = xkv[0].astype(BF16)
    xo_b = xo[0].astype(BF16)
    wr = jnp.concatenate([w_group[0], w_router[0]], axis=1).astype(F32)
    wr = jnp.pad(wr, ((0, 0), (0, LANES - wr.shape[1])))
    wr_hi = wr.astype(BF16)
    wr_lo = (wr - wr_hi.astype(F32)).astype(BF16)
    br = jnp.concatenate([b_group[0], b_router[0]]).astype(F32)
    br = jnp.pad(br, (0, LANES - br.shape[0])).reshape(1, LANES)

    h0, qkv_a, qkv_b1, qkv_b4, qkv_b16 = _input_projection(
        x.reshape(T, D), vec(ln_in_g), vec(ln_in_b), w_in_b, B)

    tiles_a = ((0, 1), (2, 3))
    nk_a = TM_ATT + 2 * A_HALF_WIN
    bias_a = _band_bias(rel_bias[:, :A_HEADS], A_HALF_WIN, 1, TM_ATT, nk_a, S, tiles_a)
    ya = _band_attention(qkv_a.reshape(B, S, QKV_A), bias_a, nk=nk_a, sub=1, q_tiles=tiles_a,
                         q_width=A_WIDTH, kv_width=A_KV_TILES * LANES, out_width=A_WIDTH, out_dtype=BF16,
                         sink=sink_a[0].astype(F32), gain=vec(norm_a_g[0]))
    tiles_b = ((0,), (1,), (2,), (3,))
    branch_out = []
    for (win, dil), qkv in zip(B_BRANCHES, (qkv_b1.reshape(B, S, QKV_B),
                                             qkv_b4.reshape(B * 4, S // 4, QKV_B),
                                             qkv_b16.reshape(B * 16, S // 16, QKV_B))):
        half = (win // 2) // dil
        n = S // dil
        nk = min(TM_ATT + 2 * half, n)
        bias_b = _band_bias(rel_bias[:, A_HEADS:], half, dil, TM_ATT, nk, n, tiles_b)
        branch_out.append(_band_attention(qkv, bias_b, nk=nk, sub=max(1, TM_ATT * 4 // n), q_tiles=tiles_b,
                                          q_width=B_WIDTH, kv_width=B_WIDTH, out_width=B_OUT, out_dtype=F32))
    o1 = branch_out[0].reshape(T, B_OUT)
    o4 = branch_out[1].reshape(B, 4, S // 4, B_OUT)
    o16 = branch_out[2].reshape(B, 16, S // 16, B_OUT)

    h1 = _merge_project(ya.reshape(T, A_WIDTH), o1, o4, o16, h0, vec(norm_b_g[0]), w_out_b,
                        vec(ln1_g[0]), vec(ln1_b[0]))

    k_mem, v_mem = _mem_kv(mem, xkv_b)
    h2_slab, route = _cross_attention_route(h1, k_mem, v_mem, xq_b, xo_b, vec(ln2_g[0]), vec(ln2_b[0]),
                                            wr_hi, wr_lo, br)
    return h0, h1, h2_slab, route


def _moe(h2_slab, route, w_gate, w_up, w_down, ln3_g, ln3_b):
    T = route.shape[0]
    wg_b = w_gate[0].reshape(N_EXPERTS, D_MODEL, D_EXPERT).astype(BF16)
    wu_b = w_up[0].reshape(N_EXPERTS, D_MODEL, D_EXPERT).astype(BF16)
    wd_b = w_down[0].reshape(N_EXPERTS, D_EXPERT, D_MODEL).astype(BF16)
    n_tiles_max = T // TM_MOE + N_CLASSES
    cls = route[:, 0].astype(jnp.int32)
    src_rows, gates, tile_blk, tile_elo, tile_ehi, tile_cnt, n_tiles = _moe_plan(
        cls, route[:, 1], route[:, 2], n_tiles_max)
    xs_sorted = _gather_rows(src_rows, n_tiles, h2_slab, n_tiles_max)
    y_sorted = _expert_mlp(tile_blk, tile_elo, tile_ehi, n_tiles, xs_sorted, gates, wg_b, wu_b, wd_b,
                           _vec(ln3_g[0]), _vec(ln3_b[0]))
    return _scatter_rows(src_rows, tile_cnt, n_tiles, y_sorted, T)


def kernel(x, mem, ln_in_g, ln_in_b, w_in, rel_bias, sink_a, norm_a_g, norm_b_g, w_out,
           ln1_g, ln1_b, xq, xkv, xo, ln2_g, ln2_b, w_group, b_group, w_router, b_router,
           w_gate, w_up, w_down, ln3_g, ln3_b):
    _, _, h2_slab, route = _mixer_and_cross_attention(
        x, mem, ln_in_g, ln_in_b, w_in, rel_bias, sink_a, norm_a_g, norm_b_g, w_out,
        ln1_g, ln1_b, xq, xkv, xo, ln2_g, ln2_b, w_group, b_group, w_router, b_router)
    return _moe(h2_slab, route, w_gate, w_up, w_down, ln3_g, ln3_b).reshape(x.shape)
```

```python
import functools

import numpy as np
import jax
import jax.numpy as jnp
from jax import lax
from jax.experimental import pallas as pl
from jax.experimental.pallas import tpu as pltpu

F32 = jnp.float32
BF16 = jnp.bfloat16

D_MODEL = 1024
SEQ = 2048
MEM_LEN = 256
HEAD_DIM = 64
A_HEADS = 8
A_KV_HEADS = 2
A_HALF_WIN = 128
B_HEADS = 8
B_BRANCHES = ((128, 1), (512, 4), (2048, 16))
N_BUCKETS = 32
MAX_DISTANCE = 1024
X_HEADS = 4
X_HEAD_DIM = D_MODEL // X_HEADS
N_GROUPS = 4
EXPERTS_PER_GROUP = 8
N_EXPERTS = N_GROUPS * EXPERTS_PER_GROUP
D_EXPERT = 512
DEPTH = 1
ALPHA = (2.0 * DEPTH) ** 0.25
LN_EPS = 1e-5
NEG = -1e30

LANES = 128
SUBLANES = 8
SLAB = D_MODEL // LANES
A_WIDTH = A_HEADS * HEAD_DIM
B_WIDTH = B_HEADS * HEAD_DIM
A_KV_TILES = A_KV_HEADS
QKV_A = A_WIDTH + 2 * A_KV_TILES * LANES
QKV_B = 3 * B_WIDTH
B_OUT = B_WIDTH + LANES

PAIRS_PER_GROUP = EXPERTS_PER_GROUP * (EXPERTS_PER_GROUP - 1) // 2
N_CLASSES = N_GROUPS * PAIRS_PER_GROUP

TM_IN = 512
TM_ATT = 128
TM_MERGE = 256
TM_X = 256
TM_MOE = 128
VMEM_LIMIT = 56 * 1024 * 1024


def _cparams(*sem):
    return pltpu.CompilerParams(dimension_semantics=sem, vmem_limit_bytes=VMEM_LIMIT)


def _layer_norm(x, g, b):
    mu = jnp.mean(x, axis=-1, keepdims=True)
    xc = x - mu
    var = jnp.mean(xc * xc, axis=-1, keepdims=True)
    return xc * lax.rsqrt(var + LN_EPS) * g + b


def _rms_norm(x, g):
    return x * lax.rsqrt(jnp.mean(x * x, axis=-1, keepdims=True) + LN_EPS) * g


def _t5_bucket(rel):
    nb = N_BUCKETS // 2
    max_exact = nb // 2
    ret = (rel > 0).astype(np.int32) * nb
    n = np.abs(rel)
    n_safe = np.maximum(n, 1).astype(np.float64)
    large = max_exact + (np.log(n_safe / max_exact) / np.log(MAX_DISTANCE / max_exact)
                         * (nb - max_exact)).astype(np.int32)
    large = np.minimum(large, nb - 1)
    return (ret + np.where(n < max_exact, n, large)).astype(np.int32)


def _inproj_kernel(x_ref, g_ref, b_ref, w_ref, h0_ref, qa_ref, qb1_ref, qb4_ref, qb16_ref, pb_scr):
    h = _layer_norm(x_ref[...], g_ref[...], b_ref[...])
    h0_ref[...] = h
    proj = jnp.dot(h.astype(BF16), w_ref[...], preferred_element_type=F32)
    qa_ref[...] = proj[:, :QKV_A].astype(BF16)
    pb = proj[:, QKV_A:]
    qb1_ref[...] = pb.astype(BF16)
    for c in range(QKV_B // LANES):
        cs = slice(c * LANES, (c + 1) * LANES)
        pb_scr[c] = pb[:, cs]
        for dil, ref in ((4, qb4_ref), (16, qb16_ref)):
            for r in range(dil):
                ref[0, r, :, cs] = pb_scr[c, pl.ds(r, TM_IN // dil, stride=dil), :].astype(BF16)


def _input_projection(x2, ln_g, ln_b, w_in_b, batch):
    T = x2.shape[0]
    tiles_per_seq = SEQ // TM_IN
    row = lambda i: (i, 0)
    const = lambda i: (0, 0)
    deint = lambda i: (i // tiles_per_seq, 0, i % tiles_per_seq, 0)
    return pl.pallas_call(
        _inproj_kernel,
        grid=(T // TM_IN,),
        in_specs=[
            pl.BlockSpec((TM_IN, D_MODEL), row),
            pl.BlockSpec((1, D_MODEL), const),
            pl.BlockSpec((1, D_MODEL), const),
            pl.BlockSpec((D_MODEL, QKV_A + QKV_B), const),
        ],
        out_specs=[
            pl.BlockSpec((TM_IN, D_MODEL), row),
            pl.BlockSpec((TM_IN, QKV_A), row),
            pl.BlockSpec((TM_IN, QKV_B), row),
            pl.BlockSpec((1, 4, TM_IN // 4, QKV_B), deint),
            pl.BlockSpec((1, 16, TM_IN // 16, QKV_B), deint),
        ],
        out_shape=[
            jax.ShapeDtypeStruct((T, D_MODEL), F32),
            jax.ShapeDtypeStruct((T, QKV_A), BF16),
            jax.ShapeDtypeStruct((T, QKV_B), BF16),
            jax.ShapeDtypeStruct((batch, 4, SEQ // 4, QKV_B), BF16),
            jax.ShapeDtypeStruct((batch, 16, SEQ // 16, QKV_B), BF16),
        ],
        scratch_shapes=[pltpu.VMEM((QKV_B // LANES, TM_IN, LANES), F32)],
        compiler_params=_cparams("parallel"),
        name="ln_in_proj",
    )(x2, ln_g, ln_b, w_in_b)


def _band_attn_kernel(*refs, m, nk, n, sub, q_tiles, with_sink):
    if with_sink:
        sink_ref, gain_ref = refs[0], refs[1]
        refs = refs[2:]
    q_ref, k_ref, v_ref, bias_ref, o_ref = refs
    if nk == n:
        start = 0
    else:
        j = pl.program_id(1)
        start = pl.multiple_of(jnp.clip(j * m - (nk - m) // 2, 0, n - nk), HEAD_DIM)
    lane_row = lax.broadcasted_iota(jnp.int32, (1, LANES), 1)
    keep_lo = jnp.where(lane_row < HEAD_DIM, 1.0, 0.0).astype(BF16)
    keep_hi = jnp.where(lane_row < HEAD_DIM, 0.0, 1.0).astype(BF16)
    lane = lax.broadcasted_iota(jnp.int32, (m, LANES), 1)
    units = [(s, t) for s in range(sub) for t in range(len(q_tiles))]

    scores = []
    for s, t in units:
        k_t = k_ref[s, pl.ds(start, nk), t * LANES:(t + 1) * LANES]
        parts = []
        for qt in q_tiles[t]:
            q2 = q_ref[s, :, qt * LANES:(qt + 1) * LANES]
            parts += [q2 * keep_lo, q2 * keep_hi]
        lhs = jnp.concatenate(parts, axis=0)
        sc = lax.dot_general(lhs, k_t, (((1,), (1,)), ((), ())), preferred_element_type=F32)
        scores.append(sc + bias_ref[0, t])

    probs, inv_l, lse = [], [], []
    for (s, t), sc in zip(units, scores):
        mx = jnp.max(sc, axis=-1, keepdims=True)
        if with_sink:
            sink_col = jnp.concatenate(
                [jnp.full((m, 1), sink_ref[2 * qt + h], F32) for qt in q_tiles[t] for h in (0, 1)], axis=0)
            mx = jnp.maximum(mx, sink_col)
        p = jnp.exp(sc - mx)
        l = jnp.sum(p, axis=-1, keepdims=True)
        if with_sink:
            l = l + jnp.exp(sink_col - mx)
        else:
            lse.append(mx + jnp.log(l))
        probs.append(p.astype(BF16))
        inv_l.append(1.0 / l)

    n_q_tiles = sum(len(ts) for ts in q_tiles)
    for s in range(sub):
        pairs = [None] * n_q_tiles
        lse_tile = jnp.zeros((m, LANES), F32)
        for t, tiles in enumerate(q_tiles):
            u = s * len(q_tiles) + t
            v_t = v_ref[s, pl.ds(start, nk), t * LANES:(t + 1) * LANES]
            o = jnp.dot(probs[u], v_t, preferred_element_type=F32) * inv_l[u]
            for i, qt in enumerate(tiles):
                top = o[(2 * i) * m:(2 * i + 1) * m]
                bot = o[(2 * i + 1) * m:(2 * i + 2) * m]
                pairs[qt] = jnp.where(lane < HEAD_DIM, top, bot)
                if not with_sink:
                    lse_tile = jnp.where(lane == 2 * qt, lse[u][(2 * i) * m:(2 * i + 1) * m], lse_tile)
                    lse_tile = jnp.where(lane == 2 * qt + 1, lse[u][(2 * i + 1) * m:(2 * i + 2) * m], lse_tile)
        o_all = jnp.concatenate(pairs, axis=1)
        if with_sink:
            o_ref[s] = _rms_norm(o_all, gain_ref[...]).astype(o_ref.dtype)
        else:
            o_ref[s] = jnp.concatenate([o_all, lse_tile], axis=1)


def _band_window(m, nk, n):
    nb = n // m
    starts = np.clip(np.arange(nb) * m - (nk - m) // 2, 0, n - nk)
    offs = [int(o) for o in starts - np.arange(nb) * m]
    uniq = sorted(set(offs), reverse=True)
    var = [uniq.index(o) for o in offs]
    assert all(v == var[1] for v in var[1:-1])
    return uniq, var


def _band_bias(rel_bias_h, half, dil, m, nk, n, q_tiles):
    uniq, _ = _band_window(m, nk, n)
    n_heads = rel_bias_h.shape[1]
    span = nk + m - 1
    out = []
    for off in uniq:
        rel = off - (m - 1) + np.arange(span)
        onehot = np.zeros((span, N_BUCKETS), np.float32)
        onehot[np.arange(span), _t5_bucket(dil * rel)] = 1.0
        table = jnp.dot(jnp.asarray(onehot), rel_bias_h.astype(F32), precision=lax.Precision.HIGHEST)
        table = jnp.where((np.abs(rel) <= half)[:, None], table, NEG)
        u = jnp.concatenate([table.T, jnp.zeros((n_heads, 1), F32)], axis=1)
        flat = jnp.tile(u, (1, m + 1))[:, m - 1:m - 1 + m * span]
        b = flat.reshape(n_heads, m, span)[:, :, :nk]
        out.append(jnp.stack([jnp.concatenate([b[2 * qt + h] for qt in tiles for h in (0, 1)], axis=0)
                              for tiles in q_tiles]))
    return jnp.stack(out)


def _band_attention(qkv, bias, *, nk, sub, q_tiles, q_width, kv_width, out_width, out_dtype,
                    sink=None, gain=None):
    Bd, n, _ = qkv.shape
    m = TM_ATT
    nb = n // m
    assert nb * m == n and Bd % sub == 0
    _, var = _band_window(m, nk, n)
    kcol = q_width // kv_width
    rows = 2 * m * len(q_tiles[0])
    with_sink = sink is not None

    def variant(b, j):
        v = jnp.where(j == nb - 1, var[-1], var[min(1, nb - 1)])
        return jnp.where(j == 0, var[0], v)

    in_specs = [
        pl.BlockSpec((sub, m, q_width), lambda b, j: (b, j, 0)),
        pl.BlockSpec((sub, n, kv_width), lambda b, j: (b, 0, kcol)),
        pl.BlockSpec((sub, n, kv_width), lambda b, j: (b, 0, kcol + 1)),
        pl.BlockSpec((1, len(q_tiles), rows, nk), lambda b, j: (variant(b, j), 0, 0, 0)),
    ]
    args = [qkv, qkv, qkv, bias]
    if with_sink:
        in_specs = [pl.BlockSpec(memory_space=pltpu.SMEM),
                    pl.BlockSpec((1, q_width), lambda b, j: (0, 0))] + in_specs
        args = [sink, gain] + args
    return pl.pallas_call(
        functools.partial(_band_attn_kernel, m=m, nk=nk, n=n, sub=sub, q_tiles=q_tiles, with_sink=with_sink),
        grid=(Bd // sub, nb),
        in_specs=in_specs,
        out_specs=pl.BlockSpec((sub, m, out_width), lambda b, j: (b, j, 0)),
        out_shape=jax.ShapeDtypeStruct((Bd, n, out_width), out_dtype),
        compiler_params=_cparams("parallel", "arbitrary"),
        name="band_attn_sink" if with_sink else f"band_attn_n{n}",
    )(*args)


def _merge_kernel(ya_ref, o1_ref, o4_ref, o16_ref, h0_ref, gb_ref, w_ref, g1_ref, b1_ref, h1_ref,
                  s4_scr, s16_scr):
    for c in range(B_OUT // LANES):
        cs = slice(c * LANES, (c + 1) * LANES)
        for dil, src, dst in ((4, o4_ref, s4_scr), (16, o16_ref, s16_scr)):
            for r in range(dil):
                dst[c, pl.ds(r, TM_MERGE // dil, stride=dil), :] = src[0, r, :, cs]
    lse_c = B_WIDTH // LANES
    lse = [o1_ref[:, B_WIDTH:], s4_scr[lse_c], s16_scr[lse_c]]
    mx = jnp.maximum(jnp.maximum(lse[0], lse[1]), lse[2])
    ex = [jnp.exp(l - mx) for l in lse]
    inv = 1.0 / (ex[0] + ex[1] + ex[2])
    wts = [e * inv for e in ex]
    lane = lax.broadcasted_iota(jnp.int32, (TM_MERGE, LANES), 1)
    pieces = []
    for c in range(lse_c):
        tiles = (o1_ref[:, c * LANES:(c + 1) * LANES], s4_scr[c], s16_scr[c])
        acc = jnp.zeros((TM_MERGE, LANES), F32)
        for w, o in zip(wts, tiles):
            acc = acc + jnp.where(lane < HEAD_DIM, w[:, 2 * c:2 * c + 1], w[:, 2 * c + 1:2 * c + 2]) * o
        pieces.append(acc)
    yb = _rms_norm(jnp.concatenate(pieces, axis=1), gb_ref[...])
    y = jnp.concatenate([ya_ref[...], yb.astype(BF16)], axis=1)
    mix = jnp.dot(y, w_ref[...], preferred_element_type=F32)
    h1_ref[...] = _layer_norm(ALPHA * h0_ref[...] + mix, g1_ref[...], b1_ref[...])


def _merge_project(ya, o1, o4, o16, h0, gain_b, w_out_b, ln_g, ln_b):
    T = h0.shape[0]
    tiles_per_seq = SEQ // TM_MERGE
    row = lambda i: (i, 0)
    const = lambda i: (0, 0)
    deint = lambda i: (i // tiles_per_seq, 0, i % tiles_per_seq, 0)
    return pl.pallas_call(
        _merge_kernel,
        grid=(T // TM_MERGE,),
        in_specs=[
            pl.BlockSpec((TM_MERGE, A_WIDTH), row),
            pl.BlockSpec((TM_MERGE, B_OUT), row),
            pl.BlockSpec((1, 4, TM_MERGE // 4, B_OUT), deint),
            pl.BlockSpec((1, 16, TM_MERGE // 16, B_OUT), deint),
            pl.BlockSpec((TM_MERGE, D_MODEL), row),
            pl.BlockSpec((1, B_WIDTH), const),
            pl.BlockSpec((D_MODEL, D_MODEL), const),
            pl.BlockSpec((1, D_MODEL), const),
            pl.BlockSpec((1, D_MODEL), const),
        ],
        out_specs=pl.BlockSpec((TM_MERGE, D_MODEL), row),
        out_shape=jax.ShapeDtypeStruct((T, D_MODEL), F32),
        scratch_shapes=[pltpu.VMEM((B_OUT // LANES, TM_MERGE, LANES), F32)] * 2,
        compiler_params=_cparams("parallel"),
        name="merge_out_proj",
    )(ya, o1, o4, o16, h0, gain_b, w_out_b, ln_g, ln_b)


def _mem_kv_kernel(mem_ref, w_ref, k_ref, v_ref):
    kv = jnp.dot(mem_ref[0].astype(BF16), w_ref[...], preferred_element_type=F32)
    k_ref[0] = kv[:, :D_MODEL].astype(BF16)
    v_ref[0] = kv[:, D_MODEL:].astype(BF16)


def _mem_kv(mem, xkv_b):
    B = mem.shape[0]
    blk = pl.BlockSpec((1, MEM_LEN, D_MODEL), lambda b: (b, 0, 0))
    return pl.pallas_call(
        _mem_kv_kernel,
        grid=(B,),
        in_specs=[blk, pl.BlockSpec((D_MODEL, 2 * D_MODEL), lambda b: (0, 0))],
        out_specs=[blk, blk],
        out_shape=[jax.ShapeDtypeStruct((B, MEM_LEN, D_MODEL), BF16)] * 2,
        compiler_params=_cparams("parallel"),
        name="mem_kv_proj",
    )(mem, xkv_b)


def _route(logits):
    rows = logits.shape[0]
    lane = lax.broadcasted_iota(jnp.int32, (rows, LANES), 1).astype(F32)
    big = float(LANES)
    ninf = -jnp.inf
    gl = jnp.where(lane < N_GROUPS, logits, ninf)
    gmax = jnp.max(gl, axis=-1, keepdims=True)
    gidx = jnp.min(jnp.where(gl == gmax, lane, big), axis=-1, keepdims=True)
    g_p = 1.0 / jnp.sum(jnp.exp(gl - gmax), axis=-1, keepdims=True)
    lo_lane = N_GROUPS + EXPERTS_PER_GROUP * gidx
    el = jnp.where((lane >= lo_lane) & (lane < lo_lane + EXPERTS_PER_GROUP), logits, ninf)
    v1 = jnp.max(el, axis=-1, keepdims=True)
    i1 = jnp.min(jnp.where(el == v1, lane, big), axis=-1, keepdims=True)
    el2 = jnp.where(lane == i1, ninf, el)
    v2 = jnp.max(el2, axis=-1, keepdims=True)
    i2 = jnp.min(jnp.where(el2 == v2, lane, big), axis=-1, keepdims=True)
    t = jnp.exp(v2 - v1)
    w1 = g_p / (1.0 + t)
    w2 = g_p * t / (1.0 + t)
    a = jnp.minimum(i1, i2) - lo_lane
    b = jnp.maximum(i1, i2) - lo_lane
    pair = a * (2 * EXPERTS_PER_GROUP - 1 - a) * 0.5 + (b - a - 1.0)
    cls = gidx * PAIRS_PER_GROUP + pair
    w_lo = jnp.where(i1 < i2, w1, w2)
    w_hi = jnp.where(i1 < i2, w2, w1)
    return jnp.where(lane == 0, cls, jnp.where(lane == 1, w_lo, jnp.where(lane == 2, w_hi, 0.0)))


def _xattn_kernel(h1_ref, k_ref, v_ref, wq_ref, wo_ref, g2_ref, b2_ref, wr_hi_ref, wr_lo_ref, br_ref,
                  h2_slab_ref, route_ref):
    h1 = h1_ref[...]
    q = jnp.dot(h1.astype(BF16), wq_ref[...], preferred_element_type=F32).astype(BF16)
    outs = []
    for hd in range(X_HEADS):
        sl = slice(hd * X_HEAD_DIM, (hd + 1) * X_HEAD_DIM)
        s = lax.dot_general(q[:, sl], k_ref[0, :, sl], (((1,), (1,)), ((), ())), preferred_element_type=F32)
        m = jnp.max(s, axis=-1, keepdims=True)
        p = jnp.exp(s - m)
        l = jnp.sum(p, axis=-1, keepdims=True)
        o = jnp.dot(p.astype(BF16), v_ref[0, :, sl], preferred_element_type=F32) / l
        outs.append(o.astype(BF16))
    xa = jnp.dot(jnp.concatenate(outs, axis=1), wo_ref[...], preferred_element_type=F32)
    h2 = _layer_norm(ALPHA * h1 + xa, g2_ref[...], b2_ref[...])
    h_hi = h2.astype(BF16)
    h_lo = (h2 - h_hi.astype(F32)).astype(BF16)
    logits = (jnp.dot(h_hi, wr_hi_ref[...], preferred_element_type=F32)
              + jnp.dot(h_hi, wr_lo_ref[...], preferred_element_type=F32)
              + jnp.dot(h_lo, wr_hi_ref[...], preferred_element_type=F32)) + br_ref[...]
    for c in range(SLAB):
        h2_slab_ref[pl.ds(c, TM_X, stride=SLAB), :] = h2[:, c * LANES:(c + 1) * LANES]
    route_ref[...] = _route(logits)


def _cross_attention_route(h1, k, v, xq_b, xo_b, ln_g, ln_b, wr_hi, wr_lo, br):
    T = h1.shape[0]
    tiles_per_seq = SEQ // TM_X
    row = lambda i: (i, 0)
    const = lambda i: (0, 0)
    kv_blk = pl.BlockSpec((1, MEM_LEN, D_MODEL), lambda i: (i // tiles_per_seq, 0, 0))
    return pl.pallas_call(
        _xattn_kernel,
        grid=(T // TM_X,),
        in_specs=[
            pl.BlockSpec((TM_X, D_MODEL), row), kv_blk, kv_blk,
            pl.BlockSpec((D_MODEL, D_MODEL), const),
            pl.BlockSpec((D_MODEL, D_MODEL), const),
            pl.BlockSpec((1, D_MODEL), const),
            pl.BlockSpec((1, D_MODEL), const),
            pl.BlockSpec((D_MODEL, LANES), const),
            pl.BlockSpec((D_MODEL, LANES), const),
            pl.BlockSpec((1, LANES), const),
        ],
        out_specs=[pl.BlockSpec((TM_X * SLAB, LANES), row), pl.BlockSpec((TM_X, LANES), row)],
        out_shape=[jax.ShapeDtypeStruct((T * SLAB, LANES), F32), jax.ShapeDtypeStruct((T, LANES), F32)],
        compiler_params=_cparams("parallel"),
        name="xattn_ln_route",
    )(h1, k, v, xq_b, xo_b, ln_g, ln_b, wr_hi, wr_lo, br)


def _expert_kernel(src_ref, cnt_ref, elo_ref, ehi_ref, ntile_ref, x_hbm, gate_ref,
                   wg_lo, wu_lo, wd_lo, wg_hi, wu_hi, wd_hi, g3_ref, b3_ref, o_hbm,
                   xbuf, obuf, gsem, ssem):
    j = pl.program_id(0)
    n_tiles = ntile_ref[0]
    slot = j % 2

    def row_copy(t, i, s, gather):
        tok = pl.ds(pl.multiple_of(src_ref[t * TM_MOE + i] * SLAB, SLAB), SLAB)
        row = pl.ds(pl.multiple_of(i * SLAB, SLAB), SLAB)
        if gather:
            return pltpu.make_async_copy(x_hbm.at[tok], xbuf.at[s, row], gsem.at[s])
        return pltpu.make_async_copy(obuf.at[s, row], o_hbm.at[tok], ssem.at[s])

    def start_rows(t, s, gather):
        def body(i, c):
            row_copy(t, i, s, gather).start()
            return c
        lax.fori_loop(0, cnt_ref[t], body, 0)

    def wait_rows(t, s, gather):
        cnt = cnt_ref[t]
        for bit in range(TM_MOE.bit_length()):
            rows = pl.ds(0, (1 << bit) * SLAB)

            @pl.when((cnt >> bit) & 1 == 1)
            def _():
                if gather:
                    pltpu.make_async_copy(x_hbm.at[rows], xbuf.at[s, rows], gsem.at[s]).wait()
                else:
                    pltpu.make_async_copy(obuf.at[s, rows], o_hbm.at[rows], ssem.at[s]).wait()

    @pl.when(j == 0)
    def _():
        xbuf[...] = jnp.zeros_like(xbuf)
        start_rows(0, 0, True)

    @pl.when(j < n_tiles)
    def _():
        @pl.when(j + 1 < n_tiles)
        def _():
            start_rows(j + 1, 1 - slot, True)

        wait_rows(j, slot, True)
        x = jnp.concatenate([xbuf[slot, pl.ds(c, TM_MOE, stride=SLAB), :] for c in range(SLAB)], axis=1)
        xb = x.astype(BF16)
        eye = (lax.broadcasted_iota(jnp.int32, (TM_MOE, TM_MOE), 0)
               == lax.broadcasted_iota(jnp.int32, (TM_MOE, TM_MOE), 1))
        y = jnp.zeros_like(x)
        for e, (wg, wu, wd) in enumerate(((wg_lo, wu_lo, wd_lo), (wg_hi, wu_hi, wd_hi))):
            gate = jnp.sum(jnp.where(eye, gate_ref[0, e:e + 1, :], 0.0), axis=1, keepdims=True)
            a = jnp.dot(xb, wg[0], preferred_element_type=F32)
            u = jnp.dot(xb, wu[0], preferred_element_type=F32)
            hid = a * jax.nn.sigmoid(a) * u
            y = y + jnp.dot((gate * hid).astype(BF16), wd[0], preferred_element_type=F32)
        out = _layer_norm(ALPHA * x + y, g3_ref[...], b3_ref[...])

        @pl.when(j >= 2)
        def _():
            wait_rows(j - 2, slot, False)

        for c in range(SLAB):
            obuf[slot, pl.ds(c, TM_MOE, stride=SLAB), :] = out[:, c * LANES:(c + 1) * LANES]
        start_rows(j, slot, False)

        @pl.when(j == n_tiles - 1)
        def _():
            @pl.when(j >= 1)
            def _():
                wait_rows(j - 1, 1 - slot, False)

            wait_rows(j, slot, False)


def _expert_mlp(src_rows, tile_cnt, tile_elo, tile_ehi, n_tiles, x_slab, gates, wg, wu, wd, ln_g, ln_b):
    n_tiles_max = tile_cnt.shape[0]
    const = lambda j, src, cnt, elo, ehi, nt: (0, 0)
    lo = lambda j, src, cnt, elo, ehi, nt: (elo[j], 0, 0)
    hi = lambda j, src, cnt, elo, ehi, nt: (ehi[j], 0, 0)
    up = (1, D_MODEL, D_EXPERT)
    down = (1, D_EXPERT, D_MODEL)
    any_spec = pl.BlockSpec(memory_space=pl.ANY)
    return pl.pallas_call(
        _expert_kernel,
        grid_spec=pltpu.PrefetchScalarGridSpec(
            num_scalar_prefetch=5,
            grid=(n_tiles_max,),
            in_specs=[
                any_spec,
                pl.BlockSpec((1, 2, TM_MOE), lambda j, src, cnt, elo, ehi, nt: (jnp.minimum(j, nt[0] - 1), 0, 0)),
                pl.BlockSpec(up, lo), pl.BlockSpec(up, lo), pl.BlockSpec(down, lo),
                pl.BlockSpec(up, hi), pl.BlockSpec(up, hi), pl.BlockSpec(down, hi),
                pl.BlockSpec((1, D_MODEL), const),
                pl.BlockSpec((1, D_MODEL), const),
            ],
            out_specs=any_spec,
            scratch_shapes=[
                pltpu.VMEM((2, TM_MOE * SLAB, LANES), F32),
                pltpu.VMEM((2, TM_MOE * SLAB, LANES), F32),
                pltpu.SemaphoreType.DMA((2,)),
                pltpu.SemaphoreType.DMA((2,)),
            ],
        ),
        out_shape=jax.ShapeDtypeStruct(x_slab.shape, F32),
        compiler_params=_cparams("arbitrary"),
        name="moe_experts",
    )(src_rows, tile_cnt, tile_elo, tile_ehi, n_tiles, x_slab, gates, wg, wu, wd, wg, wu, wd, ln_g, ln_b)


def _class_experts():
    lo, hi = [], []
    for g in range(N_GROUPS):
        for a in range(EXPERTS_PER_GROUP):
            for b in range(a + 1, EXPERTS_PER_GROUP):
                lo.append(g * EXPERTS_PER_GROUP + a)
                hi.append(g * EXPERTS_PER_GROUP + b)
    return np.asarray(lo, np.int32), np.asarray(hi, np.int32)


def _moe_plan(cls, w_lo, w_hi, n_tiles_max):
    T = cls.shape[0]
    onehot = (cls[:, None] == jnp.arange(N_CLASSES, dtype=jnp.int32)[None, :]).astype(jnp.int32)
    csum = jnp.cumsum(onehot, axis=0)
    counts = csum[-1]
    rank = jnp.sum((csum - onehot) * onehot, axis=1)
    tiles_per_class = (counts + TM_MOE - 1) // TM_MOE
    tile_end = jnp.cumsum(tiles_per_class)
    tile_start = tile_end - tiles_per_class
    n_tiles = tile_end[-1]
    pos = tile_start[cls] * TM_MOE + rank
    tok = jnp.arange(T, dtype=jnp.int32)
    src_rows = jnp.zeros((n_tiles_max * TM_MOE,), jnp.int32).at[pos].set(tok)
    gates = jnp.stack([w_lo[src_rows], w_hi[src_rows]]).reshape(2, n_tiles_max, TM_MOE).transpose(1, 0, 2)
    tile_ids = jnp.arange(n_tiles_max, dtype=jnp.int32)
    tile_cls = jnp.searchsorted(tile_end, jnp.minimum(tile_ids, n_tiles - 1), side="right").astype(jnp.int32)
    tile_cls = jnp.minimum(tile_cls, N_CLASSES - 1)
    cls_lo, cls_hi = _class_experts()
    tile_elo = jnp.asarray(cls_lo)[tile_cls]
    tile_ehi = jnp.asarray(cls_hi)[tile_cls]
    within = tile_ids - tile_start[tile_cls]
    tile_cnt = jnp.clip(counts[tile_cls] - within * TM_MOE, 0, TM_MOE)
    tile_cnt = jnp.where(tile_ids < n_tiles, tile_cnt, 0).astype(jnp.int32)
    return src_rows, gates, tile_elo, tile_ehi, tile_cnt, n_tiles.reshape(1).astype(jnp.int32)


def _vec(a):
    return a.reshape(1, -1).astype(F32)


def _mixer_and_cross_attention(x, mem, ln_in_g, ln_in_b, w_in, rel_bias, sink_a, norm_a_g, norm_b_g, w_out,
                               ln1_g, ln1_b, xq, xkv, xo, ln2_g, ln2_b, w_group, b_group, w_router, b_router):
    B, S, D = x.shape
    assert S == SEQ and D == D_MODEL and mem.shape[1:] == (MEM_LEN, D_MODEL)
    T = B * S
    vec = _vec

    w = w_in[0]
    edges = np.cumsum((0, A_WIDTH, A_KV_HEADS * HEAD_DIM, A_KV_HEADS * HEAD_DIM, B_WIDTH, B_WIDTH, B_WIDTH))
    qa, ka, va, qb, kb, vb = [w[:, a:b] for a, b in zip(edges[:-1], edges[1:])]
    dup = lambda t: jnp.repeat(t.reshape(D, A_KV_HEADS, 1, HEAD_DIM), 2, axis=2).reshape(D, A_KV_TILES * LANES)
    scale = HEAD_DIM ** -0.5
    w_in_b = jnp.concatenate([qa * scale, dup(ka), dup(va), qb * scale, kb, vb], axis=1).astype(BF16)
    w_out_b = w_out[0].astype(BF16)
    xq_b = (xq[0] * (X_HEAD_DIM ** -0.5)).astype(BF16)
    xkv_b = xkv[0].astype(BF16)
    xo_b = xo[0].astype(BF16)
    wr = jnp.concatenate([w_group[0], w_router[0]], axis=1).astype(F32)
    wr = jnp.pad(wr, ((0, 0), (0, LANES - wr.shape[1])))
    wr_hi = wr.astype(BF16)
    wr_lo = (wr - wr_hi.astype(F32)).astype(BF16)
    br = jnp.concatenate([b_group[0], b_router[0]]).astype(F32)
    br = jnp.pad(br, (0, LANES - br.shape[0])).reshape(1, LANES)

    h0, qkv_a, qkv_b1, qkv_b4, qkv_b16 = _input_projection(
        x.reshape(T, D), vec(ln_in_g), vec(ln_in_b), w_in_b, B)

    tiles_a = ((0, 1), (2, 3))
    nk_a = TM_ATT + 2 * A_HALF_WIN
    bias_a = _band_bias(rel_bias[:, :A_HEADS], A_HALF_WIN, 1, TM_ATT, nk_a, S, tiles_a)
    ya = _band_attention(qkv_a.reshape(B, S, QKV_A), bias_a, nk=nk_a, sub=1, q_tiles=tiles_a,
                         q_width=A_WIDTH, kv_width=A_KV_TILES * LANES, out_width=A_WIDTH, out_dtype=BF16,
                         sink=sink_a[0].astype(F32), gain=vec(norm_a_g[0]))
    tiles_b = ((0,), (1,), (2,), (3,))
    branch_out = []
    for (win, dil), qkv in zip(B_BRANCHES, (qkv_b1.reshape(B, S, QKV_B),
                                             qkv_b4.reshape(B * 4, S // 4, QKV_B),
                                             qkv_b16.reshape(B * 16, S // 16, QKV_B))):
        half = (win // 2) // dil
        n = S // dil
        nk = min(TM_ATT + 2 * half, n)
        bias_b = _band_bias(rel_bias[:, A_HEADS:], half, dil, TM_ATT, nk, n, tiles_b)
        branch_out.append(_band_attention(qkv, bias_b, nk=nk, sub=max(1, TM_ATT * 4 // n), q_tiles=tiles_b,
                                          q_width=B_WIDTH, kv_width=B_WIDTH, out_width=B_OUT, out_dtype=F32))
    o1 = branch_out[0].reshape(T, B_OUT)
    o4 = branch_out[1].reshape(B, 4, S // 4, B_OUT)
    o16 = branch_out[2].reshape(B, 16, S // 16, B_OUT)

    h1 = _merge_project(ya.reshape(T, A_WIDTH), o1, o4, o16, h0, vec(norm_b_g[0]), w_out_b,
                        vec(ln1_g[0]), vec(ln1_b[0]))

    k_mem, v_mem = _mem_kv(mem, xkv_b)
    h2_slab, route = _cross_attention_route(h1, k_mem, v_mem, xq_b, xo_b, vec(ln2_g[0]), vec(ln2_b[0]),
                                            wr_hi, wr_lo, br)
    return h0, h1, h2_slab, route


def _moe(h2_slab, route, w_gate, w_up, w_down, ln3_g, ln3_b):
    T = route.shape[0]
    wg_b = w_gate[0].reshape(N_EXPERTS, D_MODEL, D_EXPERT).astype(BF16)
    wu_b = w_up[0].reshape(N_EXPERTS, D_MODEL, D_EXPERT).astype(BF16)
    wd_b = w_down[0].reshape(N_EXPERTS, D_EXPERT, D_MODEL).astype(BF16)
    n_tiles_max = T // TM_MOE + N_CLASSES
    cls = route[:, 0].astype(jnp.int32)
    src_rows, gates, tile_elo, tile_ehi, tile_cnt, n_tiles = _moe_plan(cls, route[:, 1], route[:, 2], n_tiles_max)
    return _expert_mlp(src_rows, tile_cnt, tile_elo, tile_ehi, n_tiles, h2_slab, gates, wg_b, wu_b, wd_b,
                       _vec(ln3_g[0]), _vec(ln3_b[0]))


def kernel(x, mem, ln_in_g, ln_in_b, w_in, rel_bias, sink_a, norm_a_g, norm_b_g, w_out,
           ln1_g, ln1_b, xq, xkv, xo, ln2_g, ln2_b, w_group, b_group, w_router, b_router,
           w_gate, w_up, w_down, ln3_g, ln3_b):
    _, _, h2_slab, route = _mixer_and_cross_attention(
        x, mem, ln_in_g, ln_in_b, w_in, rel_bias, sink_a, norm_a_g, norm_b_g, w_out,
        ln1_g, ln1_b, xq, xkv, xo, ln2_g, ln2_b, w_group, b_group, w_router, b_router)
    return _moe(h2_slab, route, w_gate, w_up, w_down, ln3_g, ln3_b).reshape(x.shape)
```

```python
import functools

import numpy as np
import jax
import jax.numpy as jnp
from jax import lax
from jax.experimental import pallas as pl
from jax.experimental.pallas import tpu as pltpu

F32 = jnp.float32
BF16 = jnp.bfloat16

D_MODEL = 1024
SEQ = 2048
MEM_LEN = 256
HEAD_DIM = 64
A_HEADS = 8
A_KV_HEADS = 2
A_HALF_WIN = 128
B_HEADS = 8
B_BRANCHES = ((128, 1), (512, 4), (2048, 16))
N_BUCKETS = 32
MAX_DISTANCE = 1024
X_HEADS = 4
X_HEAD_DIM = D_MODEL // X_HEADS
N_GROUPS = 4
EXPERTS_PER_GROUP = 8
N_EXPERTS = N_GROUPS * EXPERTS_PER_GROUP
D_EXPERT = 512
DEPTH = 1
ALPHA = (2.0 * DEPTH) ** 0.25
LN_EPS = 1e-5
NEG = -1e30

LANES = 128
XR_WIDTH = D_MODEL + LANES
A_WIDTH = A_HEADS * HEAD_DIM
B_WIDTH = B_HEADS * HEAD_DIM
A_KV_TILES = A_KV_HEADS
QKV_A = A_WIDTH + 2 * A_KV_TILES * LANES
QKV_B = 3 * B_WIDTH
B_OUT = B_WIDTH + LANES

PAIRS_PER_GROUP = EXPERTS_PER_GROUP * (EXPERTS_PER_GROUP - 1) // 2
N_CLASSES = N_GROUPS * PAIRS_PER_GROUP

TM_IN = 512
TM_ATT = 128
TM_MERGE = 256
TM_X = 256
TM_MOE = 128
VMEM_LIMIT = 56 * 1024 * 1024


def _cparams(*sem):
    return pltpu.CompilerParams(dimension_semantics=sem, vmem_limit_bytes=VMEM_LIMIT)


def _layer_norm(x, g, b):
    mu = jnp.mean(x, axis=-1, keepdims=True)
    xc = x - mu
    var = jnp.mean(xc * xc, axis=-1, keepdims=True)
    return xc * lax.rsqrt(var + LN_EPS) * g + b


def _rms_norm(x, g):
    return x * lax.rsqrt(jnp.mean(x * x, axis=-1, keepdims=True) + LN_EPS) * g


def _t5_bucket(rel):
    nb = N_BUCKETS // 2
    max_exact = nb // 2
    ret = (rel > 0).astype(np.int32) * nb
    n = np.abs(rel)
    n_safe = np.maximum(n, 1).astype(np.float64)
    large = max_exact + (np.log(n_safe / max_exact) / np.log(MAX_DISTANCE / max_exact)
                         * (nb - max_exact)).astype(np.int32)
    large = np.minimum(large, nb - 1)
    return (ret + np.where(n < max_exact, n, large)).astype(np.int32)


def _inproj_kernel(x_ref, g_ref, b_ref, w_ref, h0_ref, qa_ref, qb1_ref, qb4_ref, qb16_ref, pb_scr):
    h = _layer_norm(x_ref[...], g_ref[...], b_ref[...])
    h0_ref[...] = h
    proj = jnp.dot(h.astype(BF16), w_ref[...], preferred_element_type=F32)
    qa_ref[...] = proj[:, :QKV_A].astype(BF16)
    pb = proj[:, QKV_A:]
    qb1_ref[...] = pb.astype(BF16)
    for c in range(QKV_B // LANES):
        cs = slice(c * LANES, (c + 1) * LANES)
        pb_scr[c] = pb[:, cs]
        for dil, ref in ((4, qb4_ref), (16, qb16_ref)):
            for r in range(dil):
                ref[0, r, :, cs] = pb_scr[c, pl.ds(r, TM_IN // dil, stride=dil), :].astype(BF16)


def _input_projection(x2, ln_g, ln_b, w_in_b, batch):
    T = x2.shape[0]
    tiles_per_seq = SEQ // TM_IN
    row = lambda i: (i, 0)
    const = lambda i: (0, 0)
    deint = lambda i: (i // tiles_per_seq, 0, i % tiles_per_seq, 0)
    return pl.pallas_call(
        _inproj_kernel,
        grid=(T // TM_IN,),
        in_specs=[
            pl.BlockSpec((TM_IN, D_MODEL), row),
            pl.BlockSpec((1, D_MODEL), const),
            pl.BlockSpec((1, D_MODEL), const),
            pl.BlockSpec((D_MODEL, QKV_A + QKV_B), const),
        ],
        out_specs=[
            pl.BlockSpec((TM_IN, D_MODEL), row),
            pl.BlockSpec((TM_IN, QKV_A), row),
            pl.BlockSpec((TM_IN, QKV_B), row),
            pl.BlockSpec((1, 4, TM_IN // 4, QKV_B), deint),
            pl.BlockSpec((1, 16, TM_IN // 16, QKV_B), deint),
        ],
        out_shape=[
            jax.ShapeDtypeStruct((T, D_MODEL), F32),
            jax.ShapeDtypeStruct((T, QKV_A), BF16),
            jax.ShapeDtypeStruct((T, QKV_B), BF16),
            jax.ShapeDtypeStruct((batch, 4, SEQ // 4, QKV_B), BF16),
            jax.ShapeDtypeStruct((batch, 16, SEQ // 16, QKV_B), BF16),
        ],
        scratch_shapes=[pltpu.VMEM((QKV_B // LANES, TM_IN, LANES), F32)],
        compiler_params=_cparams("parallel"),
        name="ln_in_proj",
    )(x2, ln_g, ln_b, w_in_b)


def _band_attn_kernel(*refs, m, nk, n, sub, q_tiles, with_sink):
    if with_sink:
        sink_ref, gain_ref = refs[0], refs[1]
        refs = refs[2:]
    q_ref, k_ref, v_ref, bias_ref, o_ref = refs
    if nk == n:
        start = 0
    else:
        j = pl.program_id(1)
        start = pl.multiple_of(jnp.clip(j * m - (nk - m) // 2, 0, n - nk), HEAD_DIM)
    lane_row = lax.broadcasted_iota(jnp.int32, (1, LANES), 1)
    keep_lo = jnp.where(lane_row < HEAD_DIM, 1.0, 0.0).astype(BF16)
    keep_hi = jnp.where(lane_row < HEAD_DIM, 0.0, 1.0).astype(BF16)
    lane = lax.broadcasted_iota(jnp.int32, (m, LANES), 1)
    units = [(s, t) for s in range(sub) for t in range(len(q_tiles))]

    scores = []
    for s, t in units:
        k_t = k_ref[s, pl.ds(start, nk), t * LANES:(t + 1) * LANES]
        parts = []
        for qt in q_tiles[t]:
            q2 = q_ref[s, :, qt * LANES:(qt + 1) * LANES]
            parts += [q2 * keep_lo, q2 * keep_hi]
        lhs = jnp.concatenate(parts, axis=0)
        sc = lax.dot_general(lhs, k_t, (((1,), (1,)), ((), ())), preferred_element_type=F32)
        scores.append(sc + bias_ref[0, t])

    probs, inv_l, lse = [], [], []
    for (s, t), sc in zip(units, scores):
        mx = jnp.max(sc, axis=-1, keepdims=True)
        if with_sink:
            sink_col = jnp.concatenate(
                [jnp.full((m, 1), sink_ref[2 * qt + h], F32) for qt in q_tiles[t] for h in (0, 1)], axis=0)
            mx = jnp.maximum(mx, sink_col)
        p = jnp.exp(sc - mx)
        l = jnp.sum(p, axis=-1, keepdims=True)
        if with_sink:
            l = l + jnp.exp(sink_col - mx)
        else:
            lse.append(mx + jnp.log(l))
        probs.append(p.astype(BF16))
        inv_l.append(1.0 / l)

    n_q_tiles = sum(len(ts) for ts in q_tiles)
    for s in range(sub):
        pairs = [None] * n_q_tiles
        lse_tile = jnp.zeros((m, LANES), F32)
        for t, tiles in enumerate(q_tiles):
            u = s * len(q_tiles) + t
            v_t = v_ref[s, pl.ds(start, nk), t * LANES:(t + 1) * LANES]
            o = jnp.dot(probs[u], v_t, preferred_element_type=F32) * inv_l[u]
            for i, qt in enumerate(tiles):
                top = o[(2 * i) * m:(2 * i + 1) * m]
                bot = o[(2 * i + 1) * m:(2 * i + 2) * m]
                pairs[qt] = jnp.where(lane < HEAD_DIM, top, bot)
                if not with_sink:
                    lse_tile = jnp.where(lane == 2 * qt, lse[u][(2 * i) * m:(2 * i + 1) * m], lse_tile)
                    lse_tile = jnp.where(lane == 2 * qt + 1, lse[u][(2 * i + 1) * m:(2 * i + 2) * m], lse_tile)
        o_all = jnp.concatenate(pairs, axis=1)
        if with_sink:
            o_ref[s] = _rms_norm(o_all, gain_ref[...]).astype(o_ref.dtype)
        else:
            o_ref[s] = jnp.concatenate([o_all, lse_tile], axis=1)


def _band_window(m, nk, n):
    nb = n // m
    starts = np.clip(np.arange(nb) * m - (nk - m) // 2, 0, n - nk)
    offs = [int(o) for o in starts - np.arange(nb) * m]
    uniq = sorted(set(offs), reverse=True)
    var = [uniq.index(o) for o in offs]
    assert all(v == var[1] for v in var[1:-1])
    return uniq, var


def _band_bias(rel_bias_h, half, dil, m, nk, n, q_tiles):
    uniq, _ = _band_window(m, nk, n)
    n_heads = rel_bias_h.shape[1]
    span = nk + m - 1
    out = []
    for off in uniq:
        rel = off - (m - 1) + np.arange(span)
        onehot = np.zeros((span, N_BUCKETS), np.float32)
        onehot[np.arange(span), _t5_bucket(dil * rel)] = 1.0
        table = jnp.dot(jnp.asarray(onehot), rel_bias_h.astype(F32), precision=lax.Precision.HIGHEST)
        table = jnp.where((np.abs(rel) <= half)[:, None], table, NEG)
        u = jnp.concatenate([table.T, jnp.zeros((n_heads, 1), F32)], axis=1)
        flat = jnp.tile(u, (1, m + 1))[:, m - 1:m - 1 + m * span]
        b = flat.reshape(n_heads, m, span)[:, :, :nk]
        out.append(jnp.stack([jnp.concatenate([b[2 * qt + h] for qt in tiles for h in (0, 1)], axis=0)
                              for tiles in q_tiles]))
    return jnp.stack(out)


def _band_attention(qkv, bias, *, nk, sub, q_tiles, q_width, kv_width, out_width, out_dtype,
                    sink=None, gain=None):
    Bd, n, _ = qkv.shape
    m = TM_ATT
    nb = n // m
    assert nb * m == n and Bd % sub == 0
    _, var = _band_window(m, nk, n)
    kcol = q_width // kv_width
    rows = 2 * m * len(q_tiles[0])
    with_sink = sink is not None

    def variant(b, j):
        v = jnp.where(j == nb - 1, var[-1], var[min(1, nb - 1)])
        return jnp.where(j == 0, var[0], v)

    in_specs = [
        pl.BlockSpec((sub, m, q_width), lambda b, j: (b, j, 0)),
        pl.BlockSpec((sub, n, kv_width), lambda b, j: (b, 0, kcol)),
        pl.BlockSpec((sub, n, kv_width), lambda b, j: (b, 0, kcol + 1)),
        pl.BlockSpec((1, len(q_tiles), rows, nk), lambda b, j: (variant(b, j), 0, 0, 0)),
    ]
    args = [qkv, qkv, qkv, bias]
    if with_sink:
        in_specs = [pl.BlockSpec(memory_space=pltpu.SMEM),
                    pl.BlockSpec((1, q_width), lambda b, j: (0, 0))] + in_specs
        args = [sink, gain] + args
    return pl.pallas_call(
        functools.partial(_band_attn_kernel, m=m, nk=nk, n=n, sub=sub, q_tiles=q_tiles, with_sink=with_sink),
        grid=(Bd // sub, nb),
        in_specs=in_specs,
        out_specs=pl.BlockSpec((sub, m, out_width), lambda b, j: (b, j, 0)),
        out_shape=jax.ShapeDtypeStruct((Bd, n, out_width), out_dtype),
        compiler_params=_cparams("parallel", "arbitrary"),
        name="band_attn_sink" if with_sink else f"band_attn_n{n}",
    )(*args)


def _merge_kernel(ya_ref, o1_ref, o4_ref, o16_ref, h0_ref, gb_ref, w_ref, g1_ref, b1_ref, h1_ref,
                  s4_scr, s16_scr):
    for c in range(B_OUT // LANES):
        cs = slice(c * LANES, (c + 1) * LANES)
        for dil, src, dst in ((4, o4_ref, s4_scr), (16, o16_ref, s16_scr)):
            for r in range(dil):
                dst[c, pl.ds(r, TM_MERGE // dil, stride=dil), :] = src[0, r, :, cs]
    lse_c = B_WIDTH // LANES
    lse = [o1_ref[:, B_WIDTH:], s4_scr[lse_c], s16_scr[lse_c]]
    mx = jnp.maximum(jnp.maximum(lse[0], lse[1]), lse[2])
    ex = [jnp.exp(l - mx) for l in lse]
    inv = 1.0 / (ex[0] + ex[1] + ex[2])
    wts = [e * inv for e in ex]
    lane = lax.broadcasted_iota(jnp.int32, (TM_MERGE, LANES), 1)
    pieces = []
    for c in range(lse_c):
        tiles = (o1_ref[:, c * LANES:(c + 1) * LANES], s4_scr[c], s16_scr[c])
        acc = jnp.zeros((TM_MERGE, LANES), F32)
        for w, o in zip(wts, tiles):
            acc = acc + jnp.where(lane < HEAD_DIM, w[:, 2 * c:2 * c + 1], w[:, 2 * c + 1:2 * c + 2]) * o
        pieces.append(acc)
    yb = _rms_norm(jnp.concatenate(pieces, axis=1), gb_ref[...])
    y = jnp.concatenate([ya_ref[...], yb.astype(BF16)], axis=1)
    mix = jnp.dot(y, w_ref[...], preferred_element_type=F32)
    h1_ref[...] = _layer_norm(ALPHA * h0_ref[...] + mix, g1_ref[...], b1_ref[...])


def _merge_project(ya, o1, o4, o16, h0, gain_b, w_out_b, ln_g, ln_b):
    T = h0.shape[0]
    tiles_per_seq = SEQ // TM_MERGE
    row = lambda i: (i, 0)
    const = lambda i: (0, 0)
    deint = lambda i: (i // tiles_per_seq, 0, i % tiles_per_seq, 0)
    return pl.pallas_call(
        _merge_kernel,
        grid=(T // TM_MERGE,),
        in_specs=[
            pl.BlockSpec((TM_MERGE, A_WIDTH), row),
            pl.BlockSpec((TM_MERGE, B_OUT), row),
            pl.BlockSpec((1, 4, TM_MERGE // 4, B_OUT), deint),
            pl.BlockSpec((1, 16, TM_MERGE // 16, B_OUT), deint),
            pl.BlockSpec((TM_MERGE, D_MODEL), row),
            pl.BlockSpec((1, B_WIDTH), const),
            pl.BlockSpec((D_MODEL, D_MODEL), const),
            pl.BlockSpec((1, D_MODEL), const),
            pl.BlockSpec((1, D_MODEL), const),
        ],
        out_specs=pl.BlockSpec((TM_MERGE, D_MODEL), row),
        out_shape=jax.ShapeDtypeStruct((T, D_MODEL), F32),
        scratch_shapes=[pltpu.VMEM((B_OUT // LANES, TM_MERGE, LANES), F32)] * 2,
        compiler_params=_cparams("parallel"),
        name="merge_out_proj",
    )(ya, o1, o4, o16, h0, gain_b, w_out_b, ln_g, ln_b)


def _mem_kv_kernel(mem_ref, w_ref, k_ref, v_ref):
    kv = jnp.dot(mem_ref[0].astype(BF16), w_ref[...], preferred_element_type=F32)
    k_ref[0] = kv[:, :D_MODEL].astype(BF16)
    v_ref[0] = kv[:, D_MODEL:].astype(BF16)


def _mem_kv(mem, xkv_b):
    B = mem.shape[0]
    blk = pl.BlockSpec((1, MEM_LEN, D_MODEL), lambda b: (b, 0, 0))
    return pl.pallas_call(
        _mem_kv_kernel,
        grid=(B,),
        in_specs=[blk, pl.BlockSpec((D_MODEL, 2 * D_MODEL), lambda b: (0, 0))],
        out_specs=[blk, blk],
        out_shape=[jax.ShapeDtypeStruct((B, MEM_LEN, D_MODEL), BF16)] * 2,
        compiler_params=_cparams("parallel"),
        name="mem_kv_proj",
    )(mem, xkv_b)


def _route(logits):
    rows = logits.shape[0]
    lane = lax.broadcasted_iota(jnp.int32, (rows, LANES), 1).astype(F32)
    big = float(LANES)
    ninf = -jnp.inf
    gl = jnp.where(lane < N_GROUPS, logits, ninf)
    gmax = jnp.max(gl, axis=-1, keepdims=True)
    gidx = jnp.min(jnp.where(gl == gmax, lane, big), axis=-1, keepdims=True)
    g_p = 1.0 / jnp.sum(jnp.exp(gl - gmax), axis=-1, keepdims=True)
    lo_lane = N_GROUPS + EXPERTS_PER_GROUP * gidx
    el = jnp.where((lane >= lo_lane) & (lane < lo_lane + EXPERTS_PER_GROUP), logits, ninf)
    v1 = jnp.max(el, axis=-1, keepdims=True)
    i1 = jnp.min(jnp.where(el == v1, lane, big), axis=-1, keepdims=True)
    el2 = jnp.where(lane == i1, ninf, el)
    v2 = jnp.max(el2, axis=-1, keepdims=True)
    i2 = jnp.min(jnp.where(el2 == v2, lane, big), axis=-1, keepdims=True)
    t = jnp.exp(v2 - v1)
    w1 = g_p / (1.0 + t)
    w2 = g_p * t / (1.0 + t)
    a = jnp.minimum(i1, i2) - lo_lane
    b = jnp.maximum(i1, i2) - lo_lane
    pair = a * (2 * EXPERTS_PER_GROUP - 1 - a) * 0.5 + (b - a - 1.0)
    cls = gidx * PAIRS_PER_GROUP + pair
    w_lo = jnp.where(i1 < i2, w1, w2)
    w_hi = jnp.where(i1 < i2, w2, w1)
    return jnp.where(lane == 0, cls, jnp.where(lane == 1, w_lo, jnp.where(lane == 2, w_hi, 0.0)))


def _xattn_kernel(h1_ref, k_ref, v_ref, wq_ref, wo_ref, g2_ref, b2_ref, wr_hi_ref, wr_lo_ref, br_ref, xr_ref):
    h1 = h1_ref[...]
    q = jnp.dot(h1.astype(BF16), wq_ref[...], preferred_element_type=F32).astype(BF16)
    outs = []
    for hd in range(X_HEADS):
        sl = slice(hd * X_HEAD_DIM, (hd + 1) * X_HEAD_DIM)
        s = lax.dot_general(q[:, sl], k_ref[0, :, sl], (((1,), (1,)), ((), ())), preferred_element_type=F32)
        m = jnp.max(s, axis=-1, keepdims=True)
        p = jnp.exp(s - m)
        l = jnp.sum(p, axis=-1, keepdims=True)
        o = jnp.dot(p.astype(BF16), v_ref[0, :, sl], preferred_element_type=F32) / l
        outs.append(o.astype(BF16))
    xa = jnp.dot(jnp.concatenate(outs, axis=1), wo_ref[...], preferred_element_type=F32)
    h2 = _layer_norm(ALPHA * h1 + xa, g2_ref[...], b2_ref[...])
    h_hi = h2.astype(BF16)
    h_lo = (h2 - h_hi.astype(F32)).astype(BF16)
    logits = (jnp.dot(h_hi, wr_hi_ref[...], preferred_element_type=F32)
              + jnp.dot(h_hi, wr_lo_ref[...], preferred_element_type=F32)
              + jnp.dot(h_lo, wr_hi_ref[...], preferred_element_type=F32)) + br_ref[...]
    xr_ref[:, :D_MODEL] = h2
    xr_ref[:, D_MODEL:] = _route(logits)


def _cross_attention_route(h1, k, v, xq_b, xo_b, ln_g, ln_b, wr_hi, wr_lo, br):
    T = h1.shape[0]
    tiles_per_seq = SEQ // TM_X
    row = lambda i: (i, 0)
    const = lambda i: (0, 0)
    kv_blk = pl.BlockSpec((1, MEM_LEN, D_MODEL), lambda i: (i // tiles_per_seq, 0, 0))
    return pl.pallas_call(
        _xattn_kernel,
        grid=(T // TM_X,),
        in_specs=[
            pl.BlockSpec((TM_X, D_MODEL), row), kv_blk, kv_blk,
            pl.BlockSpec((D_MODEL, D_MODEL), const),
            pl.BlockSpec((D_MODEL, D_MODEL), const),
            pl.BlockSpec((1, D_MODEL), const),
            pl.BlockSpec((1, D_MODEL), const),
            pl.BlockSpec((D_MODEL, LANES), const),
            pl.BlockSpec((D_MODEL, LANES), const),
            pl.BlockSpec((1, LANES), const),
        ],
        out_specs=pl.BlockSpec((TM_X, XR_WIDTH), row),
        out_shape=jax.ShapeDtypeStruct((T, XR_WIDTH), F32),
        compiler_params=_cparams("parallel"),
        name="xattn_ln_route",
    )(h1, k, v, xq_b, xo_b, ln_g, ln_b, wr_hi, wr_lo, br)


def _expert_kernel(src_ref, cnt_ref, elo_ref, ehi_ref, ntile_ref, x_hbm,
                   wg_lo, wu_lo, wd_lo, wg_hi, wu_hi, wd_hi, g3_ref, b3_ref, o_hbm,
                   xbuf, obuf, gsem, ssem):
    j = pl.program_id(0)
    n_tiles = ntile_ref[0]
    slot = j % 2

    def row_copy(t, i, s, gather):
        tok = pl.ds(src_ref[t * TM_MOE + i], 1)
        row = pl.ds(i, 1)
        if gather:
            return pltpu.make_async_copy(x_hbm.at[tok], xbuf.at[s, row], gsem.at[s])
        return pltpu.make_async_copy(obuf.at[s, row], o_hbm.at[tok], ssem.at[s])

    def start_rows(t, s, gather):
        def body(i, c):
            row_copy(t, i, s, gather).start()
            return c
        lax.fori_loop(0, cnt_ref[t], body, 0)

    def wait_rows(t, s, gather):
        cnt = cnt_ref[t]
        for bit in range(TM_MOE.bit_length()):
            rows = pl.ds(0, 1 << bit)

            @pl.when((cnt >> bit) & 1 == 1)
            def _():
                if gather:
                    pltpu.make_async_copy(x_hbm.at[rows], xbuf.at[s, rows], gsem.at[s]).wait()
                else:
                    pltpu.make_async_copy(obuf.at[s, rows], o_hbm.at[rows], ssem.at[s]).wait()

    @pl.when(j == 0)
    def _():
        xbuf[...] = jnp.zeros_like(xbuf)
        start_rows(0, 0, True)

    @pl.when(j < n_tiles)
    def _():
        @pl.when(j + 1 < n_tiles)
        def _():
            start_rows(j + 1, 1 - slot, True)

        wait_rows(j, slot, True)
        x = xbuf[slot, :, :D_MODEL]
        xb = x.astype(BF16)
        y = jnp.zeros_like(x)
        for e, (wg, wu, wd) in enumerate(((wg_lo, wu_lo, wd_lo), (wg_hi, wu_hi, wd_hi))):
            gate = xbuf[slot, :, D_MODEL + 1 + e:D_MODEL + 2 + e]
            a = jnp.dot(xb, wg[0], preferred_element_type=F32)
            u = jnp.dot(xb, wu[0], preferred_element_type=F32)
            hid = a * jax.nn.sigmoid(a) * u
            y = y + jnp.dot((gate * hid).astype(BF16), wd[0], preferred_element_type=F32)
        out = _layer_norm(ALPHA * x + y, g3_ref[...], b3_ref[...])

        @pl.when(j >= 2)
        def _():
            wait_rows(j - 2, slot, False)

        obuf[slot] = out
        start_rows(j, slot, False)

        @pl.when(j == n_tiles - 1)
        def _():
            @pl.when(j >= 1)
            def _():
                wait_rows(j - 1, 1 - slot, False)

            wait_rows(j, slot, False)


def _expert_mlp(src_rows, tile_cnt, tile_elo, tile_ehi, n_tiles, xr, wg, wu, wd, ln_g, ln_b):
    n_tiles_max = tile_cnt.shape[0]
    const = lambda j, src, cnt, elo, ehi, nt: (0, 0)
    lo = lambda j, src, cnt, elo, ehi, nt: (elo[j], 0, 0)
    hi = lambda j, src, cnt, elo, ehi, nt: (ehi[j], 0, 0)
    up = (1, D_MODEL, D_EXPERT)
    down = (1, D_EXPERT, D_MODEL)
    any_spec = pl.BlockSpec(memory_space=pl.ANY)
    return pl.pallas_call(
        _expert_kernel,
        grid_spec=pltpu.PrefetchScalarGridSpec(
            num_scalar_prefetch=5,
            grid=(n_tiles_max,),
            in_specs=[
                any_spec,
                pl.BlockSpec(up, lo), pl.BlockSpec(up, lo), pl.BlockSpec(down, lo),
                pl.BlockSpec(up, hi), pl.BlockSpec(up, hi), pl.BlockSpec(down, hi),
                pl.BlockSpec((1, D_MODEL), const),
                pl.BlockSpec((1, D_MODEL), const),
            ],
            out_specs=any_spec,
            scratch_shapes=[
                pltpu.VMEM((2, TM_MOE, XR_WIDTH), F32),
                pltpu.VMEM((2, TM_MOE, D_MODEL), F32),
                pltpu.SemaphoreType.DMA((2,)),
                pltpu.SemaphoreType.DMA((2,)),
            ],
        ),
        out_shape=jax.ShapeDtypeStruct((xr.shape[0], D_MODEL), F32),
        compiler_params=_cparams("arbitrary"),
        name="moe_experts",
    )(src_rows, tile_cnt, tile_elo, tile_ehi, n_tiles, xr, wg, wu, wd, wg, wu, wd, ln_g, ln_b)


def _class_experts():
    lo, hi = [], []
    for g in range(N_GROUPS):
        for a in range(EXPERTS_PER_GROUP):
            for b in range(a + 1, EXPERTS_PER_GROUP):
                lo.append(g * EXPERTS_PER_GROUP + a)
                hi.append(g * EXPERTS_PER_GROUP + b)
    return np.asarray(lo, np.int32), np.asarray(hi, np.int32)


def _moe_plan(cls, n_tiles_max):
    T = cls.shape[0]
    onehot = (cls[:, None] == jnp.arange(N_CLASSES, dtype=jnp.int32)[None, :]).astype(jnp.int32)
    csum = jnp.cumsum(onehot, axis=0)
    counts = csum[-1]
    rank = jnp.sum((csum - onehot) * onehot, axis=1)
    tiles_per_class = (counts + TM_MOE - 1) // TM_MOE
    tile_end = jnp.cumsum(tiles_per_class)
    tile_start = tile_end - tiles_per_class
    n_tiles = tile_end[-1]
    pos = tile_start[cls] * TM_MOE + rank
    tok = jnp.arange(T, dtype=jnp.int32)
    src_rows = jnp.zeros((n_tiles_max * TM_MOE,), jnp.int32).at[pos].set(tok)
    tile_ids = jnp.arange(n_tiles_max, dtype=jnp.int32)
    tile_cls = jnp.searchsorted(tile_end, jnp.minimum(tile_ids, n_tiles - 1), side="right").astype(jnp.int32)
    tile_cls = jnp.minimum(tile_cls, N_CLASSES - 1)
    cls_lo, cls_hi = _class_experts()
    tile_elo = jnp.asarray(cls_lo)[tile_cls]
    tile_ehi = jnp.asarray(cls_hi)[tile_cls]
    within = tile_ids - tile_start[tile_cls]
    tile_cnt = jnp.clip(counts[tile_cls] - within * TM_MOE, 0, TM_MOE)
    tile_cnt = jnp.where(tile_ids < n_tiles, tile_cnt, 0).astype(jnp.int32)
    return src_rows, tile_elo, tile_ehi, tile_cnt, n_tiles.reshape(1).astype(jnp.int32)


def _vec(a):
    return a.reshape(1, -1).astype(F32)


def _mixer_and_cross_attention(x, mem, ln_in_g, ln_in_b, w_in, rel_bias, sink_a, norm_a_g, norm_b_g, w_out,
                               ln1_g, ln1_b, xq, xkv, xo, ln2_g, ln2_b, w_group, b_group, w_router, b_router):
    B, S, D = x.shape
    assert S == SEQ and D == D_MODEL and mem.shape[1:] == (MEM_LEN, D_MODEL)
    T = B * S
    vec = _vec

    w = w_in[0]
    edges = np.cumsum((0, A_WIDTH, A_KV_HEADS * HEAD_DIM, A_KV_HEADS * HEAD_DIM, B_WIDTH, B_WIDTH, B_WIDTH))
    qa, ka, va, qb, kb, vb = [w[:, a:b] for a, b in zip(edges[:-1], edges[1:])]
    dup = lambda t: jnp.repeat(t.reshape(D, A_KV_HEADS, 1, HEAD_DIM), 2, axis=2).reshape(D, A_KV_TILES * LANES)
    scale = HEAD_DIM ** -0.5
    w_in_b = jnp.concatenate([qa * scale, dup(ka), dup(va), qb * scale, kb, vb], axis=1).astype(BF16)
    w_out_b = w_out[0].astype(BF16)
    xq_b = (xq[0] * (X_HEAD_DIM ** -0.5)).astype(BF16)
    xkv_b = xkv[0].astype(BF16)
    xo_b = xo[0].astype(BF16)
    wr = jnp.concatenate([w_group[0], w_router[0]], axis=1).astype(F32)
    wr = jnp.pad(wr, ((0, 0), (0, LANES - wr.shape[1])))
    wr_hi = wr.astype(BF16)
    wr_lo = (wr - wr_hi.astype(F32)).astype(BF16)
    br = jnp.concatenate([b_group[0], b_router[0]]).astype(F32)
    br = jnp.pad(br, (0, LANES - br.shape[0])).reshape(1, LANES)

    h0, qkv_a, qkv_b1, qkv_b4, qkv_b16 = _input_projection(
        x.reshape(T, D), vec(ln_in_g), vec(ln_in_b), w_in_b, B)

    tiles_a = ((0, 1), (2, 3))
    nk_a = TM_ATT + 2 * A_HALF_WIN
    bias_a = _band_bias(rel_bias[:, :A_HEADS], A_HALF_WIN, 1, TM_ATT, nk_a, S, tiles_a)
    ya = _band_attention(qkv_a.reshape(B, S, QKV_A), bias_a, nk=nk_a, sub=1, q_tiles=tiles_a,
                         q_width=A_WIDTH, kv_width=A_KV_TILES * LANES, out_width=A_WIDTH, out_dtype=BF16,
                         sink=sink_a[0].astype(F32), gain=vec(norm_a_g[0]))
    tiles_b = ((0,), (1,), (2,), (3,))
    branch_out = []
    for (win, dil), qkv in zip(B_BRANCHES, (qkv_b1.reshape(B, S, QKV_B),
                                             qkv_b4.reshape(B * 4, S // 4, QKV_B),
                                             qkv_b16.reshape(B * 16, S // 16, QKV_B))):
        half = (win // 2) // dil
        n = S // dil
        nk = min(TM_ATT + 2 * half, n)
        bias_b = _band_bias(rel_bias[:, A_HEADS:], half, dil, TM_ATT, nk, n, tiles_b)
        branch_out.append(_band_attention(qkv, bias_b, nk=nk, sub=max(1, TM_ATT * 4 // n), q_tiles=tiles_b,
                                          q_width=B_WIDTH, kv_width=B_WIDTH, out_width=B_OUT, out_dtype=F32))
    o1 = branch_out[0].reshape(T, B_OUT)
    o4 = branch_out[1].reshape(B, 4, S // 4, B_OUT)
    o16 = branch_out[2].reshape(B, 16, S // 16, B_OUT)

    h1 = _merge_project(ya.reshape(T, A_WIDTH), o1, o4, o16, h0, vec(norm_b_g[0]), w_out_b,
                        vec(ln1_g[0]), vec(ln1_b[0]))

    k_mem, v_mem = _mem_kv(mem, xkv_b)
    xr = _cross_attention_route(h1, k_mem, v_mem, xq_b, xo_b, vec(ln2_g[0]), vec(ln2_b[0]), wr_hi, wr_lo, br)
    return h0, h1, xr


def _moe(xr, w_gate, w_up, w_down, ln3_g, ln3_b):
    T = xr.shape[0]
    wg_b = w_gate[0].reshape(N_EXPERTS, D_MODEL, D_EXPERT).astype(BF16)
    wu_b = w_up[0].reshape(N_EXPERTS, D_MODEL, D_EXPERT).astype(BF16)
    wd_b = w_down[0].reshape(N_EXPERTS, D_EXPERT, D_MODEL).astype(BF16)
    n_tiles_max = T // TM_MOE + N_CLASSES
    cls = xr[:, D_MODEL].astype(jnp.int32)
    src_rows, tile_elo, tile_ehi, tile_cnt, n_tiles = _moe_plan(cls, n_tiles_max)
    return _expert_mlp(src_rows, tile_cnt, tile_elo, tile_ehi, n_tiles, xr, wg_b, wu_b, wd_b,
                       _vec(ln3_g[0]), _vec(ln3_b[0]))


def kernel(x, mem, ln_in_g, ln_in_b, w_in, rel_bias, sink_a, norm_a_g, norm_b_g, w_out,
           ln1_g, ln1_b, xq, xkv, xo, ln2_g, ln2_b, w_group, b_group, w_router, b_router,
           w_gate, w_up, w_down, ln3_g, ln3_b):
    _, _, xr = _mixer_and_cross_attention(
        x, mem, ln_in_g, ln_in_b, w_in, rel_bias, sink_a, norm_a_g, norm_b_g, w_out,
        ln1_g, ln1_b, xq, xkv, xo, ln2_g, ln2_b, w_group, b_group, w_router, b_router)
    return _moe(xr, w_gate, w_up, w_down, ln3_g, ln3_b).reshape(x.shape)
```

```python
import functools

import numpy as np
import jax
import jax.numpy as jnp
from jax import lax
from jax.experimental import pallas as pl
from jax.experimental.pallas import tpu as pltpu

F32 = jnp.float32
BF16 = jnp.bfloat16

D_MODEL = 1024
SEQ = 2048
MEM_LEN = 256
HEAD_DIM = 64
A_HEADS = 8
A_KV_HEADS = 2
A_HALF_WIN = 128
B_HEADS = 8
B_BRANCHES = ((128, 1), (512, 4), (2048, 16))
N_BUCKETS = 32
MAX_DISTANCE = 1024
X_HEADS = 4
X_HEAD_DIM = D_MODEL // X_HEADS
N_GROUPS = 4
EXPERTS_PER_GROUP = 8
N_EXPERTS = N_GROUPS * EXPERTS_PER_GROUP
D_EXPERT = 512
DEPTH = 1
ALPHA = (2.0 * DEPTH) ** 0.25
LN_EPS = 1e-5
NEG = -1e30

LANES = 128
XR_WIDTH = D_MODEL + LANES
A_WIDTH = A_HEADS * HEAD_DIM
B_WIDTH = B_HEADS * HEAD_DIM
A_KV_TILES = A_KV_HEADS
QKV_A = A_WIDTH + 2 * A_KV_TILES * LANES
QKV_B = 3 * B_WIDTH
B_OUT = B_WIDTH + LANES

PAIRS_PER_GROUP = EXPERTS_PER_GROUP * (EXPERTS_PER_GROUP - 1) // 2
N_CLASSES = N_GROUPS * PAIRS_PER_GROUP

TM_IN = 512
TM_ATT = 128
TM_MERGE = 256
TM_X = 256
TM_MOE = 128
ROW_GROUP = 8
TM_PLAN = 1024
VMEM_LIMIT = 56 * 1024 * 1024


def _cparams(*sem):
    return pltpu.CompilerParams(dimension_semantics=sem, vmem_limit_bytes=VMEM_LIMIT)


def _layer_norm(x, g, b):
    mu = jnp.mean(x, axis=-1, keepdims=True)
    xc = x - mu
    var = jnp.mean(xc * xc, axis=-1, keepdims=True)
    return xc * lax.rsqrt(var + LN_EPS) * g + b


def _rms_norm(x, g):
    return x * lax.rsqrt(jnp.mean(x * x, axis=-1, keepdims=True) + LN_EPS) * g


def _t5_bucket(rel):
    nb = N_BUCKETS // 2
    max_exact = nb // 2
    ret = (rel > 0).astype(np.int32) * nb
    n = np.abs(rel)
    n_safe = np.maximum(n, 1).astype(np.float64)
    large = max_exact + (np.log(n_safe / max_exact) / np.log(MAX_DISTANCE / max_exact)
                         * (nb - max_exact)).astype(np.int32)
    large = np.minimum(large, nb - 1)
    return (ret + np.where(n < max_exact, n, large)).astype(np.int32)


def _inproj_kernel(x_ref, g_ref, b_ref, w_ref, h0_ref, qa_ref, qb1_ref, qb4_ref, qb16_ref, pb_scr):
    h = _layer_norm(x_ref[...], g_ref[...], b_ref[...])
    h0_ref[...] = h
    proj = jnp.dot(h.astype(BF16), w_ref[...], preferred_element_type=F32)
    qa_ref[...] = proj[:, :QKV_A].astype(BF16)
    pb = proj[:, QKV_A:]
    qb1_ref[...] = pb.astype(BF16)
    for c in range(QKV_B // LANES):
        cs = slice(c * LANES, (c + 1) * LANES)
        pb_scr[c] = pb[:, cs]
        for dil, ref in ((4, qb4_ref), (16, qb16_ref)):
            for r in range(dil):
                ref[0, r, :, cs] = pb_scr[c, pl.ds(r, TM_IN // dil, stride=dil), :].astype(BF16)


def _input_projection(x2, ln_g, ln_b, w_in_b, batch):
    T = x2.shape[0]
    tiles_per_seq = SEQ // TM_IN
    row = lambda i: (i, 0)
    const = lambda i: (0, 0)
    deint = lambda i: (i // tiles_per_seq, 0, i % tiles_per_seq, 0)
    return pl.pallas_call(
        _inproj_kernel,
        grid=(T // TM_IN,),
        in_specs=[
            pl.BlockSpec((TM_IN, D_MODEL), row),
            pl.BlockSpec((1, D_MODEL), const),
            pl.BlockSpec((1, D_MODEL), const),
            pl.BlockSpec((D_MODEL, QKV_A + QKV_B), const),
        ],
        out_specs=[
            pl.BlockSpec((TM_IN, D_MODEL), row),
            pl.BlockSpec((TM_IN, QKV_A), row),
            pl.BlockSpec((TM_IN, QKV_B), row),
            pl.BlockSpec((1, 4, TM_IN // 4, QKV_B), deint),
            pl.BlockSpec((1, 16, TM_IN // 16, QKV_B), deint),
        ],
        out_shape=[
            jax.ShapeDtypeStruct((T, D_MODEL), F32),
            jax.ShapeDtypeStruct((T, QKV_A), BF16),
            jax.ShapeDtypeStruct((T, QKV_B), BF16),
            jax.ShapeDtypeStruct((batch, 4, SEQ // 4, QKV_B), BF16),
            jax.ShapeDtypeStruct((batch, 16, SEQ // 16, QKV_B), BF16),
        ],
        scratch_shapes=[pltpu.VMEM((QKV_B // LANES, TM_IN, LANES), F32)],
        compiler_params=_cparams("parallel"),
        name="ln_in_proj",
    )(x2, ln_g, ln_b, w_in_b)


def _band_attn_kernel(*refs, m, nk, n, sub, q_tiles, with_sink):
    if with_sink:
        sink_ref, gain_ref = refs[0], refs[1]
        refs = refs[2:]
    q_ref, k_ref, v_ref, bias_ref, o_ref = refs
    if nk == n:
        start = 0
    else:
        j = pl.program_id(1)
        start = pl.multiple_of(jnp.clip(j * m - (nk - m) // 2, 0, n - nk), HEAD_DIM)
    lane_row = lax.broadcasted_iota(jnp.int32, (1, LANES), 1)
    keep_lo = jnp.where(lane_row < HEAD_DIM, 1.0, 0.0).astype(BF16)
    keep_hi = jnp.where(lane_row < HEAD_DIM, 0.0, 1.0).astype(BF16)
    lane = lax.broadcasted_iota(jnp.int32, (m, LANES), 1)
    units = [(s, t) for s in range(sub) for t in range(len(q_tiles))]

    scores = []
    for s, t in units:
        k_t = k_ref[s, pl.ds(start, nk), t * LANES:(t + 1) * LANES]
        parts = []
        for qt in q_tiles[t]:
            q2 = q_ref[s, :, qt * LANES:(qt + 1) * LANES]
            parts += [q2 * keep_lo, q2 * keep_hi]
        lhs = jnp.concatenate(parts, axis=0)
        sc = lax.dot_general(lhs, k_t, (((1,), (1,)), ((), ())), preferred_element_type=F32)
        scores.append(sc + bias_ref[0, t])

    probs, inv_l, lse = [], [], []
    for (s, t), sc in zip(units, scores):
        mx = jnp.max(sc, axis=-1, keepdims=True)
        if with_sink:
            sink_col = jnp.concatenate(
                [jnp.full((m, 1), sink_ref[2 * qt + h], F32) for qt in q_tiles[t] for h in (0, 1)], axis=0)
            mx = jnp.maximum(mx, sink_col)
        p = jnp.exp(sc - mx)
        l = jnp.sum(p, axis=-1, keepdims=True)
        if with_sink:
            l = l + jnp.exp(sink_col - mx)
        else:
            lse.append(mx + jnp.log(l))
        probs.append(p.astype(BF16))
        inv_l.append(1.0 / l)

    n_q_tiles = sum(len(ts) for ts in q_tiles)
    for s in range(sub):
        pairs = [None] * n_q_tiles
        lse_tile = jnp.zeros((m, LANES), F32)
        for t, tiles in enumerate(q_tiles):
            u = s * len(q_tiles) + t
            v_t = v_ref[s, pl.ds(start, nk), t * LANES:(t + 1) * LANES]
            o = jnp.dot(probs[u], v_t, preferred_element_type=F32) * inv_l[u]
            for i, qt in enumerate(tiles):
                top = o[(2 * i) * m:(2 * i + 1) * m]
                bot = o[(2 * i + 1) * m:(2 * i + 2) * m]
                pairs[qt] = jnp.where(lane < HEAD_DIM, top, bot)
                if not with_sink:
                    lse_tile = jnp.where(lane == 2 * qt, lse[u][(2 * i) * m:(2 * i + 1) * m], lse_tile)
                    lse_tile = jnp.where(lane == 2 * qt + 1, lse[u][(2 * i + 1) * m:(2 * i + 2) * m], lse_tile)
        o_all = jnp.concatenate(pairs, axis=1)
        if with_sink:
            o_ref[s] = _rms_norm(o_all, gain_ref[...]).astype(o_ref.dtype)
        else:
            o_ref[s] = jnp.concatenate([o_all, lse_tile], axis=1)


def _band_window(m, nk, n):
    nb = n // m
    starts = np.clip(np.arange(nb) * m - (nk - m) // 2, 0, n - nk)
    offs = [int(o) for o in starts - np.arange(nb) * m]
    uniq = sorted(set(offs), reverse=True)
    var = [uniq.index(o) for o in offs]
    assert all(v == var[1] for v in var[1:-1])
    return uniq, var


def _band_bias(rel_bias_h, half, dil, m, nk, n, q_tiles):
    uniq, _ = _band_window(m, nk, n)
    n_heads = rel_bias_h.shape[1]
    span = nk + m - 1
    out = []
    for off in uniq:
        rel = off - (m - 1) + np.arange(span)
        onehot = np.zeros((span, N_BUCKETS), np.float32)
        onehot[np.arange(span), _t5_bucket(dil * rel)] = 1.0
        table = jnp.dot(jnp.asarray(onehot), rel_bias_h.astype(F32), precision=lax.Precision.HIGHEST)
        table = jnp.where((np.abs(rel) <= half)[:, None], table, NEG)
        u = jnp.concatenate([table.T, jnp.zeros((n_heads, 1), F32)], axis=1)
        flat = jnp.tile(u, (1, m + 1))[:, m - 1:m - 1 + m * span]
        b = flat.reshape(n_heads, m, span)[:, :, :nk]
        out.append(jnp.stack([jnp.concatenate([b[2 * qt + h] for qt in tiles for h in (0, 1)], axis=0)
                              for tiles in q_tiles]))
    return jnp.stack(out)


def _band_attention(qkv, bias, *, nk, sub, q_tiles, q_width, kv_width, out_width, out_dtype,
                    sink=None, gain=None):
    Bd, n, _ = qkv.shape
    m = TM_ATT
    nb = n // m
    assert nb * m == n and Bd % sub == 0
    _, var = _band_window(m, nk, n)
    kcol = q_width // kv_width
    rows = 2 * m * len(q_tiles[0])
    with_sink = sink is not None

    def variant(b, j):
        v = jnp.where(j == nb - 1, var[-1], var[min(1, nb - 1)])
        return jnp.where(j == 0, var[0], v)

    in_specs = [
        pl.BlockSpec((sub, m, q_width), lambda b, j: (b, j, 0)),
        pl.BlockSpec((sub, n, kv_width), lambda b, j: (b, 0, kcol)),
        pl.BlockSpec((sub, n, kv_width), lambda b, j: (b, 0, kcol + 1)),
        pl.BlockSpec((1, len(q_tiles), rows, nk), lambda b, j: (variant(b, j), 0, 0, 0)),
    ]
    args = [qkv, qkv, qkv, bias]
    if with_sink:
        in_specs = [pl.BlockSpec(memory_space=pltpu.SMEM),
                    pl.BlockSpec((1, q_width), lambda b, j: (0, 0))] + in_specs
        args = [sink, gain] + args
    return pl.pallas_call(
        functools.partial(_band_attn_kernel, m=m, nk=nk, n=n, sub=sub, q_tiles=q_tiles, with_sink=with_sink),
        grid=(Bd // sub, nb),
        in_specs=in_specs,
        out_specs=pl.BlockSpec((sub, m, out_width), lambda b, j: (b, j, 0)),
        out_shape=jax.ShapeDtypeStruct((Bd, n, out_width), out_dtype),
        compiler_params=_cparams("parallel", "arbitrary"),
        name="band_attn_sink" if with_sink else f"band_attn_n{n}",
    )(*args)


def _merge_kernel(ya_ref, o1_ref, o4_ref, o16_ref, h0_ref, gb_ref, w_ref, g1_ref, b1_ref, h1_ref,
                  s4_scr, s16_scr):
    for c in range(B_OUT // LANES):
        cs = slice(c * LANES, (c + 1) * LANES)
        for dil, src, dst in ((4, o4_ref, s4_scr), (16, o16_ref, s16_scr)):
            for r in range(dil):
                dst[c, pl.ds(r, TM_MERGE // dil, stride=dil), :] = src[0, r, :, cs]
    lse_c = B_WIDTH // LANES
    lse = [o1_ref[:, B_WIDTH:], s4_scr[lse_c], s16_scr[lse_c]]
    mx = jnp.maximum(jnp.maximum(lse[0], lse[1]), lse[2])
    ex = [jnp.exp(l - mx) for l in lse]
    inv = 1.0 / (ex[0] + ex[1] + ex[2])
    wts = [e * inv for e in ex]
    lane = lax.broadcasted_iota(jnp.int32, (TM_MERGE, LANES), 1)
    pieces = []
    for c in range(lse_c):
        tiles = (o1_ref[:, c * LANES:(c + 1) * LANES], s4_scr[c], s16_scr[c])
        acc = jnp.zeros((TM_MERGE, LANES), F32)
        for w, o in zip(wts, tiles):
            acc = acc + jnp.where(lane < HEAD_DIM, w[:, 2 * c:2 * c + 1], w[:, 2 * c + 1:2 * c + 2]) * o
        pieces.append(acc)
    yb = _rms_norm(jnp.concatenate(pieces, axis=1), gb_ref[...])
    y = jnp.concatenate([ya_ref[...], yb.astype(BF16)], axis=1)
    mix = jnp.dot(y, w_ref[...], preferred_element_type=F32)
    h1_ref[...] = _layer_norm(ALPHA * h0_ref[...] + mix, g1_ref[...], b1_ref[...])


def _merge_project(ya, o1, o4, o16, h0, gain_b, w_out_b, ln_g, ln_b):
    T = h0.shape[0]
    tiles_per_seq = SEQ // TM_MERGE
    row = lambda i: (i, 0)
    const = lambda i: (0, 0)
    deint = lambda i: (i // tiles_per_seq, 0, i % tiles_per_seq, 0)
    return pl.pallas_call(
        _merge_kernel,
        grid=(T // TM_MERGE,),
        in_specs=[
            pl.BlockSpec((TM_MERGE, A_WIDTH), row),
            pl.BlockSpec((TM_MERGE, B_OUT), row),
            pl.BlockSpec((1, 4, TM_MERGE // 4, B_OUT), deint),
            pl.BlockSpec((1, 16, TM_MERGE // 16, B_OUT), deint),
            pl.BlockSpec((TM_MERGE, D_MODEL), row),
            pl.BlockSpec((1, B_WIDTH), const),
            pl.BlockSpec((D_MODEL, D_MODEL), const),
            pl.BlockSpec((1, D_MODEL), const),
            pl.BlockSpec((1, D_MODEL), const),
        ],
        out_specs=pl.BlockSpec((TM_MERGE, D_MODEL), row),
        out_shape=jax.ShapeDtypeStruct((T, D_MODEL), F32),
        scratch_shapes=[pltpu.VMEM((B_OUT // LANES, TM_MERGE, LANES), F32)] * 2,
        compiler_params=_cparams("parallel"),
        name="merge_out_proj",
    )(ya, o1, o4, o16, h0, gain_b, w_out_b, ln_g, ln_b)


def _mem_kv_kernel(mem_ref, w_ref, k_ref, v_ref):
    kv = jnp.dot(mem_ref[0].astype(BF16), w_ref[...], preferred_element_type=F32)
    k_ref[0] = kv[:, :D_MODEL].astype(BF16)
    v_ref[0] = kv[:, D_MODEL:].astype(BF16)


def _mem_kv(mem, xkv_b):
    B = mem.shape[0]
    blk = pl.BlockSpec((1, MEM_LEN, D_MODEL), lambda b: (b, 0, 0))
    return pl.pallas_call(
        _mem_kv_kernel,
        grid=(B,),
        in_specs=[blk, pl.BlockSpec((D_MODEL, 2 * D_MODEL), lambda b: (0, 0))],
        out_specs=[blk, blk],
        out_shape=[jax.ShapeDtypeStruct((B, MEM_LEN, D_MODEL), BF16)] * 2,
        compiler_params=_cparams("parallel"),
        name="mem_kv_proj",
    )(mem, xkv_b)


def _route(logits):
    rows = logits.shape[0]
    lane = lax.broadcasted_iota(jnp.int32, (rows, LANES), 1).astype(F32)
    big = float(LANES)
    ninf = -jnp.inf
    gl = jnp.where(lane < N_GROUPS, logits, ninf)
    gmax = jnp.max(gl, axis=-1, keepdims=True)
    gidx = jnp.min(jnp.where(gl == gmax, lane, big), axis=-1, keepdims=True)
    g_p = 1.0 / jnp.sum(jnp.exp(gl - gmax), axis=-1, keepdims=True)
    lo_lane = N_GROUPS + EXPERTS_PER_GROUP * gidx
    el = jnp.where((lane >= lo_lane) & (lane < lo_lane + EXPERTS_PER_GROUP), logits, ninf)
    v1 = jnp.max(el, axis=-1, keepdims=True)
    i1 = jnp.min(jnp.where(el == v1, lane, big), axis=-1, keepdims=True)
    el2 = jnp.where(lane == i1, ninf, el)
    v2 = jnp.max(el2, axis=-1, keepdims=True)
    i2 = jnp.min(jnp.where(el2 == v2, lane, big), axis=-1, keepdims=True)
    t = jnp.exp(v2 - v1)
    w1 = g_p / (1.0 + t)
    w2 = g_p * t / (1.0 + t)
    a = jnp.minimum(i1, i2) - lo_lane
    b = jnp.maximum(i1, i2) - lo_lane
    pair = a * (2 * EXPERTS_PER_GROUP - 1 - a) * 0.5 + (b - a - 1.0)
    cls = gidx * PAIRS_PER_GROUP + pair
    w_lo = jnp.where(i1 < i2, w1, w2)
    w_hi = jnp.where(i1 < i2, w2, w1)
    return jnp.where(lane == 0, cls, jnp.where(lane == 1, w_lo, jnp.where(lane == 2, w_hi, 0.0)))


def _xattn_kernel(h1_ref, k_ref, v_ref, wq_ref, wo_ref, g2_ref, b2_ref, wr_hi_ref, wr_lo_ref, br_ref, xr_ref):
    h1 = h1_ref[...]
    q = jnp.dot(h1.astype(BF16), wq_ref[...], preferred_element_type=F32).astype(BF16)
    outs = []
    for hd in range(X_HEADS):
        sl = slice(hd * X_HEAD_DIM, (hd + 1) * X_HEAD_DIM)
        s = lax.dot_general(q[:, sl], k_ref[0, :, sl], (((1,), (1,)), ((), ())), preferred_element_type=F32)
        m = jnp.max(s, axis=-1, keepdims=True)
        p = jnp.exp(s - m)
        l = jnp.sum(p, axis=-1, keepdims=True)
        o = jnp.dot(p.astype(BF16), v_ref[0, :, sl], preferred_element_type=F32) / l
        outs.append(o.astype(BF16))
    xa = jnp.dot(jnp.concatenate(outs, axis=1), wo_ref[...], preferred_element_type=F32)
    h2 = _layer_norm(ALPHA * h1 + xa, g2_ref[...], b2_ref[...])
    h_hi = h2.astype(BF16)
    h_lo = (h2 - h_hi.astype(F32)).astype(BF16)
    logits = (jnp.dot(h_hi, wr_hi_ref[...], preferred_element_type=F32)
              + jnp.dot(h_hi, wr_lo_ref[...], preferred_element_type=F32)
              + jnp.dot(h_lo, wr_hi_ref[...], preferred_element_type=F32)) + br_ref[...]
    xr_ref[:, :D_MODEL] = h2
    xr_ref[:, D_MODEL:] = _route(logits)


def _cross_attention_route(h1, k, v, xq_b, xo_b, ln_g, ln_b, wr_hi, wr_lo, br):
    T = h1.shape[0]
    tiles_per_seq = SEQ // TM_X
    row = lambda i: (i, 0)
    const = lambda i: (0, 0)
    kv_blk = pl.BlockSpec((1, MEM_LEN, D_MODEL), lambda i: (i // tiles_per_seq, 0, 0))
    return pl.pallas_call(
        _xattn_kernel,
        grid=(T // TM_X,),
        in_specs=[
            pl.BlockSpec((TM_X, D_MODEL), row), kv_blk, kv_blk,
            pl.BlockSpec((D_MODEL, D_MODEL), const),
            pl.BlockSpec((D_MODEL, D_MODEL), const),
            pl.BlockSpec((1, D_MODEL), const),
            pl.BlockSpec((1, D_MODEL), const),
            pl.BlockSpec((D_MODEL, LANES), const),
            pl.BlockSpec((D_MODEL, LANES), const),
            pl.BlockSpec((1, LANES), const),
        ],
        out_specs=pl.BlockSpec((TM_X, XR_WIDTH), row),
        out_shape=jax.ShapeDtypeStruct((T, XR_WIDTH), F32),
        compiler_params=_cparams("parallel"),
        name="xattn_ln_route",
    )(h1, k, v, xq_b, xo_b, ln_g, ln_b, wr_hi, wr_lo, br)


def _expert_kernel(src_ref, cnt_ref, elo_ref, ehi_ref, ntile_ref, x_hbm,
                   wg_lo, wu_lo, wd_lo, wg_hi, wu_hi, wd_hi, g3_ref, b3_ref, o_hbm,
                   xbuf, obuf, gsem, ssem):
    j = pl.program_id(0)
    n_tiles = ntile_ref[0]
    slot = j % 2

    def row_copy(t, i, s, gather):
        tok = pl.ds(src_ref[t * TM_MOE + i], 1)
        row = pl.ds(i, 1)
        if gather:
            return pltpu.make_async_copy(x_hbm.at[tok], xbuf.at[s, row], gsem.at[s])
        return pltpu.make_async_copy(obuf.at[s, row], o_hbm.at[tok], ssem.at[s])

    def n_rows(t, gather):
        cnt = cnt_ref[t]
        return (cnt + ROW_GROUP - 1) // ROW_GROUP * ROW_GROUP if gather else cnt

    def start_rows(t, s, gather):
        cnt = cnt_ref[t]
        n_groups = (cnt + ROW_GROUP - 1) // ROW_GROUP if gather else cnt // ROW_GROUP

        def group(g, c):
            base = pl.multiple_of(g * ROW_GROUP, ROW_GROUP)
            for r in range(ROW_GROUP):
                row_copy(t, base + r, s, gather).start()
            return c
        lax.fori_loop(0, n_groups, group, 0)
        if not gather:
            def single(i, c):
                row_copy(t, i, s, gather).start()
                return c
            lax.fori_loop(n_groups * ROW_GROUP, cnt, single, 0)

    def wait_rows(t, s, gather):
        cnt = n_rows(t, gather)
        for bit in range(TM_MOE.bit_length()):
            rows = pl.ds(0, 1 << bit)

            @pl.when((cnt >> bit) & 1 == 1)
            def _():
                if gather:
                    pltpu.make_async_copy(x_hbm.at[rows], xbuf.at[s, rows], gsem.at[s]).wait()
                else:
                    pltpu.make_async_copy(obuf.at[s, rows], o_hbm.at[rows], ssem.at[s]).wait()

    @pl.when(j == 0)
    def _():
        xbuf[...] = jnp.zeros_like(xbuf)
        start_rows(0, 0, True)

    @pl.when(j < n_tiles)
    def _():
        @pl.when(j + 1 < n_tiles)
        def _():
            start_rows(j + 1, 1 - slot, True)

        wait_rows(j, slot, True)
        x = xbuf[slot, :, :D_MODEL]
        xb = x.astype(BF16)
        y = jnp.zeros_like(x)
        for e, (wg, wu, wd) in enumerate(((wg_lo, wu_lo, wd_lo), (wg_hi, wu_hi, wd_hi))):
            gate = xbuf[slot, :, D_MODEL + 1 + e:D_MODEL + 2 + e]
            a = jnp.dot(xb, wg[0], preferred_element_type=F32)
            u = jnp.dot(xb, wu[0], preferred_element_type=F32)
            hid = a * jax.nn.sigmoid(a) * u
            y = y + jnp.dot((gate * hid).astype(BF16), wd[0], preferred_element_type=F32)
        out = _layer_norm(ALPHA * x + y, g3_ref[...], b3_ref[...])

        @pl.when(j >= 2)
        def _():
            wait_rows(j - 2, slot, False)

        obuf[slot] = out
        start_rows(j, slot, False)

        @pl.when(j == n_tiles - 1)
        def _():
            @pl.when(j >= 1)
            def _():
                wait_rows(j - 1, 1 - slot, False)

            wait_rows(j, slot, False)


def _expert_mlp(src_rows, tile_cnt, tile_elo, tile_ehi, n_tiles, xr, wg, wu, wd, ln_g, ln_b):
    n_tiles_max = tile_cnt.shape[0]
    const = lambda j, src, cnt, elo, ehi, nt: (0, 0)
    lo = lambda j, src, cnt, elo, ehi, nt: (elo[j], 0, 0)
    hi = lambda j, src, cnt, elo, ehi, nt: (ehi[j], 0, 0)
    up = (1, D_MODEL, D_EXPERT)
    down = (1, D_EXPERT, D_MODEL)
    any_spec = pl.BlockSpec(memory_space=pl.ANY)
    return pl.pallas_call(
        _expert_kernel,
        grid_spec=pltpu.PrefetchScalarGridSpec(
            num_scalar_prefetch=5,
            grid=(n_tiles_max,),
            in_specs=[
                any_spec,
                pl.BlockSpec(up, lo), pl.BlockSpec(up, lo), pl.BlockSpec(down, lo),
                pl.BlockSpec(up, hi), pl.BlockSpec(up, hi), pl.BlockSpec(down, hi),
                pl.BlockSpec((1, D_MODEL), const),
                pl.BlockSpec((1, D_MODEL), const),
            ],
            out_specs=any_spec,
            scratch_shapes=[
                pltpu.VMEM((2, TM_MOE, XR_WIDTH), F32),
                pltpu.VMEM((2, TM_MOE, D_MODEL), F32),
                pltpu.SemaphoreType.DMA((2,)),
                pltpu.SemaphoreType.DMA((2,)),
            ],
        ),
        out_shape=jax.ShapeDtypeStruct((xr.shape[0], D_MODEL), F32),
        compiler_params=_cparams("arbitrary"),
        name="moe_experts",
    )(src_rows, tile_cnt, tile_elo, tile_ehi, n_tiles, xr, wg, wu, wd, wg, wu, wd, ln_g, ln_b)


def _class_experts():
    lo, hi = [], []
    for g in range(N_GROUPS):
        for a in range(EXPERTS_PER_GROUP):
            for b in range(a + 1, EXPERTS_PER_GROUP):
                lo.append(g * EXPERTS_PER_GROUP + a)
                hi.append(g * EXPERTS_PER_GROUP + b)
    return np.asarray(lo, np.int32), np.asarray(hi, np.int32)


def _plan_kernel(route_ref, pos_ref, counts_ref, tri_scr, run_scr, start_scr):
    phase, i = pl.program_id(0), pl.program_id(1)
    lane = lax.broadcasted_iota(jnp.int32, (TM_PLAN, LANES), 1)
    onehot = lane.astype(F32) == route_ref[:, 0:1]
    onehot_f = jnp.where(onehot, 1.0, 0.0)

    @pl.when((phase == 0) & (i == 0))
    def _():
        run_scr[...] = jnp.zeros_like(run_scr)
        r = lax.broadcasted_iota(jnp.int32, (TM_PLAN, TM_PLAN), 0)
        c = lax.broadcasted_iota(jnp.int32, (TM_PLAN, TM_PLAN), 1)
        tri_scr[...] = jnp.where(c < r, 1.0, 0.0).astype(BF16)

    @pl.when((phase == 1) & (i == 0))
    def _():
        counts = run_scr[...]
        counts_ref[...] = counts
        tiles = jnp.floor((counts + (TM_MOE - 1)) * (1.0 / TM_MOE))
        lane_row = lax.broadcasted_iota(jnp.int32, (1, LANES), 1)
        scan = tiles
        shift = 1
        while shift < LANES:
            scan = scan + jnp.where(lane_row >= shift, pltpu.roll(scan, shift, axis=1), 0.0)
            shift *= 2
        start_scr[...] = (scan - tiles) * TM_MOE
        run_scr[...] = jnp.zeros_like(run_scr)

    @pl.when(phase == 1)
    def _():
        before = jnp.dot(tri_scr[...], onehot_f.astype(BF16), preferred_element_type=F32)
        pos_col = jnp.sum(onehot_f * (before + run_scr[...] + start_scr[...]), axis=1, keepdims=True)
        eye = (lax.broadcasted_iota(jnp.int32, (LANES, LANES), 0)
               == lax.broadcasted_iota(jnp.int32, (LANES, LANES), 1))
        for r in range(TM_PLAN // LANES):
            row = jnp.sum(jnp.where(eye, pos_col[r * LANES:(r + 1) * LANES], 0.0), axis=0, keepdims=True)
            pos_ref[r:r + 1, :] = row.astype(jnp.int32)

    run_scr[...] = run_scr[...] + jnp.sum(onehot_f, axis=0, keepdims=True)


def _plan_positions(xr):
    T = xr.shape[0]
    n_steps = T // TM_PLAN
    pos, counts = pl.pallas_call(
        _plan_kernel,
        grid=(2, n_steps),
        in_specs=[pl.BlockSpec((TM_PLAN, LANES), lambda p, i: (i, D_MODEL // LANES))],
        out_specs=[pl.BlockSpec((TM_PLAN // LANES, LANES), lambda p, i: (i * p, 0)),
                   pl.BlockSpec((1, LANES), lambda p, i: (0, 0))],
        out_shape=[jax.ShapeDtypeStruct((T // LANES, LANES), jnp.int32),
                   jax.ShapeDtypeStruct((1, LANES), F32)],
        scratch_shapes=[pltpu.VMEM((TM_PLAN, TM_PLAN), BF16),
                        pltpu.VMEM((1, LANES), F32),
                        pltpu.VMEM((1, LANES), F32)],
        compiler_params=_cparams("arbitrary", "arbitrary"),
        name="moe_plan",
    )(xr)
    return pos.reshape(T), counts[0, :N_CLASSES].astype(jnp.int32)


def _moe_plan(xr, n_tiles_max):
    T = xr.shape[0]
    pos, counts = _plan_positions(xr)
    tiles_per_class = (counts + TM_MOE - 1) // TM_MOE
    tile_end = jnp.cumsum(tiles_per_class)
    tile_start = tile_end - tiles_per_class
    n_tiles = tile_end[-1]
    tok = jnp.arange(T, dtype=jnp.int32)
    src_rows = jnp.zeros((n_tiles_max * TM_MOE,), jnp.int32).at[pos].set(tok)
    tile_ids = jnp.arange(n_tiles_max, dtype=jnp.int32)
    used = jnp.minimum(tile_ids, n_tiles - 1)
    tile_cls = jnp.sum((tile_end[None, :] <= used[:, None]).astype(jnp.int32), axis=1)
    tile_cls = jnp.minimum(tile_cls, N_CLASSES - 1)
    cls_lo, cls_hi = _class_experts()
    tile_elo = jnp.asarray(cls_lo)[tile_cls]
    tile_ehi = jnp.asarray(cls_hi)[tile_cls]
    within = tile_ids - tile_start[tile_cls]
    tile_cnt = jnp.clip(counts[tile_cls] - within * TM_MOE, 0, TM_MOE)
    tile_cnt = jnp.where(tile_ids < n_tiles, tile_cnt, 0).astype(jnp.int32)
    return src_rows, tile_elo, tile_ehi, tile_cnt, n_tiles.reshape(1).astype(jnp.int32)


def _vec(a):
    return a.reshape(1, -1).astype(F32)


def _mixer_and_cross_attention(x, mem, ln_in_g, ln_in_b, w_in, rel_bias, sink_a, norm_a_g, norm_b_g, w_out,
                               ln1_g, ln1_b, xq, xkv, xo, ln2_g, ln2_b, w_group, b_group, w_router, b_router):
    B, S, D = x.shape
    assert S == SEQ and D == D_MODEL and mem.shape[1:] == (MEM_LEN, D_MODEL)
    T = B * S
    vec = _vec

    w = w_in[0]
    edges = np.cumsum((0, A_WIDTH, A_KV_HEADS * HEAD_DIM, A_KV_HEADS * HEAD_DIM, B_WIDTH, B_WIDTH, B_WIDTH))
    qa, ka, va, qb, kb, vb = [w[:, a:b] for a, b in zip(edges[:-1], edges[1:])]
    dup = lambda t: jnp.repeat(t.reshape(D, A_KV_HEADS, 1, HEAD_DIM), 2, axis=2).reshape(D, A_KV_TILES * LANES)
    scale = HEAD_DIM ** -0.5
    w_in_b = jnp.concatenate([qa * scale, dup(ka), dup(va), qb * scale, kb, vb], axis=1).astype(BF16)
    w_out_b = w_out[0].astype(BF16)
    xq_b = (xq[0] * (X_HEAD_DIM ** -0.5)).astype(BF16)
    xkv_b = xkv[0].astype(BF16)
    xo_b = xo[0].astype(BF16)
    wr = jnp.concatenate([w_group[0], w_router[0]], axis=1).astype(F32)
    wr = jnp.pad(wr, ((0, 0), (0, LANES - wr.shape[1])))
    wr_hi = wr.astype(BF16)
    wr_lo = (wr - wr_hi.astype(F32)).astype(BF16)
    br = jnp.concatenate([b_group[0], b_router[0]]).astype(F32)
    br = jnp.pad(br, (0, LANES - br.shape[0])).reshape(1, LANES)

    h0, qkv_a, qkv_b1, qkv_b4, qkv_b16 = _input_projection(
        x.reshape(T, D), vec(ln_in_g), vec(ln_in_b), w_in_b, B)

    tiles_a = ((0, 1), (2, 3))
    nk_a = TM_ATT + 2 * A_HALF_WIN
    bias_a = _band_bias(rel_bias[:, :A_HEADS], A_HALF_WIN, 1, TM_ATT, nk_a, S, tiles_a)
    ya = _band_attention(qkv_a.reshape(B, S, QKV_A), bias_a, nk=nk_a, sub=1, q_tiles=tiles_a,
                         q_width=A_WIDTH, kv_width=A_KV_TILES * LANES, out_width=A_WIDTH, out_dtype=BF16,
                         sink=sink_a[0].astype(F32), gain=vec(norm_a_g[0]))
    tiles_b = ((0,), (1,), (2,), (3,))
    branch_out = []
    for (win, dil), qkv in zip(B_BRANCHES, (qkv_b1.reshape(B, S, QKV_B),
                                             qkv_b4.reshape(B * 4, S // 4, QKV_B),
                                             qkv_b16.reshape(B * 16, S // 16, QKV_B))):
        half = (win // 2) // dil
        n = S // dil
        nk = min(TM_ATT + 2 * half, n)
        bias_b = _band_bias(rel_bias[:, A_HEADS:], half, dil, TM_ATT, nk, n, tiles_b)
        branch_out.append(_band_attention(qkv, bias_b, nk=nk, sub=max(1, TM_ATT * 4 // n), q_tiles=tiles_b,
                                          q_width=B_WIDTH, kv_width=B_WIDTH, out_width=B_OUT, out_dtype=F32))
    o1 = branch_out[0].reshape(T, B_OUT)
    o4 = branch_out[1].reshape(B, 4, S // 4, B_OUT)
    o16 = branch_out[2].reshape(B, 16, S // 16, B_OUT)

    h1 = _merge_project(ya.reshape(T, A_WIDTH), o1, o4, o16, h0, vec(norm_b_g[0]), w_out_b,
                        vec(ln1_g[0]), vec(ln1_b[0]))

    k_mem, v_mem = _mem_kv(mem, xkv_b)
    xr = _cross_attention_route(h1, k_mem, v_mem, xq_b, xo_b, vec(ln2_g[0]), vec(ln2_b[0]), wr_hi, wr_lo, br)
    return h0, h1, xr


def _moe(xr, w_gate, w_up, w_down, ln3_g, ln3_b):
    T = xr.shape[0]
    wg_b = w_gate[0].reshape(N_EXPERTS, D_MODEL, D_EXPERT).astype(BF16)
    wu_b = w_up[0].reshape(N_EXPERTS, D_MODEL, D_EXPERT).astype(BF16)
    wd_b = w_down[0].reshape(N_EXPERTS, D_EXPERT, D_MODEL).astype(BF16)
    n_tiles_max = T // TM_MOE + N_CLASSES
    src_rows, tile_elo, tile_ehi, tile_cnt, n_tiles = _moe_plan(xr, n_tiles_max)
    return _expert_mlp(src_rows, tile_cnt, tile_elo, tile_ehi, n_tiles, xr, wg_b, wu_b, wd_b,
                       _vec(ln3_g[0]), _vec(ln3_b[0]))


def kernel(x, mem, ln_in_g, ln_in_b, w_in, rel_bias, sink_a, norm_a_g, norm_b_g, w_out,
           ln1_g, ln1_b, xq, xkv, xo, ln2_g, ln2_b, w_group, b_group, w_router, b_router,
           w_gate, w_up, w_down, ln3_g, ln3_b):
    _, _, xr = _mixer_and_cross_attention(
        x, mem, ln_in_g, ln_in_b, w_in, rel_bias, sink_a, norm_a_g, norm_b_g, w_out,
        ln1_g, ln1_b, xq, xkv, xo, ln2_g, ln2_b, w_group, b_group, w_router, b_router)
    return _moe(xr, w_gate, w_up, w_down, ln3_g, ln3_b).reshape(x.shape)
```

```python
import functools

import numpy as np
import jax
import jax.numpy as jnp
from jax import lax
from jax.experimental import pallas as pl
from jax.experimental.pallas import tpu as pltpu

F32 = jnp.float32
BF16 = jnp.bfloat16

D_MODEL = 1024
SEQ = 2048
MEM_LEN = 256
HEAD_DIM = 64
A_HEADS = 8
A_KV_HEADS = 2
A_HALF_WIN = 128
B_HEADS = 8
B_BRANCHES = ((128, 1), (512, 4), (2048, 16))
N_BUCKETS = 32
MAX_DISTANCE = 1024
X_HEADS = 4
X_HEAD_DIM = D_MODEL // X_HEADS
N_GROUPS = 4
EXPERTS_PER_GROUP = 8
N_EXPERTS = N_GROUPS * EXPERTS_PER_GROUP
D_EXPERT = 512
DEPTH = 1
ALPHA = (2.0 * DEPTH) ** 0.25
LN_EPS = 1e-5
NEG = -1e30
LOG2E = 1.4426950408889634
LN2 = 0.6931471805599453

LANES = 128
XR_WIDTH = D_MODEL + LANES
A_WIDTH = A_HEADS * HEAD_DIM
B_WIDTH = B_HEADS * HEAD_DIM
A_KV_TILES = A_KV_HEADS
QKV_A = A_WIDTH + 2 * A_KV_TILES * LANES
QKV_B = 3 * B_WIDTH
B_OUT = B_WIDTH + LANES

PAIRS_PER_GROUP = EXPERTS_PER_GROUP * (EXPERTS_PER_GROUP - 1) // 2
N_CLASSES = N_GROUPS * PAIRS_PER_GROUP

TM_IN = 512
TM_ATT = 128
TM_MERGE = 256
TM_X = 256
TM_MOE = 128
ROW_GROUP = 8
TM_PLAN = 1024
VMEM_LIMIT = 56 * 1024 * 1024


def _cparams(*sem):
    return pltpu.CompilerParams(dimension_semantics=sem, vmem_limit_bytes=VMEM_LIMIT)


def _layer_norm(x, g, b):
    mu = jnp.mean(x, axis=-1, keepdims=True)
    xc = x - mu
    var = jnp.mean(xc * xc, axis=-1, keepdims=True)
    return xc * lax.rsqrt(var + LN_EPS) * g + b


def _rms_norm(x, g):
    return x * lax.rsqrt(jnp.mean(x * x, axis=-1, keepdims=True) + LN_EPS) * g


def _t5_bucket(rel):
    nb = N_BUCKETS // 2
    max_exact = nb // 2
    ret = (rel > 0).astype(np.int32) * nb
    n = np.abs(rel)
    n_safe = np.maximum(n, 1).astype(np.float64)
    large = max_exact + (np.log(n_safe / max_exact) / np.log(MAX_DISTANCE / max_exact)
                         * (nb - max_exact)).astype(np.int32)
    large = np.minimum(large, nb - 1)
    return (ret + np.where(n < max_exact, n, large)).astype(np.int32)


def _inproj_kernel(x_ref, g_ref, b_ref, w_ref, h0_ref, qa_ref, qb1_ref, qb4_ref, qb16_ref, pb_scr):
    h = _layer_norm(x_ref[...], g_ref[...], b_ref[...])
    h0_ref[...] = h
    proj = jnp.dot(h.astype(BF16), w_ref[...], preferred_element_type=F32)
    qa_ref[...] = proj[:, :QKV_A].astype(BF16)
    pb = proj[:, QKV_A:]
    qb1_ref[...] = pb.astype(BF16)
    for c in range(QKV_B // LANES):
        cs = slice(c * LANES, (c + 1) * LANES)
        pb_scr[c] = pb[:, cs]
        for dil, ref in ((4, qb4_ref), (16, qb16_ref)):
            for r in range(dil):
                ref[0, r, :, cs] = pb_scr[c, pl.ds(r, TM_IN // dil, stride=dil), :].astype(BF16)


def _input_projection(x2, ln_g, ln_b, w_in_b, batch):
    T = x2.shape[0]
    tiles_per_seq = SEQ // TM_IN
    row = lambda i: (i, 0)
    const = lambda i: (0, 0)
    deint = lambda i: (i // tiles_per_seq, 0, i % tiles_per_seq, 0)
    return pl.pallas_call(
        _inproj_kernel,
        grid=(T // TM_IN,),
        in_specs=[
            pl.BlockSpec((TM_IN, D_MODEL), row),
            pl.BlockSpec((1, D_MODEL), const),
            pl.BlockSpec((1, D_MODEL), const),
            pl.BlockSpec((D_MODEL, QKV_A + QKV_B), const),
        ],
        out_specs=[
            pl.BlockSpec((TM_IN, D_MODEL), row),
            pl.BlockSpec((TM_IN, QKV_A), row),
            pl.BlockSpec((TM_IN, QKV_B), row),
            pl.BlockSpec((1, 4, TM_IN // 4, QKV_B), deint),
            pl.BlockSpec((1, 16, TM_IN // 16, QKV_B), deint),
        ],
        out_shape=[
            jax.ShapeDtypeStruct((T, D_MODEL), F32),
            jax.ShapeDtypeStruct((T, QKV_A), BF16),
            jax.ShapeDtypeStruct((T, QKV_B), BF16),
            jax.ShapeDtypeStruct((batch, 4, SEQ // 4, QKV_B), BF16),
            jax.ShapeDtypeStruct((batch, 16, SEQ // 16, QKV_B), BF16),
        ],
        scratch_shapes=[pltpu.VMEM((QKV_B // LANES, TM_IN, LANES), F32)],
        compiler_params=_cparams("parallel"),
        name="ln_in_proj",
    )(x2, ln_g, ln_b, w_in_b)


def _band_attn_kernel(*refs, m, nk, n, sub, kv_of, with_sink):
    if with_sink:
        sink_ref, gain_ref = refs[0], refs[1]
        refs = refs[2:]
    q_ref, k_ref, v_ref, bias_ref, o_ref = refs
    if nk == n:
        start = 0
    else:
        j = pl.program_id(1)
        start = pl.multiple_of(jnp.clip(j * m - (nk - m) // 2, 0, n - nk), HEAD_DIM)
    lane_row = lax.broadcasted_iota(jnp.int32, (1, LANES), 1)
    keep_lo = jnp.where(lane_row < HEAD_DIM, 1.0, 0.0).astype(BF16)
    keep_hi = jnp.where(lane_row < HEAD_DIM, 0.0, 1.0).astype(BF16)
    lane = lax.broadcasted_iota(jnp.int32, (m, LANES), 1)
    n_q = len(kv_of)
    units = [(s, qt) for s in range(sub) for qt in range(n_q)]

    scores = []
    for s, qt in units:
        t = kv_of[qt]
        k_t = k_ref[s, pl.ds(start, nk), t * LANES:(t + 1) * LANES]
        q2 = q_ref[s, :, qt * LANES:(qt + 1) * LANES]
        lhs = jnp.concatenate([q2 * keep_lo, q2 * keep_hi], axis=0)
        sc = lax.dot_general(lhs, k_t, (((1,), (1,)), ((), ())), preferred_element_type=F32)
        scores.append(sc + bias_ref[0, qt])

    probs, denom, row_max = [], [], []
    for (s, qt), sc in zip(units, scores):
        mx = jnp.max(sc, axis=-1, keepdims=True)
        p = jnp.exp2(sc - mx)
        l = jnp.sum(p, axis=-1, keepdims=True)
        halves = []
        for h in (0, 1):
            l_h = l[h * m:(h + 1) * m]
            if with_sink:
                l_h = l_h + jnp.exp2(sink_ref[2 * qt + h] - mx[h * m:(h + 1) * m])
            halves.append(l_h)
        probs.append(p.astype(BF16))
        denom.append(halves)
        row_max.append(mx)

    for s in range(sub):
        pairs = []
        lse_tile = jnp.zeros((m, LANES), F32)
        for qt in range(n_q):
            u = s * n_q + qt
            t = kv_of[qt]
            v_t = v_ref[s, pl.ds(start, nk), t * LANES:(t + 1) * LANES]
            o = jnp.dot(probs[u], v_t, preferred_element_type=F32)
            top = o[:m] * (1.0 / denom[u][0])
            bot = o[m:] * (1.0 / denom[u][1])
            pairs.append(jnp.where(lane < HEAD_DIM, top, bot))
            if not with_sink:
                for h in (0, 1):
                    lse_h = (row_max[u][h * m:(h + 1) * m] + jnp.log2(denom[u][h])) * LN2
                    lse_tile = jnp.where(lane == 2 * qt + h, lse_h, lse_tile)
        o_all = jnp.concatenate(pairs, axis=1)
        if with_sink:
            o_ref[s] = _rms_norm(o_all, gain_ref[...]).astype(o_ref.dtype)
        else:
            o_ref[s] = jnp.concatenate([o_all, lse_tile], axis=1)


def _band_window(m, nk, n):
    nb = n // m
    starts = np.clip(np.arange(nb) * m - (nk - m) // 2, 0, n - nk)
    offs = [int(o) for o in starts - np.arange(nb) * m]
    uniq = sorted(set(offs), reverse=True)
    var = [uniq.index(o) for o in offs]
    assert all(v == var[1] for v in var[1:-1])
    return uniq, var


def _band_bias(rel_bias_h, half, dil, m, nk, n):
    uniq, _ = _band_window(m, nk, n)
    n_heads = rel_bias_h.shape[1]
    span = nk + m - 1
    out = []
    for off in uniq:
        rel = off - (m - 1) + np.arange(span)
        onehot = np.zeros((span, N_BUCKETS), np.float32)
        onehot[np.arange(span), _t5_bucket(dil * rel)] = 1.0
        table = jnp.dot(jnp.asarray(onehot), rel_bias_h.astype(F32), precision=lax.Precision.HIGHEST)
        table = jnp.where((np.abs(rel) <= half)[:, None], table, NEG)
        u = jnp.concatenate([table.T, jnp.zeros((n_heads, 1), F32)], axis=1)
        flat = jnp.tile(u, (1, m + 1))[:, m - 1:m - 1 + m * span]
        b = flat.reshape(n_heads, m, span)[:, :, :nk]
        out.append(b.reshape(n_heads // 2, 2 * m, nk))
    return jnp.stack(out) * LOG2E


def _band_attention(qkv, bias, *, nk, sub, kv_of, q_width, kv_width, out_width, out_dtype,
                    sink=None, gain=None):
    Bd, n, _ = qkv.shape
    m = TM_ATT
    nb = n // m
    assert nb * m == n and Bd % sub == 0
    _, var = _band_window(m, nk, n)
    kcol = q_width // kv_width
    with_sink = sink is not None

    def variant(b, j):
        v = jnp.where(j == nb - 1, var[-1], var[min(1, nb - 1)])
        return jnp.where(j == 0, var[0], v)

    in_specs = [
        pl.BlockSpec((sub, m, q_width), lambda b, j: (b, j, 0)),
        pl.BlockSpec((sub, n, kv_width), lambda b, j: (b, 0, kcol)),
        pl.BlockSpec((sub, n, kv_width), lambda b, j: (b, 0, kcol + 1)),
        pl.BlockSpec((1, len(kv_of), 2 * m, nk), lambda b, j: (variant(b, j), 0, 0, 0)),
    ]
    args = [qkv, qkv, qkv, bias]
    if with_sink:
        in_specs = [pl.BlockSpec(memory_space=pltpu.SMEM),
                    pl.BlockSpec((1, q_width), lambda b, j: (0, 0))] + in_specs
        args = [sink, gain] + args
    return pl.pallas_call(
        functools.partial(_band_attn_kernel, m=m, nk=nk, n=n, sub=sub, kv_of=kv_of, with_sink=with_sink),
        grid=(Bd // sub, nb),
        in_specs=in_specs,
        out_specs=pl.BlockSpec((sub, m, out_width), lambda b, j: (b, j, 0)),
        out_shape=jax.ShapeDtypeStruct((Bd, n, out_width), out_dtype),
        compiler_params=_cparams("parallel", "arbitrary"),
        name="band_attn_sink" if with_sink else f"band_attn_n{n}",
    )(*args)


def _merge_kernel(ya_ref, o1_ref, o4_ref, o16_ref, h0_ref, gb_ref, w_ref, g1_ref, b1_ref, h1_ref,
                  s4_scr, s16_scr):
    for c in range(B_OUT // LANES):
        cs = slice(c * LANES, (c + 1) * LANES)
        for dil, src, dst in ((4, o4_ref, s4_scr), (16, o16_ref, s16_scr)):
            for r in range(dil):
                dst[c, pl.ds(r, TM_MERGE // dil, stride=dil), :] = src[0, r, :, cs]
    lse_c = B_WIDTH // LANES
    lse = [o1_ref[:, B_WIDTH:], s4_scr[lse_c], s16_scr[lse_c]]
    mx = jnp.maximum(jnp.maximum(lse[0], lse[1]), lse[2])
    ex = [jnp.exp(l - mx) for l in lse]
    inv = 1.0 / (ex[0] + ex[1] + ex[2])
    wts = [e * inv for e in ex]
    lane = lax.broadcasted_iota(jnp.int32, (TM_MERGE, LANES), 1)
    pieces = []
    for c in range(lse_c):
        tiles = (o1_ref[:, c * LANES:(c + 1) * LANES], s4_scr[c], s16_scr[c])
        acc = jnp.zeros((TM_MERGE, LANES), F32)
        for w, o in zip(wts, tiles):
            acc = acc + jnp.where(lane < HEAD_DIM, w[:, 2 * c:2 * c + 1], w[:, 2 * c + 1:2 * c + 2]) * o
        pieces.append(acc)
    yb = _rms_norm(jnp.concatenate(pieces, axis=1), gb_ref[...])
    y = jnp.concatenate([ya_ref[...], yb.astype(BF16)], axis=1)
    mix = jnp.dot(y, w_ref[...], preferred_element_type=F32)
    h1_ref[...] = _layer_norm(ALPHA * h0_ref[...] + mix, g1_ref[...], b1_ref[...])


def _merge_project(ya, o1, o4, o16, h0, gain_b, w_out_b, ln_g, ln_b):
    T = h0.shape[0]
    tiles_per_seq = SEQ // TM_MERGE
    row = lambda i: (i, 0)
    const = lambda i: (0, 0)
    deint = lambda i: (i // tiles_per_seq, 0, i % tiles_per_seq, 0)
    return pl.pallas_call(
        _merge_kernel,
        grid=(T // TM_MERGE,),
        in_specs=[
            pl.BlockSpec((TM_MERGE, A_WIDTH), row),
            pl.BlockSpec((TM_MERGE, B_OUT), row),
            pl.BlockSpec((1, 4, TM_MERGE // 4, B_OUT), deint),
            pl.BlockSpec((1, 16, TM_MERGE // 16, B_OUT), deint),
            pl.BlockSpec((TM_MERGE, D_MODEL), row),
            pl.BlockSpec((1, B_WIDTH), const),
            pl.BlockSpec((D_MODEL, D_MODEL), const),
            pl.BlockSpec((1, D_MODEL), const),
            pl.BlockSpec((1, D_MODEL), const),
        ],
        out_specs=pl.BlockSpec((TM_MERGE, D_MODEL), row),
        out_shape=jax.ShapeDtypeStruct((T, D_MODEL), F32),
        scratch_shapes=[pltpu.VMEM((B_OUT // LANES, TM_MERGE, LANES), F32)] * 2,
        compiler_params=_cparams("parallel"),
        name="merge_out_proj",
    )(ya, o1, o4, o16, h0, gain_b, w_out_b, ln_g, ln_b)


def _mem_kv_kernel(mem_ref, w_ref, k_ref, v_ref):
    kv = jnp.dot(mem_ref[0].astype(BF16), w_ref[...], preferred_element_type=F32)
    k_ref[0] = kv[:, :D_MODEL].astype(BF16)
    v_ref[0] = kv[:, D_MODEL:].astype(BF16)


def _mem_kv(mem, xkv_b):
    B = mem.shape[0]
    blk = pl.BlockSpec((1, MEM_LEN, D_MODEL), lambda b: (b, 0, 0))
    return pl.pallas_call(
        _mem_kv_kernel,
        grid=(B,),
        in_specs=[blk, pl.BlockSpec((D_MODEL, 2 * D_MODEL), lambda b: (0, 0))],
        out_specs=[blk, blk],
        out_shape=[jax.ShapeDtypeStruct((B, MEM_LEN, D_MODEL), BF16)] * 2,
        compiler_params=_cparams("parallel"),
        name="mem_kv_proj",
    )(mem, xkv_b)


def _route(logits):
    rows = logits.shape[0]
    lane = lax.broadcasted_iota(jnp.int32, (rows, LANES), 1).astype(F32)
    big = float(LANES)
    ninf = -jnp.inf
    gl = jnp.where(lane < N_GROUPS, logits, ninf)
    gmax = jnp.max(gl, axis=-1, keepdims=True)
    gidx = jnp.min(jnp.where(gl == gmax, lane, big), axis=-1, keepdims=True)
    g_p = 1.0 / jnp.sum(jnp.exp(gl - gmax), axis=-1, keepdims=True)
    lo_lane = N_GROUPS + EXPERTS_PER_GROUP * gidx
    el = jnp.where((lane >= lo_lane) & (lane < lo_lane + EXPERTS_PER_GROUP), logits, ninf)
    v1 = jnp.max(el, axis=-1, keepdims=True)
    i1 = jnp.min(jnp.where(el == v1, lane, big), axis=-1, keepdims=True)
    el2 = jnp.where(lane == i1, ninf, el)
    v2 = jnp.max(el2, axis=-1, keepdims=True)
    i2 = jnp.min(jnp.where(el2 == v2, lane, big), axis=-1, keepdims=True)
    t = jnp.exp(v2 - v1)
    w1 = g_p / (1.0 + t)
    w2 = g_p * t / (1.0 + t)
    a = jnp.minimum(i1, i2) - lo_lane
    b = jnp.maximum(i1, i2) - lo_lane
    pair = a * (2 * EXPERTS_PER_GROUP - 1 - a) * 0.5 + (b - a - 1.0)
    cls = gidx * PAIRS_PER_GROUP + pair
    w_lo = jnp.where(i1 < i2, w1, w2)
    w_hi = jnp.where(i1 < i2, w2, w1)
    return jnp.where(lane == 0, cls, jnp.where(lane == 1, w_lo, jnp.where(lane == 2, w_hi, 0.0)))


def _xattn_kernel(h1_ref, k_ref, v_ref, wq_ref, wo_ref, g2_ref, b2_ref, wr_hi_ref, wr_lo_ref, br_ref, xr_ref):
    h1 = h1_ref[...]
    q = jnp.dot(h1.astype(BF16), wq_ref[...], preferred_element_type=F32).astype(BF16)
    outs = []
    for hd in range(X_HEADS):
        sl = slice(hd * X_HEAD_DIM, (hd + 1) * X_HEAD_DIM)
        s = lax.dot_general(q[:, sl], k_ref[0, :, sl], (((1,), (1,)), ((), ())), preferred_element_type=F32)
        m = jnp.max(s, axis=-1, keepdims=True)
        p = jnp.exp2(s - m)
        l = jnp.sum(p, axis=-1, keepdims=True)
        o = jnp.dot(p.astype(BF16), v_ref[0, :, sl], preferred_element_type=F32) / l
        outs.append(o.astype(BF16))
    xa = jnp.dot(jnp.concatenate(outs, axis=1), wo_ref[...], preferred_element_type=F32)
    h2 = _layer_norm(ALPHA * h1 + xa, g2_ref[...], b2_ref[...])
    h_hi = h2.astype(BF16)
    h_lo = (h2 - h_hi.astype(F32)).astype(BF16)
    logits = (jnp.dot(h_hi, wr_hi_ref[...], preferred_element_type=F32)
              + jnp.dot(h_hi, wr_lo_ref[...], preferred_element_type=F32)
              + jnp.dot(h_lo, wr_hi_ref[...], preferred_element_type=F32)) + br_ref[...]
    xr_ref[:, :D_MODEL] = h2
    xr_ref[:, D_MODEL:] = _route(logits)


def _cross_attention_route(h1, k, v, xq_b, xo_b, ln_g, ln_b, wr_hi, wr_lo, br):
    T = h1.shape[0]
    tiles_per_seq = SEQ // TM_X
    row = lambda i: (i, 0)
    const = lambda i: (0, 0)
    kv_blk = pl.BlockSpec((1, MEM_LEN, D_MODEL), lambda i: (i // tiles_per_seq, 0, 0))
    return pl.pallas_call(
        _xattn_kernel,
        grid=(T // TM_X,),
        in_specs=[
            pl.BlockSpec((TM_X, D_MODEL), row), kv_blk, kv_blk,
            pl.BlockSpec((D_MODEL, D_MODEL), const),
            pl.BlockSpec((D_MODEL, D_MODEL), const),
            pl.BlockSpec((1, D_MODEL), const),
            pl.BlockSpec((1, D_MODEL), const),
            pl.BlockSpec((D_MODEL, LANES), const),
            pl.BlockSpec((D_MODEL, LANES), const),
            pl.BlockSpec((1, LANES), const),
        ],
        out_specs=pl.BlockSpec((TM_X, XR_WIDTH), row),
        out_shape=jax.ShapeDtypeStruct((T, XR_WIDTH), F32),
        compiler_params=_cparams("parallel"),
        name="xattn_ln_route",
    )(h1, k, v, xq_b, xo_b, ln_g, ln_b, wr_hi, wr_lo, br)


def _expert_kernel(src_ref, cnt_ref, elo_ref, ehi_ref, ntile_ref, x_hbm,
                   wg_lo, wu_lo, wd_lo, wg_hi, wu_hi, wd_hi, g3_ref, b3_ref, o_hbm,
                   xbuf, obuf, gsem, ssem):
    j = pl.program_id(0)
    n_tiles = ntile_ref[0]
    slot = j % 2

    def row_copy(t, i, s, gather):
        tok = pl.ds(src_ref[t * TM_MOE + i], 1)
        row = pl.ds(i, 1)
        if gather:
            return pltpu.make_async_copy(x_hbm.at[tok], xbuf.at[s, row], gsem.at[s])
        return pltpu.make_async_copy(obuf.at[s, row], o_hbm.at[tok], ssem.at[s])

    def n_rows(t, gather):
        cnt = cnt_ref[t]
        return (cnt + ROW_GROUP - 1) // ROW_GROUP * ROW_GROUP if gather else cnt

    def start_rows(t, s, gather):
        cnt = cnt_ref[t]
        n_groups = (cnt + ROW_GROUP - 1) // ROW_GROUP if gather else cnt // ROW_GROUP

        def group(g, c):
            base = pl.multiple_of(g * ROW_GROUP, ROW_GROUP)
            for r in range(ROW_GROUP):
                row_copy(t, base + r, s, gather).start()
            return c
        lax.fori_loop(0, n_groups, group, 0)
        if not gather:
            def single(i, c):
                row_copy(t, i, s, gather).start()
                return c
            lax.fori_loop(n_groups * ROW_GROUP, cnt, single, 0)

    def wait_rows(t, s, gather):
        cnt = n_rows(t, gather)
        for bit in range(TM_MOE.bit_length()):
            rows = pl.ds(0, 1 << bit)

            @pl.when((cnt >> bit) & 1 == 1)
            def _():
                if gather:
                    pltpu.make_async_copy(x_hbm.at[rows], xbuf.at[s, rows], gsem.at[s]).wait()
                else:
                    pltpu.make_async_copy(obuf.at[s, rows], o_hbm.at[rows], ssem.at[s]).wait()

    @pl.when(j == 0)
    def _():
        xbuf[...] = jnp.zeros_like(xbuf)
        start_rows(0, 0, True)

    @pl.when(j < n_tiles)
    def _():
        @pl.when(j + 1 < n_tiles)
        def _():
            start_rows(j + 1, 1 - slot, True)

        wait_rows(j, slot, True)
        x = xbuf[slot, :, :D_MODEL]
        xb = x.astype(BF16)
        y = jnp.zeros_like(x)
        for e, (wg, wu, wd) in enumerate(((wg_lo, wu_lo, wd_lo), (wg_hi, wu_hi, wd_hi))):
            gate = xbuf[slot, :, D_MODEL + 1 + e:D_MODEL + 2 + e]
            a = jnp.dot(xb, wg[0], preferred_element_type=F32)
            u = jnp.dot(xb, wu[0], preferred_element_type=F32)
            hid = a * jax.nn.sigmoid(a) * u
            y = y + jnp.dot((gate * hid).astype(BF16), wd[0], preferred_element_type=F32)
        out = _layer_norm(ALPHA * x + y, g3_ref[...], b3_ref[...])

        @pl.when(j >= 2)
        def _():
            wait_rows(j - 2, slot, False)

        obuf[slot] = out
        start_rows(j, slot, False)

        @pl.when(j == n_tiles - 1)
        def _():
            @pl.when(j >= 1)
            def _():
                wait_rows(j - 1, 1 - slot, False)

            wait_rows(j, slot, False)


def _expert_mlp(src_rows, tile_cnt, tile_elo, tile_ehi, n_tiles, xr, wg, wu, wd, ln_g, ln_b):
    n_tiles_max = tile_cnt.shape[0]
    const = lambda j, src, cnt, elo, ehi, nt: (0, 0)
    lo = lambda j, src, cnt, elo, ehi, nt: (elo[j], 0, 0)
    hi = lambda j, src, cnt, elo, ehi, nt: (ehi[j], 0, 0)
    up = (1, D_MODEL, D_EXPERT)
    down = (1, D_EXPERT, D_MODEL)
    any_spec = pl.BlockSpec(memory_space=pl.ANY)
    return pl.pallas_call(
        _expert_kernel,
        grid_spec=pltpu.PrefetchScalarGridSpec(
            num_scalar_prefetch=5,
            grid=(n_tiles_max,),
            in_specs=[
                any_spec,
                pl.BlockSpec(up, lo), pl.BlockSpec(up, lo), pl.BlockSpec(down, lo),
                pl.BlockSpec(up, hi), pl.BlockSpec(up, hi), pl.BlockSpec(down, hi),
                pl.BlockSpec((1, D_MODEL), const),
                pl.BlockSpec((1, D_MODEL), const),
            ],
            out_specs=any_spec,
            scratch_shapes=[
                pltpu.VMEM((2, TM_MOE, XR_WIDTH), F32),
                pltpu.VMEM((2, TM_MOE, D_MODEL), F32),
                pltpu.SemaphoreType.DMA((2,)),
                pltpu.SemaphoreType.DMA((2,)),
            ],
        ),
        out_shape=jax.ShapeDtypeStruct((xr.shape[0], D_MODEL), F32),
        compiler_params=_cparams("arbitrary"),
        name="moe_experts",
    )(src_rows, tile_cnt, tile_elo, tile_ehi, n_tiles, xr, wg, wu, wd, wg, wu, wd, ln_g, ln_b)


def _class_experts():
    lo, hi = [], []
    for g in range(N_GROUPS):
        for a in range(EXPERTS_PER_GROUP):
            for b in range(a + 1, EXPERTS_PER_GROUP):
                lo.append(g * EXPERTS_PER_GROUP + a)
                hi.append(g * EXPERTS_PER_GROUP + b)
    return np.asarray(lo, np.int32), np.asarray(hi, np.int32)


def _plan_kernel(route_ref, pos_ref, counts_ref, tri_scr, run_scr, start_scr):
    phase, i = pl.program_id(0), pl.program_id(1)
    lane = lax.broadcasted_iota(jnp.int32, (TM_PLAN, LANES), 1)
    onehot = lane.astype(F32) == route_ref[:, 0:1]
    onehot_f = jnp.where(onehot, 1.0, 0.0)

    @pl.when((phase == 0) & (i == 0))
    def _():
        run_scr[...] = jnp.zeros_like(run_scr)
        r = lax.broadcasted_iota(jnp.int32, (TM_PLAN, TM_PLAN), 0)
        c = lax.broadcasted_iota(jnp.int32, (TM_PLAN, TM_PLAN), 1)
        tri_scr[...] = jnp.where(c < r, 1.0, 0.0).astype(BF16)

    @pl.when((phase == 1) & (i == 0))
    def _():
        counts = run_scr[...]
        counts_ref[...] = counts
        tiles = jnp.floor((counts + (TM_MOE - 1)) * (1.0 / TM_MOE))
        lane_row = lax.broadcasted_iota(jnp.int32, (1, LANES), 1)
        scan = tiles
        shift = 1
        while shift < LANES:
            scan = scan + jnp.where(lane_row >= shift, pltpu.roll(scan, shift, axis=1), 0.0)
            shift *= 2
        start_scr[...] = (scan - tiles) * TM_MOE
        run_scr[...] = jnp.zeros_like(run_scr)

    @pl.when(phase == 1)
    def _():
        before = jnp.dot(tri_scr[...], onehot_f.astype(BF16), preferred_element_type=F32)
        pos_col = jnp.sum(onehot_f * (before + run_scr[...] + start_scr[...]), axis=1, keepdims=True)
        eye = (lax.broadcasted_iota(jnp.int32, (LANES, LANES), 0)
               == lax.broadcasted_iota(jnp.int32, (LANES, LANES), 1))
        for r in range(TM_PLAN // LANES):
            row = jnp.sum(jnp.where(eye, pos_col[r * LANES:(r + 1) * LANES], 0.0), axis=0, keepdims=True)
            pos_ref[r:r + 1, :] = row.astype(jnp.int32)

    run_scr[...] = run_scr[...] + jnp.sum(onehot_f, axis=0, keepdims=True)


def _plan_positions(xr):
    T = xr.shape[0]
    n_steps = T // TM_PLAN
    pos, counts = pl.pallas_call(
        _plan_kernel,
        grid=(2, n_steps),
        in_specs=[pl.BlockSpec((TM_PLAN, LANES), lambda p, i: (i, D_MODEL // LANES))],
        out_specs=[pl.BlockSpec((TM_PLAN // LANES, LANES), lambda p, i: (i * p, 0)),
                   pl.BlockSpec((1, LANES), lambda p, i: (0, 0))],
        out_shape=[jax.ShapeDtypeStruct((T // LANES, LANES), jnp.int32),
                   jax.ShapeDtypeStruct((1, LANES), F32)],
        scratch_shapes=[pltpu.VMEM((TM_PLAN, TM_PLAN), BF16),
                        pltpu.VMEM((1, LANES), F32),
                        pltpu.VMEM((1, LANES), F32)],
        compiler_params=_cparams("arbitrary", "arbitrary"),
        name="moe_plan",
    )(xr)
    return pos.reshape(T), counts[0, :N_CLASSES].astype(jnp.int32)


def _moe_plan(xr, n_tiles_max):
    T = xr.shape[0]
    pos, counts = _plan_positions(xr)
    tiles_per_class = (counts + TM_MOE - 1) // TM_MOE
    tile_end = jnp.cumsum(tiles_per_class)
    tile_start = tile_end - tiles_per_class
    n_tiles = tile_end[-1]
    tok = jnp.arange(T, dtype=jnp.int32)
    src_rows = jnp.zeros((n_tiles_max * TM_MOE,), jnp.int32).at[pos].set(tok)
    tile_ids = jnp.arange(n_tiles_max, dtype=jnp.int32)
    used = jnp.minimum(tile_ids, n_tiles - 1)
    tile_cls = jnp.sum((tile_end[None, :] <= used[:, None]).astype(jnp.int32), axis=1)
    tile_cls = jnp.minimum(tile_cls, N_CLASSES - 1)
    cls_lo, cls_hi = _class_experts()
    tile_elo = jnp.asarray(cls_lo)[tile_cls]
    tile_ehi = jnp.asarray(cls_hi)[tile_cls]
    within = tile_ids - tile_start[tile_cls]
    tile_cnt = jnp.clip(counts[tile_cls] - within * TM_MOE, 0, TM_MOE)
    tile_cnt = jnp.where(tile_ids < n_tiles, tile_cnt, 0).astype(jnp.int32)
    return src_rows, tile_elo, tile_ehi, tile_cnt, n_tiles.reshape(1).astype(jnp.int32)


def _vec(a):
    return a.reshape(1, -1).astype(F32)


def _mixer_and_cross_attention(x, mem, ln_in_g, ln_in_b, w_in, rel_bias, sink_a, norm_a_g, norm_b_g, w_out,
                               ln1_g, ln1_b, xq, xkv, xo, ln2_g, ln2_b, w_group, b_group, w_router, b_router):
    B, S, D = x.shape
    assert S == SEQ and D == D_MODEL and mem.shape[1:] == (MEM_LEN, D_MODEL)
    T = B * S
    vec = _vec

    w = w_in[0]
    edges = np.cumsum((0, A_WIDTH, A_KV_HEADS * HEAD_DIM, A_KV_HEADS * HEAD_DIM, B_WIDTH, B_WIDTH, B_WIDTH))
    qa, ka, va, qb, kb, vb = [w[:, a:b] for a, b in zip(edges[:-1], edges[1:])]
    dup = lambda t: jnp.repeat(t.reshape(D, A_KV_HEADS, 1, HEAD_DIM), 2, axis=2).reshape(D, A_KV_TILES * LANES)
    scale = HEAD_DIM ** -0.5 * LOG2E
    w_in_b = jnp.concatenate([qa * scale, dup(ka), dup(va), qb * scale, kb, vb], axis=1).astype(BF16)
    w_out_b = w_out[0].astype(BF16)
    xq_b = (xq[0] * (X_HEAD_DIM ** -0.5 * LOG2E)).astype(BF16)
    xkv_b = xkv[0].astype(BF16)
    xo_b = xo[0].astype(BF16)
    wr = jnp.concatenate([w_group[0], w_router[0]], axis=1).astype(F32)
    wr = jnp.pad(wr, ((0, 0), (0, LANES - wr.shape[1])))
    wr_hi = wr.astype(BF16)
    wr_lo = (wr - wr_hi.astype(F32)).astype(BF16)
    br = jnp.concatenate([b_group[0], b_router[0]]).astype(F32)
    br = jnp.pad(br, (0, LANES - br.shape[0])).reshape(1, LANES)

    h0, qkv_a, qkv_b1, qkv_b4, qkv_b16 = _input_projection(
        x.reshape(T, D), vec(ln_in_g), vec(ln_in_b), w_in_b, B)

    nk_a = TM_ATT + 2 * A_HALF_WIN
    bias_a = _band_bias(rel_bias[:, :A_HEADS], A_HALF_WIN, 1, TM_ATT, nk_a, S)
    ya = _band_attention(qkv_a.reshape(B, S, QKV_A), bias_a, nk=nk_a, sub=1, kv_of=(0, 0, 1, 1),
                         q_width=A_WIDTH, kv_width=A_KV_TILES * LANES, out_width=A_WIDTH, out_dtype=BF16,
                         sink=sink_a[0].astype(F32) * LOG2E, gain=vec(norm_a_g[0]))
    branch_out = []
    for (win, dil), qkv in zip(B_BRANCHES, (qkv_b1.reshape(B, S, QKV_B),
                                             qkv_b4.reshape(B * 4, S // 4, QKV_B),
                                             qkv_b16.reshape(B * 16, S // 16, QKV_B))):
        half = (win // 2) // dil
        n = S // dil
        nk = min(TM_ATT + 2 * half, n)
        bias_b = _band_bias(rel_bias[:, A_HEADS:], half, dil, TM_ATT, nk, n)
        branch_out.append(_band_attention(qkv, bias_b, nk=nk, sub=max(1, TM_ATT * 4 // n), kv_of=(0, 1, 2, 3),
                                          q_width=B_WIDTH, kv_width=B_WIDTH, out_width=B_OUT, out_dtype=F32))
    o1 = branch_out[0].reshape(T, B_OUT)
    o4 = branch_out[1].reshape(B, 4, S // 4, B_OUT)
    o16 = branch_out[2].reshape(B, 16, S // 16, B_OUT)

    h1 = _merge_project(ya.reshape(T, A_WIDTH), o1, o4, o16, h0, vec(norm_b_g[0]), w_out_b,
                        vec(ln1_g[0]), vec(ln1_b[0]))

    k_mem, v_mem = _mem_kv(mem, xkv_b)
    xr = _cross_attention_route(h1, k_mem, v_mem, xq_b, xo_b, vec(ln2_g[0]), vec(ln2_b[0]), wr_hi, wr_lo, br)
    return h0, h1, xr


def _moe(xr, w_gate, w_up, w_down, ln3_g, ln3_b):
    T = xr.shape[0]
    wg_b = w_gate[0].reshape(N_EXPERTS, D_MODEL, D_EXPERT).astype(BF16)
    wu_b = w_up[0].reshape(N_EXPERTS, D_MODEL, D_EXPERT).astype(BF16)
    wd_b = w_down[0].reshape(N_EXPERTS, D_EXPERT, D_MODEL).astype(BF16)
    n_tiles_max = T // TM_MOE + N_CLASSES
    src_rows, tile_elo, tile_ehi, tile_cnt, n_tiles = _moe_plan(xr, n_tiles_max)
    return _expert_mlp(src_rows, tile_cnt, tile_elo, tile_ehi, n_tiles, xr, wg_b, wu_b, wd_b,
                       _vec(ln3_g[0]), _vec(ln3_b[0]))


def kernel(x, mem, ln_in_g, ln_in_b, w_in, rel_bias, sink_a, norm_a_g, norm_b_g, w_out,
           ln1_g, ln1_b, xq, xkv, xo, ln2_g, ln2_b, w_group, b_group, w_router, b_router,
           w_gate, w_up, w_down, ln3_g, ln3_b):
    _, _, xr = _mixer_and_cross_attention(
        x, mem, ln_in_g, ln_in_b, w_in, rel_bias, sink_a, norm_a_g, norm_b_g, w_out,
        ln1_g, ln1_b, xq, xkv, xo, ln2_g, ln2_b, w_group, b_group, w_router, b_router)
    return _moe(xr, w_gate, w_up, w_down, ln3_g, ln3_b).reshape(x.shape)
```

```python
import functools

import numpy as np
import jax
import jax.numpy as jnp
from jax import lax
from jax.experimental import pallas as pl
from jax.experimental.pallas import tpu as pltpu

F32 = jnp.float32
BF16 = jnp.bfloat16

D_MODEL = 1024
SEQ = 2048
MEM_LEN = 256
HEAD_DIM = 64
A_HEADS = 8
A_KV_HEADS = 2
A_HALF_WIN = 128
B_HEADS = 8
B_BRANCHES = ((128, 1), (512, 4), (2048, 16))
N_BUCKETS = 32
MAX_DISTANCE = 1024
X_HEADS = 4
X_HEAD_DIM = D_MODEL // X_HEADS
N_GROUPS = 4
EXPERTS_PER_GROUP = 8
N_EXPERTS = N_GROUPS * EXPERTS_PER_GROUP
D_EXPERT = 512
DEPTH = 1
ALPHA = (2.0 * DEPTH) ** 0.25
LN_EPS = 1e-5
NEG = -1e30
LOG2E = 1.4426950408889634
LN2 = 0.6931471805599453

LANES = 128
XR_WIDTH = D_MODEL + LANES
A_WIDTH = A_HEADS * HEAD_DIM
B_WIDTH = B_HEADS * HEAD_DIM
A_KV_TILES = A_KV_HEADS
QKV_A = A_WIDTH + 2 * A_KV_TILES * LANES
QKV_B = 3 * B_WIDTH
B_OUT = 2 * B_WIDTH
MERGE_STRIDE = 4

PAIRS_PER_GROUP = EXPERTS_PER_GROUP * (EXPERTS_PER_GROUP - 1) // 2
N_CLASSES = N_GROUPS * PAIRS_PER_GROUP

TM_IN = 512
TM_ATT = 128
TM_MERGE = 256
TM_X = 256
TM_MOE = 128
ROW_GROUP = 8
TM_PLAN = 1024
VMEM_LIMIT = 56 * 1024 * 1024


def _cparams(*sem):
    return pltpu.CompilerParams(dimension_semantics=sem, vmem_limit_bytes=VMEM_LIMIT)


def _layer_norm(x, g, b):
    mu = jnp.mean(x, axis=-1, keepdims=True)
    xc = x - mu
    var = jnp.mean(xc * xc, axis=-1, keepdims=True)
    return xc * lax.rsqrt(var + LN_EPS) * g + b


def _rms_norm(x, g):
    return x * lax.rsqrt(jnp.mean(x * x, axis=-1, keepdims=True) + LN_EPS) * g


def _t5_bucket(rel):
    nb = N_BUCKETS // 2
    max_exact = nb // 2
    ret = (rel > 0).astype(np.int32) * nb
    n = np.abs(rel)
    n_safe = np.maximum(n, 1).astype(np.float64)
    large = max_exact + (np.log(n_safe / max_exact) / np.log(MAX_DISTANCE / max_exact)
                         * (nb - max_exact)).astype(np.int32)
    large = np.minimum(large, nb - 1)
    return (ret + np.where(n < max_exact, n, large)).astype(np.int32)


def _inproj_kernel(x_ref, g_ref, b_ref, w_ref, h0_ref, qa_ref, qb1_ref, qb4_ref, qb16_ref, pb_scr, p4_scr):
    h = _layer_norm(x_ref[...], g_ref[...], b_ref[...])
    for c in range(D_MODEL // LANES):
        h0_ref[0, c] = h[:, c * LANES:(c + 1) * LANES]
    proj = jnp.dot(h.astype(BF16), w_ref[...], preferred_element_type=F32)
    qa_ref[...] = proj[:, :QKV_A].astype(BF16)
    pb = proj[:, QKV_A:]
    qb1_ref[...] = pb.astype(BF16)
    n4 = TM_IN // 4
    for c in range(QKV_B // LANES):
        cs = slice(c * LANES, (c + 1) * LANES)
        pb_scr[c] = pb[:, cs]
        for r4 in range(4):
            rows = pb_scr[c, pl.ds(r4, n4, stride=4), :]
            qb4_ref[0, r4, :, cs] = rows.astype(BF16)
            p4_scr[c, r4 * n4:(r4 + 1) * n4, :] = rows
        for r16 in range(16):
            rows = p4_scr[c, pl.ds((r16 % 4) * n4 + r16 // 4, TM_IN // 16, stride=4), :]
            qb16_ref[0, r16, :, cs] = rows.astype(BF16)


def _input_projection(x2, ln_g, ln_b, w_in_b, batch):
    T = x2.shape[0]
    tiles_per_seq = SEQ // TM_IN
    row = lambda i: (i, 0)
    const = lambda i: (0, 0)
    deint = lambda i: (i // tiles_per_seq, 0, i % tiles_per_seq, 0)
    return pl.pallas_call(
        _inproj_kernel,
        grid=(T // TM_IN,),
        in_specs=[
            pl.BlockSpec((TM_IN, D_MODEL), row),
            pl.BlockSpec((1, D_MODEL), const),
            pl.BlockSpec((1, D_MODEL), const),
            pl.BlockSpec((D_MODEL, QKV_A + QKV_B), const),
        ],
        out_specs=[
            pl.BlockSpec((1, D_MODEL // LANES, TM_IN, LANES), deint),
            pl.BlockSpec((TM_IN, QKV_A), row),
            pl.BlockSpec((TM_IN, QKV_B), row),
            pl.BlockSpec((1, 4, TM_IN // 4, QKV_B), deint),
            pl.BlockSpec((1, 16, TM_IN // 16, QKV_B), deint),
        ],
        out_shape=[
            jax.ShapeDtypeStruct((batch, D_MODEL // LANES, SEQ, LANES), F32),
            jax.ShapeDtypeStruct((T, QKV_A), BF16),
            jax.ShapeDtypeStruct((T, QKV_B), BF16),
            jax.ShapeDtypeStruct((batch, 4, SEQ // 4, QKV_B), BF16),
            jax.ShapeDtypeStruct((batch, 16, SEQ // 16, QKV_B), BF16),
        ],
        scratch_shapes=[pltpu.VMEM((QKV_B // LANES, TM_IN, LANES), F32)] * 2,
        compiler_params=_cparams("parallel"),
        name="ln_in_proj",
    )(x2, ln_g, ln_b, w_in_b)


def _band_attn_kernel(*refs, m, nk, n, sub, kv_of, with_sink):
    if with_sink:
        sink_ref, gain_ref = refs[0], refs[1]
        refs = refs[2:]
    q_ref, k_ref, v_ref, bias_ref, o_ref = refs
    if nk == n:
        start = 0
    else:
        j = pl.program_id(1)
        start = pl.multiple_of(jnp.clip(j * m - (nk - m) // 2, 0, n - nk), HEAD_DIM)
    lane_row = lax.broadcasted_iota(jnp.int32, (1, LANES), 1)
    keep_lo = jnp.where(lane_row < HEAD_DIM, 1.0, 0.0).astype(BF16)
    keep_hi = jnp.where(lane_row < HEAD_DIM, 0.0, 1.0).astype(BF16)
    lane = lax.broadcasted_iota(jnp.int32, (m, LANES), 1)
    n_q = len(kv_of)
    units = [(s, qt) for s in range(sub) for qt in range(n_q)]

    scores = []
    for s, qt in units:
        t = kv_of[qt]
        k_t = k_ref[s, pl.ds(start, nk), t * LANES:(t + 1) * LANES]
        q2 = q_ref[s, :, qt * LANES:(qt + 1) * LANES]
        lhs = jnp.concatenate([q2 * keep_lo, q2 * keep_hi], axis=0)
        sc = lax.dot_general(lhs, k_t, (((1,), (1,)), ((), ())), preferred_element_type=F32)
        scores.append(sc + bias_ref[0, qt])

    probs, denom, row_max = [], [], []
    for (s, qt), sc in zip(units, scores):
        mx = jnp.max(sc, axis=-1, keepdims=True)
        p = jnp.exp2(sc - mx)
        l = jnp.sum(p, axis=-1, keepdims=True)
        halves = []
        for h in (0, 1):
            l_h = l[h * m:(h + 1) * m]
            if with_sink:
                l_h = l_h + jnp.exp2(sink_ref[2 * qt + h] - mx[h * m:(h + 1) * m])
            halves.append(l_h)
        probs.append(p.astype(BF16))
        denom.append(halves)
        row_max.append(mx)

    for s in range(sub):
        pairs = []
        for qt in range(n_q):
            u = s * n_q + qt
            t = kv_of[qt]
            v_t = v_ref[s, pl.ds(start, nk), t * LANES:(t + 1) * LANES]
            o = jnp.dot(probs[u], v_t, preferred_element_type=F32)
            top = o[:m] * (1.0 / denom[u][0])
            bot = o[m:] * (1.0 / denom[u][1])
            pairs.append(jnp.where(lane < HEAD_DIM, top, bot))
            if not with_sink:
                lse = [(row_max[u][h * m:(h + 1) * m] + jnp.log2(denom[u][h])) * LN2 for h in (0, 1)]
                o_ref[s, n_q + qt] = jnp.where(lane < HEAD_DIM, lse[0], lse[1])
        if with_sink:
            normed = _rms_norm(jnp.concatenate(pairs, axis=1), gain_ref[...])
            pairs = [normed[:, c * LANES:(c + 1) * LANES] for c in range(n_q)]
        for c in range(n_q):
            o_ref[s, c] = pairs[c]


def _band_window(m, nk, n):
    nb = n // m
    starts = np.clip(np.arange(nb) * m - (nk - m) // 2, 0, n - nk)
    offs = [int(o) for o in starts - np.arange(nb) * m]
    uniq = sorted(set(offs), reverse=True)
    var = [uniq.index(o) for o in offs]
    assert all(v == var[1] for v in var[1:-1])
    return uniq, var


def _band_bias(rel_bias_h, half, dil, m, nk, n):
    uniq, _ = _band_window(m, nk, n)
    n_heads = rel_bias_h.shape[1]
    span = nk + m - 1
    out = []
    for off in uniq:
        rel = off - (m - 1) + np.arange(span)
        onehot = np.zeros((span, N_BUCKETS), np.float32)
        onehot[np.arange(span), _t5_bucket(dil * rel)] = 1.0
        table = jnp.dot(jnp.asarray(onehot), rel_bias_h.astype(F32), precision=lax.Precision.HIGHEST)
        table = jnp.where((np.abs(rel) <= half)[:, None], table, NEG)
        u = jnp.concatenate([table.T, jnp.zeros((n_heads, 1), F32)], axis=1)
        flat = jnp.tile(u, (1, m + 1))[:, m - 1:m - 1 + m * span]
        b = flat.reshape(n_heads, m, span)[:, :, :nk]
        out.append(b.reshape(n_heads // 2, 2 * m, nk))
    return jnp.stack(out) * LOG2E


def _band_attention(qkv, bias, *, nk, sub, kv_of, q_width, kv_width, out_tiles, sink=None, gain=None):
    Bd, n, _ = qkv.shape
    m = TM_ATT
    nb = n // m
    assert nb * m == n and Bd % sub == 0
    _, var = _band_window(m, nk, n)
    kcol = q_width // kv_width
    with_sink = sink is not None

    def variant(b, j):
        v = jnp.where(j == nb - 1, var[-1], var[min(1, nb - 1)])
        return jnp.where(j == 0, var[0], v)

    in_specs = [
        pl.BlockSpec((sub, m, q_width), lambda b, j: (b, j, 0)),
        pl.BlockSpec((sub, n, kv_width), lambda b, j: (b, 0, kcol)),
        pl.BlockSpec((sub, n, kv_width), lambda b, j: (b, 0, kcol + 1)),
        pl.BlockSpec((1, len(kv_of), 2 * m, nk), lambda b, j: (variant(b, j), 0, 0, 0)),
    ]
    args = [qkv, qkv, qkv, bias]
    if with_sink:
        in_specs = [pl.BlockSpec(memory_space=pltpu.SMEM),
                    pl.BlockSpec((1, q_width), lambda b, j: (0, 0))] + in_specs
        args = [sink, gain] + args
    return pl.pallas_call(
        functools.partial(_band_attn_kernel, m=m, nk=nk, n=n, sub=sub, kv_of=kv_of, with_sink=with_sink),
        grid=(Bd // sub, nb),
        in_specs=in_specs,
        out_specs=pl.BlockSpec((sub, out_tiles, m, LANES), lambda b, j: (b, 0, j, 0)),
        out_shape=jax.ShapeDtypeStruct((Bd, out_tiles, n, LANES), F32),
        compiler_params=_cparams("parallel", "arbitrary"),
        name="band_attn_sink" if with_sink else f"band_attn_n{n}",
    )(*args)


def _merge_kernel(ya_ref, o1_ref, o4_ref, o16_ref, h0_ref, gb_ref, w_ref, g1_ref, b1_ref, h1_ref, s16_scr):
    n4 = TM_MERGE // MERGE_STRIDE
    n16 = TM_MERGE // 16
    for r16 in range(16):
        for c in range(B_OUT // LANES):
            s16_scr[c, pl.ds((r16 % 4) * n4 + r16 // 4, n16, stride=4), :] = o16_ref[0, r16, c]

    def natural(ref, c):
        return jnp.concatenate([ref[0, c, pl.ds(r, n4, stride=MERGE_STRIDE), :]
                                for r in range(MERGE_STRIDE)], axis=0)

    def stride4(c):
        return jnp.concatenate([o4_ref[0, r, c] for r in range(MERGE_STRIDE)], axis=0)

    def stride16(c):
        return s16_scr[c]

    n_val = B_WIDTH // LANES
    pieces = []
    for c in range(n_val):
        outs = (natural(o1_ref, c), stride4(c), stride16(c))
        lses = (natural(o1_ref, n_val + c), stride4(n_val + c), stride16(n_val + c))
        mx = jnp.maximum(jnp.maximum(lses[0], lses[1]), lses[2])
        ex = [jnp.exp(l - mx) for l in lses]
        acc = ex[0] * outs[0] + ex[1] * outs[1] + ex[2] * outs[2]
        pieces.append(acc * (1.0 / (ex[0] + ex[1] + ex[2])))
    yb = _rms_norm(jnp.concatenate(pieces, axis=1), gb_ref[...])
    ya = jnp.concatenate([natural(ya_ref, c) for c in range(A_WIDTH // LANES)], axis=1)
    y = jnp.concatenate([ya.astype(BF16), yb.astype(BF16)], axis=1)
    mix = jnp.dot(y, w_ref[...], preferred_element_type=F32)
    h0 = jnp.concatenate([natural(h0_ref, c) for c in range(D_MODEL // LANES)], axis=1)
    h1_ref[...] = _layer_norm(ALPHA * h0 + mix, g1_ref[...], b1_ref[...])


def _token_of_row(p):
    n16 = TM_MERGE // MERGE_STRIDE
    tile, rest = p // TM_MERGE, p % TM_MERGE
    return tile * TM_MERGE + (rest % n16) * MERGE_STRIDE + rest // n16


def _merge_project(ya, o1, o4, o16, h0, gain_b, w_out_b, ln_g, ln_b):
    B = h0.shape[0]
    T = B * SEQ
    tiles_per_seq = SEQ // TM_MERGE
    const = lambda i: (0, 0)
    nat = lambda i: (i // tiles_per_seq, 0, i % tiles_per_seq, 0)
    deint = lambda i: (i // tiles_per_seq, 0, 0, i % tiles_per_seq, 0)
    n_b = B_OUT // LANES
    return pl.pallas_call(
        _merge_kernel,
        grid=(T // TM_MERGE,),
        in_specs=[
            pl.BlockSpec((1, A_WIDTH // LANES, TM_MERGE, LANES), nat),
            pl.BlockSpec((1, n_b, TM_MERGE, LANES), nat),
            pl.BlockSpec((1, 4, n_b, TM_MERGE // 4, LANES), deint),
            pl.BlockSpec((1, 16, n_b, TM_MERGE // 16, LANES), deint),
            pl.BlockSpec((1, D_MODEL // LANES, TM_MERGE, LANES), nat),
            pl.BlockSpec((1, B_WIDTH), const),
            pl.BlockSpec((D_MODEL, D_MODEL), const),
            pl.BlockSpec((1, D_MODEL), const),
            pl.BlockSpec((1, D_MODEL), const),
        ],
        out_specs=pl.BlockSpec((TM_MERGE, D_MODEL), lambda i: (i, 0)),
        out_shape=jax.ShapeDtypeStruct((T, D_MODEL), F32),
        scratch_shapes=[pltpu.VMEM((n_b, TM_MERGE, LANES), F32)],
        compiler_params=_cparams("parallel"),
        name="merge_out_proj",
    )(ya, o1, o4, o16, h0, gain_b, w_out_b, ln_g, ln_b)


def _mem_kv_kernel(mem_ref, w_ref, k_ref, v_ref):
    kv = jnp.dot(mem_ref[0].astype(BF16), w_ref[...], preferred_element_type=F32)
    k_ref[0] = kv[:, :D_MODEL].astype(BF16)
    v_ref[0] = kv[:, D_MODEL:].astype(BF16)


def _mem_kv(mem, xkv_b):
    B = mem.shape[0]
    blk = pl.BlockSpec((1, MEM_LEN, D_MODEL), lambda b: (b, 0, 0))
    return pl.pallas_call(
        _mem_kv_kernel,
        grid=(B,),
        in_specs=[blk, pl.BlockSpec((D_MODEL, 2 * D_MODEL), lambda b: (0, 0))],
        out_specs=[blk, blk],
        out_shape=[jax.ShapeDtypeStruct((B, MEM_LEN, D_MODEL), BF16)] * 2,
        compiler_params=_cparams("parallel"),
        name="mem_kv_proj",
    )(mem, xkv_b)


def _route(logits):
    rows = logits.shape[0]
    lane = lax.broadcasted_iota(jnp.int32, (rows, LANES), 1).astype(F32)
    big = float(LANES)
    ninf = -jnp.inf
    gl = jnp.where(lane < N_GROUPS, logits, ninf)
    gmax = jnp.max(gl, axis=-1, keepdims=True)
    gidx = jnp.min(jnp.where(gl == gmax, lane, big), axis=-1, keepdims=True)
    g_p = 1.0 / jnp.sum(jnp.exp(gl - gmax), axis=-1, keepdims=True)
    lo_lane = N_GROUPS + EXPERTS_PER_GROUP * gidx
    el = jnp.where((lane >= lo_lane) & (lane < lo_lane + EXPERTS_PER_GROUP), logits, ninf)
    v1 = jnp.max(el, axis=-1, keepdims=True)
    i1 = jnp.min(jnp.where(el == v1, lane, big), axis=-1, keepdims=True)
    el2 = jnp.where(lane == i1, ninf, el)
    v2 = jnp.max(el2, axis=-1, keepdims=True)
    i2 = jnp.min(jnp.where(el2 == v2, lane, big), axis=-1, keepdims=True)
    t = jnp.exp(v2 - v1)
    w1 = g_p / (1.0 + t)
    w2 = g_p * t / (1.0 + t)
    a = jnp.minimum(i1, i2) - lo_lane
    b = jnp.maximum(i1, i2) - lo_lane
    pair = a * (2 * EXPERTS_PER_GROUP - 1 - a) * 0.5 + (b - a - 1.0)
    cls = gidx * PAIRS_PER_GROUP + pair
    w_lo = jnp.where(i1 < i2, w1, w2)
    w_hi = jnp.where(i1 < i2, w2, w1)
    return jnp.where(lane == 0, cls, jnp.where(lane == 1, w_lo, jnp.where(lane == 2, w_hi, 0.0)))


def _xattn_kernel(h1_ref, k_ref, v_ref, wq_ref, wo_ref, g2_ref, b2_ref, wr_hi_ref, wr_lo_ref, br_ref, xr_ref):
    h1 = h1_ref[...]
    q = jnp.dot(h1.astype(BF16), wq_ref[...], preferred_element_type=F32).astype(BF16)
    outs = []
    for hd in range(X_HEADS):
        sl = slice(hd * X_HEAD_DIM, (hd + 1) * X_HEAD_DIM)
        s = lax.dot_general(q[:, sl], k_ref[0, :, sl], (((1,), (1,)), ((), ())), preferred_element_type=F32)
        m = jnp.max(s, axis=-1, keepdims=True)
        p = jnp.exp2(s - m)
        l = jnp.sum(p, axis=-1, keepdims=True)
        o = jnp.dot(p.astype(BF16), v_ref[0, :, sl], preferred_element_type=F32) / l
        outs.append(o.astype(BF16))
    xa = jnp.dot(jnp.concatenate(outs, axis=1), wo_ref[...], preferred_element_type=F32)
    h2 = _layer_norm(ALPHA * h1 + xa, g2_ref[...], b2_ref[...])
    h_hi = h2.astype(BF16)
    h_lo = (h2 - h_hi.astype(F32)).astype(BF16)
    logits = (jnp.dot(h_hi, wr_hi_ref[...], preferred_element_type=F32)
              + jnp.dot(h_hi, wr_lo_ref[...], preferred_element_type=F32)
              + jnp.dot(h_lo, wr_hi_ref[...], preferred_element_type=F32)) + br_ref[...]
    xr_ref[:, :D_MODEL] = h2
    xr_ref[:, D_MODEL:] = _route(logits)


def _cross_attention_route(h1, k, v, xq_b, xo_b, ln_g, ln_b, wr_hi, wr_lo, br):
    T = h1.shape[0]
    tiles_per_seq = SEQ // TM_X
    row = lambda i: (i, 0)
    const = lambda i: (0, 0)
    kv_blk = pl.BlockSpec((1, MEM_LEN, D_MODEL), lambda i: (i // tiles_per_seq, 0, 0))
    return pl.pallas_call(
        _xattn_kernel,
        grid=(T // TM_X,),
        in_specs=[
            pl.BlockSpec((TM_X, D_MODEL), row), kv_blk, kv_blk,
            pl.BlockSpec((D_MODEL, D_MODEL), const),
            pl.BlockSpec((D_MODEL, D_MODEL), const),
            pl.BlockSpec((1, D_MODEL), const),
            pl.BlockSpec((1, D_MODEL), const),
            pl.BlockSpec((D_MODEL, LANES), const),
            pl.BlockSpec((D_MODEL, LANES), const),
            pl.BlockSpec((1, LANES), const),
        ],
        out_specs=pl.BlockSpec((TM_X, XR_WIDTH), row),
        out_shape=jax.ShapeDtypeStruct((T, XR_WIDTH), F32),
        compiler_params=_cparams("parallel"),
        name="xattn_ln_route",
    )(h1, k, v, xq_b, xo_b, ln_g, ln_b, wr_hi, wr_lo, br)


def _expert_kernel(src_ref, dst_ref, cnt_ref, elo_ref, ehi_ref, ntile_ref, x_hbm,
                   wg_lo, wu_lo, wd_lo, wg_hi, wu_hi, wd_hi, g3_ref, b3_ref, o_hbm,
                   xbuf, obuf, gsem, ssem):
    j = pl.program_id(0)
    n_tiles = ntile_ref[0]
    slot = j % 2

    def row_copy(t, i, s, gather):
        row = pl.ds(i, 1)
        if gather:
            return pltpu.make_async_copy(x_hbm.at[pl.ds(src_ref[t * TM_MOE + i], 1)], xbuf.at[s, row], gsem.at[s])
        return pltpu.make_async_copy(obuf.at[s, row], o_hbm.at[pl.ds(dst_ref[t * TM_MOE + i], 1)], ssem.at[s])

    def n_rows(t, gather):
        cnt = cnt_ref[t]
        return (cnt + ROW_GROUP - 1) // ROW_GROUP * ROW_GROUP if gather else cnt

    def start_rows(t, s, gather):
        cnt = cnt_ref[t]
        n_groups = (cnt + ROW_GROUP - 1) // ROW_GROUP if gather else cnt // ROW_GROUP

        def group(g, c):
            base = pl.multiple_of(g * ROW_GROUP, ROW_GROUP)
            for r in range(ROW_GROUP):
                row_copy(t, base + r, s, gather).start()
            return c
        lax.fori_loop(0, n_groups, group, 0)
        if not gather:
            def single(i, c):
                row_copy(t, i, s, gather).start()
                return c
            lax.fori_loop(n_groups * ROW_GROUP, cnt, single, 0)

    def wait_rows(t, s, gather):
        cnt = n_rows(t, gather)
        for bit in range(TM_MOE.bit_length()):
            rows = pl.ds(0, 1 << bit)

            @pl.when((cnt >> bit) & 1 == 1)
            def _():
                if gather:
                    pltpu.make_async_copy(x_hbm.at[rows], xbuf.at[s, rows], gsem.at[s]).wait()
                else:
                    pltpu.make_async_copy(obuf.at[s, rows], o_hbm.at[rows], ssem.at[s]).wait()

    @pl.when(j == 0)
    def _():
        xbuf[...] = jnp.zeros_like(xbuf)
        start_rows(0, 0, True)

    @pl.when(j < n_tiles)
    def _():
        @pl.when(j + 1 < n_tiles)
        def _():
            start_rows(j + 1, 1 - slot, True)

        wait_rows(j, slot, True)
        x = xbuf[slot, :, :D_MODEL]
        xb = x.astype(BF16)
        y = jnp.zeros_like(x)
        for e, (wg, wu, wd) in enumerate(((wg_lo, wu_lo, wd_lo), (wg_hi, wu_hi, wd_hi))):
            gate = xbuf[slot, :, D_MODEL + 1 + e:D_MODEL + 2 + e]
            a = jnp.dot(xb, wg[0], preferred_element_type=F32)
            u = jnp.dot(xb, wu[0], preferred_element_type=F32)
            hid = a * jax.nn.sigmoid(a) * u
            y = y + jnp.dot((gate * hid).astype(BF16), wd[0], preferred_element_type=F32)
        out = _layer_norm(ALPHA * x + y, g3_ref[...], b3_ref[...])

        @pl.when(j >= 2)
        def _():
            wait_rows(j - 2, slot, False)

        obuf[slot] = out
        start_rows(j, slot, False)

        @pl.when(j == n_tiles - 1)
        def _():
            @pl.when(j >= 1)
            def _():
                wait_rows(j - 1, 1 - slot, False)

            wait_rows(j, slot, False)


def _expert_mlp(src_rows, dst_rows, tile_cnt, tile_elo, tile_ehi, n_tiles, xr, wg, wu, wd, ln_g, ln_b):
    n_tiles_max = tile_cnt.shape[0]
    const = lambda j, src, dst, cnt, elo, ehi, nt: (0, 0)
    lo = lambda j, src, dst, cnt, elo, ehi, nt: (elo[j], 0, 0)
    hi = lambda j, src, dst, cnt, elo, ehi, nt: (ehi[j], 0, 0)
    up = (1, D_MODEL, D_EXPERT)
    down = (1, D_EXPERT, D_MODEL)
    any_spec = pl.BlockSpec(memory_space=pl.ANY)
    return pl.pallas_call(
        _expert_kernel,
        grid_spec=pltpu.PrefetchScalarGridSpec(
            num_scalar_prefetch=6,
            grid=(n_tiles_max,),
            in_specs=[
                any_spec,
                pl.BlockSpec(up, lo), pl.BlockSpec(up, lo), pl.BlockSpec(down, lo),
                pl.BlockSpec(up, hi), pl.BlockSpec(up, hi), pl.BlockSpec(down, hi),
                pl.BlockSpec((1, D_MODEL), const),
                pl.BlockSpec((1, D_MODEL), const),
            ],
            out_specs=any_spec,
            scratch_shapes=[
                pltpu.VMEM((2, TM_MOE, XR_WIDTH), F32),
                pltpu.VMEM((2, TM_MOE, D_MODEL), F32),
                pltpu.SemaphoreType.DMA((2,)),
                pltpu.SemaphoreType.DMA((2,)),
            ],
        ),
        out_shape=jax.ShapeDtypeStruct((xr.shape[0], D_MODEL), F32),
        compiler_params=_cparams("arbitrary"),
        name="moe_experts",
    )(src_rows, dst_rows, tile_cnt, tile_elo, tile_ehi, n_tiles, xr, wg, wu, wd, wg, wu, wd, ln_g, ln_b)


def _class_experts():
    lo, hi = [], []
    for g in range(N_GROUPS):
        for a in range(EXPERTS_PER_GROUP):
            for b in range(a + 1, EXPERTS_PER_GROUP):
                lo.append(g * EXPERTS_PER_GROUP + a)
                hi.append(g * EXPERTS_PER_GROUP + b)
    return np.asarray(lo, np.int32), np.asarray(hi, np.int32)


def _plan_kernel(route_ref, pos_ref, counts_ref, tri_scr, run_scr, start_scr):
    phase, i = pl.program_id(0), pl.program_id(1)
    lane = lax.broadcasted_iota(jnp.int32, (TM_PLAN, LANES), 1)
    onehot = lane.astype(F32) == route_ref[:, 0:1]
    onehot_f = jnp.where(onehot, 1.0, 0.0)

    @pl.when((phase == 0) & (i == 0))
    def _():
        run_scr[...] = jnp.zeros_like(run_scr)
        r = lax.broadcasted_iota(jnp.int32, (TM_PLAN, TM_PLAN), 0)
        c = lax.broadcasted_iota(jnp.int32, (TM_PLAN, TM_PLAN), 1)
        tri_scr[...] = jnp.where(c < r, 1.0, 0.0).astype(BF16)

    @pl.when((phase == 1) & (i == 0))
    def _():
        counts = run_scr[...]
        counts_ref[...] = counts
        tiles = jnp.floor((counts + (TM_MOE - 1)) * (1.0 / TM_MOE))
        lane_row = lax.broadcasted_iota(jnp.int32, (1, LANES), 1)
        scan = tiles
        shift = 1
        while shift < LANES:
            scan = scan + jnp.where(lane_row >= shift, pltpu.roll(scan, shift, axis=1), 0.0)
            shift *= 2
        start_scr[...] = (scan - tiles) * TM_MOE
        run_scr[...] = jnp.zeros_like(run_scr)

    @pl.when(phase == 1)
    def _():
        before = jnp.dot(tri_scr[...], onehot_f.astype(BF16), preferred_element_type=F32)
        pos_col = jnp.sum(onehot_f * (before + run_scr[...] + start_scr[...]), axis=1, keepdims=True)
        eye = (lax.broadcasted_iota(jnp.int32, (LANES, LANES), 0)
               == lax.broadcasted_iota(jnp.int32, (LANES, LANES), 1))
        for r in range(TM_PLAN // LANES):
            row = jnp.sum(jnp.where(eye, pos_col[r * LANES:(r + 1) * LANES], 0.0), axis=0, keepdims=True)
            pos_ref[r:r + 1, :] = row.astype(jnp.int32)

    run_scr[...] = run_scr[...] + jnp.sum(onehot_f, axis=0, keepdims=True)


def _plan_positions(xr):
    T = xr.shape[0]
    n_steps = T // TM_PLAN
    pos, counts = pl.pallas_call(
        _plan_kernel,
        grid=(2, n_steps),
        in_specs=[pl.BlockSpec((TM_PLAN, LANES), lambda p, i: (i, D_MODEL // LANES))],
        out_specs=[pl.BlockSpec((TM_PLAN // LANES, LANES), lambda p, i: (i * p, 0)),
                   pl.BlockSpec((1, LANES), lambda p, i: (0, 0))],
        out_shape=[jax.ShapeDtypeStruct((T // LANES, LANES), jnp.int32),
                   jax.ShapeDtypeStruct((1, LANES), F32)],
        scratch_shapes=[pltpu.VMEM((TM_PLAN, TM_PLAN), BF16),
                        pltpu.VMEM((1, LANES), F32),
                        pltpu.VMEM((1, LANES), F32)],
        compiler_params=_cparams("arbitrary", "arbitrary"),
        name="moe_plan",
    )(xr)
    return pos.reshape(T), counts[0, :N_CLASSES].astype(jnp.int32)


def _moe_plan(xr, n_tiles_max):
    T = xr.shape[0]
    pos, counts = _plan_positions(xr)
    tiles_per_class = (counts + TM_MOE - 1) // TM_MOE
    tile_end = jnp.cumsum(tiles_per_class)
    tile_start = tile_end - tiles_per_class
    n_tiles = tile_end[-1]
    tok = jnp.arange(T, dtype=jnp.int32)
    src_rows = jnp.zeros((n_tiles_max * TM_MOE,), jnp.int32).at[pos].set(tok)
    tile_ids = jnp.arange(n_tiles_max, dtype=jnp.int32)
    used = jnp.minimum(tile_ids, n_tiles - 1)
    tile_cls = jnp.sum((tile_end[None, :] <= used[:, None]).astype(jnp.int32), axis=1)
    tile_cls = jnp.minimum(tile_cls, N_CLASSES - 1)
    cls_lo, cls_hi = _class_experts()
    tile_elo = jnp.asarray(cls_lo)[tile_cls]
    tile_ehi = jnp.asarray(cls_hi)[tile_cls]
    within = tile_ids - tile_start[tile_cls]
    tile_cnt = jnp.clip(counts[tile_cls] - within * TM_MOE, 0, TM_MOE)
    tile_cnt = jnp.where(tile_ids < n_tiles, tile_cnt, 0).astype(jnp.int32)
    return src_rows, tile_elo, tile_ehi, tile_cnt, n_tiles.reshape(1).astype(jnp.int32)


def _vec(a):
    return a.reshape(1, -1).astype(F32)


def _mixer_and_cross_attention(x, mem, ln_in_g, ln_in_b, w_in, rel_bias, sink_a, norm_a_g, norm_b_g, w_out,
                               ln1_g, ln1_b, xq, xkv, xo, ln2_g, ln2_b, w_group, b_group, w_router, b_router):
    B, S, D = x.shape
    assert S == SEQ and D == D_MODEL and mem.shape[1:] == (MEM_LEN, D_MODEL)
    T = B * S
    vec = _vec

    w = w_in[0]
    edges = np.cumsum((0, A_WIDTH, A_KV_HEADS * HEAD_DIM, A_KV_HEADS * HEAD_DIM, B_WIDTH, B_WIDTH, B_WIDTH))
    qa, ka, va, qb, kb, vb = [w[:, a:b] for a, b in zip(edges[:-1], edges[1:])]
    dup = lambda t: jnp.repeat(t.reshape(D, A_KV_HEADS, 1, HEAD_DIM), 2, axis=2).reshape(D, A_KV_TILES * LANES)
    scale = HEAD_DIM ** -0.5 * LOG2E
    w_in_b = jnp.concatenate([qa * scale, dup(ka), dup(va), qb * scale, kb, vb], axis=1).astype(BF16)
    w_out_b = w_out[0].astype(BF16)
    xq_b = (xq[0] * (X_HEAD_DIM ** -0.5 * LOG2E)).astype(BF16)
    xkv_b = xkv[0].astype(BF16)
    xo_b = xo[0].astype(BF16)
    wr = jnp.concatenate([w_group[0], w_router[0]], axis=1).astype(F32)
    wr = jnp.pad(wr, ((0, 0), (0, LANES - wr.shape[1])))
    wr_hi = wr.astype(BF16)
    wr_lo = (wr - wr_hi.astype(F32)).astype(BF16)
    br = jnp.concatenate([b_group[0], b_router[0]]).astype(F32)
    br = jnp.pad(br, (0, LANES - br.shape[0])).reshape(1, LANES)

    h0, qkv_a, qkv_b1, qkv_b4, qkv_b16 = _input_projection(
        x.reshape(T, D), vec(ln_in_g), vec(ln_in_b), w_in_b, B)

    nk_a = TM_ATT + 2 * A_HALF_WIN
    bias_a = _band_bias(rel_bias[:, :A_HEADS], A_HALF_WIN, 1, TM_ATT, nk_a, S)
    ya = _band_attention(qkv_a.reshape(B, S, QKV_A), bias_a, nk=nk_a, sub=1, kv_of=(0, 0, 1, 1),
                         q_width=A_WIDTH, kv_width=A_KV_TILES * LANES, out_tiles=A_WIDTH // LANES,
                         sink=sink_a[0].astype(F32) * LOG2E, gain=vec(norm_a_g[0]))
    branch_out = []
    for (win, dil), qkv in zip(B_BRANCHES, (qkv_b1.reshape(B, S, QKV_B),
                                             qkv_b4.reshape(B * 4, S // 4, QKV_B),
                                             qkv_b16.reshape(B * 16, S // 16, QKV_B))):
        half = (win // 2) // dil
        n = S // dil
        nk = min(TM_ATT + 2 * half, n)
        bias_b = _band_bias(rel_bias[:, A_HEADS:], half, dil, TM_ATT, nk, n)
        branch_out.append(_band_attention(qkv, bias_b, nk=nk, sub=max(1, TM_ATT * 4 // n), kv_of=(0, 1, 2, 3),
                                          q_width=B_WIDTH, kv_width=B_WIDTH, out_tiles=B_OUT // LANES))
    o1 = branch_out[0]
    o4 = branch_out[1].reshape(B, 4, B_OUT // LANES, S // 4, LANES)
    o16 = branch_out[2].reshape(B, 16, B_OUT // LANES, S // 16, LANES)

    h1 = _merge_project(ya, o1, o4, o16, h0, vec(norm_b_g[0]), w_out_b,
                        vec(ln1_g[0]), vec(ln1_b[0]))

    k_mem, v_mem = _mem_kv(mem, xkv_b)
    xr = _cross_attention_route(h1, k_mem, v_mem, xq_b, xo_b, vec(ln2_g[0]), vec(ln2_b[0]), wr_hi, wr_lo, br)
    return h0, h1, xr


def _moe(xr, w_gate, w_up, w_down, ln3_g, ln3_b):
    T = xr.shape[0]
    wg_b = w_gate[0].reshape(N_EXPERTS, D_MODEL, D_EXPERT).astype(BF16)
    wu_b = w_up[0].reshape(N_EXPERTS, D_MODEL, D_EXPERT).astype(BF16)
    wd_b = w_down[0].reshape(N_EXPERTS, D_EXPERT, D_MODEL).astype(BF16)
    n_tiles_max = T // TM_MOE + N_CLASSES
    src_rows, tile_elo, tile_ehi, tile_cnt, n_tiles = _moe_plan(xr, n_tiles_max)
    return _expert_mlp(src_rows, _token_of_row(src_rows), tile_cnt, tile_elo, tile_ehi, n_tiles, xr, wg_b, wu_b, wd_b,
                       _vec(ln3_g[0]), _vec(ln3_b[0]))


def kernel(x, mem, ln_in_g, ln_in_b, w_in, rel_bias, sink_a, norm_a_g, norm_b_g, w_out,
           ln1_g, ln1_b, xq, xkv, xo, ln2_g, ln2_b, w_group, b_group, w_router, b_router,
           w_gate, w_up, w_down, ln3_g, ln3_b):
    _, _, xr = _mixer_and_cross_attention(
        x, mem, ln_in_g, ln_in_b, w_in, rel_bias, sink_a, norm_a_g, norm_b_g, w_out,
        ln1_g, ln1_b, xq, xkv, xo, ln2_g, ln2_b, w_group, b_group, w_router, b_router)
    return _moe(xr, w_gate, w_up, w_down, ln3_g, ln3_b).reshape(x.shape)
```

```python
import functools

import numpy as np
import jax
import jax.numpy as jnp
from jax import lax
from jax.experimental import pallas as pl
from jax.experimental.pallas import tpu as pltpu

F32 = jnp.float32
BF16 = jnp.bfloat16

D_MODEL = 1024
SEQ = 2048
MEM_LEN = 256
HEAD_DIM = 64
A_HEADS = 8
A_KV_HEADS = 2
A_HALF_WIN = 128
B_HEADS = 8
B_BRANCHES = ((128, 1), (512, 4), (2048, 16))
N_BUCKETS = 32
MAX_DISTANCE = 1024
X_HEADS = 4
X_HEAD_DIM = D_MODEL // X_HEADS
N_GROUPS = 4
EXPERTS_PER_GROUP = 8
N_EXPERTS = N_GROUPS * EXPERTS_PER_GROUP
D_EXPERT = 512
DEPTH = 1
ALPHA = (2.0 * DEPTH) ** 0.25
LN_EPS = 1e-5
NEG = -1e30
LOG2E = 1.4426950408889634
LN2 = 0.6931471805599453

LANES = 128
XR_WIDTH = D_MODEL + LANES
A_WIDTH = A_HEADS * HEAD_DIM
B_WIDTH = B_HEADS * HEAD_DIM
A_KV_TILES = A_KV_HEADS
QKV_A = A_WIDTH + 2 * A_KV_TILES * LANES
QKV_B = 3 * B_WIDTH
B_OUT = B_WIDTH + LANES
MERGE_STRIDE = 4

PAIRS_PER_GROUP = EXPERTS_PER_GROUP * (EXPERTS_PER_GROUP - 1) // 2
N_CLASSES = N_GROUPS * PAIRS_PER_GROUP

TM_IN = 512
TM_ATT = 128
TM_MERGE = 512
TM_X = 512
TM_MOE = 128
ROW_GROUP = 8
TM_PLAN = 1024
VMEM_LIMIT = 56 * 1024 * 1024


def _cparams(*sem):
    return pltpu.CompilerParams(dimension_semantics=sem, vmem_limit_bytes=VMEM_LIMIT)


def _layer_norm(x, g, b):
    mu = jnp.mean(x, axis=-1, keepdims=True)
    xc = x - mu
    var = jnp.mean(xc * xc, axis=-1, keepdims=True)
    return xc * lax.rsqrt(var + LN_EPS) * g + b


def _rms_norm(x, g):
    return x * lax.rsqrt(jnp.mean(x * x, axis=-1, keepdims=True) + LN_EPS) * g


def _t5_bucket(rel):
    nb = N_BUCKETS // 2
    max_exact = nb // 2
    ret = (rel > 0).astype(np.int32) * nb
    n = np.abs(rel)
    n_safe = np.maximum(n, 1).astype(np.float64)
    large = max_exact + (np.log(n_safe / max_exact) / np.log(MAX_DISTANCE / max_exact)
                         * (nb - max_exact)).astype(np.int32)
    large = np.minimum(large, nb - 1)
    return (ret + np.where(n < max_exact, n, large)).astype(np.int32)


def _inproj_kernel(x_ref, g_ref, b_ref, w_ref, h0_ref, qa_ref, qb1_ref, qb4_ref, qb16_ref, pb_scr, p4_scr):
    h = _layer_norm(x_ref[...], g_ref[...], b_ref[...])
    for c in range(D_MODEL // LANES):
        h0_ref[0, c] = h[:, c * LANES:(c + 1) * LANES]
    proj = jnp.dot(h.astype(BF16), w_ref[...], preferred_element_type=F32)
    qa_ref[...] = proj[:, :QKV_A].astype(BF16)
    pb = proj[:, QKV_A:]
    qb1_ref[...] = pb.astype(BF16)
    n4 = TM_IN // 4
    for c in range(QKV_B // LANES):
        cs = slice(c * LANES, (c + 1) * LANES)
        pb_scr[c] = pb[:, cs]
        for r4 in range(4):
            rows = pb_scr[c, pl.ds(r4, n4, stride=4), :]
            qb4_ref[0, r4, :, cs] = rows.astype(BF16)
            p4_scr[c, r4 * n4:(r4 + 1) * n4, :] = rows
        for r16 in range(16):
            rows = p4_scr[c, pl.ds((r16 % 4) * n4 + r16 // 4, TM_IN // 16, stride=4), :]
            qb16_ref[0, r16, :, cs] = rows.astype(BF16)


def _input_projection(x2, ln_g, ln_b, w_in_b, batch):
    T = x2.shape[0]
    tiles_per_seq = SEQ // TM_IN
    row = lambda i: (i, 0)
    const = lambda i: (0, 0)
    deint = lambda i: (i // tiles_per_seq, 0, i % tiles_per_seq, 0)
    return pl.pallas_call(
        _inproj_kernel,
        grid=(T // TM_IN,),
        in_specs=[
            pl.BlockSpec((TM_IN, D_MODEL), row),
            pl.BlockSpec((1, D_MODEL), const),
            pl.BlockSpec((1, D_MODEL), const),
            pl.BlockSpec((D_MODEL, QKV_A + QKV_B), const),
        ],
        out_specs=[
            pl.BlockSpec((1, D_MODEL // LANES, TM_IN, LANES), deint),
            pl.BlockSpec((TM_IN, QKV_A), row),
            pl.BlockSpec((TM_IN, QKV_B), row),
            pl.BlockSpec((1, 4, TM_IN // 4, QKV_B), deint),
            pl.BlockSpec((1, 16, TM_IN // 16, QKV_B), deint),
        ],
        out_shape=[
            jax.ShapeDtypeStruct((batch, D_MODEL // LANES, SEQ, LANES), F32),
            jax.ShapeDtypeStruct((T, QKV_A), BF16),
            jax.ShapeDtypeStruct((T, QKV_B), BF16),
            jax.ShapeDtypeStruct((batch, 4, SEQ // 4, QKV_B), BF16),
            jax.ShapeDtypeStruct((batch, 16, SEQ // 16, QKV_B), BF16),
        ],
        scratch_shapes=[pltpu.VMEM((QKV_B // LANES, TM_IN, LANES), F32)] * 2,
        compiler_params=_cparams("parallel"),
        name="ln_in_proj",
    )(x2, ln_g, ln_b, w_in_b)


def _band_attn_kernel(*refs, m, nk, n, sub, kv_of, with_sink):
    if with_sink:
        sink_ref, gain_ref = refs[0], refs[1]
        refs = refs[2:]
    q_ref, k_ref, v_ref, bias_ref, o_ref = refs
    if nk == n:
        start = 0
    else:
        j = pl.program_id(1)
        start = pl.multiple_of(jnp.clip(j * m - (nk - m) // 2, 0, n - nk), HEAD_DIM)
    lane_row = lax.broadcasted_iota(jnp.int32, (1, LANES), 1)
    keep_lo = jnp.where(lane_row < HEAD_DIM, 1.0, 0.0).astype(BF16)
    keep_hi = jnp.where(lane_row < HEAD_DIM, 0.0, 1.0).astype(BF16)
    lane = lax.broadcasted_iota(jnp.int32, (m, LANES), 1)
    n_q = len(kv_of)
    units = [(s, qt) for s in range(sub) for qt in range(n_q)]

    scores = []
    for s, qt in units:
        t = kv_of[qt]
        k_t = k_ref[s, pl.ds(start, nk), t * LANES:(t + 1) * LANES]
        q2 = q_ref[s, :, qt * LANES:(qt + 1) * LANES]
        lhs = jnp.concatenate([q2 * keep_lo, q2 * keep_hi], axis=0)
        sc = lax.dot_general(lhs, k_t, (((1,), (1,)), ((), ())), preferred_element_type=F32)
        scores.append(sc + bias_ref[0, qt])

    probs, denom, row_max = [], [], []
    for (s, qt), sc in zip(units, scores):
        mx = jnp.max(sc, axis=-1, keepdims=True)
        p = jnp.exp2(sc - mx)
        l = jnp.sum(p, axis=-1, keepdims=True)
        halves = []
        for h in (0, 1):
            l_h = l[h * m:(h + 1) * m]
            if with_sink:
                l_h = l_h + jnp.exp2(sink_ref[2 * qt + h] - mx[h * m:(h + 1) * m])
            halves.append(l_h)
        probs.append(p.astype(BF16))
        denom.append(halves)
        row_max.append(mx)

    for s in range(sub):
        pairs = []
        lse_tile = jnp.zeros((m, LANES), F32)
        for qt in range(n_q):
            u = s * n_q + qt
            t = kv_of[qt]
            v_t = v_ref[s, pl.ds(start, nk), t * LANES:(t + 1) * LANES]
            o = jnp.dot(probs[u], v_t, preferred_element_type=F32)
            top = o[:m] * (1.0 / denom[u][0])
            bot = o[m:] * (1.0 / denom[u][1])
            pairs.append(jnp.where(lane < HEAD_DIM, top, bot))
            if not with_sink:
                for h in (0, 1):
                    lse = (row_max[u][h * m:(h + 1) * m] + jnp.log2(denom[u][h])) * LN2
                    lse_tile = jnp.where(lane == 2 * qt + h, lse, lse_tile)
        if with_sink:
            normed = _rms_norm(jnp.concatenate(pairs, axis=1), gain_ref[...])
            pairs = [normed[:, c * LANES:(c + 1) * LANES] for c in range(n_q)]
        else:
            o_ref[s, n_q] = lse_tile
        for c in range(n_q):
            o_ref[s, c] = pairs[c]


def _band_window(m, nk, n):
    nb = n // m
    starts = np.clip(np.arange(nb) * m - (nk - m) // 2, 0, n - nk)
    offs = [int(o) for o in starts - np.arange(nb) * m]
    uniq = sorted(set(offs), reverse=True)
    var = [uniq.index(o) for o in offs]
    assert all(v == var[1] for v in var[1:-1])
    return uniq, var


def _band_bias(rel_bias_h, half, dil, m, nk, n):
    uniq, _ = _band_window(m, nk, n)
    n_heads = rel_bias_h.shape[1]
    span = nk + m - 1
    out = []
    for off in uniq:
        rel = off - (m - 1) + np.arange(span)
        onehot = np.zeros((span, N_BUCKETS), np.float32)
        onehot[np.arange(span), _t5_bucket(dil * rel)] = 1.0
        table = jnp.dot(jnp.asarray(onehot), rel_bias_h.astype(F32), precision=lax.Precision.HIGHEST)
        table = jnp.where((np.abs(rel) <= half)[:, None], table, NEG)
        u = jnp.concatenate([table.T, jnp.zeros((n_heads, 1), F32)], axis=1)
        flat = jnp.tile(u, (1, m + 1))[:, m - 1:m - 1 + m * span]
        b = flat.reshape(n_heads, m, span)[:, :, :nk]
        out.append(b.reshape(n_heads // 2, 2 * m, nk))
    return jnp.stack(out) * LOG2E


def _band_attention(qkv, bias, *, nk, sub, kv_of, q_width, kv_width, out_tiles, sink=None, gain=None):
    Bd, n, _ = qkv.shape
    m = TM_ATT
    nb = n // m
    assert nb * m == n and Bd % sub == 0
    _, var = _band_window(m, nk, n)
    kcol = q_width // kv_width
    with_sink = sink is not None

    def variant(b, j):
        v = jnp.where(j == nb - 1, var[-1], var[min(1, nb - 1)])
        return jnp.where(j == 0, var[0], v)

    in_specs = [
        pl.BlockSpec((sub, m, q_width), lambda b, j: (b, j, 0)),
        pl.BlockSpec((sub, n, kv_width), lambda b, j: (b, 0, kcol)),
        pl.BlockSpec((sub, n, kv_width), lambda b, j: (b, 0, kcol + 1)),
        pl.BlockSpec((1, len(kv_of), 2 * m, nk), lambda b, j: (variant(b, j), 0, 0, 0)),
    ]
    args = [qkv, qkv, qkv, bias]
    if with_sink:
        in_specs = [pl.BlockSpec(memory_space=pltpu.SMEM),
                    pl.BlockSpec((1, q_width), lambda b, j: (0, 0))] + in_specs
        args = [sink, gain] + args
    return pl.pallas_call(
        functools.partial(_band_attn_kernel, m=m, nk=nk, n=n, sub=sub, kv_of=kv_of, with_sink=with_sink),
        grid=(Bd // sub, nb),
        in_specs=in_specs,
        out_specs=pl.BlockSpec((sub, out_tiles, m, LANES), lambda b, j: (b, 0, j, 0)),
        out_shape=jax.ShapeDtypeStruct((Bd, out_tiles, n, LANES), F32),
        compiler_params=_cparams("parallel", "arbitrary"),
        name="band_attn_sink" if with_sink else f"band_attn_n{n}",
    )(*args)


def _merge_kernel(ya_ref, o1_ref, o4_ref, o16_ref, h0_ref, gb_ref, w_ref, g1_ref, b1_ref, h1_ref, s16_scr):
    n4 = TM_MERGE // MERGE_STRIDE
    n16 = TM_MERGE // 16
    for r16 in range(16):
        for c in range(B_OUT // LANES):
            s16_scr[c, pl.ds((r16 % 4) * n4 + r16 // 4, n16, stride=4), :] = o16_ref[0, r16, c]

    def natural(ref, c):
        return jnp.concatenate([ref[0, c, pl.ds(r, n4, stride=MERGE_STRIDE), :]
                                for r in range(MERGE_STRIDE)], axis=0)

    def stride4(c):
        return jnp.concatenate([o4_ref[0, r, c] for r in range(MERGE_STRIDE)], axis=0)

    def stride16(c):
        return s16_scr[c]

    n_val = B_WIDTH // LANES
    lses = (natural(o1_ref, n_val), stride4(n_val), stride16(n_val))
    mx = jnp.maximum(jnp.maximum(lses[0], lses[1]), lses[2])
    ex = [jnp.exp(l - mx) for l in lses]
    inv = 1.0 / (ex[0] + ex[1] + ex[2])
    head = lax.broadcasted_iota(jnp.int32, (LANES, B_WIDTH), 0)
    col = lax.broadcasted_iota(jnp.int32, (LANES, B_WIDTH), 1)
    spread = jnp.where(col // HEAD_DIM == head, 1.0, 0.0).astype(BF16)
    wide = []
    for e in ex:
        w = e * inv
        w_hi = w.astype(BF16)
        w_lo = (w - w_hi.astype(F32)).astype(BF16)
        wide.append(jnp.dot(w_hi, spread, preferred_element_type=F32)
                    + jnp.dot(w_lo, spread, preferred_element_type=F32))
    pieces = []
    for c in range(n_val):
        cs = slice(c * LANES, (c + 1) * LANES)
        outs = (natural(o1_ref, c), stride4(c), stride16(c))
        pieces.append(wide[0][:, cs] * outs[0] + wide[1][:, cs] * outs[1] + wide[2][:, cs] * outs[2])
    yb = _rms_norm(jnp.concatenate(pieces, axis=1), gb_ref[...])
    ya = jnp.concatenate([natural(ya_ref, c) for c in range(A_WIDTH // LANES)], axis=1)
    y = jnp.concatenate([ya.astype(BF16), yb.astype(BF16)], axis=1)
    mix = jnp.dot(y, w_ref[...], preferred_element_type=F32)
    h0 = jnp.concatenate([natural(h0_ref, c) for c in range(D_MODEL // LANES)], axis=1)
    h1_ref[...] = _layer_norm(ALPHA * h0 + mix, g1_ref[...], b1_ref[...])


def _token_of_row(p):
    n16 = TM_MERGE // MERGE_STRIDE
    tile, rest = p // TM_MERGE, p % TM_MERGE
    return tile * TM_MERGE + (rest % n16) * MERGE_STRIDE + rest // n16


def _merge_project(ya, o1, o4, o16, h0, gain_b, w_out_b, ln_g, ln_b):
    B = h0.shape[0]
    T = B * SEQ
    tiles_per_seq = SEQ // TM_MERGE
    const = lambda i: (0, 0)
    nat = lambda i: (i // tiles_per_seq, 0, i % tiles_per_seq, 0)
    deint = lambda i: (i // tiles_per_seq, 0, 0, i % tiles_per_seq, 0)
    n_b = B_OUT // LANES
    return pl.pallas_call(
        _merge_kernel,
        grid=(T // TM_MERGE,),
        in_specs=[
            pl.BlockSpec((1, A_WIDTH // LANES, TM_MERGE, LANES), nat),
            pl.BlockSpec((1, n_b, TM_MERGE, LANES), nat),
            pl.BlockSpec((1, 4, n_b, TM_MERGE // 4, LANES), deint),
            pl.BlockSpec((1, 16, n_b, TM_MERGE // 16, LANES), deint),
            pl.BlockSpec((1, D_MODEL // LANES, TM_MERGE, LANES), nat),
            pl.BlockSpec((1, B_WIDTH), const),
            pl.BlockSpec((D_MODEL, D_MODEL), const),
            pl.BlockSpec((1, D_MODEL), const),
            pl.BlockSpec((1, D_MODEL), const),
        ],
        out_specs=pl.BlockSpec((TM_MERGE, D_MODEL), lambda i: (i, 0)),
        out_shape=jax.ShapeDtypeStruct((T, D_MODEL), F32),
        scratch_shapes=[pltpu.VMEM((n_b, TM_MERGE, LANES), F32)],
        compiler_params=_cparams("parallel"),
        name="merge_out_proj",
    )(ya, o1, o4, o16, h0, gain_b, w_out_b, ln_g, ln_b)


def _mem_kv_kernel(mem_ref, w_ref, k_ref, v_ref):
    kv = jnp.dot(mem_ref[0].astype(BF16), w_ref[...], preferred_element_type=F32)
    k_ref[0] = kv[:, :D_MODEL].astype(BF16)
    v_ref[0] = kv[:, D_MODEL:].astype(BF16)


def _mem_kv(mem, xkv_b):
    B = mem.shape[0]
    blk = pl.BlockSpec((1, MEM_LEN, D_MODEL), lambda b: (b, 0, 0))
    return pl.pallas_call(
        _mem_kv_kernel,
        grid=(B,),
        in_specs=[blk, pl.BlockSpec((D_MODEL, 2 * D_MODEL), lambda b: (0, 0))],
        out_specs=[blk, blk],
        out_shape=[jax.ShapeDtypeStruct((B, MEM_LEN, D_MODEL), BF16)] * 2,
        compiler_params=_cparams("parallel"),
        name="mem_kv_proj",
    )(mem, xkv_b)


def _route(logits):
    rows = logits.shape[0]
    lane = lax.broadcasted_iota(jnp.int32, (rows, LANES), 1).astype(F32)
    big = float(LANES)
    ninf = -jnp.inf
    gl = jnp.where(lane < N_GROUPS, logits, ninf)
    gmax = jnp.max(gl, axis=-1, keepdims=True)
    gidx = jnp.min(jnp.where(gl == gmax, lane, big), axis=-1, keepdims=True)
    g_p = 1.0 / jnp.sum(jnp.exp(gl - gmax), axis=-1, keepdims=True)
    lo_lane = N_GROUPS + EXPERTS_PER_GROUP * gidx
    el = jnp.where((lane >= lo_lane) & (lane < lo_lane + EXPERTS_PER_GROUP), logits, ninf)
    v1 = jnp.max(el, axis=-1, keepdims=True)
    i1 = jnp.min(jnp.where(el == v1, lane, big), axis=-1, keepdims=True)
    el2 = jnp.where(lane == i1, ninf, el)
    v2 = jnp.max(el2, axis=-1, keepdims=True)
    i2 = jnp.min(jnp.where(el2 == v2, lane, big), axis=-1, keepdims=True)
    t = jnp.exp(v2 - v1)
    w1 = g_p / (1.0 + t)
    w2 = g_p * t / (1.0 + t)
    a = jnp.minimum(i1, i2) - lo_lane
    b = jnp.maximum(i1, i2) - lo_lane
    pair = a * (2 * EXPERTS_PER_GROUP - 1 - a) * 0.5 + (b - a - 1.0)
    cls = gidx * PAIRS_PER_GROUP + pair
    w_lo = jnp.where(i1 < i2, w1, w2)
    w_hi = jnp.where(i1 < i2, w2, w1)
    return jnp.where(lane == 0, cls, jnp.where(lane == 1, w_lo, jnp.where(lane == 2, w_hi, 0.0)))


def _xattn_kernel(h1_ref, k_ref, v_ref, wq_ref, wo_ref, g2_ref, b2_ref, wr_ref, br_ref, xr_ref):
    h1 = h1_ref[...]
    q = jnp.dot(h1.astype(BF16), wq_ref[...], preferred_element_type=F32).astype(BF16)
    outs = []
    for hd in range(X_HEADS):
        sl = slice(hd * X_HEAD_DIM, (hd + 1) * X_HEAD_DIM)
        s = lax.dot_general(q[:, sl], k_ref[0, :, sl], (((1,), (1,)), ((), ())), preferred_element_type=F32)
        m = jnp.max(s, axis=-1, keepdims=True)
        p = jnp.exp2(s - m)
        l = jnp.sum(p, axis=-1, keepdims=True)
        o = jnp.dot(p.astype(BF16), v_ref[0, :, sl], preferred_element_type=F32) / l
        outs.append(o.astype(BF16))
    xa = jnp.dot(jnp.concatenate(outs, axis=1), wo_ref[...], preferred_element_type=F32)
    h2 = _layer_norm(ALPHA * h1 + xa, g2_ref[...], b2_ref[...])
    h_hi = h2.astype(BF16)
    h_lo = (h2 - h_hi.astype(F32)).astype(BF16)
    t_hi = jnp.dot(h_hi, wr_ref[...], preferred_element_type=F32)
    t_lo = jnp.dot(h_lo, wr_ref[...], preferred_element_type=F32)
    logits = (t_hi[:, :LANES] + t_hi[:, LANES:]) + (t_lo[:, :LANES] + t_lo[:, LANES:]) + br_ref[...]
    xr_ref[:, :D_MODEL] = h2
    xr_ref[:, D_MODEL:] = _route(logits)


def _cross_attention_route(h1, k, v, xq_b, xo_b, ln_g, ln_b, wr_split, br):
    T = h1.shape[0]
    tiles_per_seq = SEQ // TM_X
    row = lambda i: (i, 0)
    const = lambda i: (0, 0)
    kv_blk = pl.BlockSpec((1, MEM_LEN, D_MODEL), lambda i: (i // tiles_per_seq, 0, 0))
    return pl.pallas_call(
        _xattn_kernel,
        grid=(T // TM_X,),
        in_specs=[
            pl.BlockSpec((TM_X, D_MODEL), row), kv_blk, kv_blk,
            pl.BlockSpec((D_MODEL, D_MODEL), const),
            pl.BlockSpec((D_MODEL, D_MODEL), const),
            pl.BlockSpec((1, D_MODEL), const),
            pl.BlockSpec((1, D_MODEL), const),
            pl.BlockSpec((D_MODEL, 2 * LANES), const),
            pl.BlockSpec((1, LANES), const),
        ],
        out_specs=pl.BlockSpec((TM_X, XR_WIDTH), row),
        out_shape=jax.ShapeDtypeStruct((T, XR_WIDTH), F32),
        compiler_params=_cparams("parallel"),
        name="xattn_ln_route",
    )(h1, k, v, xq_b, xo_b, ln_g, ln_b, wr_split, br)


def _expert_kernel(src_ref, dst_ref, cnt_ref, elo_ref, ehi_ref, ntile_ref, x_hbm,
                   wg_lo, wu_lo, wd_lo, wg_hi, wu_hi, wd_hi, g3_ref, b3_ref, o_hbm,
                   xbuf, obuf, gsem, ssem):
    j = pl.program_id(0)
    n_tiles = ntile_ref[0]
    slot = j % 2

    def row_copy(t, i, s, gather):
        row = pl.ds(i, 1)
        if gather:
            return pltpu.make_async_copy(x_hbm.at[pl.ds(src_ref[t * TM_MOE + i], 1)], xbuf.at[s, row], gsem.at[s])
        return pltpu.make_async_copy(obuf.at[s, row], o_hbm.at[pl.ds(dst_ref[t * TM_MOE + i], 1)], ssem.at[s])

    def n_rows(t, gather):
        cnt = cnt_ref[t]
        return (cnt + ROW_GROUP - 1) // ROW_GROUP * ROW_GROUP if gather else cnt

    def start_rows(t, s, gather):
        cnt = cnt_ref[t]
        n_groups = (cnt + ROW_GROUP - 1) // ROW_GROUP if gather else cnt // ROW_GROUP

        def group(g, c):
            base = pl.multiple_of(g * ROW_GROUP, ROW_GROUP)
            for r in range(ROW_GROUP):
                row_copy(t, base + r, s, gather).start()
            return c
        lax.fori_loop(0, n_groups, group, 0)
        if not gather:
            def single(i, c):
                row_copy(t, i, s, gather).start()
                return c
            lax.fori_loop(n_groups * ROW_GROUP, cnt, single, 0)

    def wait_rows(t, s, gather):
        cnt = n_rows(t, gather)
        for bit in range(TM_MOE.bit_length()):
            rows = pl.ds(0, 1 << bit)

            @pl.when((cnt >> bit) & 1 == 1)
            def _():
                if gather:
                    pltpu.make_async_copy(x_hbm.at[rows], xbuf.at[s, rows], gsem.at[s]).wait()
                else:
                    pltpu.make_async_copy(obuf.at[s, rows], o_hbm.at[rows], ssem.at[s]).wait()

    @pl.when(j == 0)
    def _():
        xbuf[...] = jnp.zeros_like(xbuf)
        start_rows(0, 0, True)

    @pl.when(j < n_tiles)
    def _():
        @pl.when(j + 1 < n_tiles)
        def _():
            start_rows(j + 1, 1 - slot, True)

        wait_rows(j, slot, True)
        x = xbuf[slot, :, :D_MODEL]
        xb = x.astype(BF16)
        y = jnp.zeros_like(x)
        for e, (wg, wu, wd) in enumerate(((wg_lo, wu_lo, wd_lo), (wg_hi, wu_hi, wd_hi))):
            gate = xbuf[slot, :, D_MODEL + 1 + e:D_MODEL + 2 + e]
            a = jnp.dot(xb, wg[0], preferred_element_type=F32)
            u = jnp.dot(xb, wu[0], preferred_element_type=F32)
            hid = a * jax.nn.sigmoid(a) * u
            y = y + jnp.dot((gate * hid).astype(BF16), wd[0], preferred_element_type=F32)
        out = _layer_norm(ALPHA * x + y, g3_ref[...], b3_ref[...])

        @pl.when(j >= 2)
        def _():
            wait_rows(j - 2, slot, False)

        obuf[slot] = out
        start_rows(j, slot, False)

        @pl.when(j == n_tiles - 1)
        def _():
            @pl.when(j >= 1)
            def _():
                wait_rows(j - 1, 1 - slot, False)

            wait_rows(j, slot, False)


def _expert_mlp(src_rows, dst_rows, tile_cnt, tile_elo, tile_ehi, n_tiles, xr, wg, wu, wd, ln_g, ln_b):
    n_tiles_max = tile_cnt.shape[0]
    const = lambda j, src, dst, cnt, elo, ehi, nt: (0, 0)
    lo = lambda j, src, dst, cnt, elo, ehi, nt: (elo[j], 0, 0)
    hi = lambda j, src, dst, cnt, elo, ehi, nt: (ehi[j], 0, 0)
    up = (1, D_MODEL, D_EXPERT)
    down = (1, D_EXPERT, D_MODEL)
    any_spec = pl.BlockSpec(memory_space=pl.ANY)
    return pl.pallas_call(
        _expert_kernel,
        grid_spec=pltpu.PrefetchScalarGridSpec(
            num_scalar_prefetch=6,
            grid=(n_tiles_max,),
            in_specs=[
                any_spec,
                pl.BlockSpec(up, lo), pl.BlockSpec(up, lo), pl.BlockSpec(down, lo),
                pl.BlockSpec(up, hi), pl.BlockSpec(up, hi), pl.BlockSpec(down, hi),
                pl.BlockSpec((1, D_MODEL), const),
                pl.BlockSpec((1, D_MODEL), const),
            ],
            out_specs=any_spec,
            scratch_shapes=[
                pltpu.VMEM((2, TM_MOE, XR_WIDTH), F32),
                pltpu.VMEM((2, TM_MOE, D_MODEL), F32),
                pltpu.SemaphoreType.DMA((2,)),
                pltpu.SemaphoreType.DMA((2,)),
            ],
        ),
        out_shape=jax.ShapeDtypeStruct((xr.shape[0], D_MODEL), F32),
        compiler_params=_cparams("arbitrary"),
        name="moe_experts",
    )(src_rows, dst_rows, tile_cnt, tile_elo, tile_ehi, n_tiles, xr, wg, wu, wd, wg, wu, wd, ln_g, ln_b)


def _class_experts():
    lo, hi = [], []
    for g in range(N_GROUPS):
        for a in range(EXPERTS_PER_GROUP):
            for b in range(a + 1, EXPERTS_PER_GROUP):
                lo.append(g * EXPERTS_PER_GROUP + a)
                hi.append(g * EXPERTS_PER_GROUP + b)
    return np.asarray(lo, np.int32), np.asarray(hi, np.int32)


def _plan_kernel(route_ref, pos_ref, counts_ref, tri_scr, run_scr, start_scr):
    phase, i = pl.program_id(0), pl.program_id(1)
    lane = lax.broadcasted_iota(jnp.int32, (TM_PLAN, LANES), 1)
    onehot = lane.astype(F32) == route_ref[:, 0:1]
    onehot_f = jnp.where(onehot, 1.0, 0.0)

    @pl.when((phase == 0) & (i == 0))
    def _():
        run_scr[...] = jnp.zeros_like(run_scr)
        r = lax.broadcasted_iota(jnp.int32, (TM_PLAN, TM_PLAN), 0)
        c = lax.broadcasted_iota(jnp.int32, (TM_PLAN, TM_PLAN), 1)
        tri_scr[...] = jnp.where(c < r, 1.0, 0.0).astype(BF16)

    @pl.when((phase == 1) & (i == 0))
    def _():
        counts = run_scr[...]
        counts_ref[...] = counts
        tiles = jnp.floor((counts + (TM_MOE - 1)) * (1.0 / TM_MOE))
        lane_row = lax.broadcasted_iota(jnp.int32, (1, LANES), 1)
        scan = tiles
        shift = 1
        while shift < LANES:
            scan = scan + jnp.where(lane_row >= shift, pltpu.roll(scan, shift, axis=1), 0.0)
            shift *= 2
        start_scr[...] = (scan - tiles) * TM_MOE
        run_scr[...] = jnp.zeros_like(run_scr)

    @pl.when(phase == 1)
    def _():
        before = jnp.dot(tri_scr[...], onehot_f.astype(BF16), preferred_element_type=F32)
        pos_col = jnp.sum(onehot_f * (before + run_scr[...] + start_scr[...]), axis=1, keepdims=True)
        eye = (lax.broadcasted_iota(jnp.int32, (LANES, LANES), 0)
               == lax.broadcasted_iota(jnp.int32, (LANES, LANES), 1))
        for r in range(TM_PLAN // LANES):
            row = jnp.sum(jnp.where(eye, pos_col[r * LANES:(r + 1) * LANES], 0.0), axis=0, keepdims=True)
            pos_ref[r:r + 1, :] = row.astype(jnp.int32)

    run_scr[...] = run_scr[...] + jnp.sum(onehot_f, axis=0, keepdims=True)


def _plan_positions(xr):
    T = xr.shape[0]
    n_steps = T // TM_PLAN
    pos, counts = pl.pallas_call(
        _plan_kernel,
        grid=(2, n_steps),
        in_specs=[pl.BlockSpec((TM_PLAN, LANES), lambda p, i: (i, D_MODEL // LANES))],
        out_specs=[pl.BlockSpec((TM_PLAN // LANES, LANES), lambda p, i: (i * p, 0)),
                   pl.BlockSpec((1, LANES), lambda p, i: (0, 0))],
        out_shape=[jax.ShapeDtypeStruct((T // LANES, LANES), jnp.int32),
                   jax.ShapeDtypeStruct((1, LANES), F32)],
        scratch_shapes=[pltpu.VMEM((TM_PLAN, TM_PLAN), BF16),
                        pltpu.VMEM((1, LANES), F32),
                        pltpu.VMEM((1, LANES), F32)],
        compiler_params=_cparams("arbitrary", "arbitrary"),
        name="moe_plan",
    )(xr)
    return pos.reshape(T), counts[0, :N_CLASSES].astype(jnp.int32)


def _moe_plan(xr, n_tiles_max):
    T = xr.shape[0]
    pos, counts = _plan_positions(xr)
    tiles_per_class = (counts + TM_MOE - 1) // TM_MOE
    tile_end = jnp.cumsum(tiles_per_class)
    tile_start = tile_end - tiles_per_class
    n_tiles = tile_end[-1]
    tok = jnp.arange(T, dtype=jnp.int32)
    src_rows = jnp.zeros((n_tiles_max * TM_MOE,), jnp.int32).at[pos].set(tok)
    tile_ids = jnp.arange(n_tiles_max, dtype=jnp.int32)
    used = jnp.minimum(tile_ids, n_tiles - 1)
    tile_cls = jnp.sum((tile_end[None, :] <= used[:, None]).astype(jnp.int32), axis=1)
    tile_cls = jnp.minimum(tile_cls, N_CLASSES - 1)
    cls_lo, cls_hi = _class_experts()
    tile_elo = jnp.asarray(cls_lo)[tile_cls]
    tile_ehi = jnp.asarray(cls_hi)[tile_cls]
    within = tile_ids - tile_start[tile_cls]
    tile_cnt = jnp.clip(counts[tile_cls] - within * TM_MOE, 0, TM_MOE)
    tile_cnt = jnp.where(tile_ids < n_tiles, tile_cnt, 0).astype(jnp.int32)
    return src_rows, tile_elo, tile_ehi, tile_cnt, n_tiles.reshape(1).astype(jnp.int32)


def _vec(a):
    return a.reshape(1, -1).astype(F32)


def _mixer_and_cross_attention(x, mem, ln_in_g, ln_in_b, w_in, rel_bias, sink_a, norm_a_g, norm_b_g, w_out,
                               ln1_g, ln1_b, xq, xkv, xo, ln2_g, ln2_b, w_group, b_group, w_router, b_router):
    B, S, D = x.shape
    assert S == SEQ and D == D_MODEL and mem.shape[1:] == (MEM_LEN, D_MODEL)
    T = B * S
    vec = _vec

    w = w_in[0]
    edges = np.cumsum((0, A_WIDTH, A_KV_HEADS * HEAD_DIM, A_KV_HEADS * HEAD_DIM, B_WIDTH, B_WIDTH, B_WIDTH))
    qa, ka, va, qb, kb, vb = [w[:, a:b] for a, b in zip(edges[:-1], edges[1:])]
    dup = lambda t: jnp.repeat(t.reshape(D, A_KV_HEADS, 1, HEAD_DIM), 2, axis=2).reshape(D, A_KV_TILES * LANES)
    scale = HEAD_DIM ** -0.5 * LOG2E
    w_in_b = jnp.concatenate([qa * scale, dup(ka), dup(va), qb * scale, kb, vb], axis=1).astype(BF16)
    w_out_b = w_out[0].astype(BF16)
    xq_b = (xq[0] * (X_HEAD_DIM ** -0.5 * LOG2E)).astype(BF16)
    xkv_b = xkv[0].astype(BF16)
    xo_b = xo[0].astype(BF16)
    wr = jnp.concatenate([w_group[0], w_router[0]], axis=1).astype(F32)
    wr = jnp.pad(wr, ((0, 0), (0, LANES - wr.shape[1])))
    wr_hi = wr.astype(BF16)
    wr_split = jnp.concatenate([wr_hi, (wr - wr_hi.astype(F32)).astype(BF16)], axis=1)
    br = jnp.concatenate([b_group[0], b_router[0]]).astype(F32)
    br = jnp.pad(br, (0, LANES - br.shape[0])).reshape(1, LANES)

    h0, qkv_a, qkv_b1, qkv_b4, qkv_b16 = _input_projection(
        x.reshape(T, D), vec(ln_in_g), vec(ln_in_b), w_in_b, B)

    nk_a = TM_ATT + 2 * A_HALF_WIN
    bias_a = _band_bias(rel_bias[:, :A_HEADS], A_HALF_WIN, 1, TM_ATT, nk_a, S)
    ya = _band_attention(qkv_a.reshape(B, S, QKV_A), bias_a, nk=nk_a, sub=1, kv_of=(0, 0, 1, 1),
                         q_width=A_WIDTH, kv_width=A_KV_TILES * LANES, out_tiles=A_WIDTH // LANES,
                         sink=sink_a[0].astype(F32) * LOG2E, gain=vec(norm_a_g[0]))
    branch_out = []
    for (win, dil), qkv in zip(B_BRANCHES, (qkv_b1.reshape(B, S, QKV_B),
                                             qkv_b4.reshape(B * 4, S // 4, QKV_B),
                                             qkv_b16.reshape(B * 16, S // 16, QKV_B))):
        half = (win // 2) // dil
        n = S // dil
        nk = min(TM_ATT + 2 * half, n)
        bias_b = _band_bias(rel_bias[:, A_HEADS:], half, dil, TM_ATT, nk, n)
        branch_out.append(_band_attention(qkv, bias_b, nk=nk, sub=max(1, TM_ATT * 4 // n), kv_of=(0, 1, 2, 3),
                                          q_width=B_WIDTH, kv_width=B_WIDTH, out_tiles=B_OUT // LANES))
    o1 = branch_out[0]
    o4 = branch_out[1].reshape(B, 4, B_OUT // LANES, S // 4, LANES)
    o16 = branch_out[2].reshape(B, 16, B_OUT // LANES, S // 16, LANES)

    h1 = _merge_project(ya, o1, o4, o16, h0, vec(norm_b_g[0]), w_out_b,
                        vec(ln1_g[0]), vec(ln1_b[0]))

    k_mem, v_mem = _mem_kv(mem, xkv_b)
    xr = _cross_attention_route(h1, k_mem, v_mem, xq_b, xo_b, vec(ln2_g[0]), vec(ln2_b[0]), wr_split, br)
    return h0, h1, xr


def _moe(xr, w_gate, w_up, w_down, ln3_g, ln3_b):
    T = xr.shape[0]
    wg_b = w_gate[0].reshape(N_EXPERTS, D_MODEL, D_EXPERT).astype(BF16)
    wu_b = w_up[0].reshape(N_EXPERTS, D_MODEL, D_EXPERT).astype(BF16)
    wd_b = w_down[0].reshape(N_EXPERTS, D_EXPERT, D_MODEL).astype(BF16)
    n_tiles_max = T // TM_MOE + N_CLASSES
    src_rows, tile_elo, tile_ehi, tile_cnt, n_tiles = _moe_plan(xr, n_tiles_max)
    return _expert_mlp(src_rows, _token_of_row(src_rows), tile_cnt, tile_elo, tile_ehi, n_tiles, xr, wg_b, wu_b, wd_b,
                       _vec(ln3_g[0]), _vec(ln3_b[0]))


def kernel(x, mem, ln_in_g, ln_in_b, w_in, rel_bias, sink_a, norm_a_g, norm_b_g, w_out,
           ln1_g, ln1_b, xq, xkv, xo, ln2_g, ln2_b, w_group, b_group, w_router, b_router,
           w_gate, w_up, w_down, ln3_g, ln3_b):
    _, _, xr = _mixer_and_cross_attention(
        x, mem, ln_in_g, ln_in_b, w_in, rel_bias, sink_a, norm_a_g, norm_b_g, w_out,
        ln1_g, ln1_b, xq, xkv, xo, ln2_g, ln2_b, w_group, b_group, w_router, b_router)
    return _moe(xr, w_gate, w_up, w_down, ln3_g, ln3_b).reshape(x.shape)
```

```python
import functools

import numpy as np
import jax
import jax.numpy as jnp
from jax import lax
from jax.experimental import pallas as pl
from jax.experimental.pallas import tpu as pltpu

F32 = jnp.float32
BF16 = jnp.bfloat16

D_MODEL = 1024
SEQ = 2048
MEM_LEN = 256
HEAD_DIM = 64
A_HEADS = 8
A_KV_HEADS = 2
A_HALF_WIN = 128
B_HEADS = 8
B_BRANCHES = ((128, 1), (512, 4), (2048, 16))
N_BUCKETS = 32
MAX_DISTANCE = 1024
X_HEADS = 4
X_HEAD_DIM = D_MODEL // X_HEADS
N_GROUPS = 4
EXPERTS_PER_GROUP = 8
N_EXPERTS = N_GROUPS * EXPERTS_PER_GROUP
D_EXPERT = 512
DEPTH = 1
ALPHA = (2.0 * DEPTH) ** 0.25
LN_EPS = 1e-5
NEG = -1e30
LOG2E = 1.4426950408889634
LN2 = 0.6931471805599453

LANES = 128
XR_WIDTH = D_MODEL + LANES
A_WIDTH = A_HEADS * HEAD_DIM
B_WIDTH = B_HEADS * HEAD_DIM
A_KV_TILES = A_KV_HEADS
QKV_A = A_WIDTH + 2 * A_KV_TILES * LANES
QKV_B = 3 * B_WIDTH
B_OUT = B_WIDTH + LANES
MERGE_STRIDE = 4

PAIRS_PER_GROUP = EXPERTS_PER_GROUP * (EXPERTS_PER_GROUP - 1) // 2
N_CLASSES = N_GROUPS * PAIRS_PER_GROUP

TM_IN = 512
TM_ATT = 128
TM_MERGE = 512
TM_X = 512
TM_MOE = 128
ROW_GROUP = 8
TM_PLAN = 1024
VMEM_LIMIT = 56 * 1024 * 1024


def _cparams(*sem):
    return pltpu.CompilerParams(dimension_semantics=sem, vmem_limit_bytes=VMEM_LIMIT)


def _layer_norm(x, g, b):
    mu = jnp.mean(x, axis=-1, keepdims=True)
    xc = x - mu
    var = jnp.mean(xc * xc, axis=-1, keepdims=True)
    return xc * lax.rsqrt(var + LN_EPS) * g + b


def _rms_norm(x, g):
    return x * lax.rsqrt(jnp.mean(x * x, axis=-1, keepdims=True) + LN_EPS) * g


def _t5_bucket(rel):
    nb = N_BUCKETS // 2
    max_exact = nb // 2
    ret = (rel > 0).astype(np.int32) * nb
    n = np.abs(rel)
    n_safe = np.maximum(n, 1).astype(np.float64)
    large = max_exact + (np.log(n_safe / max_exact) / np.log(MAX_DISTANCE / max_exact)
                         * (nb - max_exact)).astype(np.int32)
    large = np.minimum(large, nb - 1)
    return (ret + np.where(n < max_exact, n, large)).astype(np.int32)


def _inproj_kernel(x_ref, g_ref, b_ref, w_ref, h0_ref, qa_ref, qb1_ref, qb4_ref, qb16_ref, pb_scr, p4_scr):
    h = _layer_norm(x_ref[...], g_ref[...], b_ref[...])
    for c in range(D_MODEL // LANES):
        h0_ref[0, c] = h[:, c * LANES:(c + 1) * LANES]
    proj = jnp.dot(h.astype(BF16), w_ref[...], preferred_element_type=F32)
    qa_ref[...] = proj[:, :QKV_A].astype(BF16)
    pb = proj[:, QKV_A:]
    qb1_ref[...] = pb.astype(BF16)
    n4 = TM_IN // 4
    for c in range(QKV_B // LANES):
        cs = slice(c * LANES, (c + 1) * LANES)
        pb_scr[c] = pb[:, cs]
        for r4 in range(4):
            rows = pb_scr[c, pl.ds(r4, n4, stride=4), :]
            qb4_ref[0, r4, :, cs] = rows.astype(BF16)
            p4_scr[c, r4 * n4:(r4 + 1) * n4, :] = rows
        for r16 in range(16):
            rows = p4_scr[c, pl.ds((r16 % 4) * n4 + r16 // 4, TM_IN // 16, stride=4), :]
            qb16_ref[0, r16, :, cs] = rows.astype(BF16)


def _input_projection(x2, ln_g, ln_b, w_in_b, batch):
    T = x2.shape[0]
    tiles_per_seq = SEQ // TM_IN
    row = lambda i: (i, 0)
    const = lambda i: (0, 0)
    deint = lambda i: (i // tiles_per_seq, 0, i % tiles_per_seq, 0)
    return pl.pallas_call(
        _inproj_kernel,
        grid=(T // TM_IN,),
        in_specs=[
            pl.BlockSpec((TM_IN, D_MODEL), row),
            pl.BlockSpec((1, D_MODEL), const),
            pl.BlockSpec((1, D_MODEL), const),
            pl.BlockSpec((D_MODEL, QKV_A + QKV_B), const),
        ],
        out_specs=[
            pl.BlockSpec((1, D_MODEL // LANES, TM_IN, LANES), deint),
            pl.BlockSpec((TM_IN, QKV_A), row),
            pl.BlockSpec((TM_IN, QKV_B), row),
            pl.BlockSpec((1, 4, TM_IN // 4, QKV_B), deint),
            pl.BlockSpec((1, 16, TM_IN // 16, QKV_B), deint),
        ],
        out_shape=[
            jax.ShapeDtypeStruct((batch, D_MODEL // LANES, SEQ, LANES), F32),
            jax.ShapeDtypeStruct((T, QKV_A), BF16),
            jax.ShapeDtypeStruct((T, QKV_B), BF16),
            jax.ShapeDtypeStruct((batch, 4, SEQ // 4, QKV_B), BF16),
            jax.ShapeDtypeStruct((batch, 16, SEQ // 16, QKV_B), BF16),
        ],
        scratch_shapes=[pltpu.VMEM((QKV_B // LANES, TM_IN, LANES), F32)] * 2,
        compiler_params=_cparams("parallel"),
        name="ln_in_proj",
    )(x2, ln_g, ln_b, w_in_b)


def _band_attn_kernel(*refs, m, nk, n, sub, kv_of, with_sink):
    if with_sink:
        sink_ref, gain_ref = refs[0], refs[1]
        refs = refs[2:]
    q_ref, k_ref, v_ref, bias_ref, o_ref = refs
    if nk == n:
        start = 0
    else:
        j = pl.program_id(1)
        start = pl.multiple_of(jnp.clip(j * m - (nk - m) // 2, 0, n - nk), HEAD_DIM)
    lane_row = lax.broadcasted_iota(jnp.int32, (1, LANES), 1)
    keep_lo = jnp.where(lane_row < HEAD_DIM, 1.0, 0.0).astype(BF16)
    keep_hi = jnp.where(lane_row < HEAD_DIM, 0.0, 1.0).astype(BF16)
    lane = lax.broadcasted_iota(jnp.int32, (m, LANES), 1)
    n_q = len(kv_of)
    units = [(s, qt) for s in range(sub) for qt in range(n_q)]

    scores = []
    for s, qt in units:
        t = kv_of[qt]
        k_t = k_ref[s, pl.ds(start, nk), t * LANES:(t + 1) * LANES]
        q2 = q_ref[s, :, qt * LANES:(qt + 1) * LANES]
        lhs = jnp.concatenate([q2 * keep_lo, q2 * keep_hi], axis=0)
        sc = lax.dot_general(lhs, k_t, (((1,), (1,)), ((), ())), preferred_element_type=F32)
        scores.append(sc + bias_ref[0, qt])

    probs, denom, row_max = [], [], []
    for (s, qt), sc in zip(units, scores):
        mx = jnp.max(sc, axis=-1, keepdims=True)
        p = jnp.exp2(sc - mx)
        l = jnp.sum(p, axis=-1, keepdims=True)
        halves = []
        for h in (0, 1):
            l_h = l[h * m:(h + 1) * m]
            if with_sink:
                l_h = l_h + jnp.exp2(sink_ref[2 * qt + h] - mx[h * m:(h + 1) * m])
            halves.append(l_h)
        probs.append(p.astype(BF16))
        denom.append(halves)
        row_max.append(mx)

    for s in range(sub):
        pairs = []
        lse_tile = jnp.zeros((m, LANES), F32)
        for qt in range(n_q):
            u = s * n_q + qt
            t = kv_of[qt]
            v_t = v_ref[s, pl.ds(start, nk), t * LANES:(t + 1) * LANES]
            o = jnp.dot(probs[u], v_t, preferred_element_type=F32)
            top = o[:m] * (1.0 / denom[u][0])
            bot = o[m:] * (1.0 / denom[u][1])
            pairs.append(jnp.where(lane < HEAD_DIM, top, bot))
            if not with_sink:
                for h in (0, 1):
                    lse = (row_max[u][h * m:(h + 1) * m] + jnp.log2(denom[u][h])) * LN2
                    lse_tile = jnp.where(lane == 2 * qt + h, lse, lse_tile)
        if with_sink:
            normed = _rms_norm(jnp.concatenate(pairs, axis=1), gain_ref[...])
            pairs = [normed[:, c * LANES:(c + 1) * LANES] for c in range(n_q)]
        else:
            o_ref[s, n_q] = lse_tile
        for c in range(n_q):
            o_ref[s, c] = pairs[c]


def _band_window(m, nk, n):
    nb = n // m
    starts = np.clip(np.arange(nb) * m - (nk - m) // 2, 0, n - nk)
    offs = [int(o) for o in starts - np.arange(nb) * m]
    uniq = sorted(set(offs), reverse=True)
    var = [uniq.index(o) for o in offs]
    assert all(v == var[1] for v in var[1:-1])
    return uniq, var


def _band_bias(rel_bias_h, half, dil, m, nk, n):
    uniq, _ = _band_window(m, nk, n)
    n_heads = rel_bias_h.shape[1]
    span = nk + m - 1
    out = []
    for off in uniq:
        rel = off - (m - 1) + np.arange(span)
        onehot = np.zeros((span, N_BUCKETS), np.float32)
        onehot[np.arange(span), _t5_bucket(dil * rel)] = 1.0
        table = jnp.dot(jnp.asarray(onehot), rel_bias_h.astype(F32), precision=lax.Precision.HIGHEST)
        table = jnp.where((np.abs(rel) <= half)[:, None], table, NEG)
        u = jnp.concatenate([table.T, jnp.zeros((n_heads, 1), F32)], axis=1)
        flat = jnp.tile(u, (1, m + 1))[:, m - 1:m - 1 + m * span]
        b = flat.reshape(n_heads, m, span)[:, :, :nk]
        out.append(b.reshape(n_heads // 2, 2 * m, nk))
    return jnp.stack(out) * LOG2E


def _band_attention(qkv, bias, *, nk, sub, kv_of, q_width, kv_width, out_tiles, sink=None, gain=None):
    Bd, n, _ = qkv.shape
    m = TM_ATT
    nb = n // m
    assert nb * m == n and Bd % sub == 0
    _, var = _band_window(m, nk, n)
    kcol = q_width // kv_width
    with_sink = sink is not None

    def variant(b, j):
        v = jnp.where(j == nb - 1, var[-1], var[min(1, nb - 1)])
        return jnp.where(j == 0, var[0], v)

    in_specs = [
        pl.BlockSpec((sub, m, q_width), lambda b, j: (b, j, 0)),
        pl.BlockSpec((sub, n, kv_width), lambda b, j: (b, 0, kcol)),
        pl.BlockSpec((sub, n, kv_width), lambda b, j: (b, 0, kcol + 1)),
        pl.BlockSpec((1, len(kv_of), 2 * m, nk), lambda b, j: (variant(b, j), 0, 0, 0)),
    ]
    args = [qkv, qkv, qkv, bias]
    if with_sink:
        in_specs = [pl.BlockSpec(memory_space=pltpu.SMEM),
                    pl.BlockSpec((1, q_width), lambda b, j: (0, 0))] + in_specs
        args = [sink, gain] + args
    return pl.pallas_call(
        functools.partial(_band_attn_kernel, m=m, nk=nk, n=n, sub=sub, kv_of=kv_of, with_sink=with_sink),
        grid=(Bd // sub, nb),
        in_specs=in_specs,
        out_specs=pl.BlockSpec((sub, out_tiles, m, LANES), lambda b, j: (b, 0, j, 0)),
        out_shape=jax.ShapeDtypeStruct((Bd, out_tiles, n, LANES), F32),
        compiler_params=_cparams("parallel", "arbitrary"),
        name="band_attn_sink" if with_sink else f"band_attn_n{n}",
    )(*args)


def _merge_kernel(ya_ref, o1_ref, o4_ref, o16_ref, h0_ref, gb_ref, w_ref, g1_ref, b1_ref, h1_ref, s16_scr):
    n4 = TM_MERGE // MERGE_STRIDE
    n16 = TM_MERGE // 16
    for r16 in range(16):
        for c in range(B_OUT // LANES):
            s16_scr[c, pl.ds((r16 % 4) * n4 + r16 // 4, n16, stride=4), :] = o16_ref[0, r16, c]

    def natural(ref, c):
        return jnp.concatenate([ref[0, c, pl.ds(r, n4, stride=MERGE_STRIDE), :]
                                for r in range(MERGE_STRIDE)], axis=0)

    def stride4(c):
        return jnp.concatenate([o4_ref[0, r, c] for r in range(MERGE_STRIDE)], axis=0)

    def stride16(c):
        return s16_scr[c]

    n_val = B_WIDTH // LANES
    lses = (natural(o1_ref, n_val), stride4(n_val), stride16(n_val))
    mx = jnp.maximum(jnp.maximum(lses[0], lses[1]), lses[2])
    ex = [jnp.exp(l - mx) for l in lses]
    inv = 1.0 / (ex[0] + ex[1] + ex[2])
    head = lax.broadcasted_iota(jnp.int32, (LANES, B_WIDTH), 0)
    col = lax.broadcasted_iota(jnp.int32, (LANES, B_WIDTH), 1)
    spread = jnp.where(col // HEAD_DIM == head, 1.0, 0.0).astype(BF16)
    wide = []
    for e in ex:
        w = e * inv
        w_hi = w.astype(BF16)
        w_lo = (w - w_hi.astype(F32)).astype(BF16)
        wide.append(jnp.dot(w_hi, spread, preferred_element_type=F32)
                    + jnp.dot(w_lo, spread, preferred_element_type=F32))
    pieces = []
    for c in range(n_val):
        cs = slice(c * LANES, (c + 1) * LANES)
        outs = (natural(o1_ref, c), stride4(c), stride16(c))
        pieces.append(wide[0][:, cs] * outs[0] + wide[1][:, cs] * outs[1] + wide[2][:, cs] * outs[2])
    yb = _rms_norm(jnp.concatenate(pieces, axis=1), gb_ref[...])
    ya = jnp.concatenate([natural(ya_ref, c) for c in range(A_WIDTH // LANES)], axis=1)
    y = jnp.concatenate([ya.astype(BF16), yb.astype(BF16)], axis=1)
    mix = jnp.dot(y, w_ref[...], preferred_element_type=F32)
    h0 = jnp.concatenate([natural(h0_ref, c) for c in range(D_MODEL // LANES)], axis=1)
    h1_ref[...] = _layer_norm(ALPHA * h0 + mix, g1_ref[...], b1_ref[...])


def _token_of_row(p):
    n16 = TM_MERGE // MERGE_STRIDE
    tile, rest = p // TM_MERGE, p % TM_MERGE
    return tile * TM_MERGE + (rest % n16) * MERGE_STRIDE + rest // n16


def _merge_project(ya, o1, o4, o16, h0, gain_b, w_out_b, ln_g, ln_b):
    B = h0.shape[0]
    T = B * SEQ
    tiles_per_seq = SEQ // TM_MERGE
    const = lambda i: (0, 0)
    nat = lambda i: (i // tiles_per_seq, 0, i % tiles_per_seq, 0)
    deint = lambda i: (i // tiles_per_seq, 0, 0, i % tiles_per_seq, 0)
    n_b = B_OUT // LANES
    return pl.pallas_call(
        _merge_kernel,
        grid=(T // TM_MERGE,),
        in_specs=[
            pl.BlockSpec((1, A_WIDTH // LANES, TM_MERGE, LANES), nat),
            pl.BlockSpec((1, n_b, TM_MERGE, LANES), nat),
            pl.BlockSpec((1, 4, n_b, TM_MERGE // 4, LANES), deint),
            pl.BlockSpec((1, 16, n_b, TM_MERGE // 16, LANES), deint),
            pl.BlockSpec((1, D_MODEL // LANES, TM_MERGE, LANES), nat),
            pl.BlockSpec((1, B_WIDTH), const),
            pl.BlockSpec((D_MODEL, D_MODEL), const),
            pl.BlockSpec((1, D_MODEL), const),
            pl.BlockSpec((1, D_MODEL), const),
        ],
        out_specs=pl.BlockSpec((TM_MERGE, D_MODEL), lambda i: (i, 0)),
        out_shape=jax.ShapeDtypeStruct((T, D_MODEL), F32),
        scratch_shapes=[pltpu.VMEM((n_b, TM_MERGE, LANES), F32)],
        compiler_params=_cparams("parallel"),
        name="merge_out_proj",
    )(ya, o1, o4, o16, h0, gain_b, w_out_b, ln_g, ln_b)


def _mem_kv_kernel(mem_ref, w_ref, k_ref, v_ref):
    kv = jnp.dot(mem_ref[0].astype(BF16), w_ref[...], preferred_element_type=F32)
    k_ref[0] = kv[:, :D_MODEL].astype(BF16)
    v_ref[0] = kv[:, D_MODEL:].astype(BF16)


def _mem_kv(mem, xkv_b):
    B = mem.shape[0]
    blk = pl.BlockSpec((1, MEM_LEN, D_MODEL), lambda b: (b, 0, 0))
    return pl.pallas_call(
        _mem_kv_kernel,
        grid=(B,),
        in_specs=[blk, pl.BlockSpec((D_MODEL, 2 * D_MODEL), lambda b: (0, 0))],
        out_specs=[blk, blk],
        out_shape=[jax.ShapeDtypeStruct((B, MEM_LEN, D_MODEL), BF16)] * 2,
        compiler_params=_cparams("parallel"),
        name="mem_kv_proj",
    )(mem, xkv_b)


def _route(logits):
    rows = logits.shape[0]
    lane = lax.broadcasted_iota(jnp.int32, (rows, LANES), 1).astype(F32)
    big = float(LANES)
    ninf = -jnp.inf
    gl = jnp.where(lane < N_GROUPS, logits, ninf)
    gmax = jnp.max(gl, axis=-1, keepdims=True)
    gidx = jnp.min(jnp.where(gl == gmax, lane, big), axis=-1, keepdims=True)
    g_p = 1.0 / jnp.sum(jnp.exp(gl - gmax), axis=-1, keepdims=True)
    lo_lane = N_GROUPS + EXPERTS_PER_GROUP * gidx
    el = jnp.where((lane >= lo_lane) & (lane < lo_lane + EXPERTS_PER_GROUP), logits, ninf)
    v1 = jnp.max(el, axis=-1, keepdims=True)
    i1 = jnp.min(jnp.where(el == v1, lane, big), axis=-1, keepdims=True)
    el2 = jnp.where(lane == i1, ninf, el)
    v2 = jnp.max(el2, axis=-1, keepdims=True)
    i2 = jnp.min(jnp.where(el2 == v2, lane, big), axis=-1, keepdims=True)
    t = jnp.exp(v2 - v1)
    w1 = g_p / (1.0 + t)
    w2 = g_p * t / (1.0 + t)
    a = jnp.minimum(i1, i2) - lo_lane
    b = jnp.maximum(i1, i2) - lo_lane
    pair = a * (2 * EXPERTS_PER_GROUP - 1 - a) * 0.5 + (b - a - 1.0)
    cls = gidx * PAIRS_PER_GROUP + pair
    w_lo = jnp.where(i1 < i2, w1, w2)
    w_hi = jnp.where(i1 < i2, w2, w1)
    return jnp.where(lane == 0, cls, jnp.where(lane == 1, w_lo, jnp.where(lane == 2, w_hi, 0.0)))


def _xattn_kernel(h1_ref, k_ref, v_ref, wq_ref, wo_ref, g2_ref, b2_ref, wr_ref, br_ref, xr_ref):
    h1 = h1_ref[...]
    q = jnp.dot(h1.astype(BF16), wq_ref[...], preferred_element_type=F32).astype(BF16)
    outs = []
    for hd in range(X_HEADS):
        sl = slice(hd * X_HEAD_DIM, (hd + 1) * X_HEAD_DIM)
        s = lax.dot_general(q[:, sl], k_ref[0, :, sl], (((1,), (1,)), ((), ())), preferred_element_type=F32)
        m = jnp.max(s, axis=-1, keepdims=True)
        p = jnp.exp2(s - m)
        l = jnp.sum(p, axis=-1, keepdims=True)
        o = jnp.dot(p.astype(BF16), v_ref[0, :, sl], preferred_element_type=F32) / l
        outs.append(o.astype(BF16))
    xa = jnp.dot(jnp.concatenate(outs, axis=1), wo_ref[...], preferred_element_type=F32)
    h2 = _layer_norm(ALPHA * h1 + xa, g2_ref[...], b2_ref[...])
    h_hi = h2.astype(BF16)
    h_lo = (h2 - h_hi.astype(F32)).astype(BF16)
    t_hi = jnp.dot(h_hi, wr_ref[...], preferred_element_type=F32)
    t_lo = jnp.dot(h_lo, wr_ref[...], preferred_element_type=F32)
    logits = (t_hi[:, :LANES] + t_hi[:, LANES:]) + (t_lo[:, :LANES] + t_lo[:, LANES:]) + br_ref[...]
    xr_ref[:, :D_MODEL] = h2
    xr_ref[:, D_MODEL:] = _route(logits)


def _cross_attention_route(h1, k, v, xq_b, xo_b, ln_g, ln_b, wr_split, br):
    T = h1.shape[0]
    tiles_per_seq = SEQ // TM_X
    row = lambda i: (i, 0)
    const = lambda i: (0, 0)
    kv_blk = pl.BlockSpec((1, MEM_LEN, D_MODEL), lambda i: (i // tiles_per_seq, 0, 0))
    return pl.pallas_call(
        _xattn_kernel,
        grid=(T // TM_X,),
        in_specs=[
            pl.BlockSpec((TM_X, D_MODEL), row), kv_blk, kv_blk,
            pl.BlockSpec((D_MODEL, D_MODEL), const),
            pl.BlockSpec((D_MODEL, D_MODEL), const),
            pl.BlockSpec((1, D_MODEL), const),
            pl.BlockSpec((1, D_MODEL), const),
            pl.BlockSpec((D_MODEL, 2 * LANES), const),
            pl.BlockSpec((1, LANES), const),
        ],
        out_specs=pl.BlockSpec((TM_X, XR_WIDTH), row),
        out_shape=jax.ShapeDtypeStruct((T, XR_WIDTH), F32),
        compiler_params=_cparams("parallel"),
        name="xattn_ln_route",
    )(h1, k, v, xq_b, xo_b, ln_g, ln_b, wr_split, br)


def _expert_kernel(src_ref, dst_ref, cnt_ref, elo_ref, ehi_ref, ntile_ref, x_hbm,
                   wg_lo, wu_lo, wd_lo, wg_hi, wu_hi, wd_hi, g3_ref, b3_ref, o_hbm,
                   xbuf, obuf, gsem, ssem):
    j = pl.program_id(0)
    n_tiles = ntile_ref[0]
    slot = j % 2

    def row_copy(t, i, s, gather):
        row = pl.ds(i, 1)
        if gather:
            return pltpu.make_async_copy(x_hbm.at[pl.ds(src_ref[t * TM_MOE + i], 1)], xbuf.at[s, row], gsem.at[s])
        return pltpu.make_async_copy(obuf.at[s, row], o_hbm.at[pl.ds(dst_ref[t * TM_MOE + i], 1)], ssem.at[s])

    def gather_tile(t, s):
        for i in range(TM_MOE):
            row_copy(t, i, s, True).start()

    def wait_gather(s):
        rows = pl.ds(0, TM_MOE)
        pltpu.make_async_copy(x_hbm.at[rows], xbuf.at[s, rows], gsem.at[s]).wait()

    def scatter_tile(t, s):
        cnt = cnt_ref[t]
        n_groups = cnt // ROW_GROUP

        def group(g, c):
            base = pl.multiple_of(g * ROW_GROUP, ROW_GROUP)
            for r in range(ROW_GROUP):
                row_copy(t, base + r, s, False).start()
            return c
        lax.fori_loop(0, n_groups, group, 0)

        def single(i, c):
            row_copy(t, i, s, False).start()
            return c
        lax.fori_loop(n_groups * ROW_GROUP, cnt, single, 0)

    def wait_scatter(t, s):
        cnt = cnt_ref[t]
        for bit in range(TM_MOE.bit_length()):
            rows = pl.ds(0, 1 << bit)

            @pl.when((cnt >> bit) & 1 == 1)
            def _():
                pltpu.make_async_copy(obuf.at[s, rows], o_hbm.at[rows], ssem.at[s]).wait()

    @pl.when(j == 0)
    def _():
        gather_tile(0, 0)

    @pl.when(j < n_tiles)
    def _():
        wait_gather(slot)
        x = xbuf[slot, :, :D_MODEL]
        xb = x.astype(BF16)

        gather_tile(jnp.minimum(j + 1, n_tiles - 1), 1 - slot)

        y = jnp.zeros_like(x)
        for e, (wg, wu, wd) in enumerate(((wg_lo, wu_lo, wd_lo), (wg_hi, wu_hi, wd_hi))):
            gate = xbuf[slot, :, D_MODEL + 1 + e:D_MODEL + 2 + e]
            a = jnp.dot(xb, wg[0], preferred_element_type=F32)
            u = jnp.dot(xb, wu[0], preferred_element_type=F32)
            hid = a * jax.nn.sigmoid(a) * u
            y = y + jnp.dot((gate * hid).astype(BF16), wd[0], preferred_element_type=F32)
        out = _layer_norm(ALPHA * x + y, g3_ref[...], b3_ref[...])

        @pl.when(j >= 2)
        def _():
            wait_scatter(j - 2, slot)

        obuf[slot] = out
        scatter_tile(j, slot)

        @pl.when(j == n_tiles - 1)
        def _():
            @pl.when(j >= 1)
            def _():
                wait_scatter(j - 1, 1 - slot)

            wait_scatter(j, slot)
            wait_gather(1 - slot)


def _expert_mlp(src_rows, dst_rows, tile_cnt, tile_elo, tile_ehi, n_tiles, xr, wg, wu, wd, ln_g, ln_b):
    n_tiles_max = tile_cnt.shape[0]
    const = lambda j, src, dst, cnt, elo, ehi, nt: (0, 0)
    lo = lambda j, src, dst, cnt, elo, ehi, nt: (elo[j], 0, 0)
    hi = lambda j, src, dst, cnt, elo, ehi, nt: (ehi[j], 0, 0)
    up = (1, D_MODEL, D_EXPERT)
    down = (1, D_EXPERT, D_MODEL)
    any_spec = pl.BlockSpec(memory_space=pl.ANY)
    return pl.pallas_call(
        _expert_kernel,
        grid_spec=pltpu.PrefetchScalarGridSpec(
            num_scalar_prefetch=6,
            grid=(n_tiles_max,),
            in_specs=[
                any_spec,
                pl.BlockSpec(up, lo), pl.BlockSpec(up, lo), pl.BlockSpec(down, lo),
                pl.BlockSpec(up, hi), pl.BlockSpec(up, hi), pl.BlockSpec(down, hi),
                pl.BlockSpec((1, D_MODEL), const),
                pl.BlockSpec((1, D_MODEL), const),
            ],
            out_specs=any_spec,
            scratch_shapes=[
                pltpu.VMEM((2, TM_MOE, XR_WIDTH), F32),
                pltpu.VMEM((2, TM_MOE, D_MODEL), F32),
                pltpu.SemaphoreType.DMA((2,)),
                pltpu.SemaphoreType.DMA((2,)),
            ],
        ),
        out_shape=jax.ShapeDtypeStruct((xr.shape[0], D_MODEL), F32),
        compiler_params=_cparams("arbitrary"),
        name="moe_experts",
    )(src_rows, dst_rows, tile_cnt, tile_elo, tile_ehi, n_tiles, xr, wg, wu, wd, wg, wu, wd, ln_g, ln_b)


def _class_experts():
    lo, hi = [], []
    for g in range(N_GROUPS):
        for a in range(EXPERTS_PER_GROUP):
            for b in range(a + 1, EXPERTS_PER_GROUP):
                lo.append(g * EXPERTS_PER_GROUP + a)
                hi.append(g * EXPERTS_PER_GROUP + b)
    return np.asarray(lo, np.int32), np.asarray(hi, np.int32)


def _plan_kernel(route_ref, pos_ref, counts_ref, tri_scr, run_scr, start_scr):
    phase, i = pl.program_id(0), pl.program_id(1)
    lane = lax.broadcasted_iota(jnp.int32, (TM_PLAN, LANES), 1)
    onehot = lane.astype(F32) == route_ref[:, 0:1]
    onehot_f = jnp.where(onehot, 1.0, 0.0)

    @pl.when((phase == 0) & (i == 0))
    def _():
        run_scr[...] = jnp.zeros_like(run_scr)
        r = lax.broadcasted_iota(jnp.int32, (TM_PLAN, TM_PLAN), 0)
        c = lax.broadcasted_iota(jnp.int32, (TM_PLAN, TM_PLAN), 1)
        tri_scr[...] = jnp.where(c < r, 1.0, 0.0).astype(BF16)

    @pl.when((phase == 1) & (i == 0))
    def _():
        counts = run_scr[...]
        counts_ref[...] = counts
        tiles = jnp.floor((counts + (TM_MOE - 1)) * (1.0 / TM_MOE))
        lane_row = lax.broadcasted_iota(jnp.int32, (1, LANES), 1)
        scan = tiles
        shift = 1
        while shift < LANES:
            scan = scan + jnp.where(lane_row >= shift, pltpu.roll(scan, shift, axis=1), 0.0)
            shift *= 2
        start_scr[...] = (scan - tiles) * TM_MOE
        run_scr[...] = jnp.zeros_like(run_scr)

    @pl.when(phase == 1)
    def _():
        before = jnp.dot(tri_scr[...], onehot_f.astype(BF16), preferred_element_type=F32)
        pos_col = jnp.sum(onehot_f * (before + run_scr[...] + start_scr[...]), axis=1, keepdims=True)
        eye = (lax.broadcasted_iota(jnp.int32, (LANES, LANES), 0)
               == lax.broadcasted_iota(jnp.int32, (LANES, LANES), 1))
        for r in range(TM_PLAN // LANES):
            row = jnp.sum(jnp.where(eye, pos_col[r * LANES:(r + 1) * LANES], 0.0), axis=0, keepdims=True)
            pos_ref[r:r + 1, :] = row.astype(jnp.int32)

    run_scr[...] = run_scr[...] + jnp.sum(onehot_f, axis=0, keepdims=True)


def _plan_positions(xr):
    T = xr.shape[0]
    n_steps = T // TM_PLAN
    pos, counts = pl.pallas_call(
        _plan_kernel,
        grid=(2, n_steps),
        in_specs=[pl.BlockSpec((TM_PLAN, LANES), lambda p, i: (i, D_MODEL // LANES))],
        out_specs=[pl.BlockSpec((TM_PLAN // LANES, LANES), lambda p, i: (i * p, 0)),
                   pl.BlockSpec((1, LANES), lambda p, i: (0, 0))],
        out_shape=[jax.ShapeDtypeStruct((T // LANES, LANES), jnp.int32),
                   jax.ShapeDtypeStruct((1, LANES), F32)],
        scratch_shapes=[pltpu.VMEM((TM_PLAN, TM_PLAN), BF16),
                        pltpu.VMEM((1, LANES), F32),
                        pltpu.VMEM((1, LANES), F32)],
        compiler_params=_cparams("arbitrary", "arbitrary"),
        name="moe_plan",
    )(xr)
    return pos.reshape(T), counts[0, :N_CLASSES].astype(jnp.int32)


def _moe_plan(xr, n_tiles_max):
    T = xr.shape[0]
    pos, counts = _plan_positions(xr)
    tiles_per_class = (counts + TM_MOE - 1) // TM_MOE
    tile_end = jnp.cumsum(tiles_per_class)
    tile_start = tile_end - tiles_per_class
    n_tiles = tile_end[-1]
    tok = jnp.arange(T, dtype=jnp.int32)
    src_rows = jnp.zeros((n_tiles_max * TM_MOE,), jnp.int32).at[pos].set(tok)
    tile_ids = jnp.arange(n_tiles_max, dtype=jnp.int32)
    used = jnp.minimum(tile_ids, n_tiles - 1)
    tile_cls = jnp.sum((tile_end[None, :] <= used[:, None]).astype(jnp.int32), axis=1)
    tile_cls = jnp.minimum(tile_cls, N_CLASSES - 1)
    cls_lo, cls_hi = _class_experts()
    tile_elo = jnp.asarray(cls_lo)[tile_cls]
    tile_ehi = jnp.asarray(cls_hi)[tile_cls]
    within = tile_ids - tile_start[tile_cls]
    tile_cnt = jnp.clip(counts[tile_cls] - within * TM_MOE, 0, TM_MOE)
    tile_cnt = jnp.where(tile_ids < n_tiles, tile_cnt, 0).astype(jnp.int32)
    return src_rows, tile_elo, tile_ehi, tile_cnt, n_tiles.reshape(1).astype(jnp.int32)


def _vec(a):
    return a.reshape(1, -1).astype(F32)


def _mixer_and_cross_attention(x, mem, ln_in_g, ln_in_b, w_in, rel_bias, sink_a, norm_a_g, norm_b_g, w_out,
                               ln1_g, ln1_b, xq, xkv, xo, ln2_g, ln2_b, w_group, b_group, w_router, b_router):
    B, S, D = x.shape
    assert S == SEQ and D == D_MODEL and mem.shape[1:] == (MEM_LEN, D_MODEL)
    T = B * S
    vec = _vec

    w = w_in[0]
    edges = np.cumsum((0, A_WIDTH, A_KV_HEADS * HEAD_DIM, A_KV_HEADS * HEAD_DIM, B_WIDTH, B_WIDTH, B_WIDTH))
    qa, ka, va, qb, kb, vb = [w[:, a:b] for a, b in zip(edges[:-1], edges[1:])]
    dup = lambda t: jnp.repeat(t.reshape(D, A_KV_HEADS, 1, HEAD_DIM), 2, axis=2).reshape(D, A_KV_TILES * LANES)
    scale = HEAD_DIM ** -0.5 * LOG2E
    w_in_b = jnp.concatenate([qa * scale, dup(ka), dup(va), qb * scale, kb, vb], axis=1).astype(BF16)
    w_out_b = w_out[0].astype(BF16)
    xq_b = (xq[0] * (X_HEAD_DIM ** -0.5 * LOG2E)).astype(BF16)
    xkv_b = xkv[0].astype(BF16)
    xo_b = xo[0].astype(BF16)
    wr = jnp.concatenate([w_group[0], w_router[0]], axis=1).astype(F32)
    wr = jnp.pad(wr, ((0, 0), (0, LANES - wr.shape[1])))
    wr_hi = wr.astype(BF16)
    wr_split = jnp.concatenate([wr_hi, (wr - wr_hi.astype(F32)).astype(BF16)], axis=1)
    br = jnp.concatenate([b_group[0], b_router[0]]).astype(F32)
    br = jnp.pad(br, (0, LANES - br.shape[0])).reshape(1, LANES)

    h0, qkv_a, qkv_b1, qkv_b4, qkv_b16 = _input_projection(
        x.reshape(T, D), vec(ln_in_g), vec(ln_in_b), w_in_b, B)

    nk_a = TM_ATT + 2 * A_HALF_WIN
    bias_a = _band_bias(rel_bias[:, :A_HEADS], A_HALF_WIN, 1, TM_ATT, nk_a, S)
    ya = _band_attention(qkv_a.reshape(B, S, QKV_A), bias_a, nk=nk_a, sub=1, kv_of=(0, 0, 1, 1),
                         q_width=A_WIDTH, kv_width=A_KV_TILES * LANES, out_tiles=A_WIDTH // LANES,
                         sink=sink_a[0].astype(F32) * LOG2E, gain=vec(norm_a_g[0]))
    branch_out = []
    for (win, dil), qkv in zip(B_BRANCHES, (qkv_b1.reshape(B, S, QKV_B),
                                             qkv_b4.reshape(B * 4, S // 4, QKV_B),
                                             qkv_b16.reshape(B * 16, S // 16, QKV_B))):
        half = (win // 2) // dil
        n = S // dil
        nk = min(TM_ATT + 2 * half, n)
        bias_b = _band_bias(rel_bias[:, A_HEADS:], half, dil, TM_ATT, nk, n)
        branch_out.append(_band_attention(qkv, bias_b, nk=nk, sub=max(1, TM_ATT * 4 // n), kv_of=(0, 1, 2, 3),
                                          q_width=B_WIDTH, kv_width=B_WIDTH, out_tiles=B_OUT // LANES))
    o1 = branch_out[0]
    o4 = branch_out[1].reshape(B, 4, B_OUT // LANES, S // 4, LANES)
    o16 = branch_out[2].reshape(B, 16, B_OUT // LANES, S // 16, LANES)

    h1 = _merge_project(ya, o1, o4, o16, h0, vec(norm_b_g[0]), w_out_b,
                        vec(ln1_g[0]), vec(ln1_b[0]))

    k_mem, v_mem = _mem_kv(mem, xkv_b)
    xr = _cross_attention_route(h1, k_mem, v_mem, xq_b, xo_b, vec(ln2_g[0]), vec(ln2_b[0]), wr_split, br)
    return h0, h1, xr


def _moe(xr, w_gate, w_up, w_down, ln3_g, ln3_b):
    T = xr.shape[0]
    wg_b = w_gate[0].reshape(N_EXPERTS, D_MODEL, D_EXPERT).astype(BF16)
    wu_b = w_up[0].reshape(N_EXPERTS, D_MODEL, D_EXPERT).astype(BF16)
    wd_b = w_down[0].reshape(N_EXPERTS, D_EXPERT, D_MODEL).astype(BF16)
    n_tiles_max = T // TM_MOE + N_CLASSES
    src_rows, tile_elo, tile_ehi, tile_cnt, n_tiles = _moe_plan(xr, n_tiles_max)
    return _expert_mlp(src_rows, _token_of_row(src_rows), tile_cnt, tile_elo, tile_ehi, n_tiles, xr, wg_b, wu_b, wd_b,
                       _vec(ln3_g[0]), _vec(ln3_b[0]))


def kernel(x, mem, ln_in_g, ln_in_b, w_in, rel_bias, sink_a, norm_a_g, norm_b_g, w_out,
           ln1_g, ln1_b, xq, xkv, xo, ln2_g, ln2_b, w_group, b_group, w_router, b_router,
           w_gate, w_up, w_down, ln3_g, ln3_b):
    _, _, xr = _mixer_and_cross_attention(
        x, mem, ln_in_g, ln_in_b, w_in, rel_bias, sink_a, norm_a_g, norm_b_g, w_out,
        ln1_g, ln1_b, xq, xkv, xo, ln2_g, ln2_b, w_group, b_group, w_router, b_router)
    return _moe(xr, w_gate, w_up, w_down, ln3_g, ln3_b).reshape(x.shape)
```

```python
import functools

import numpy as np
import jax
import jax.numpy as jnp
from jax import lax
from jax.experimental import pallas as pl
from jax.experimental.pallas import tpu as pltpu

F32 = jnp.float32
BF16 = jnp.bfloat16

D_MODEL = 1024
SEQ = 2048
MEM_LEN = 256
HEAD_DIM = 64
A_HEADS = 8
A_KV_HEADS = 2
A_HALF_WIN = 128
B_HEADS = 8
B_BRANCHES = ((128, 1), (512, 4), (2048, 16))
N_BUCKETS = 32
MAX_DISTANCE = 1024
X_HEADS = 4
X_HEAD_DIM = D_MODEL // X_HEADS
N_GROUPS = 4
EXPERTS_PER_GROUP = 8
N_EXPERTS = N_GROUPS * EXPERTS_PER_GROUP
D_EXPERT = 512
DEPTH = 1
ALPHA = (2.0 * DEPTH) ** 0.25
LN_EPS = 1e-5
NEG = -1e30
LOG2E = 1.4426950408889634
LN2 = 0.6931471805599453

LANES = 128
XR_WIDTH = D_MODEL + LANES
A_WIDTH = A_HEADS * HEAD_DIM
B_WIDTH = B_HEADS * HEAD_DIM
A_KV_TILES = A_KV_HEADS
QKV_A = A_WIDTH + 2 * A_KV_TILES * LANES
QKV_B = 3 * B_WIDTH
B_OUT = B_WIDTH + LANES
MERGE_STRIDE = 4

PAIRS_PER_GROUP = EXPERTS_PER_GROUP * (EXPERTS_PER_GROUP - 1) // 2
N_CLASSES = N_GROUPS * PAIRS_PER_GROUP

TM_IN = 512
TM_ATT = 128
ATT_BLOCKS = 4
TM_MERGE = 512
TM_X = 512
TM_MOE = 128
ROW_GROUP = 8
TM_PLAN = 1024
VMEM_LIMIT = 56 * 1024 * 1024


def _cparams(*sem):
    return pltpu.CompilerParams(dimension_semantics=sem, vmem_limit_bytes=VMEM_LIMIT)


def _layer_norm(x, g, b):
    mu = jnp.mean(x, axis=-1, keepdims=True)
    xc = x - mu
    var = jnp.mean(xc * xc, axis=-1, keepdims=True)
    return xc * lax.rsqrt(var + LN_EPS) * g + b


def _rms_norm(x, g):
    return x * lax.rsqrt(jnp.mean(x * x, axis=-1, keepdims=True) + LN_EPS) * g


def _t5_bucket(rel):
    nb = N_BUCKETS // 2
    max_exact = nb // 2
    ret = (rel > 0).astype(np.int32) * nb
    n = np.abs(rel)
    n_safe = np.maximum(n, 1).astype(np.float64)
    large = max_exact + (np.log(n_safe / max_exact) / np.log(MAX_DISTANCE / max_exact)
                         * (nb - max_exact)).astype(np.int32)
    large = np.minimum(large, nb - 1)
    return (ret + np.where(n < max_exact, n, large)).astype(np.int32)


def _inproj_kernel(x_ref, g_ref, b_ref, w_ref, h0_ref, qa_ref, qb1_ref, qb4_ref, qb16_ref, pb_scr, p4_scr):
    h = _layer_norm(x_ref[...], g_ref[...], b_ref[...])
    for c in range(D_MODEL // LANES):
        h0_ref[0, c] = h[:, c * LANES:(c + 1) * LANES]
    proj = jnp.dot(h.astype(BF16), w_ref[...], preferred_element_type=F32)
    qa_ref[...] = proj[:, :QKV_A].astype(BF16)
    pb = proj[:, QKV_A:]
    qb1_ref[...] = pb.astype(BF16)
    n4 = TM_IN // 4
    for c in range(QKV_B // LANES):
        cs = slice(c * LANES, (c + 1) * LANES)
        pb_scr[c] = pb[:, cs]
        for r4 in range(4):
            rows = pb_scr[c, pl.ds(r4, n4, stride=4), :]
            qb4_ref[0, r4, :, cs] = rows.astype(BF16)
            p4_scr[c, r4 * n4:(r4 + 1) * n4, :] = rows
        for r16 in range(16):
            rows = p4_scr[c, pl.ds((r16 % 4) * n4 + r16 // 4, TM_IN // 16, stride=4), :]
            qb16_ref[0, r16, :, cs] = rows.astype(BF16)


def _input_projection(x2, ln_g, ln_b, w_in_b, batch):
    T = x2.shape[0]
    tiles_per_seq = SEQ // TM_IN
    row = lambda i: (i, 0)
    const = lambda i: (0, 0)
    deint = lambda i: (i // tiles_per_seq, 0, i % tiles_per_seq, 0)
    return pl.pallas_call(
        _inproj_kernel,
        grid=(T // TM_IN,),
        in_specs=[
            pl.BlockSpec((TM_IN, D_MODEL), row),
            pl.BlockSpec((1, D_MODEL), const),
            pl.BlockSpec((1, D_MODEL), const),
            pl.BlockSpec((D_MODEL, QKV_A + QKV_B), const),
        ],
        out_specs=[
            pl.BlockSpec((1, D_MODEL // LANES, TM_IN, LANES), deint),
            pl.BlockSpec((TM_IN, QKV_A), row),
            pl.BlockSpec((TM_IN, QKV_B), row),
            pl.BlockSpec((1, 4, TM_IN // 4, QKV_B), deint),
            pl.BlockSpec((1, 16, TM_IN // 16, QKV_B), deint),
        ],
        out_shape=[
            jax.ShapeDtypeStruct((batch, D_MODEL // LANES, SEQ, LANES), F32),
            jax.ShapeDtypeStruct((T, QKV_A), BF16),
            jax.ShapeDtypeStruct((T, QKV_B), BF16),
            jax.ShapeDtypeStruct((batch, 4, SEQ // 4, QKV_B), BF16),
            jax.ShapeDtypeStruct((batch, 16, SEQ // 16, QKV_B), BF16),
        ],
        scratch_shapes=[pltpu.VMEM((QKV_B // LANES, TM_IN, LANES), F32)] * 2,
        compiler_params=_cparams("parallel"),
        name="ln_in_proj",
    )(x2, ln_g, ln_b, w_in_b)


def _band_attn_kernel(*refs, m, nk, n, sub, qb, kv_of, variants, with_sink):
    if with_sink:
        sink_ref, gain_ref = refs[0], refs[1]
        refs = refs[2:]
    q_ref, k_ref, v_ref, bias_ref, o_ref = refs
    nb = n // m
    first, middle, last = variants
    starts, bias_var = [], []
    for b in range(qb):
        if nk == n:
            starts.append(0)
            bias_var.append(first)
        else:
            jj = pl.program_id(1) * qb + b
            starts.append(pl.multiple_of(jnp.clip(jj * m - (nk - m) // 2, 0, n - nk), HEAD_DIM))
            bias_var.append(jnp.where(jj == 0, first, jnp.where(jj == nb - 1, last, middle)))
    lane_row = lax.broadcasted_iota(jnp.int32, (1, LANES), 1)
    keep_lo = jnp.where(lane_row < HEAD_DIM, 1.0, 0.0).astype(BF16)
    keep_hi = jnp.where(lane_row < HEAD_DIM, 0.0, 1.0).astype(BF16)
    lane = lax.broadcasted_iota(jnp.int32, (m, LANES), 1)
    n_q = len(kv_of)
    units = [(s, b, qt) for s in range(sub) for b in range(qb) for qt in range(n_q)]

    scores = []
    for s, b, qt in units:
        t = kv_of[qt]
        k_t = k_ref[s, pl.ds(starts[b], nk), t * LANES:(t + 1) * LANES]
        q2 = q_ref[s, b * m:(b + 1) * m, qt * LANES:(qt + 1) * LANES]
        lhs = jnp.concatenate([q2 * keep_lo, q2 * keep_hi], axis=0)
        sc = lax.dot_general(lhs, k_t, (((1,), (1,)), ((), ())), preferred_element_type=F32)
        scores.append(sc + bias_ref[bias_var[b], qt])

    probs, denom, row_max = [], [], []
    for (s, b, qt), sc in zip(units, scores):
        mx = jnp.max(sc, axis=-1, keepdims=True)
        p = jnp.exp2(sc - mx)
        l = jnp.sum(p, axis=-1, keepdims=True)
        halves = []
        for h in (0, 1):
            l_h = l[h * m:(h + 1) * m]
            if with_sink:
                l_h = l_h + jnp.exp2(sink_ref[2 * qt + h] - mx[h * m:(h + 1) * m])
            halves.append(l_h)
        probs.append(p.astype(BF16))
        denom.append(halves)
        row_max.append(mx)

    for s, b in [(s, b) for s in range(sub) for b in range(qb)]:
        rows = slice(b * m, (b + 1) * m)
        pairs = []
        lse_tile = jnp.zeros((m, LANES), F32)
        for qt in range(n_q):
            u = (s * qb + b) * n_q + qt
            t = kv_of[qt]
            v_t = v_ref[s, pl.ds(starts[b], nk), t * LANES:(t + 1) * LANES]
            o = jnp.dot(probs[u], v_t, preferred_element_type=F32)
            top = o[:m] * (1.0 / denom[u][0])
            bot = o[m:] * (1.0 / denom[u][1])
            pairs.append(jnp.where(lane < HEAD_DIM, top, bot))
            if not with_sink:
                for h in (0, 1):
                    lse = (row_max[u][h * m:(h + 1) * m] + jnp.log2(denom[u][h])) * LN2
                    lse_tile = jnp.where(lane == 2 * qt + h, lse, lse_tile)
        if with_sink:
            normed = _rms_norm(jnp.concatenate(pairs, axis=1), gain_ref[...])
            pairs = [normed[:, c * LANES:(c + 1) * LANES] for c in range(n_q)]
        else:
            o_ref[s, n_q, rows, :] = lse_tile
        for c in range(n_q):
            o_ref[s, c, rows, :] = pairs[c]


def _band_window(m, nk, n):
    nb = n // m
    starts = np.clip(np.arange(nb) * m - (nk - m) // 2, 0, n - nk)
    offs = [int(o) for o in starts - np.arange(nb) * m]
    uniq = sorted(set(offs), reverse=True)
    var = [uniq.index(o) for o in offs]
    assert all(v == var[1] for v in var[1:-1])
    return uniq, var


def _band_bias(rel_bias_h, half, dil, m, nk, n):
    uniq, _ = _band_window(m, nk, n)
    n_heads = rel_bias_h.shape[1]
    span = nk + m - 1
    out = []
    for off in uniq:
        rel = off - (m - 1) + np.arange(span)
        onehot = np.zeros((span, N_BUCKETS), np.float32)
        onehot[np.arange(span), _t5_bucket(dil * rel)] = 1.0
        table = jnp.dot(jnp.asarray(onehot), rel_bias_h.astype(F32), precision=lax.Precision.HIGHEST)
        table = jnp.where((np.abs(rel) <= half)[:, None], table, NEG)
        u = jnp.concatenate([table.T, jnp.zeros((n_heads, 1), F32)], axis=1)
        flat = jnp.tile(u, (1, m + 1))[:, m - 1:m - 1 + m * span]
        b = flat.reshape(n_heads, m, span)[:, :, :nk]
        out.append(b.reshape(n_heads // 2, 2 * m, nk))
    return jnp.stack(out) * LOG2E


def _band_attention(qkv, bias, *, nk, sub, kv_of, q_width, kv_width, out_tiles, sink=None, gain=None):
    Bd, n, _ = qkv.shape
    m = TM_ATT
    nb = n // m
    qb = min(nb, ATT_BLOCKS)
    assert nb % qb == 0 and nb * m == n and Bd % sub == 0
    _, var = _band_window(m, nk, n)
    variants = (var[0], var[min(1, nb - 1)], var[-1])
    kcol = q_width // kv_width
    with_sink = sink is not None
    in_specs = [
        pl.BlockSpec((sub, qb * m, q_width), lambda b, j: (b, j, 0)),
        pl.BlockSpec((sub, n, kv_width), lambda b, j: (b, 0, kcol)),
        pl.BlockSpec((sub, n, kv_width), lambda b, j: (b, 0, kcol + 1)),
        pl.BlockSpec(bias.shape, lambda b, j: (0, 0, 0, 0)),
    ]
    args = [qkv, qkv, qkv, bias]
    if with_sink:
        in_specs = [pl.BlockSpec(memory_space=pltpu.SMEM),
                    pl.BlockSpec((1, q_width), lambda b, j: (0, 0))] + in_specs
        args = [sink, gain] + args
    return pl.pallas_call(
        functools.partial(_band_attn_kernel, m=m, nk=nk, n=n, sub=sub, qb=qb, kv_of=kv_of, variants=variants,
                          with_sink=with_sink),
        grid=(Bd // sub, nb // qb),
        in_specs=in_specs,
        out_specs=pl.BlockSpec((sub, out_tiles, qb * m, LANES), lambda b, j: (b, 0, j, 0)),
        out_shape=jax.ShapeDtypeStruct((Bd, out_tiles, n, LANES), F32),
        compiler_params=_cparams("parallel", "arbitrary"),
        name="band_attn_sink" if with_sink else f"band_attn_n{n}",
    )(*args)


def _merge_kernel(ya_ref, o1_ref, o4_ref, o16_ref, h0_ref, gb_ref, w_ref, g1_ref, b1_ref, h1_ref, s16_scr):
    n4 = TM_MERGE // MERGE_STRIDE
    n16 = TM_MERGE // 16
    for r16 in range(16):
        for c in range(B_OUT // LANES):
            s16_scr[c, pl.ds((r16 % 4) * n4 + r16 // 4, n16, stride=4), :] = o16_ref[0, r16, c]

    def natural(ref, c):
        return jnp.concatenate([ref[0, c, pl.ds(r, n4, stride=MERGE_STRIDE), :]
                                for r in range(MERGE_STRIDE)], axis=0)

    def stride4(c):
        return jnp.concatenate([o4_ref[0, r, c] for r in range(MERGE_STRIDE)], axis=0)

    def stride16(c):
        return s16_scr[c]

    n_val = B_WIDTH // LANES
    lses = (natural(o1_ref, n_val), stride4(n_val), stride16(n_val))
    mx = jnp.maximum(jnp.maximum(lses[0], lses[1]), lses[2])
    ex = [jnp.exp(l - mx) for l in lses]
    inv = 1.0 / (ex[0] + ex[1] + ex[2])
    head = lax.broadcasted_iota(jnp.int32, (LANES, B_WIDTH), 0)
    col = lax.broadcasted_iota(jnp.int32, (LANES, B_WIDTH), 1)
    spread = jnp.where(col // HEAD_DIM == head, 1.0, 0.0).astype(BF16)
    wide = []
    for e in ex:
        w = e * inv
        w_hi = w.astype(BF16)
        w_lo = (w - w_hi.astype(F32)).astype(BF16)
        wide.append(jnp.dot(w_hi, spread, preferred_element_type=F32)
                    + jnp.dot(w_lo, spread, preferred_element_type=F32))
    pieces = []
    for c in range(n_val):
        cs = slice(c * LANES, (c + 1) * LANES)
        outs = (natural(o1_ref, c), stride4(c), stride16(c))
        pieces.append(wide[0][:, cs] * outs[0] + wide[1][:, cs] * outs[1] + wide[2][:, cs] * outs[2])
    yb = _rms_norm(jnp.concatenate(pieces, axis=1), gb_ref[...])
    ya = jnp.concatenate([natural(ya_ref, c) for c in range(A_WIDTH // LANES)], axis=1)
    y = jnp.concatenate([ya.astype(BF16), yb.astype(BF16)], axis=1)
    mix = jnp.dot(y, w_ref[...], preferred_element_type=F32)
    h0 = jnp.concatenate([natural(h0_ref, c) for c in range(D_MODEL // LANES)], axis=1)
    h1_ref[...] = _layer_norm(ALPHA * h0 + mix, g1_ref[...], b1_ref[...])


def _token_of_row(p):
    n16 = TM_MERGE // MERGE_STRIDE
    tile, rest = p // TM_MERGE, p % TM_MERGE
    return tile * TM_MERGE + (rest % n16) * MERGE_STRIDE + rest // n16


def _merge_project(ya, o1, o4, o16, h0, gain_b, w_out_b, ln_g, ln_b):
    B = h0.shape[0]
    T = B * SEQ
    tiles_per_seq = SEQ // TM_MERGE
    const = lambda i: (0, 0)
    nat = lambda i: (i // tiles_per_seq, 0, i % tiles_per_seq, 0)
    deint = lambda i: (i // tiles_per_seq, 0, 0, i % tiles_per_seq, 0)
    n_b = B_OUT // LANES
    return pl.pallas_call(
        _merge_kernel,
        grid=(T // TM_MERGE,),
        in_specs=[
            pl.BlockSpec((1, A_WIDTH // LANES, TM_MERGE, LANES), nat),
            pl.BlockSpec((1, n_b, TM_MERGE, LANES), nat),
            pl.BlockSpec((1, 4, n_b, TM_MERGE // 4, LANES), deint),
            pl.BlockSpec((1, 16, n_b, TM_MERGE // 16, LANES), deint),
            pl.BlockSpec((1, D_MODEL // LANES, TM_MERGE, LANES), nat),
            pl.BlockSpec((1, B_WIDTH), const),
            pl.BlockSpec((D_MODEL, D_MODEL), const),
            pl.BlockSpec((1, D_MODEL), const),
            pl.BlockSpec((1, D_MODEL), const),
        ],
        out_specs=pl.BlockSpec((TM_MERGE, D_MODEL), lambda i: (i, 0)),
        out_shape=jax.ShapeDtypeStruct((T, D_MODEL), F32),
        scratch_shapes=[pltpu.VMEM((n_b, TM_MERGE, LANES), F32)],
        compiler_params=_cparams("parallel"),
        name="merge_out_proj",
    )(ya, o1, o4, o16, h0, gain_b, w_out_b, ln_g, ln_b)


def _mem_kv_kernel(mem_ref, w_ref, k_ref, v_ref):
    kv = jnp.dot(mem_ref[0].astype(BF16), w_ref[...], preferred_element_type=F32)
    k_ref[0] = kv[:, :D_MODEL].astype(BF16)
    v_ref[0] = kv[:, D_MODEL:].astype(BF16)


def _mem_kv(mem, xkv_b):
    B = mem.shape[0]
    blk = pl.BlockSpec((1, MEM_LEN, D_MODEL), lambda b: (b, 0, 0))
    return pl.pallas_call(
        _mem_kv_kernel,
        grid=(B,),
        in_specs=[blk, pl.BlockSpec((D_MODEL, 2 * D_MODEL), lambda b: (0, 0))],
        out_specs=[blk, blk],
        out_shape=[jax.ShapeDtypeStruct((B, MEM_LEN, D_MODEL), BF16)] * 2,
        compiler_params=_cparams("parallel"),
        name="mem_kv_proj",
    )(mem, xkv_b)


def _route(logits):
    rows = logits.shape[0]
    lane = lax.broadcasted_iota(jnp.int32, (rows, LANES), 1).astype(F32)
    big = float(LANES)
    ninf = -jnp.inf
    gl = jnp.where(lane < N_GROUPS, logits, ninf)
    gmax = jnp.max(gl, axis=-1, keepdims=True)
    gidx = jnp.min(jnp.where(gl == gmax, lane, big), axis=-1, keepdims=True)
    g_p = 1.0 / jnp.sum(jnp.exp(gl - gmax), axis=-1, keepdims=True)
    lo_lane = N_GROUPS + EXPERTS_PER_GROUP * gidx
    el = jnp.where((lane >= lo_lane) & (lane < lo_lane + EXPERTS_PER_GROUP), logits, ninf)
    v1 = jnp.max(el, axis=-1, keepdims=True)
    i1 = jnp.min(jnp.where(el == v1, lane, big), axis=-1, keepdims=True)
    el2 = jnp.where(lane == i1, ninf, el)
    v2 = jnp.max(el2, axis=-1, keepdims=True)
    i2 = jnp.min(jnp.where(el2 == v2, lane, big), axis=-1, keepdims=True)
    t = jnp.exp(v2 - v1)
    w1 = g_p / (1.0 + t)
    w2 = g_p * t / (1.0 + t)
    a = jnp.minimum(i1, i2) - lo_lane
    b = jnp.maximum(i1, i2) - lo_lane
    pair = a * (2 * EXPERTS_PER_GROUP - 1 - a) * 0.5 + (b - a - 1.0)
    cls = gidx * PAIRS_PER_GROUP + pair
    w_lo = jnp.where(i1 < i2, w1, w2)
    w_hi = jnp.where(i1 < i2, w2, w1)
    return jnp.where(lane == 0, cls, jnp.where(lane == 1, w_lo, jnp.where(lane == 2, w_hi, 0.0)))


def _xattn_kernel(h1_ref, k_ref, v_ref, wq_ref, wo_ref, g2_ref, b2_ref, wr_ref, br_ref, xr_ref):
    h1 = h1_ref[...]
    q = jnp.dot(h1.astype(BF16), wq_ref[...], preferred_element_type=F32).astype(BF16)
    outs = []
    for hd in range(X_HEADS):
        sl = slice(hd * X_HEAD_DIM, (hd + 1) * X_HEAD_DIM)
        s = lax.dot_general(q[:, sl], k_ref[0, :, sl], (((1,), (1,)), ((), ())), preferred_element_type=F32)
        m = jnp.max(s, axis=-1, keepdims=True)
        p = jnp.exp2(s - m)
        l = jnp.sum(p, axis=-1, keepdims=True)
        o = jnp.dot(p.astype(BF16), v_ref[0, :, sl], preferred_element_type=F32) / l
        outs.append(o.astype(BF16))
    xa = jnp.dot(jnp.concatenate(outs, axis=1), wo_ref[...], preferred_element_type=F32)
    h2 = _layer_norm(ALPHA * h1 + xa, g2_ref[...], b2_ref[...])
    h_hi = h2.astype(BF16)
    h_lo = (h2 - h_hi.astype(F32)).astype(BF16)
    t_hi = jnp.dot(h_hi, wr_ref[...], preferred_element_type=F32)
    t_lo = jnp.dot(h_lo, wr_ref[...], preferred_element_type=F32)
    logits = (t_hi[:, :LANES] + t_hi[:, LANES:]) + (t_lo[:, :LANES] + t_lo[:, LANES:]) + br_ref[...]
    xr_ref[:, :D_MODEL] = h2
    xr_ref[:, D_MODEL:] = _route(logits)


def _cross_attention_route(h1, k, v, xq_b, xo_b, ln_g, ln_b, wr_split, br):
    T = h1.shape[0]
    tiles_per_seq = SEQ // TM_X
    row = lambda i: (i, 0)
    const = lambda i: (0, 0)
    kv_blk = pl.BlockSpec((1, MEM_LEN, D_MODEL), lambda i: (i // tiles_per_seq, 0, 0))
    return pl.pallas_call(
        _xattn_kernel,
        grid=(T // TM_X,),
        in_specs=[
            pl.BlockSpec((TM_X, D_MODEL), row), kv_blk, kv_blk,
            pl.BlockSpec((D_MODEL, D_MODEL), const),
            pl.BlockSpec((D_MODEL, D_MODEL), const),
            pl.BlockSpec((1, D_MODEL), const),
            pl.BlockSpec((1, D_MODEL), const),
            pl.BlockSpec((D_MODEL, 2 * LANES), const),
            pl.BlockSpec((1, LANES), const),
        ],
        out_specs=pl.BlockSpec((TM_X, XR_WIDTH), row),
        out_shape=jax.ShapeDtypeStruct((T, XR_WIDTH), F32),
        compiler_params=_cparams("parallel"),
        name="xattn_ln_route",
    )(h1, k, v, xq_b, xo_b, ln_g, ln_b, wr_split, br)


def _expert_kernel(src_ref, dst_ref, cnt_ref, elo_ref, ehi_ref, ntile_ref, x_hbm,
                   wg_lo, wu_lo, wd_lo, wg_hi, wu_hi, wd_hi, g3_ref, b3_ref, o_hbm,
                   xbuf, obuf, gsem, ssem):
    j = pl.program_id(0)
    n_tiles = ntile_ref[0]
    slot = j % 2

    def row_copy(t, i, s, gather):
        row = pl.ds(i, 1)
        if gather:
            return pltpu.make_async_copy(x_hbm.at[pl.ds(src_ref[t * TM_MOE + i], 1)], xbuf.at[s, row], gsem.at[s])
        return pltpu.make_async_copy(obuf.at[s, row], o_hbm.at[pl.ds(dst_ref[t * TM_MOE + i], 1)], ssem.at[s])

    def n_rows(t, gather):
        cnt = cnt_ref[t]
        return (cnt + ROW_GROUP - 1) // ROW_GROUP * ROW_GROUP if gather else cnt

    def start_rows(t, s, gather):
        cnt = cnt_ref[t]
        n_groups = (cnt + ROW_GROUP - 1) // ROW_GROUP if gather else cnt // ROW_GROUP

        def group(g, c):
            base = pl.multiple_of(g * ROW_GROUP, ROW_GROUP)
            for r in range(ROW_GROUP):
                row_copy(t, base + r, s, gather).start()
            return c
        lax.fori_loop(0, n_groups, group, 0)
        if not gather:
            def single(i, c):
                row_copy(t, i, s, gather).start()
                return c
            lax.fori_loop(n_groups * ROW_GROUP, cnt, single, 0)

    def wait_rows(t, s, gather):
        cnt = n_rows(t, gather)
        for bit in range(TM_MOE.bit_length()):
            rows = pl.ds(0, 1 << bit)

            @pl.when((cnt >> bit) & 1 == 1)
            def _():
                if gather:
                    pltpu.make_async_copy(x_hbm.at[rows], xbuf.at[s, rows], gsem.at[s]).wait()
                else:
                    pltpu.make_async_copy(obuf.at[s, rows], o_hbm.at[rows], ssem.at[s]).wait()

    @pl.when(j == 0)
    def _():
        xbuf[...] = jnp.zeros_like(xbuf)
        start_rows(0, 0, True)

    @pl.when(j < n_tiles)
    def _():
        @pl.when(j + 1 < n_tiles)
        def _():
            start_rows(j + 1, 1 - slot, True)

        wait_rows(j, slot, True)
        x = xbuf[slot, :, :D_MODEL]
        xb = x.astype(BF16)
        y = jnp.zeros_like(x)
        for e, (wg, wu, wd) in enumerate(((wg_lo, wu_lo, wd_lo), (wg_hi, wu_hi, wd_hi))):
            gate = xbuf[slot, :, D_MODEL + 1 + e:D_MODEL + 2 + e]
            a = jnp.dot(xb, wg[0], preferred_element_type=F32)
            u = jnp.dot(xb, wu[0], preferred_element_type=F32)
            hid = a * jax.nn.sigmoid(a) * u
            y = y + jnp.dot((gate * hid).astype(BF16), wd[0], preferred_element_type=F32)
        out = _layer_norm(ALPHA * x + y, g3_ref[...], b3_ref[...])

        @pl.when(j >= 2)
        def _():
            wait_rows(j - 2, slot, False)

        obuf[slot] = out
        start_rows(j, slot, False)

        @pl.when(j == n_tiles - 1)
        def _():
            @pl.when(j >= 1)
            def _():
                wait_rows(j - 1, 1 - slot, False)

            wait_rows(j, slot, False)


def _expert_mlp(src_rows, dst_rows, tile_cnt, tile_elo, tile_ehi, n_tiles, xr, wg, wu, wd, ln_g, ln_b):
    n_tiles_max = tile_cnt.shape[0]
    const = lambda j, src, dst, cnt, elo, ehi, nt: (0, 0)
    lo = lambda j, src, dst, cnt, elo, ehi, nt: (elo[j], 0, 0)
    hi = lambda j, src, dst, cnt, elo, ehi, nt: (ehi[j], 0, 0)
    up = (1, D_MODEL, D_EXPERT)
    down = (1, D_EXPERT, D_MODEL)
    any_spec = pl.BlockSpec(memory_space=pl.ANY)
    return pl.pallas_call(
        _expert_kernel,
        grid_spec=pltpu.PrefetchScalarGridSpec(
            num_scalar_prefetch=6,
            grid=(n_tiles_max,),
            in_specs=[
                any_spec,
                pl.BlockSpec(up, lo), pl.BlockSpec(up, lo), pl.BlockSpec(down, lo),
                pl.BlockSpec(up, hi), pl.BlockSpec(up, hi), pl.BlockSpec(down, hi),
                pl.BlockSpec((1, D_MODEL), const),
                pl.BlockSpec((1, D_MODEL), const),
            ],
            out_specs=any_spec,
            scratch_shapes=[
                pltpu.VMEM((2, TM_MOE, XR_WIDTH), F32),
                pltpu.VMEM((2, TM_MOE, D_MODEL), F32),
                pltpu.SemaphoreType.DMA((2,)),
                pltpu.SemaphoreType.DMA((2,)),
            ],
        ),
        out_shape=jax.ShapeDtypeStruct((xr.shape[0], D_MODEL), F32),
        compiler_params=_cparams("arbitrary"),
        name="moe_experts",
    )(src_rows, dst_rows, tile_cnt, tile_elo, tile_ehi, n_tiles, xr, wg, wu, wd, wg, wu, wd, ln_g, ln_b)


def _class_experts():
    lo, hi = [], []
    for g in range(N_GROUPS):
        for a in range(EXPERTS_PER_GROUP):
            for b in range(a + 1, EXPERTS_PER_GROUP):
                lo.append(g * EXPERTS_PER_GROUP + a)
                hi.append(g * EXPERTS_PER_GROUP + b)
    return np.asarray(lo, np.int32), np.asarray(hi, np.int32)


def _plan_kernel(route_ref, pos_ref, counts_ref, tri_scr, run_scr, start_scr):
    phase, i = pl.program_id(0), pl.program_id(1)
    lane = lax.broadcasted_iota(jnp.int32, (TM_PLAN, LANES), 1)
    onehot = lane.astype(F32) == route_ref[:, 0:1]
    onehot_f = jnp.where(onehot, 1.0, 0.0)

    @pl.when((phase == 0) & (i == 0))
    def _():
        run_scr[...] = jnp.zeros_like(run_scr)
        r = lax.broadcasted_iota(jnp.int32, (TM_PLAN, TM_PLAN), 0)
        c = lax.broadcasted_iota(jnp.int32, (TM_PLAN, TM_PLAN), 1)
        tri_scr[...] = jnp.where(c < r, 1.0, 0.0).astype(BF16)

    @pl.when((phase == 1) & (i == 0))
    def _():
        counts = run_scr[...]
        counts_ref[...] = counts
        tiles = jnp.floor((counts + (TM_MOE - 1)) * (1.0 / TM_MOE))
        lane_row = lax.broadcasted_iota(jnp.int32, (1, LANES), 1)
        scan = tiles
        shift = 1
        while shift < LANES:
            scan = scan + jnp.where(lane_row >= shift, pltpu.roll(scan, shift, axis=1), 0.0)
            shift *= 2
        start_scr[...] = (scan - tiles) * TM_MOE
        run_scr[...] = jnp.zeros_like(run_scr)

    @pl.when(phase == 1)
    def _():
        before = jnp.dot(tri_scr[...], onehot_f.astype(BF16), preferred_element_type=F32)
        pos_col = jnp.sum(onehot_f * (before + run_scr[...] + start_scr[...]), axis=1, keepdims=True)
        eye = (lax.broadcasted_iota(jnp.int32, (LANES, LANES), 0)
               == lax.broadcasted_iota(jnp.int32, (LANES, LANES), 1))
        for r in range(TM_PLAN // LANES):
            row = jnp.sum(jnp.where(eye, pos_col[r * LANES:(r + 1) * LANES], 0.0), axis=0, keepdims=True)
            pos_ref[r:r + 1, :] = row.astype(jnp.int32)

    run_scr[...] = run_scr[...] + jnp.sum(onehot_f, axis=0, keepdims=True)


def _plan_positions(xr):
    T = xr.shape[0]
    n_steps = T // TM_PLAN
    pos, counts = pl.pallas_call(
        _plan_kernel,
        grid=(2, n_steps),
        in_specs=[pl.BlockSpec((TM_PLAN, LANES), lambda p, i: (i, D_MODEL // LANES))],
        out_specs=[pl.BlockSpec((TM_PLAN // LANES, LANES), lambda p, i: (i * p, 0)),
                   pl.BlockSpec((1, LANES), lambda p, i: (0, 0))],
        out_shape=[jax.ShapeDtypeStruct((T // LANES, LANES), jnp.int32),
                   jax.ShapeDtypeStruct((1, LANES), F32)],
        scratch_shapes=[pltpu.VMEM((TM_PLAN, TM_PLAN), BF16),
                        pltpu.VMEM((1, LANES), F32),
                        pltpu.VMEM((1, LANES), F32)],
        compiler_params=_cparams("arbitrary", "arbitrary"),
        name="moe_plan",
    )(xr)
    return pos.reshape(T), counts[0, :N_CLASSES].astype(jnp.int32)


def _moe_plan(xr, n_tiles_max):
    T = xr.shape[0]
    pos, counts = _plan_positions(xr)
    tiles_per_class = (counts + TM_MOE - 1) // TM_MOE
    tile_end = jnp.cumsum(tiles_per_class)
    tile_start = tile_end - tiles_per_class
    n_tiles = tile_end[-1]
    tok = jnp.arange(T, dtype=jnp.int32)
    src_rows = jnp.zeros((n_tiles_max * TM_MOE,), jnp.int32).at[pos].set(tok)
    tile_ids = jnp.arange(n_tiles_max, dtype=jnp.int32)
    used = jnp.minimum(tile_ids, n_tiles - 1)
    tile_cls = jnp.sum((tile_end[None, :] <= used[:, None]).astype(jnp.int32), axis=1)
    tile_cls = jnp.minimum(tile_cls, N_CLASSES - 1)
    cls_lo, cls_hi = _class_experts()
    tile_elo = jnp.asarray(cls_lo)[tile_cls]
    tile_ehi = jnp.asarray(cls_hi)[tile_cls]
    within = tile_ids - tile_start[tile_cls]
    tile_cnt = jnp.clip(counts[tile_cls] - within * TM_MOE, 0, TM_MOE)
    tile_cnt = jnp.where(tile_ids < n_tiles, tile_cnt, 0).astype(jnp.int32)
    return src_rows, tile_elo, tile_ehi, tile_cnt, n_tiles.reshape(1).astype(jnp.int32)


def _vec(a):
    return a.reshape(1, -1).astype(F32)


def _mixer_and_cross_attention(x, mem, ln_in_g, ln_in_b, w_in, rel_bias, sink_a, norm_a_g, norm_b_g, w_out,
                               ln1_g, ln1_b, xq, xkv, xo, ln2_g, ln2_b, w_group, b_group, w_router, b_router):
    B, S, D = x.shape
    assert S == SEQ and D == D_MODEL and mem.shape[1:] == (MEM_LEN, D_MODEL)
    T = B * S
    vec = _vec

    w = w_in[0]
    edges = np.cumsum((0, A_WIDTH, A_KV_HEADS * HEAD_DIM, A_KV_HEADS * HEAD_DIM, B_WIDTH, B_WIDTH, B_WIDTH))
    qa, ka, va, qb, kb, vb = [w[:, a:b] for a, b in zip(edges[:-1], edges[1:])]
    dup = lambda t: jnp.repeat(t.reshape(D, A_KV_HEADS, 1, HEAD_DIM), 2, axis=2).reshape(D, A_KV_TILES * LANES)
    scale = HEAD_DIM ** -0.5 * LOG2E
    w_in_b = jnp.concatenate([qa * scale, dup(ka), dup(va), qb * scale, kb, vb], axis=1).astype(BF16)
    w_out_b = w_out[0].astype(BF16)
    xq_b = (xq[0] * (X_HEAD_DIM ** -0.5 * LOG2E)).astype(BF16)
    xkv_b = xkv[0].astype(BF16)
    xo_b = xo[0].astype(BF16)
    wr = jnp.concatenate([w_group[0], w_router[0]], axis=1).astype(F32)
    wr = jnp.pad(wr, ((0, 0), (0, LANES - wr.shape[1])))
    wr_hi = wr.astype(BF16)
    wr_split = jnp.concatenate([wr_hi, (wr - wr_hi.astype(F32)).astype(BF16)], axis=1)
    br = jnp.concatenate([b_group[0], b_router[0]]).astype(F32)
    br = jnp.pad(br, (0, LANES - br.shape[0])).reshape(1, LANES)

    h0, qkv_a, qkv_b1, qkv_b4, qkv_b16 = _input_projection(
        x.reshape(T, D), vec(ln_in_g), vec(ln_in_b), w_in_b, B)

    nk_a = TM_ATT + 2 * A_HALF_WIN
    bias_a = _band_bias(rel_bias[:, :A_HEADS], A_HALF_WIN, 1, TM_ATT, nk_a, S)
    ya = _band_attention(qkv_a.reshape(B, S, QKV_A), bias_a, nk=nk_a, sub=1, kv_of=(0, 0, 1, 1),
                         q_width=A_WIDTH, kv_width=A_KV_TILES * LANES, out_tiles=A_WIDTH // LANES,
                         sink=sink_a[0].astype(F32) * LOG2E, gain=vec(norm_a_g[0]))
    branch_out = []
    for (win, dil), qkv in zip(B_BRANCHES, (qkv_b1.reshape(B, S, QKV_B),
                                             qkv_b4.reshape(B * 4, S // 4, QKV_B),
                                             qkv_b16.reshape(B * 16, S // 16, QKV_B))):
        half = (win // 2) // dil
        n = S // dil
        nk = min(TM_ATT + 2 * half, n)
        bias_b = _band_bias(rel_bias[:, A_HEADS:], half, dil, TM_ATT, nk, n)
        branch_out.append(_band_attention(qkv, bias_b, nk=nk, sub=max(1, TM_ATT * 4 // n), kv_of=(0, 1, 2, 3),
                                          q_width=B_WIDTH, kv_width=B_WIDTH, out_tiles=B_OUT // LANES))
    o1 = branch_out[0]
    o4 = branch_out[1].reshape(B, 4, B_OUT // LANES, S // 4, LANES)
    o16 = branch_out[2].reshape(B, 16, B_OUT // LANES, S // 16, LANES)

    h1 = _merge_project(ya, o1, o4, o16, h0, vec(norm_b_g[0]), w_out_b,
                        vec(ln1_g[0]), vec(ln1_b[0]))

    k_mem, v_mem = _mem_kv(mem, xkv_b)
    xr = _cross_attention_route(h1, k_mem, v_mem, xq_b, xo_b, vec(ln2_g[0]), vec(ln2_b[0]), wr_split, br)
    return h0, h1, xr


def _moe(xr, w_gate, w_up, w_down, ln3_g, ln3_b):
    T = xr.shape[0]
    wg_b = w_gate[0].reshape(N_EXPERTS, D_MODEL, D_EXPERT).astype(BF16)
    wu_b = w_up[0].reshape(N_EXPERTS, D_MODEL, D_EXPERT).astype(BF16)
    wd_b = w_down[0].reshape(N_EXPERTS, D_EXPERT, D_MODEL).astype(BF16)
    n_tiles_max = T // TM_MOE + N_CLASSES
    src_rows, tile_elo, tile_ehi, tile_cnt, n_tiles = _moe_plan(xr, n_tiles_max)
    return _expert_mlp(src_rows, _token_of_row(src_rows), tile_cnt, tile_elo, tile_ehi, n_tiles, xr, wg_b, wu_b, wd_b,
                       _vec(ln3_g[0]), _vec(ln3_b[0]))


def kernel(x, mem, ln_in_g, ln_in_b, w_in, rel_bias, sink_a, norm_a_g, norm_b_g, w_out,
           ln1_g, ln1_b, xq, xkv, xo, ln2_g, ln2_b, w_group, b_group, w_router, b_router,
           w_gate, w_up, w_down, ln3_g, ln3_b):
    _, _, xr = _mixer_and_cross_attention(
        x, mem, ln_in_g, ln_in_b, w_in, rel_bias, sink_a, norm_a_g, norm_b_g, w_out,
        ln1_g, ln1_b, xq, xkv, xo, ln2_g, ln2_b, w_group, b_group, w_router, b_router)
    return _moe(xr, w_gate, w_up, w_down, ln3_g, ln3_b).reshape(x.shape)
```

```python
import functools

import numpy as np
import jax
import jax.numpy as jnp
from jax import lax
from jax.experimental import pallas as pl
from jax.experimental.pallas import tpu as pltpu

F32 = jnp.float32
BF16 = jnp.bfloat16

D_MODEL = 1024
SEQ = 2048
MEM_LEN = 256
HEAD_DIM = 64
A_HEADS = 8
A_KV_HEADS = 2
A_HALF_WIN = 128
B_HEADS = 8
B_BRANCHES = ((128, 1), (512, 4), (2048, 16))
N_BUCKETS = 32
MAX_DISTANCE = 1024
X_HEADS = 4
X_HEAD_DIM = D_MODEL // X_HEADS
N_GROUPS = 4
EXPERTS_PER_GROUP = 8
N_EXPERTS = N_GROUPS * EXPERTS_PER_GROUP
D_EXPERT = 512
DEPTH = 1
ALPHA = (2.0 * DEPTH) ** 0.25
LN_EPS = 1e-5
NEG = -1e30
LOG2E = 1.4426950408889634
LN2 = 0.6931471805599453

LANES = 128
XR_WIDTH = D_MODEL + LANES
A_WIDTH = A_HEADS * HEAD_DIM
B_WIDTH = B_HEADS * HEAD_DIM
A_KV_TILES = A_KV_HEADS
QKV_A = A_WIDTH + 2 * A_KV_TILES * LANES
QKV_B = 3 * B_WIDTH
B_OUT = B_WIDTH + LANES
MERGE_STRIDE = 4

PAIRS_PER_GROUP = EXPERTS_PER_GROUP * (EXPERTS_PER_GROUP - 1) // 2
N_CLASSES = N_GROUPS * PAIRS_PER_GROUP

TM_IN = 512
TM_ATT = 128
ATT_BLOCKS = 4
TM_MERGE = 512
TM_X = 512
TM_MOE = 128
ROW_GROUP = 8
TM_PLAN = 1024
VMEM_LIMIT = 56 * 1024 * 1024


def _cparams(*sem):
    return pltpu.CompilerParams(dimension_semantics=sem, vmem_limit_bytes=VMEM_LIMIT)


def _layer_norm(x, g, b):
    mu = jnp.mean(x, axis=-1, keepdims=True)
    xc = x - mu
    var = jnp.mean(xc * xc, axis=-1, keepdims=True)
    return xc * lax.rsqrt(var + LN_EPS) * g + b


def _rms_norm(x, g):
    return x * lax.rsqrt(jnp.mean(x * x, axis=-1, keepdims=True) + LN_EPS) * g


def _t5_bucket(rel):
    nb = N_BUCKETS // 2
    max_exact = nb // 2
    ret = (rel > 0).astype(np.int32) * nb
    n = np.abs(rel)
    n_safe = np.maximum(n, 1).astype(np.float64)
    large = max_exact + (np.log(n_safe / max_exact) / np.log(MAX_DISTANCE / max_exact)
                         * (nb - max_exact)).astype(np.int32)
    large = np.minimum(large, nb - 1)
    return (ret + np.where(n < max_exact, n, large)).astype(np.int32)


def _inproj_kernel(x_ref, g_ref, b_ref, w_ref, h0_ref, qa_ref, qb1_ref, qb4_ref, qb16_ref, pb_scr, p4_scr):
    h = _layer_norm(x_ref[...], g_ref[...], b_ref[...])
    for c in range(D_MODEL // LANES):
        h0_ref[0, c] = h[:, c * LANES:(c + 1) * LANES]
    proj = jnp.dot(h.astype(BF16), w_ref[...], preferred_element_type=F32)
    qa_ref[...] = proj[:, :QKV_A].astype(BF16)
    pb = proj[:, QKV_A:]
    qb1_ref[...] = pb.astype(BF16)
    n4 = TM_IN // 4
    for c in range(QKV_B // LANES):
        cs = slice(c * LANES, (c + 1) * LANES)
        pb_scr[c] = pb[:, cs]
        for r4 in range(4):
            rows = pb_scr[c, pl.ds(r4, n4, stride=4), :]
            qb4_ref[0, r4, :, cs] = rows.astype(BF16)
            p4_scr[c, r4 * n4:(r4 + 1) * n4, :] = rows
        for r16 in range(16):
            rows = p4_scr[c, pl.ds((r16 % 4) * n4 + r16 // 4, TM_IN // 16, stride=4), :]
            qb16_ref[0, r16, :, cs] = rows.astype(BF16)


def _input_projection(x2, ln_g, ln_b, w_in_b, batch):
    T = x2.shape[0]
    tiles_per_seq = SEQ // TM_IN
    row = lambda i: (i, 0)
    const = lambda i: (0, 0)
    deint = lambda i: (i // tiles_per_seq, 0, i % tiles_per_seq, 0)
    return pl.pallas_call(
        _inproj_kernel,
        grid=(T // TM_IN,),
        in_specs=[
            pl.BlockSpec((TM_IN, D_MODEL), row),
            pl.BlockSpec((1, D_MODEL), const),
            pl.BlockSpec((1, D_MODEL), const),
            pl.BlockSpec((D_MODEL, QKV_A + QKV_B), const),
        ],
        out_specs=[
            pl.BlockSpec((1, D_MODEL // LANES, TM_IN, LANES), deint),
            pl.BlockSpec((TM_IN, QKV_A), row),
            pl.BlockSpec((TM_IN, QKV_B), row),
            pl.BlockSpec((1, 4, TM_IN // 4, QKV_B), deint),
            pl.BlockSpec((1, 16, TM_IN // 16, QKV_B), deint),
        ],
        out_shape=[
            jax.ShapeDtypeStruct((batch, D_MODEL // LANES, SEQ, LANES), F32),
            jax.ShapeDtypeStruct((T, QKV_A), BF16),
            jax.ShapeDtypeStruct((T, QKV_B), BF16),
            jax.ShapeDtypeStruct((batch, 4, SEQ // 4, QKV_B), BF16),
            jax.ShapeDtypeStruct((batch, 16, SEQ // 16, QKV_B), BF16),
        ],
        scratch_shapes=[pltpu.VMEM((QKV_B // LANES, TM_IN, LANES), F32)] * 2,
        compiler_params=_cparams("parallel"),
        name="ln_in_proj",
    )(x2, ln_g, ln_b, w_in_b)


def _band_attn_kernel(*refs, m, nk, n, sub, qb, kv_of, variants, with_sink):
    if with_sink:
        sink_ref, gain_ref = refs[0], refs[1]
        refs = refs[2:]
    q_ref, k_ref, v_ref, bias_ref, o_ref = refs
    nb = n // m
    first, middle, last = variants
    starts, bias_var = [], []
    for b in range(qb):
        if nk == n:
            starts.append(0)
            bias_var.append(first)
        else:
            jj = pl.program_id(1) * qb + b
            starts.append(pl.multiple_of(jnp.clip(jj * m - (nk - m) // 2, 0, n - nk), HEAD_DIM))
            bias_var.append(jnp.where(jj == 0, first, jnp.where(jj == nb - 1, last, middle)))
    lane_row = lax.broadcasted_iota(jnp.int32, (1, LANES), 1)
    keep_lo = jnp.where(lane_row < HEAD_DIM, 1.0, 0.0).astype(BF16)
    keep_hi = jnp.where(lane_row < HEAD_DIM, 0.0, 1.0).astype(BF16)
    lane = lax.broadcasted_iota(jnp.int32, (m, LANES), 1)
    n_q = len(kv_of)
    units = [(s, b, qt) for s in range(sub) for b in range(qb) for qt in range(n_q)]

    scores = []
    for s, b, qt in units:
        t = kv_of[qt]
        k_t = k_ref[s, pl.ds(starts[b], nk), t * LANES:(t + 1) * LANES]
        q2 = q_ref[s, b * m:(b + 1) * m, qt * LANES:(qt + 1) * LANES]
        lhs = jnp.concatenate([q2 * keep_lo, q2 * keep_hi], axis=0)
        sc = lax.dot_general(lhs, k_t, (((1,), (1,)), ((), ())), preferred_element_type=F32)
        scores.append(sc + bias_ref[bias_var[b], qt])

    probs, denom, row_max = [], [], []
    for (s, b, qt), sc in zip(units, scores):
        mx = jnp.max(sc, axis=-1, keepdims=True)
        p = jnp.exp2(sc - mx)
        l = jnp.sum(p, axis=-1, keepdims=True)
        halves = []
        for h in (0, 1):
            l_h = l[h * m:(h + 1) * m]
            if with_sink:
                l_h = l_h + jnp.exp2(sink_ref[2 * qt + h] - mx[h * m:(h + 1) * m])
            halves.append(l_h)
        probs.append(p.astype(BF16))
        denom.append(halves)
        row_max.append(mx)

    for s, b in [(s, b) for s in range(sub) for b in range(qb)]:
        rows = slice(b * m, (b + 1) * m)
        pairs = []
        lse_tile = jnp.zeros((m, LANES), F32)
        for qt in range(n_q):
            u = (s * qb + b) * n_q + qt
            t = kv_of[qt]
            v_t = v_ref[s, pl.ds(starts[b], nk), t * LANES:(t + 1) * LANES]
            o = jnp.dot(probs[u], v_t, preferred_element_type=F32)
            top = o[:m] * (1.0 / denom[u][0])
            bot = o[m:] * (1.0 / denom[u][1])
            pairs.append(jnp.where(lane < HEAD_DIM, top, bot))
            if not with_sink:
                for h in (0, 1):
                    lse = (row_max[u][h * m:(h + 1) * m] + jnp.log2(denom[u][h])) * LN2
                    lse_tile = jnp.where(lane == 2 * qt + h, lse, lse_tile)
        if with_sink:
            normed = _rms_norm(jnp.concatenate(pairs, axis=1), gain_ref[...])
            pairs = [normed[:, c * LANES:(c + 1) * LANES] for c in range(n_q)]
        else:
            o_ref[s, n_q, rows, :] = lse_tile
        for c in range(n_q):
            o_ref[s, c, rows, :] = pairs[c]


def _band_window(m, nk, n):
    nb = n // m
    starts = np.clip(np.arange(nb) * m - (nk - m) // 2, 0, n - nk)
    offs = [int(o) for o in starts - np.arange(nb) * m]
    uniq = sorted(set(offs), reverse=True)
    var = [uniq.index(o) for o in offs]
    assert all(v == var[1] for v in var[1:-1])
    return uniq, var


def _band_bias(rel_bias_h, half, dil, m, nk, n):
    uniq, _ = _band_window(m, nk, n)
    n_heads = rel_bias_h.shape[1]
    span = nk + m - 1
    out = []
    for off in uniq:
        rel = off - (m - 1) + np.arange(span)
        onehot = np.zeros((span, N_BUCKETS), np.float32)
        onehot[np.arange(span), _t5_bucket(dil * rel)] = 1.0
        table = jnp.dot(jnp.asarray(onehot), rel_bias_h.astype(F32), precision=lax.Precision.HIGHEST)
        table = jnp.where((np.abs(rel) <= half)[:, None], table, NEG)
        u = jnp.concatenate([table.T, jnp.zeros((n_heads, 1), F32)], axis=1)
        flat = jnp.tile(u, (1, m + 1))[:, m - 1:m - 1 + m * span]
        b = flat.reshape(n_heads, m, span)[:, :, :nk]
        out.append(b.reshape(n_heads // 2, 2 * m, nk))
    return jnp.stack(out) * LOG2E


def _band_attention(qkv, bias, *, nk, sub, kv_of, q_width, kv_width, out_tiles, sink=None, gain=None):
    Bd, n, _ = qkv.shape
    m = TM_ATT
    nb = n // m
    qb = min(nb, ATT_BLOCKS)
    assert nb % qb == 0 and nb * m == n and Bd % sub == 0
    _, var = _band_window(m, nk, n)
    variants = (var[0], var[min(1, nb - 1)], var[-1])
    kcol = q_width // kv_width
    with_sink = sink is not None
    in_specs = [
        pl.BlockSpec((sub, qb * m, q_width), lambda b, j: (b, j, 0)),
        pl.BlockSpec((sub, n, kv_width), lambda b, j: (b, 0, kcol)),
        pl.BlockSpec((sub, n, kv_width), lambda b, j: (b, 0, kcol + 1)),
        pl.BlockSpec(bias.shape, lambda b, j: (0, 0, 0, 0)),
    ]
    args = [qkv, qkv, qkv, bias]
    if with_sink:
        in_specs = [pl.BlockSpec(memory_space=pltpu.SMEM),
                    pl.BlockSpec((1, q_width), lambda b, j: (0, 0))] + in_specs
        args = [sink, gain] + args
    return pl.pallas_call(
        functools.partial(_band_attn_kernel, m=m, nk=nk, n=n, sub=sub, qb=qb, kv_of=kv_of, variants=variants,
                          with_sink=with_sink),
        grid=(Bd // sub, nb // qb),
        in_specs=in_specs,
        out_specs=pl.BlockSpec((sub, out_tiles, qb * m, LANES), lambda b, j: (b, 0, j, 0)),
        out_shape=jax.ShapeDtypeStruct((Bd, out_tiles, n, LANES), F32),
        compiler_params=_cparams("parallel", "arbitrary"),
        name="band_attn_sink" if with_sink else f"band_attn_n{n}",
    )(*args)


def _merge_kernel(ya_ref, o1_ref, o4_ref, o16_ref, h0_ref, gb_ref, w_ref, g1_ref, b1_ref, h1_ref, s16_scr):
    n4 = TM_MERGE // MERGE_STRIDE
    n16 = TM_MERGE // 16
    for r16 in range(16):
        for c in range(B_OUT // LANES):
            s16_scr[c, pl.ds((r16 % 4) * n4 + r16 // 4, n16, stride=4), :] = o16_ref[0, r16, c]

    def natural(ref, c):
        return jnp.concatenate([ref[0, c, pl.ds(r, n4, stride=MERGE_STRIDE), :]
                                for r in range(MERGE_STRIDE)], axis=0)

    def stride4(c):
        return jnp.concatenate([o4_ref[0, r, c] for r in range(MERGE_STRIDE)], axis=0)

    def stride16(c):
        return s16_scr[c]

    n_val = B_WIDTH // LANES
    lses = (natural(o1_ref, n_val), stride4(n_val), stride16(n_val))
    mx = jnp.maximum(jnp.maximum(lses[0], lses[1]), lses[2])
    ex = [jnp.exp(l - mx) for l in lses]
    inv = 1.0 / (ex[0] + ex[1] + ex[2])
    head = lax.broadcasted_iota(jnp.int32, (LANES, B_WIDTH), 0)
    col = lax.broadcasted_iota(jnp.int32, (LANES, B_WIDTH), 1)
    spread = jnp.where(col // HEAD_DIM == head, 1.0, 0.0).astype(BF16)
    wide = []
    for e in ex:
        w = e * inv
        w_hi = w.astype(BF16)
        w_lo = (w - w_hi.astype(F32)).astype(BF16)
        wide.append(jnp.dot(w_hi, spread, preferred_element_type=F32)
                    + jnp.dot(w_lo, spread, preferred_element_type=F32))
    pieces = []
    for c in range(n_val):
        cs = slice(c * LANES, (c + 1) * LANES)
        outs = (natural(o1_ref, c), stride4(c), stride16(c))
        pieces.append(wide[0][:, cs] * outs[0] + wide[1][:, cs] * outs[1] + wide[2][:, cs] * outs[2])
    yb = _rms_norm(jnp.concatenate(pieces, axis=1), gb_ref[...])
    ya = jnp.concatenate([natural(ya_ref, c) for c in range(A_WIDTH // LANES)], axis=1)
    y = jnp.concatenate([ya.astype(BF16), yb.astype(BF16)], axis=1)
    mix = jnp.dot(y, w_ref[...], preferred_element_type=F32)
    h0 = jnp.concatenate([natural(h0_ref, c) for c in range(D_MODEL // LANES)], axis=1)
    h1_ref[...] = _layer_norm(ALPHA * h0 + mix, g1_ref[...], b1_ref[...])


def _token_of_row(p):
    n16 = TM_MERGE // MERGE_STRIDE
    tile, rest = p // TM_MERGE, p % TM_MERGE
    return tile * TM_MERGE + (rest % n16) * MERGE_STRIDE + rest // n16


def _merge_project(ya, o1, o4, o16, h0, gain_b, w_out_b, ln_g, ln_b):
    B = h0.shape[0]
    T = B * SEQ
    tiles_per_seq = SEQ // TM_MERGE
    const = lambda i: (0, 0)
    nat = lambda i: (i // tiles_per_seq, 0, i % tiles_per_seq, 0)
    deint = lambda i: (i // tiles_per_seq, 0, 0, i % tiles_per_seq, 0)
    n_b = B_OUT // LANES
    return pl.pallas_call(
        _merge_kernel,
        grid=(T // TM_MERGE,),
        in_specs=[
            pl.BlockSpec((1, A_WIDTH // LANES, TM_MERGE, LANES), nat),
            pl.BlockSpec((1, n_b, TM_MERGE, LANES), nat),
            pl.BlockSpec((1, 4, n_b, TM_MERGE // 4, LANES), deint),
            pl.BlockSpec((1, 16, n_b, TM_MERGE // 16, LANES), deint),
            pl.BlockSpec((1, D_MODEL // LANES, TM_MERGE, LANES), nat),
            pl.BlockSpec((1, B_WIDTH), const),
            pl.BlockSpec((D_MODEL, D_MODEL), const),
            pl.BlockSpec((1, D_MODEL), const),
            pl.BlockSpec((1, D_MODEL), const),
        ],
        out_specs=pl.BlockSpec((TM_MERGE, D_MODEL), lambda i: (i, 0)),
        out_shape=jax.ShapeDtypeStruct((T, D_MODEL), F32),
        scratch_shapes=[pltpu.VMEM((n_b, TM_MERGE, LANES), F32)],
        compiler_params=_cparams("parallel"),
        name="merge_out_proj",
    )(ya, o1, o4, o16, h0, gain_b, w_out_b, ln_g, ln_b)


def _mem_kv_kernel(mem_ref, w_ref, k_ref, v_ref):
    kv = jnp.dot(mem_ref[0].astype(BF16), w_ref[...], preferred_element_type=F32)
    k_ref[0] = kv[:, :D_MODEL].astype(BF16)
    v_ref[0] = kv[:, D_MODEL:].astype(BF16)


def _mem_kv(mem, xkv_b):
    B = mem.shape[0]
    blk = pl.BlockSpec((1, MEM_LEN, D_MODEL), lambda b: (b, 0, 0))
    return pl.pallas_call(
        _mem_kv_kernel,
        grid=(B,),
        in_specs=[blk, pl.BlockSpec((D_MODEL, 2 * D_MODEL), lambda b: (0, 0))],
        out_specs=[blk, blk],
        out_shape=[jax.ShapeDtypeStruct((B, MEM_LEN, D_MODEL), BF16)] * 2,
        compiler_params=_cparams("parallel"),
        name="mem_kv_proj",
    )(mem, xkv_b)


def _route(logits):
    rows = logits.shape[0]
    lane = lax.broadcasted_iota(jnp.int32, (rows, LANES), 1).astype(F32)
    big = float(LANES)
    ninf = -jnp.inf
    gl = jnp.where(lane < N_GROUPS, logits, ninf)
    gmax = jnp.max(gl, axis=-1, keepdims=True)
    gidx = jnp.min(jnp.where(gl == gmax, lane, big), axis=-1, keepdims=True)
    g_p = 1.0 / jnp.sum(jnp.exp(gl - gmax), axis=-1, keepdims=True)
    lo_lane = N_GROUPS + EXPERTS_PER_GROUP * gidx
    el = jnp.where((lane >= lo_lane) & (lane < lo_lane + EXPERTS_PER_GROUP), logits, ninf)
    v1 = jnp.max(el, axis=-1, keepdims=True)
    i1 = jnp.min(jnp.where(el == v1, lane, big), axis=-1, keepdims=True)
    el2 = jnp.where(lane == i1, ninf, el)
    v2 = jnp.max(el2, axis=-1, keepdims=True)
    i2 = jnp.min(jnp.where(el2 == v2, lane, big), axis=-1, keepdims=True)
    t = jnp.exp(v2 - v1)
    w1 = g_p / (1.0 + t)
    w2 = g_p * t / (1.0 + t)
    a = jnp.minimum(i1, i2) - lo_lane
    b = jnp.maximum(i1, i2) - lo_lane
    pair = a * (2 * EXPERTS_PER_GROUP - 1 - a) * 0.5 + (b - a - 1.0)
    cls = gidx * PAIRS_PER_GROUP + pair
    w_lo = jnp.where(i1 < i2, w1, w2)
    w_hi = jnp.where(i1 < i2, w2, w1)
    return jnp.where(lane == 0, cls, jnp.where(lane == 1, w_lo, jnp.where(lane == 2, w_hi, 0.0)))


def _xattn_kernel(h1_ref, k_ref, v_ref, wq_ref, wo_ref, g2_ref, b2_ref, wr_ref, br_ref, xr_ref):
    h1 = h1_ref[...]
    q = jnp.dot(h1.astype(BF16), wq_ref[...], preferred_element_type=F32).astype(BF16)
    outs = []
    for hd in range(X_HEADS):
        sl = slice(hd * X_HEAD_DIM, (hd + 1) * X_HEAD_DIM)
        s = lax.dot_general(q[:, sl], k_ref[0, :, sl], (((1,), (1,)), ((), ())), preferred_element_type=F32)
        m = jnp.max(s, axis=-1, keepdims=True)
        p = jnp.exp2(s - m)
        l = jnp.sum(p, axis=-1, keepdims=True)
        o = jnp.dot(p.astype(BF16), v_ref[0, :, sl], preferred_element_type=F32) / l
        outs.append(o.astype(BF16))
    xa = jnp.dot(jnp.concatenate(outs, axis=1), wo_ref[...], preferred_element_type=F32)
    h2 = _layer_norm(ALPHA * h1 + xa, g2_ref[...], b2_ref[...])
    h_hi = h2.astype(BF16)
    h_lo = (h2 - h_hi.astype(F32)).astype(BF16)
    t_hi = jnp.dot(h_hi, wr_ref[...], preferred_element_type=F32)
    t_lo = jnp.dot(h_lo, wr_ref[...], preferred_element_type=F32)
    logits = (t_hi[:, :LANES] + t_hi[:, LANES:]) + (t_lo[:, :LANES] + t_lo[:, LANES:]) + br_ref[...]
    xr_ref[:, :D_MODEL] = h2
    xr_ref[:, D_MODEL:] = _route(logits)


def _cross_attention_route(h1, k, v, xq_b, xo_b, ln_g, ln_b, wr_split, br):
    T = h1.shape[0]
    tiles_per_seq = SEQ // TM_X
    row = lambda i: (i, 0)
    const = lambda i: (0, 0)
    kv_blk = pl.BlockSpec((1, MEM_LEN, D_MODEL), lambda i: (i // tiles_per_seq, 0, 0))
    return pl.pallas_call(
        _xattn_kernel,
        grid=(T // TM_X,),
        in_specs=[
            pl.BlockSpec((TM_X, D_MODEL), row), kv_blk, kv_blk,
            pl.BlockSpec((D_MODEL, D_MODEL), const),
            pl.BlockSpec((D_MODEL, D_MODEL), const),
            pl.BlockSpec((1, D_MODEL), const),
            pl.BlockSpec((1, D_MODEL), const),
            pl.BlockSpec((D_MODEL, 2 * LANES), const),
            pl.BlockSpec((1, LANES), const),
        ],
        out_specs=pl.BlockSpec((TM_X, XR_WIDTH), row),
        out_shape=jax.ShapeDtypeStruct((T, XR_WIDTH), F32),
        compiler_params=_cparams("parallel"),
        name="xattn_ln_route",
    )(h1, k, v, xq_b, xo_b, ln_g, ln_b, wr_split, br)


def _expert_kernel(src_ref, dst_ref, cnt_ref, elo_ref, ehi_ref, ntile_ref, x_hbm,
                   wg_lo, wu_lo, wd_lo, wg_hi, wu_hi, wd_hi, g3_ref, b3_ref, o_hbm,
                   xbuf, obuf, gsem, ssem):
    j = pl.program_id(0)
    n_tiles = ntile_ref[0]
    slot = j % 2

    def row_copy(t, i, s, gather):
        row = pl.ds(i, 1)
        if gather:
            return pltpu.make_async_copy(x_hbm.at[pl.ds(src_ref[t * TM_MOE + i], 1)], xbuf.at[s, row], gsem.at[s])
        return pltpu.make_async_copy(obuf.at[s, row], o_hbm.at[pl.ds(dst_ref[t * TM_MOE + i], 1)], ssem.at[s])

    def n_rows(t, gather):
        cnt = cnt_ref[t]
        return (cnt + ROW_GROUP - 1) // ROW_GROUP * ROW_GROUP if gather else cnt

    def start_rows(t, s, gather):
        cnt = cnt_ref[t]
        n_groups = (cnt + ROW_GROUP - 1) // ROW_GROUP if gather else cnt // ROW_GROUP

        def group(g, c):
            base = pl.multiple_of(g * ROW_GROUP, ROW_GROUP)
            for r in range(ROW_GROUP):
                row_copy(t, base + r, s, gather).start(priority=0 if gather else 1)
            return c
        lax.fori_loop(0, n_groups, group, 0)
        if not gather:
            def single(i, c):
                row_copy(t, i, s, gather).start(priority=1)
                return c
            lax.fori_loop(n_groups * ROW_GROUP, cnt, single, 0)

    def wait_rows(t, s, gather):
        cnt = n_rows(t, gather)
        for bit in range(TM_MOE.bit_length()):
            rows = pl.ds(0, 1 << bit)

            @pl.when((cnt >> bit) & 1 == 1)
            def _():
                if gather:
                    pltpu.make_async_copy(x_hbm.at[rows], xbuf.at[s, rows], gsem.at[s]).wait()
                else:
                    pltpu.make_async_copy(obuf.at[s, rows], o_hbm.at[rows], ssem.at[s]).wait()

    @pl.when(j == 0)
    def _():
        xbuf[...] = jnp.zeros_like(xbuf)
        start_rows(0, 0, True)

    @pl.when(j < n_tiles)
    def _():
        @pl.when(j + 1 < n_tiles)
        def _():
            start_rows(j + 1, 1 - slot, True)

        wait_rows(j, slot, True)
        x = xbuf[slot, :, :D_MODEL]
        xb = x.astype(BF16)
        y = jnp.zeros_like(x)
        for e, (wg, wu, wd) in enumerate(((wg_lo, wu_lo, wd_lo), (wg_hi, wu_hi, wd_hi))):
            gate = xbuf[slot, :, D_MODEL + 1 + e:D_MODEL + 2 + e]
            a = jnp.dot(xb, wg[0], preferred_element_type=F32)
            u = jnp.dot(xb, wu[0], preferred_element_type=F32)
            hid = a * jax.nn.sigmoid(a) * u
            y = y + jnp.dot((gate * hid).astype(BF16), wd[0], preferred_element_type=F32)
        out = _layer_norm(ALPHA * x + y, g3_ref[...], b3_ref[...])

        @pl.when(j >= 2)
        def _():
            wait_rows(j - 2, slot, False)

        obuf[slot] = out
        start_rows(j, slot, False)

        @pl.when(j == n_tiles - 1)
        def _():
            @pl.when(j >= 1)
            def _():
                wait_rows(j - 1, 1 - slot, False)

            wait_rows(j, slot, False)


def _expert_mlp(src_rows, dst_rows, tile_cnt, tile_elo, tile_ehi, n_tiles, xr, wg, wu, wd, ln_g, ln_b):
    n_tiles_max = tile_cnt.shape[0]
    const = lambda j, src, dst, cnt, elo, ehi, nt: (0, 0)
    lo = lambda j, src, dst, cnt, elo, ehi, nt: (elo[j], 0, 0)
    hi = lambda j, src, dst, cnt, elo, ehi, nt: (ehi[j], 0, 0)
    up = (1, D_MODEL, D_EXPERT)
    down = (1, D_EXPERT, D_MODEL)
    any_spec = pl.BlockSpec(memory_space=pl.ANY)
    return pl.pallas_call(
        _expert_kernel,
        grid_spec=pltpu.PrefetchScalarGridSpec(
            num_scalar_prefetch=6,
            grid=(n_tiles_max,),
            in_specs=[
                any_spec,
                pl.BlockSpec(up, lo), pl.BlockSpec(up, lo), pl.BlockSpec(down, lo),
                pl.BlockSpec(up, hi), pl.BlockSpec(up, hi), pl.BlockSpec(down, hi),
                pl.BlockSpec((1, D_MODEL), const),
                pl.BlockSpec((1, D_MODEL), const),
            ],
            out_specs=any_spec,
            scratch_shapes=[
                pltpu.VMEM((2, TM_MOE, XR_WIDTH), F32),
                pltpu.VMEM((2, TM_MOE, D_MODEL), F32),
                pltpu.SemaphoreType.DMA((2,)),
                pltpu.SemaphoreType.DMA((2,)),
            ],
        ),
        out_shape=jax.ShapeDtypeStruct((xr.shape[0], D_MODEL), F32),
        compiler_params=_cparams("arbitrary"),
        name="moe_experts",
    )(src_rows, dst_rows, tile_cnt, tile_elo, tile_ehi, n_tiles, xr, wg, wu, wd, wg, wu, wd, ln_g, ln_b)


def _class_experts():
    lo, hi = [], []
    for g in range(N_GROUPS):
        for a in range(EXPERTS_PER_GROUP):
            for b in range(a + 1, EXPERTS_PER_GROUP):
                lo.append(g * EXPERTS_PER_GROUP + a)
                hi.append(g * EXPERTS_PER_GROUP + b)
    return np.asarray(lo, np.int32), np.asarray(hi, np.int32)


def _plan_kernel(route_ref, pos_ref, counts_ref, tri_scr, run_scr, start_scr):
    phase, i = pl.program_id(0), pl.program_id(1)
    lane = lax.broadcasted_iota(jnp.int32, (TM_PLAN, LANES), 1)
    onehot = lane.astype(F32) == route_ref[:, 0:1]
    onehot_f = jnp.where(onehot, 1.0, 0.0)

    @pl.when((phase == 0) & (i == 0))
    def _():
        run_scr[...] = jnp.zeros_like(run_scr)
        r = lax.broadcasted_iota(jnp.int32, (TM_PLAN, TM_PLAN), 0)
        c = lax.broadcasted_iota(jnp.int32, (TM_PLAN, TM_PLAN), 1)
        tri_scr[...] = jnp.where(c < r, 1.0, 0.0).astype(BF16)

    @pl.when((phase == 1) & (i == 0))
    def _():
        counts = run_scr[...]
        counts_ref[...] = counts
        tiles = jnp.floor((counts + (TM_MOE - 1)) * (1.0 / TM_MOE))
        lane_row = lax.broadcasted_iota(jnp.int32, (1, LANES), 1)
        scan = tiles
        shift = 1
        while shift < LANES:
            scan = scan + jnp.where(lane_row >= shift, pltpu.roll(scan, shift, axis=1), 0.0)
            shift *= 2
        start_scr[...] = (scan - tiles) * TM_MOE
        run_scr[...] = jnp.zeros_like(run_scr)

    @pl.when(phase == 1)
    def _():
        before = jnp.dot(tri_scr[...], onehot_f.astype(BF16), preferred_element_type=F32)
        pos_col = jnp.sum(onehot_f * (before + run_scr[...] + start_scr[...]), axis=1, keepdims=True)
        eye = (lax.broadcasted_iota(jnp.int32, (LANES, LANES), 0)
               == lax.broadcasted_iota(jnp.int32, (LANES, LANES), 1))
        for r in range(TM_PLAN // LANES):
            row = jnp.sum(jnp.where(eye, pos_col[r * LANES:(r + 1) * LANES], 0.0), axis=0, keepdims=True)
            pos_ref[r:r + 1, :] = row.astype(jnp.int32)

    run_scr[...] = run_scr[...] + jnp.sum(onehot_f, axis=0, keepdims=True)


def _plan_positions(xr):
    T = xr.shape[0]
    n_steps = T // TM_PLAN
    pos, counts = pl.pallas_call(
        _plan_kernel,
        grid=(2, n_steps),
        in_specs=[pl.BlockSpec((TM_PLAN, LANES), lambda p, i: (i, D_MODEL // LANES))],
        out_specs=[pl.BlockSpec((TM_PLAN // LANES, LANES), lambda p, i: (i * p, 0)),
                   pl.BlockSpec((1, LANES), lambda p, i: (0, 0))],
        out_shape=[jax.ShapeDtypeStruct((T // LANES, LANES), jnp.int32),
                   jax.ShapeDtypeStruct((1, LANES), F32)],
        scratch_shapes=[pltpu.VMEM((TM_PLAN, TM_PLAN), BF16),
                        pltpu.VMEM((1, LANES), F32),
                        pltpu.VMEM((1, LANES), F32)],
        compiler_params=_cparams("arbitrary", "arbitrary"),
        name="moe_plan",
    )(xr)
    return pos.reshape(T), counts[0, :N_CLASSES].astype(jnp.int32)


def _moe_plan(xr, n_tiles_max):
    T = xr.shape[0]
    pos, counts = _plan_positions(xr)
    tiles_per_class = (counts + TM_MOE - 1) // TM_MOE
    tile_end = jnp.cumsum(tiles_per_class)
    tile_start = tile_end - tiles_per_class
    n_tiles = tile_end[-1]
    tok = jnp.arange(T, dtype=jnp.int32)
    src_rows = jnp.zeros((n_tiles_max * TM_MOE,), jnp.int32).at[pos].set(tok)
    tile_ids = jnp.arange(n_tiles_max, dtype=jnp.int32)
    used = jnp.minimum(tile_ids, n_tiles - 1)
    tile_cls = jnp.sum((tile_end[None, :] <= used[:, None]).astype(jnp.int32), axis=1)
    tile_cls = jnp.minimum(tile_cls, N_CLASSES - 1)
    cls_lo, cls_hi = _class_experts()
    tile_elo = jnp.asarray(cls_lo)[tile_cls]
    tile_ehi = jnp.asarray(cls_hi)[tile_cls]
    within = tile_ids - tile_start[tile_cls]
    tile_cnt = jnp.clip(counts[tile_cls] - within * TM_MOE, 0, TM_MOE)
    tile_cnt = jnp.where(tile_ids < n_tiles, tile_cnt, 0).astype(jnp.int32)
    return src_rows, tile_elo, tile_ehi, tile_cnt, n_tiles.reshape(1).astype(jnp.int32)


def _vec(a):
    return a.reshape(1, -1).astype(F32)


def _mixer_and_cross_attention(x, mem, ln_in_g, ln_in_b, w_in, rel_bias, sink_a, norm_a_g, norm_b_g, w_out,
                               ln1_g, ln1_b, xq, xkv, xo, ln2_g, ln2_b, w_group, b_group, w_router, b_router):
    B, S, D = x.shape
    assert S == SEQ and D == D_MODEL and mem.shape[1:] == (MEM_LEN, D_MODEL)
    T = B * S
    vec = _vec

    w = w_in[0]
    edges = np.cumsum((0, A_WIDTH, A_KV_HEADS * HEAD_DIM, A_KV_HEADS * HEAD_DIM, B_WIDTH, B_WIDTH, B_WIDTH))
    qa, ka, va, qb, kb, vb = [w[:, a:b] for a, b in zip(edges[:-1], edges[1:])]
    dup = lambda t: jnp.repeat(t.reshape(D, A_KV_HEADS, 1, HEAD_DIM), 2, axis=2).reshape(D, A_KV_TILES * LANES)
    scale = HEAD_DIM ** -0.5 * LOG2E
    w_in_b = jnp.concatenate([qa * scale, dup(ka), dup(va), qb * scale, kb, vb], axis=1).astype(BF16)
    w_out_b = w_out[0].astype(BF16)
    xq_b = (xq[0] * (X_HEAD_DIM ** -0.5 * LOG2E)).astype(BF16)
    xkv_b = xkv[0].astype(BF16)
    xo_b = xo[0].astype(BF16)
    wr = jnp.concatenate([w_group[0], w_router[0]], axis=1).astype(F32)
    wr = jnp.pad(wr, ((0, 0), (0, LANES - wr.shape[1])))
    wr_hi = wr.astype(BF16)
    wr_split = jnp.concatenate([wr_hi, (wr - wr_hi.astype(F32)).astype(BF16)], axis=1)
    br = jnp.concatenate([b_group[0], b_router[0]]).astype(F32)
    br = jnp.pad(br, (0, LANES - br.shape[0])).reshape(1, LANES)

    h0, qkv_a, qkv_b1, qkv_b4, qkv_b16 = _input_projection(
        x.reshape(T, D), vec(ln_in_g), vec(ln_in_b), w_in_b, B)

    nk_a = TM_ATT + 2 * A_HALF_WIN
    bias_a = _band_bias(rel_bias[:, :A_HEADS], A_HALF_WIN, 1, TM_ATT, nk_a, S)
    ya = _band_attention(qkv_a.reshape(B, S, QKV_A), bias_a, nk=nk_a, sub=1, kv_of=(0, 0, 1, 1),
                         q_width=A_WIDTH, kv_width=A_KV_TILES * LANES, out_tiles=A_WIDTH // LANES,
                         sink=sink_a[0].astype(F32) * LOG2E, gain=vec(norm_a_g[0]))
    branch_out = []
    for (win, dil), qkv in zip(B_BRANCHES, (qkv_b1.reshape(B, S, QKV_B),
                                             qkv_b4.reshape(B * 4, S // 4, QKV_B),
                                             qkv_b16.reshape(B * 16, S // 16, QKV_B))):
        half = (win // 2) // dil
        n = S // dil
        nk = min(TM_ATT + 2 * half, n)
        bias_b = _band_bias(rel_bias[:, A_HEADS:], half, dil, TM_ATT, nk, n)
        branch_out.append(_band_attention(qkv, bias_b, nk=nk, sub=max(1, TM_ATT * 8 // n), kv_of=(0, 1, 2, 3),
                                          q_width=B_WIDTH, kv_width=B_WIDTH, out_tiles=B_OUT // LANES))
    o1 = branch_out[0]
    o4 = branch_out[1].reshape(B, 4, B_OUT // LANES, S // 4, LANES)
    o16 = branch_out[2].reshape(B, 16, B_OUT // LANES, S // 16, LANES)

    h1 = _merge_project(ya, o1, o4, o16, h0, vec(norm_b_g[0]), w_out_b,
                        vec(ln1_g[0]), vec(ln1_b[0]))

    k_mem, v_mem = _mem_kv(mem, xkv_b)
    xr = _cross_attention_route(h1, k_mem, v_mem, xq_b, xo_b, vec(ln2_g[0]), vec(ln2_b[0]), wr_split, br)
    return h0, h1, xr


def _moe(xr, w_gate, w_up, w_down, ln3_g, ln3_b):
    T = xr.shape[0]
    wg_b = w_gate[0].reshape(N_EXPERTS, D_MODEL, D_EXPERT).astype(BF16)
    wu_b = w_up[0].reshape(N_EXPERTS, D_MODEL, D_EXPERT).astype(BF16)
    wd_b = w_down[0].reshape(N_EXPERTS, D_EXPERT, D_MODEL).astype(BF16)
    n_tiles_max = T // TM_MOE + N_CLASSES
    src_rows, tile_elo, tile_ehi, tile_cnt, n_tiles = _moe_plan(xr, n_tiles_max)
    return _expert_mlp(src_rows, _token_of_row(src_rows), tile_cnt, tile_elo, tile_ehi, n_tiles, xr, wg_b, wu_b, wd_b,
                       _vec(ln3_g[0]), _vec(ln3_b[0]))


def kernel(x, mem, ln_in_g, ln_in_b, w_in, rel_bias, sink_a, norm_a_g, norm_b_g, w_out,
           ln1_g, ln1_b, xq, xkv, xo, ln2_g, ln2_b, w_group, b_group, w_router, b_router,
           w_gate, w_up, w_down, ln3_g, ln3_b):
    _, _, xr = _mixer_and_cross_attention(
        x, mem, ln_in_g, ln_in_b, w_in, rel_bias, sink_a, norm_a_g, norm_b_g, w_out,
        ln1_g, ln1_b, xq, xkv, xo, ln2_g, ln2_b, w_group, b_group, w_router, b_router)
    return _moe(xr, w_gate, w_up, w_down, ln3_g, ln3_b).reshape(x.shape)
```

```python
import functools

import numpy as np
import jax
import jax.numpy as jnp
from jax import lax
from jax.experimental import pallas as pl
from jax.experimental.pallas import tpu as pltpu

F32 = jnp.float32
BF16 = jnp.bfloat16

D_MODEL = 1024
SEQ = 2048
MEM_LEN = 256
HEAD_DIM = 64
A_HEADS = 8
A_KV_HEADS = 2
A_HALF_WIN = 128
B_HEADS = 8
B_BRANCHES = ((128, 1), (512, 4), (2048, 16))
N_BUCKETS = 32
MAX_DISTANCE = 1024
X_HEADS = 4
X_HEAD_DIM = D_MODEL // X_HEADS
N_GROUPS = 4
EXPERTS_PER_GROUP = 8
N_EXPERTS = N_GROUPS * EXPERTS_PER_GROUP
D_EXPERT = 512
DEPTH = 1
ALPHA = (2.0 * DEPTH) ** 0.25
LN_EPS = 1e-5
NEG = -1e30
LOG2E = 1.4426950408889634
LN2 = 0.6931471805599453

LANES = 128
XR_WIDTH = D_MODEL + LANES
A_WIDTH = A_HEADS * HEAD_DIM
B_WIDTH = B_HEADS * HEAD_DIM
A_KV_TILES = A_KV_HEADS
QKV_A = A_WIDTH + 2 * A_KV_TILES * LANES
QKV_B = 3 * B_WIDTH
B_OUT = B_WIDTH + LANES
MERGE_STRIDE = 4

PAIRS_PER_GROUP = EXPERTS_PER_GROUP * (EXPERTS_PER_GROUP - 1) // 2
N_CLASSES = N_GROUPS * PAIRS_PER_GROUP

TM_IN = 512
TM_ATT = 128
ATT_BLOCKS = 4
TM_MERGE = 512
TM_MOE = 128
ROW_GROUP = 8
TM_PLAN = 1024
VMEM_LIMIT = 56 * 1024 * 1024


def _cparams(*sem):
    return pltpu.CompilerParams(dimension_semantics=sem, vmem_limit_bytes=VMEM_LIMIT)


def _layer_norm(x, g, b):
    mu = jnp.mean(x, axis=-1, keepdims=True)
    xc = x - mu
    var = jnp.mean(xc * xc, axis=-1, keepdims=True)
    return xc * lax.rsqrt(var + LN_EPS) * g + b


def _rms_norm(x, g):
    return x * lax.rsqrt(jnp.mean(x * x, axis=-1, keepdims=True) + LN_EPS) * g


def _t5_bucket(rel):
    nb = N_BUCKETS // 2
    max_exact = nb // 2
    ret = (rel > 0).astype(np.int32) * nb
    n = np.abs(rel)
    n_safe = np.maximum(n, 1).astype(np.float64)
    large = max_exact + (np.log(n_safe / max_exact) / np.log(MAX_DISTANCE / max_exact)
                         * (nb - max_exact)).astype(np.int32)
    large = np.minimum(large, nb - 1)
    return (ret + np.where(n < max_exact, n, large)).astype(np.int32)


def _inproj_kernel(x_ref, g_ref, b_ref, w_ref, qa_ref, qb1_ref, qb4_ref, qb16_ref, pb_scr, p4_scr):
    h = _layer_norm(x_ref[...], g_ref[...], b_ref[...])
    proj = jnp.dot(h.astype(BF16), w_ref[...], preferred_element_type=F32)
    qa_ref[...] = proj[:, :QKV_A].astype(BF16)
    pb = proj[:, QKV_A:]
    qb1_ref[...] = pb.astype(BF16)
    n4 = TM_IN // 4
    for c in range(QKV_B // LANES):
        cs = slice(c * LANES, (c + 1) * LANES)
        pb_scr[c] = pb[:, cs]
        for r4 in range(4):
            rows = pb_scr[c, pl.ds(r4, n4, stride=4), :]
            qb4_ref[0, r4, :, cs] = rows.astype(BF16)
            p4_scr[c, r4 * n4:(r4 + 1) * n4, :] = rows
        for r16 in range(16):
            rows = p4_scr[c, pl.ds((r16 % 4) * n4 + r16 // 4, TM_IN // 16, stride=4), :]
            qb16_ref[0, r16, :, cs] = rows.astype(BF16)


def _input_projection(x2, ln_g, ln_b, w_in_b, batch):
    T = x2.shape[0]
    tiles_per_seq = SEQ // TM_IN
    row = lambda i: (i, 0)
    const = lambda i: (0, 0)
    deint = lambda i: (i // tiles_per_seq, 0, i % tiles_per_seq, 0)
    return pl.pallas_call(
        _inproj_kernel,
        grid=(T // TM_IN,),
        in_specs=[
            pl.BlockSpec((TM_IN, D_MODEL), row),
            pl.BlockSpec((1, D_MODEL), const),
            pl.BlockSpec((1, D_MODEL), const),
            pl.BlockSpec((D_MODEL, QKV_A + QKV_B), const),
        ],
        out_specs=[
            pl.BlockSpec((TM_IN, QKV_A), row),
            pl.BlockSpec((TM_IN, QKV_B), row),
            pl.BlockSpec((1, 4, TM_IN // 4, QKV_B), deint),
            pl.BlockSpec((1, 16, TM_IN // 16, QKV_B), deint),
        ],
        out_shape=[
            jax.ShapeDtypeStruct((T, QKV_A), BF16),
            jax.ShapeDtypeStruct((T, QKV_B), BF16),
            jax.ShapeDtypeStruct((batch, 4, SEQ // 4, QKV_B), BF16),
            jax.ShapeDtypeStruct((batch, 16, SEQ // 16, QKV_B), BF16),
        ],
        scratch_shapes=[pltpu.VMEM((QKV_B // LANES, TM_IN, LANES), F32)] * 2,
        compiler_params=_cparams("parallel"),
        name="ln_in_proj",
    )(x2, ln_g, ln_b, w_in_b)


def _band_attn_kernel(*refs, m, nk, n, sub, qb, kv_of, variants, with_sink):
    if with_sink:
        sink_ref, gain_ref = refs[0], refs[1]
        refs = refs[2:]
    q_ref, k_ref, v_ref, bias_ref, o_ref = refs
    nb = n // m
    first, middle, last = variants
    starts, bias_var = [], []
    for b in range(qb):
        if nk == n:
            starts.append(0)
            bias_var.append(first)
        else:
            jj = pl.program_id(1) * qb + b
            starts.append(pl.multiple_of(jnp.clip(jj * m - (nk - m) // 2, 0, n - nk), HEAD_DIM))
            bias_var.append(jnp.where(jj == 0, first, jnp.where(jj == nb - 1, last, middle)))
    lane_row = lax.broadcasted_iota(jnp.int32, (1, LANES), 1)
    keep_lo = jnp.where(lane_row < HEAD_DIM, 1.0, 0.0).astype(BF16)
    keep_hi = jnp.where(lane_row < HEAD_DIM, 0.0, 1.0).astype(BF16)
    lane = lax.broadcasted_iota(jnp.int32, (m, LANES), 1)
    n_q = len(kv_of)
    units = [(s, b, qt) for s in range(sub) for b in range(qb) for qt in range(n_q)]

    scores = []
    for s, b, qt in units:
        t = kv_of[qt]
        k_t = k_ref[s, pl.ds(starts[b], nk), t * LANES:(t + 1) * LANES]
        q2 = q_ref[s, b * m:(b + 1) * m, qt * LANES:(qt + 1) * LANES]
        lhs = jnp.concatenate([q2 * keep_lo, q2 * keep_hi], axis=0)
        sc = lax.dot_general(lhs, k_t, (((1,), (1,)), ((), ())), preferred_element_type=F32)
        scores.append(sc + bias_ref[bias_var[b], qt])

    probs, denom, row_max = [], [], []
    for (s, b, qt), sc in zip(units, scores):
        mx = jnp.max(sc, axis=-1, keepdims=True)
        p = jnp.exp2(sc - mx)
        l = jnp.sum(p, axis=-1, keepdims=True)
        halves = []
        for h in (0, 1):
            l_h = l[h * m:(h + 1) * m]
            if with_sink:
                l_h = l_h + jnp.exp2(sink_ref[2 * qt + h] - mx[h * m:(h + 1) * m])
            halves.append(l_h)
        probs.append(p.astype(BF16))
        denom.append(halves)
        row_max.append(mx)

    for s, b in [(s, b) for s in range(sub) for b in range(qb)]:
        rows = slice(b * m, (b + 1) * m)
        pairs = []
        lse_tile = jnp.zeros((m, LANES), F32)
        for qt in range(n_q):
            u = (s * qb + b) * n_q + qt
            t = kv_of[qt]
            v_t = v_ref[s, pl.ds(starts[b], nk), t * LANES:(t + 1) * LANES]
            o = jnp.dot(probs[u], v_t, preferred_element_type=F32)
            top = o[:m] * (1.0 / denom[u][0])
            bot = o[m:] * (1.0 / denom[u][1])
            pairs.append(jnp.where(lane < HEAD_DIM, top, bot))
            if not with_sink:
                for h in (0, 1):
                    lse = (row_max[u][h * m:(h + 1) * m] + jnp.log2(denom[u][h])) * LN2
                    lse_tile = jnp.where(lane == 2 * qt + h, lse, lse_tile)
        if with_sink:
            normed = _rms_norm(jnp.concatenate(pairs, axis=1), gain_ref[...])
            pairs = [normed[:, c * LANES:(c + 1) * LANES] for c in range(n_q)]
        else:
            o_ref[s, n_q, rows, :] = lse_tile
        for c in range(n_q):
            o_ref[s, c, rows, :] = pairs[c]


def _band_window(m, nk, n):
    nb = n // m
    starts = np.clip(np.arange(nb) * m - (nk - m) // 2, 0, n - nk)
    offs = [int(o) for o in starts - np.arange(nb) * m]
    uniq = sorted(set(offs), reverse=True)
    var = [uniq.index(o) for o in offs]
    assert all(v == var[1] for v in var[1:-1])
    return uniq, var


def _band_bias(rel_bias_h, half, dil, m, nk, n):
    uniq, _ = _band_window(m, nk, n)
    n_heads = rel_bias_h.shape[1]
    span = nk + m - 1
    out = []
    for off in uniq:
        rel = off - (m - 1) + np.arange(span)
        onehot = np.zeros((span, N_BUCKETS), np.float32)
        onehot[np.arange(span), _t5_bucket(dil * rel)] = 1.0
        table = jnp.dot(jnp.asarray(onehot), rel_bias_h.astype(F32), precision=lax.Precision.HIGHEST)
        table = jnp.where((np.abs(rel) <= half)[:, None], table, NEG)
        u = jnp.concatenate([table.T, jnp.zeros((n_heads, 1), F32)], axis=1)
        flat = jnp.tile(u, (1, m + 1))[:, m - 1:m - 1 + m * span]
        b = flat.reshape(n_heads, m, span)[:, :, :nk]
        out.append(b.reshape(n_heads // 2, 2 * m, nk))
    return jnp.stack(out) * LOG2E


def _band_attention(qkv, bias, *, nk, sub, kv_of, q_width, kv_width, out_tiles, sink=None, gain=None):
    Bd, n, _ = qkv.shape
    m = TM_ATT
    nb = n // m
    qb = min(nb, ATT_BLOCKS)
    assert nb % qb == 0 and nb * m == n and Bd % sub == 0
    _, var = _band_window(m, nk, n)
    variants = (var[0], var[min(1, nb - 1)], var[-1])
    kcol = q_width // kv_width
    with_sink = sink is not None
    in_specs = [
        pl.BlockSpec((sub, qb * m, q_width), lambda b, j: (b, j, 0)),
        pl.BlockSpec((sub, n, kv_width), lambda b, j: (b, 0, kcol)),
        pl.BlockSpec((sub, n, kv_width), lambda b, j: (b, 0, kcol + 1)),
        pl.BlockSpec(bias.shape, lambda b, j: (0, 0, 0, 0)),
    ]
    args = [qkv, qkv, qkv, bias]
    if with_sink:
        in_specs = [pl.BlockSpec(memory_space=pltpu.SMEM),
                    pl.BlockSpec((1, q_width), lambda b, j: (0, 0))] + in_specs
        args = [sink, gain] + args
    return pl.pallas_call(
        functools.partial(_band_attn_kernel, m=m, nk=nk, n=n, sub=sub, qb=qb, kv_of=kv_of, variants=variants,
                          with_sink=with_sink),
        grid=(Bd // sub, nb // qb),
        in_specs=in_specs,
        out_specs=pl.BlockSpec((sub, out_tiles, qb * m, LANES), lambda b, j: (b, 0, j, 0)),
        out_shape=jax.ShapeDtypeStruct((Bd, out_tiles, n, LANES), F32),
        compiler_params=_cparams("parallel", "arbitrary"),
        name="band_attn_sink" if with_sink else f"band_attn_n{n}",
    )(*args)


def _merge_rows(ya_ref, o1_ref, o4_ref, o16_ref, x_ref, gin_ref, bin_ref, gb_ref, w_ref, g1_ref, b1_ref,
                s16_scr, x_scr):
    n4 = TM_MERGE // MERGE_STRIDE
    n16 = TM_MERGE // 16
    h0_rows = _layer_norm(x_ref[...], gin_ref[...], bin_ref[...])
    for c in range(D_MODEL // LANES):
        x_scr[c] = h0_rows[:, c * LANES:(c + 1) * LANES]
    for r16 in range(16):
        for c in range(B_OUT // LANES):
            s16_scr[c, pl.ds((r16 % 4) * n4 + r16 // 4, n16, stride=4), :] = o16_ref[0, r16, c]

    def natural(ref, c):
        return jnp.concatenate([ref[0, c, pl.ds(r, n4, stride=MERGE_STRIDE), :]
                                for r in range(MERGE_STRIDE)], axis=0)

    def stride4(c):
        return jnp.concatenate([o4_ref[0, r, c] for r in range(MERGE_STRIDE)], axis=0)

    def stride16(c):
        return s16_scr[c]

    n_val = B_WIDTH // LANES
    lses = (natural(o1_ref, n_val), stride4(n_val), stride16(n_val))
    mx = jnp.maximum(jnp.maximum(lses[0], lses[1]), lses[2])
    ex = [jnp.exp(l - mx) for l in lses]
    inv = 1.0 / (ex[0] + ex[1] + ex[2])
    head = lax.broadcasted_iota(jnp.int32, (LANES, B_WIDTH), 0)
    col = lax.broadcasted_iota(jnp.int32, (LANES, B_WIDTH), 1)
    spread = jnp.where(col // HEAD_DIM == head, 1.0, 0.0).astype(BF16)
    wide = []
    for e in ex:
        w = e * inv
        w_hi = w.astype(BF16)
        w_lo = (w - w_hi.astype(F32)).astype(BF16)
        wide.append(jnp.dot(w_hi, spread, preferred_element_type=F32)
                    + jnp.dot(w_lo, spread, preferred_element_type=F32))
    pieces = []
    for c in range(n_val):
        cs = slice(c * LANES, (c + 1) * LANES)
        outs = (natural(o1_ref, c), stride4(c), stride16(c))
        pieces.append(wide[0][:, cs] * outs[0] + wide[1][:, cs] * outs[1] + wide[2][:, cs] * outs[2])
    yb = _rms_norm(jnp.concatenate(pieces, axis=1), gb_ref[...])
    ya = jnp.concatenate([natural(ya_ref, c) for c in range(A_WIDTH // LANES)], axis=1)
    y = jnp.concatenate([ya.astype(BF16), yb.astype(BF16)], axis=1)
    mix = jnp.dot(y, w_ref[...], preferred_element_type=F32)
    h0 = jnp.concatenate(
        [jnp.concatenate([x_scr[c, pl.ds(r, n4, stride=MERGE_STRIDE), :] for r in range(MERGE_STRIDE)], axis=0)
         for c in range(D_MODEL // LANES)], axis=1)
    return _layer_norm(ALPHA * h0 + mix, g1_ref[...], b1_ref[...])


def _token_of_row(p):
    n4 = TM_MERGE // MERGE_STRIDE
    tile, rest = p // TM_MERGE, p % TM_MERGE
    return tile * TM_MERGE + (rest % n4) * MERGE_STRIDE + rest // n4


def _mem_kv_kernel(mem_ref, w_ref, k_ref, v_ref):
    kv = jnp.dot(mem_ref[0].astype(BF16), w_ref[...], preferred_element_type=F32)
    k_ref[0] = kv[:, :D_MODEL].astype(BF16)
    v_ref[0] = kv[:, D_MODEL:].astype(BF16)


def _mem_kv(mem, xkv_b):
    B = mem.shape[0]
    blk = pl.BlockSpec((1, MEM_LEN, D_MODEL), lambda b: (b, 0, 0))
    return pl.pallas_call(
        _mem_kv_kernel,
        grid=(B,),
        in_specs=[blk, pl.BlockSpec((D_MODEL, 2 * D_MODEL), lambda b: (0, 0))],
        out_specs=[blk, blk],
        out_shape=[jax.ShapeDtypeStruct((B, MEM_LEN, D_MODEL), BF16)] * 2,
        compiler_params=_cparams("parallel"),
        name="mem_kv_proj",
    )(mem, xkv_b)


def _route(logits):
    rows = logits.shape[0]
    lane = lax.broadcasted_iota(jnp.int32, (rows, LANES), 1).astype(F32)
    big = float(LANES)
    ninf = -jnp.inf
    gl = jnp.where(lane < N_GROUPS, logits, ninf)
    gmax = jnp.max(gl, axis=-1, keepdims=True)
    gidx = jnp.min(jnp.where(gl == gmax, lane, big), axis=-1, keepdims=True)
    g_p = 1.0 / jnp.sum(jnp.exp(gl - gmax), axis=-1, keepdims=True)
    lo_lane = N_GROUPS + EXPERTS_PER_GROUP * gidx
    el = jnp.where((lane >= lo_lane) & (lane < lo_lane + EXPERTS_PER_GROUP), logits, ninf)
    v1 = jnp.max(el, axis=-1, keepdims=True)
    i1 = jnp.min(jnp.where(el == v1, lane, big), axis=-1, keepdims=True)
    el2 = jnp.where(lane == i1, ninf, el)
    v2 = jnp.max(el2, axis=-1, keepdims=True)
    i2 = jnp.min(jnp.where(el2 == v2, lane, big), axis=-1, keepdims=True)
    t = jnp.exp(v2 - v1)
    w1 = g_p / (1.0 + t)
    w2 = g_p * t / (1.0 + t)
    a = jnp.minimum(i1, i2) - lo_lane
    b = jnp.maximum(i1, i2) - lo_lane
    pair = a * (2 * EXPERTS_PER_GROUP - 1 - a) * 0.5 + (b - a - 1.0)
    cls = gidx * PAIRS_PER_GROUP + pair
    w_lo = jnp.where(i1 < i2, w1, w2)
    w_hi = jnp.where(i1 < i2, w2, w1)
    return jnp.where(lane == 0, cls, jnp.where(lane == 1, w_lo, jnp.where(lane == 2, w_hi, 0.0)))


def _merge_xattn_kernel(ya_ref, o1_ref, o4_ref, o16_ref, x_ref, gin_ref, bin_ref, gb_ref, w_ref, g1_ref, b1_ref,
                        k_ref, v_ref, wq_ref, wo_ref, g2_ref, b2_ref, wr_ref, br_ref, xr_ref, s16_scr, x_scr):
    h1 = _merge_rows(ya_ref, o1_ref, o4_ref, o16_ref, x_ref, gin_ref, bin_ref, gb_ref, w_ref, g1_ref, b1_ref,
                     s16_scr, x_scr)
    q = jnp.dot(h1.astype(BF16), wq_ref[...], preferred_element_type=F32).astype(BF16)
    outs = []
    for hd in range(X_HEADS):
        sl = slice(hd * X_HEAD_DIM, (hd + 1) * X_HEAD_DIM)
        s = lax.dot_general(q[:, sl], k_ref[0, :, sl], (((1,), (1,)), ((), ())), preferred_element_type=F32)
        m = jnp.max(s, axis=-1, keepdims=True)
        p = jnp.exp2(s - m)
        l = jnp.sum(p, axis=-1, keepdims=True)
        o = jnp.dot(p.astype(BF16), v_ref[0, :, sl], preferred_element_type=F32) / l
        outs.append(o.astype(BF16))
    xa = jnp.dot(jnp.concatenate(outs, axis=1), wo_ref[...], preferred_element_type=F32)
    h2 = _layer_norm(ALPHA * h1 + xa, g2_ref[...], b2_ref[...])
    h_hi = h2.astype(BF16)
    h_lo = (h2 - h_hi.astype(F32)).astype(BF16)
    t_hi = jnp.dot(h_hi, wr_ref[...], preferred_element_type=F32)
    t_lo = jnp.dot(h_lo, wr_ref[...], preferred_element_type=F32)
    logits = (t_hi[:, :LANES] + t_hi[:, LANES:]) + (t_lo[:, :LANES] + t_lo[:, LANES:]) + br_ref[...]
    xr_ref[:, :D_MODEL] = h2
    xr_ref[:, D_MODEL:] = _route(logits)


def _merge_cross_attention_route(ya, o1, o4, o16, x2, ln_in_g, ln_in_b, gain_b, w_out_b, ln1_g, ln1_b,
                                 k, v, xq_b, xo_b, ln2_g, ln2_b, wr_split, br):
    T = x2.shape[0]
    tiles_per_seq = SEQ // TM_MERGE
    row = lambda i: (i, 0)
    const = lambda i: (0, 0)
    nat = lambda i: (i // tiles_per_seq, 0, i % tiles_per_seq, 0)
    deint = lambda i: (i // tiles_per_seq, 0, 0, i % tiles_per_seq, 0)
    vec_d = pl.BlockSpec((1, D_MODEL), const)
    mat_d = pl.BlockSpec((D_MODEL, D_MODEL), const)
    kv_blk = pl.BlockSpec((1, MEM_LEN, D_MODEL), lambda i: (i // tiles_per_seq, 0, 0))
    n_b = B_OUT // LANES
    return pl.pallas_call(
        _merge_xattn_kernel,
        grid=(T // TM_MERGE,),
        in_specs=[
            pl.BlockSpec((1, A_WIDTH // LANES, TM_MERGE, LANES), nat),
            pl.BlockSpec((1, n_b, TM_MERGE, LANES), nat),
            pl.BlockSpec((1, 4, n_b, TM_MERGE // 4, LANES), deint),
            pl.BlockSpec((1, 16, n_b, TM_MERGE // 16, LANES), deint),
            pl.BlockSpec((TM_MERGE, D_MODEL), row), vec_d, vec_d,
            pl.BlockSpec((1, B_WIDTH), const), mat_d, vec_d, vec_d,
            kv_blk, kv_blk, mat_d, mat_d, vec_d, vec_d,
            pl.BlockSpec((D_MODEL, 2 * LANES), const),
            pl.BlockSpec((1, LANES), const),
        ],
        out_specs=pl.BlockSpec((TM_MERGE, XR_WIDTH), row),
        out_shape=jax.ShapeDtypeStruct((T, XR_WIDTH), F32),
        scratch_shapes=[pltpu.VMEM((n_b, TM_MERGE, LANES), F32),
                        pltpu.VMEM((D_MODEL // LANES, TM_MERGE, LANES), F32)],
        compiler_params=_cparams("parallel"),
        name="merge_xattn_route",
    )(ya, o1, o4, o16, x2, ln_in_g, ln_in_b, gain_b, w_out_b, ln1_g, ln1_b,
      k, v, xq_b, xo_b, ln2_g, ln2_b, wr_split, br)


def _expert_kernel(src_ref, dst_ref, cnt_ref, elo_ref, ehi_ref, ntile_ref, x_hbm,
                   wg_lo, wu_lo, wd_lo, wg_hi, wu_hi, wd_hi, g3_ref, b3_ref, o_hbm,
                   xbuf, obuf, gsem, ssem):
    j = pl.program_id(0)
    n_tiles = ntile_ref[0]
    slot = j % 2

    def row_copy(t, i, s, gather):
        row = pl.ds(i, 1)
        if gather:
            return pltpu.make_async_copy(x_hbm.at[pl.ds(src_ref[t * TM_MOE + i], 1)], xbuf.at[s, row], gsem.at[s])
        return pltpu.make_async_copy(obuf.at[s, row], o_hbm.at[pl.ds(dst_ref[t * TM_MOE + i], 1)], ssem.at[s])

    def n_rows(t, gather):
        cnt = cnt_ref[t]
        return (cnt + ROW_GROUP - 1) // ROW_GROUP * ROW_GROUP if gather else cnt

    def start_rows(t, s, gather):
        cnt = cnt_ref[t]
        n_groups = (cnt + ROW_GROUP - 1) // ROW_GROUP if gather else cnt // ROW_GROUP

        def group(g, c):
            base = pl.multiple_of(g * ROW_GROUP, ROW_GROUP)
            for r in range(ROW_GROUP):
                row_copy(t, base + r, s, gather).start(priority=0 if gather else 1)
            return c
        lax.fori_loop(0, n_groups, group, 0)
        if not gather:
            def single(i, c):
                row_copy(t, i, s, gather).start(priority=1)
                return c
            lax.fori_loop(n_groups * ROW_GROUP, cnt, single, 0)

    def wait_rows(t, s, gather):
        cnt = n_rows(t, gather)
        for bit in range(TM_MOE.bit_length()):
            rows = pl.ds(0, 1 << bit)

            @pl.when((cnt >> bit) & 1 == 1)
            def _():
                if gather:
                    pltpu.make_async_copy(x_hbm.at[rows], xbuf.at[s, rows], gsem.at[s]).wait()
                else:
                    pltpu.make_async_copy(obuf.at[s, rows], o_hbm.at[rows], ssem.at[s]).wait()

    @pl.when(j == 0)
    def _():
        xbuf[...] = jnp.zeros_like(xbuf)
        start_rows(0, 0, True)

    @pl.when(j < n_tiles)
    def _():
        @pl.when(j + 1 < n_tiles)
        def _():
            start_rows(j + 1, 1 - slot, True)

        wait_rows(j, slot, True)
        x = xbuf[slot, :, :D_MODEL]
        xb = x.astype(BF16)
        y = jnp.zeros_like(x)
        for e, (wg, wu, wd) in enumerate(((wg_lo, wu_lo, wd_lo), (wg_hi, wu_hi, wd_hi))):
            gate = xbuf[slot, :, D_MODEL + 1 + e:D_MODEL + 2 + e]
            a = jnp.dot(xb, wg[0], preferred_element_type=F32)
            u = jnp.dot(xb, wu[0], preferred_element_type=F32)
            hid = a * jax.nn.sigmoid(a) * u
            y = y + jnp.dot((gate * hid).astype(BF16), wd[0], preferred_element_type=F32)
        out = _layer_norm(ALPHA * x + y, g3_ref[...], b3_ref[...])

        @pl.when(j >= 2)
        def _():
            wait_rows(j - 2, slot, False)

        obuf[slot] = out
        start_rows(j, slot, False)

        @pl.when(j == n_tiles - 1)
        def _():
            @pl.when(j >= 1)
            def _():
                wait_rows(j - 1, 1 - slot, False)

            wait_rows(j, slot, False)


def _expert_mlp(src_rows, dst_rows, tile_cnt, tile_elo, tile_ehi, n_tiles, xr, wg, wu, wd, ln_g, ln_b):
    n_tiles_max = tile_cnt.shape[0]
    const = lambda j, src, dst, cnt, elo, ehi, nt: (0, 0)
    lo = lambda j, src, dst, cnt, elo, ehi, nt: (elo[j], 0, 0)
    hi = lambda j, src, dst, cnt, elo, ehi, nt: (ehi[j], 0, 0)
    up = (1, D_MODEL, D_EXPERT)
    down = (1, D_EXPERT, D_MODEL)
    any_spec = pl.BlockSpec(memory_space=pl.ANY)
    return pl.pallas_call(
        _expert_kernel,
        grid_spec=pltpu.PrefetchScalarGridSpec(
            num_scalar_prefetch=6,
            grid=(n_tiles_max,),
            in_specs=[
                any_spec,
                pl.BlockSpec(up, lo), pl.BlockSpec(up, lo), pl.BlockSpec(down, lo),
                pl.BlockSpec(up, hi), pl.BlockSpec(up, hi), pl.BlockSpec(down, hi),
                pl.BlockSpec((1, D_MODEL), const),
                pl.BlockSpec((1, D_MODEL), const),
            ],
            out_specs=any_spec,
            scratch_shapes=[
                pltpu.VMEM((2, TM_MOE, XR_WIDTH), F32),
                pltpu.VMEM((2, TM_MOE, D_MODEL), F32),
                pltpu.SemaphoreType.DMA((2,)),
                pltpu.SemaphoreType.DMA((2,)),
            ],
        ),
        out_shape=jax.ShapeDtypeStruct((xr.shape[0], D_MODEL), F32),
        compiler_params=_cparams("arbitrary"),
        name="moe_experts",
    )(src_rows, dst_rows, tile_cnt, tile_elo, tile_ehi, n_tiles, xr, wg, wu, wd, wg, wu, wd, ln_g, ln_b)


def _class_experts():
    lo, hi = [], []
    for g in range(N_GROUPS):
        for a in range(EXPERTS_PER_GROUP):
            for b in range(a + 1, EXPERTS_PER_GROUP):
                lo.append(g * EXPERTS_PER_GROUP + a)
                hi.append(g * EXPERTS_PER_GROUP + b)
    return np.asarray(lo, np.int32), np.asarray(hi, np.int32)


def _plan_kernel(route_ref, pos_ref, counts_ref, tri_scr, run_scr, start_scr):
    phase, i = pl.program_id(0), pl.program_id(1)
    lane = lax.broadcasted_iota(jnp.int32, (TM_PLAN, LANES), 1)
    onehot = lane.astype(F32) == route_ref[:, 0:1]
    onehot_f = jnp.where(onehot, 1.0, 0.0)

    @pl.when((phase == 0) & (i == 0))
    def _():
        run_scr[...] = jnp.zeros_like(run_scr)
        r = lax.broadcasted_iota(jnp.int32, (TM_PLAN, TM_PLAN), 0)
        c = lax.broadcasted_iota(jnp.int32, (TM_PLAN, TM_PLAN), 1)
        tri_scr[...] = jnp.where(c < r, 1.0, 0.0).astype(BF16)

    @pl.when((phase == 1) & (i == 0))
    def _():
        counts = run_scr[...]
        counts_ref[...] = counts
        tiles = jnp.floor((counts + (TM_MOE - 1)) * (1.0 / TM_MOE))
        lane_row = lax.broadcasted_iota(jnp.int32, (1, LANES), 1)
        scan = tiles
        shift = 1
        while shift < LANES:
            scan = scan + jnp.where(lane_row >= shift, pltpu.roll(scan, shift, axis=1), 0.0)
            shift *= 2
        start_scr[...] = (scan - tiles) * TM_MOE
        run_scr[...] = jnp.zeros_like(run_scr)

    @pl.when(phase == 1)
    def _():
        before = jnp.dot(tri_scr[...], onehot_f.astype(BF16), preferred_element_type=F32)
        pos_col = jnp.sum(onehot_f * (before + run_scr[...] + start_scr[...]), axis=1, keepdims=True)
        eye = (lax.broadcasted_iota(jnp.int32, (LANES, LANES), 0)
               == lax.broadcasted_iota(jnp.int32, (LANES, LANES), 1))
        for r in range(TM_PLAN // LANES):
            row = jnp.sum(jnp.where(eye, pos_col[r * LANES:(r + 1) * LANES], 0.0), axis=0, keepdims=True)
            pos_ref[r:r + 1, :] = row.astype(jnp.int32)

    run_scr[...] = run_scr[...] + jnp.sum(onehot_f, axis=0, keepdims=True)


def _plan_positions(xr):
    T = xr.shape[0]
    n_steps = T // TM_PLAN
    pos, counts = pl.pallas_call(
        _plan_kernel,
        grid=(2, n_steps),
        in_specs=[pl.BlockSpec((TM_PLAN, LANES), lambda p, i: (i, D_MODEL // LANES))],
        out_specs=[pl.BlockSpec((TM_PLAN // LANES, LANES), lambda p, i: (i * p, 0)),
                   pl.BlockSpec((1, LANES), lambda p, i: (0, 0))],
        out_shape=[jax.ShapeDtypeStruct((T // LANES, LANES), jnp.int32),
                   jax.ShapeDtypeStruct((1, LANES), F32)],
        scratch_shapes=[pltpu.VMEM((TM_PLAN, TM_PLAN), BF16),
                        pltpu.VMEM((1, LANES), F32),
                        pltpu.VMEM((1, LANES), F32)],
        compiler_params=_cparams("arbitrary", "arbitrary"),
        name="moe_plan",
    )(xr)
    return pos.reshape(T), counts[0, :N_CLASSES].astype(jnp.int32)


def _moe_plan(xr, n_tiles_max):
    T = xr.shape[0]
    pos, counts = _plan_positions(xr)
    tiles_per_class = (counts + TM_MOE - 1) // TM_MOE
    tile_end = jnp.cumsum(tiles_per_class)
    tile_start = tile_end - tiles_per_class
    n_tiles = tile_end[-1]
    tok = jnp.arange(T, dtype=jnp.int32)
    src_rows = jnp.zeros((n_tiles_max * TM_MOE,), jnp.int32).at[pos].set(tok)
    tile_ids = jnp.arange(n_tiles_max, dtype=jnp.int32)
    used = jnp.minimum(tile_ids, n_tiles - 1)
    tile_cls = jnp.sum((tile_end[None, :] <= used[:, None]).astype(jnp.int32), axis=1)
    tile_cls = jnp.minimum(tile_cls, N_CLASSES - 1)
    cls_lo, cls_hi = _class_experts()
    tile_elo = jnp.asarray(cls_lo)[tile_cls]
    tile_ehi = jnp.asarray(cls_hi)[tile_cls]
    within = tile_ids - tile_start[tile_cls]
    tile_cnt = jnp.clip(counts[tile_cls] - within * TM_MOE, 0, TM_MOE)
    tile_cnt = jnp.where(tile_ids < n_tiles, tile_cnt, 0).astype(jnp.int32)
    return src_rows, tile_elo, tile_ehi, tile_cnt, n_tiles.reshape(1).astype(jnp.int32)


def _vec(a):
    return a.reshape(1, -1).astype(F32)


def _mixer_and_cross_attention(x, mem, ln_in_g, ln_in_b, w_in, rel_bias, sink_a, norm_a_g, norm_b_g, w_out,
                               ln1_g, ln1_b, xq, xkv, xo, ln2_g, ln2_b, w_group, b_group, w_router, b_router):
    B, S, D = x.shape
    assert S == SEQ and D == D_MODEL and mem.shape[1:] == (MEM_LEN, D_MODEL)
    T = B * S
    vec = _vec

    w = w_in[0]
    edges = np.cumsum((0, A_WIDTH, A_KV_HEADS * HEAD_DIM, A_KV_HEADS * HEAD_DIM, B_WIDTH, B_WIDTH, B_WIDTH))
    qa, ka, va, qb, kb, vb = [w[:, a:b] for a, b in zip(edges[:-1], edges[1:])]
    dup = lambda t: jnp.repeat(t.reshape(D, A_KV_HEADS, 1, HEAD_DIM), 2, axis=2).reshape(D, A_KV_TILES * LANES)
    scale = HEAD_DIM ** -0.5 * LOG2E
    w_in_b = jnp.concatenate([qa * scale, dup(ka), dup(va), qb * scale, kb, vb], axis=1).astype(BF16)
    w_out_b = w_out[0].astype(BF16)
    xq_b = (xq[0] * (X_HEAD_DIM ** -0.5 * LOG2E)).astype(BF16)
    xkv_b = xkv[0].astype(BF16)
    xo_b = xo[0].astype(BF16)
    wr = jnp.concatenate([w_group[0], w_router[0]], axis=1).astype(F32)
    wr = jnp.pad(wr, ((0, 0), (0, LANES - wr.shape[1])))
    wr_hi = wr.astype(BF16)
    wr_split = jnp.concatenate([wr_hi, (wr - wr_hi.astype(F32)).astype(BF16)], axis=1)
    br = jnp.concatenate([b_group[0], b_router[0]]).astype(F32)
    br = jnp.pad(br, (0, LANES - br.shape[0])).reshape(1, LANES)

    x2 = x.reshape(T, D)
    qkv_a, qkv_b1, qkv_b4, qkv_b16 = _input_projection(x2, vec(ln_in_g), vec(ln_in_b), w_in_b, B)

    nk_a = TM_ATT + 2 * A_HALF_WIN
    bias_a = _band_bias(rel_bias[:, :A_HEADS], A_HALF_WIN, 1, TM_ATT, nk_a, S)
    ya = _band_attention(qkv_a.reshape(B, S, QKV_A), bias_a, nk=nk_a, sub=1, kv_of=(0, 0, 1, 1),
                         q_width=A_WIDTH, kv_width=A_KV_TILES * LANES, out_tiles=A_WIDTH // LANES,
                         sink=sink_a[0].astype(F32) * LOG2E, gain=vec(norm_a_g[0]))
    branch_out = []
    for (win, dil), qkv in zip(B_BRANCHES, (qkv_b1.reshape(B, S, QKV_B),
                                             qkv_b4.reshape(B * 4, S // 4, QKV_B),
                                             qkv_b16.reshape(B * 16, S // 16, QKV_B))):
        half = (win // 2) // dil
        n = S // dil
        nk = min(TM_ATT + 2 * half, n)
        bias_b = _band_bias(rel_bias[:, A_HEADS:], half, dil, TM_ATT, nk, n)
        branch_out.append(_band_attention(qkv, bias_b, nk=nk, sub=max(1, TM_ATT * 8 // n), kv_of=(0, 1, 2, 3),
                                          q_width=B_WIDTH, kv_width=B_WIDTH, out_tiles=B_OUT // LANES))
    o1 = branch_out[0]
    o4 = branch_out[1].reshape(B, 4, B_OUT // LANES, S // 4, LANES)
    o16 = branch_out[2].reshape(B, 16, B_OUT // LANES, S // 16, LANES)

    k_mem, v_mem = _mem_kv(mem, xkv_b)
    return _merge_cross_attention_route(
        ya, o1, o4, o16, x2, vec(ln_in_g), vec(ln_in_b), vec(norm_b_g[0]), w_out_b, vec(ln1_g[0]), vec(ln1_b[0]),
        k_mem, v_mem, xq_b, xo_b, vec(ln2_g[0]), vec(ln2_b[0]), wr_split, br)


def _moe(xr, w_gate, w_up, w_down, ln3_g, ln3_b):
    T = xr.shape[0]
    wg_b = w_gate[0].reshape(N_EXPERTS, D_MODEL, D_EXPERT).astype(BF16)
    wu_b = w_up[0].reshape(N_EXPERTS, D_MODEL, D_EXPERT).astype(BF16)
    wd_b = w_down[0].reshape(N_EXPERTS, D_EXPERT, D_MODEL).astype(BF16)
    n_tiles_max = T // TM_MOE + N_CLASSES
    src_rows, tile_elo, tile_ehi, tile_cnt, n_tiles = _moe_plan(xr, n_tiles_max)
    return _expert_mlp(src_rows, _token_of_row(src_rows), tile_cnt, tile_elo, tile_ehi, n_tiles, xr, wg_b, wu_b, wd_b,
                       _vec(ln3_g[0]), _vec(ln3_b[0]))


def kernel(x, mem, ln_in_g, ln_in_b, w_in, rel_bias, sink_a, norm_a_g, norm_b_g, w_out,
           ln1_g, ln1_b, xq, xkv, xo, ln2_g, ln2_b, w_group, b_group, w_router, b_router,
           w_gate, w_up, w_down, ln3_g, ln3_b):
    xr = _mixer_and_cross_attention(
        x, mem, ln_in_g, ln_in_b, w_in, rel_bias, sink_a, norm_a_g, norm_b_g, w_out,
        ln1_g, ln1_b, xq, xkv, xo, ln2_g, ln2_b, w_group, b_group, w_router, b_router)
    return _moe(xr, w_gate, w_up, w_down, ln3_g, ln3_b).reshape(x.shape)
```

```python
import functools

import numpy as np
import jax
import jax.numpy as jnp
from jax import lax
from jax.experimental import pallas as pl
from jax.experimental.pallas import tpu as pltpu

F32 = jnp.float32
BF16 = jnp.bfloat16

D_MODEL = 1024
SEQ = 2048
MEM_LEN = 256
HEAD_DIM = 64
A_HEADS = 8
A_KV_HEADS = 2
A_HALF_WIN = 128
B_HEADS = 8
B_BRANCHES = ((128, 1), (512, 4), (2048, 16))
N_BUCKETS = 32
MAX_DISTANCE = 1024
X_HEADS = 4
X_HEAD_DIM = D_MODEL // X_HEADS
N_GROUPS = 4
EXPERTS_PER_GROUP = 8
N_EXPERTS = N_GROUPS * EXPERTS_PER_GROUP
D_EXPERT = 512
DEPTH = 1
ALPHA = (2.0 * DEPTH) ** 0.25
LN_EPS = 1e-5
NEG = -1e30
LOG2E = 1.4426950408889634
LN2 = 0.6931471805599453

LANES = 128
XR_WIDTH = D_MODEL + LANES
A_WIDTH = A_HEADS * HEAD_DIM
B_WIDTH = B_HEADS * HEAD_DIM
A_KV_TILES = A_KV_HEADS
QKV_A = A_WIDTH + 2 * A_KV_TILES * LANES
QKV_B = 3 * B_WIDTH
B_OUT = B_WIDTH + LANES
MERGE_STRIDE = 4

PAIRS_PER_GROUP = EXPERTS_PER_GROUP * (EXPERTS_PER_GROUP - 1) // 2
N_CLASSES = N_GROUPS * PAIRS_PER_GROUP

TM_IN = 512
TM_ATT = 128
ATT_BLOCKS = 4
TM_MERGE = 512
TM_MOE = 128
ROW_GROUP = 8
TM_PLAN = 1024
VMEM_LIMIT = 56 * 1024 * 1024


def _cparams(*sem):
    return pltpu.CompilerParams(dimension_semantics=sem, vmem_limit_bytes=VMEM_LIMIT)


def _layer_norm(x, g, b):
    mu = jnp.mean(x, axis=-1, keepdims=True)
    xc = x - mu
    var = jnp.mean(xc * xc, axis=-1, keepdims=True)
    return xc * lax.rsqrt(var + LN_EPS) * g + b


def _rms_norm(x, g):
    return x * lax.rsqrt(jnp.mean(x * x, axis=-1, keepdims=True) + LN_EPS) * g


def _t5_bucket(rel):
    nb = N_BUCKETS // 2
    max_exact = nb // 2
    ret = (rel > 0).astype(np.int32) * nb
    n = np.abs(rel)
    n_safe = np.maximum(n, 1).astype(np.float64)
    large = max_exact + (np.log(n_safe / max_exact) / np.log(MAX_DISTANCE / max_exact)
                         * (nb - max_exact)).astype(np.int32)
    large = np.minimum(large, nb - 1)
    return (ret + np.where(n < max_exact, n, large)).astype(np.int32)


def _inproj_kernel(x_ref, g_ref, b_ref, w_ref, qa_ref, qb1_ref, qb4_ref, qb16_ref, pb_scr, p4_scr):
    h = _layer_norm(x_ref[...], g_ref[...], b_ref[...])
    proj = jnp.dot(h.astype(BF16), w_ref[...], preferred_element_type=F32)
    qa_ref[...] = proj[:, :QKV_A].astype(BF16)
    pb = proj[:, QKV_A:]
    qb1_ref[...] = pb.astype(BF16)
    n4 = TM_IN // 4
    for c in range(QKV_B // LANES):
        cs = slice(c * LANES, (c + 1) * LANES)
        pb_scr[c] = pb[:, cs]
        for r4 in range(4):
            rows = pb_scr[c, pl.ds(r4, n4, stride=4), :]
            qb4_ref[0, r4, :, cs] = rows.astype(BF16)
            p4_scr[c, r4 * n4:(r4 + 1) * n4, :] = rows
        for r16 in range(16):
            rows = p4_scr[c, pl.ds((r16 % 4) * n4 + r16 // 4, TM_IN // 16, stride=4), :]
            qb16_ref[0, r16, :, cs] = rows.astype(BF16)


def _input_projection(x2, ln_g, ln_b, w_in_b, batch):
    T = x2.shape[0]
    tiles_per_seq = SEQ // TM_IN
    row = lambda i: (i, 0)
    const = lambda i: (0, 0)
    deint = lambda i: (i // tiles_per_seq, 0, i % tiles_per_seq, 0)
    return pl.pallas_call(
        _inproj_kernel,
        grid=(T // TM_IN,),
        in_specs=[
            pl.BlockSpec((TM_IN, D_MODEL), row),
            pl.BlockSpec((1, D_MODEL), const),
            pl.BlockSpec((1, D_MODEL), const),
            pl.BlockSpec((D_MODEL, QKV_A + QKV_B), const),
        ],
        out_specs=[
            pl.BlockSpec((TM_IN, QKV_A), row),
            pl.BlockSpec((TM_IN, QKV_B), row),
            pl.BlockSpec((1, 4, TM_IN // 4, QKV_B), deint),
            pl.BlockSpec((1, 16, TM_IN // 16, QKV_B), deint),
        ],
        out_shape=[
            jax.ShapeDtypeStruct((T, QKV_A), BF16),
            jax.ShapeDtypeStruct((T, QKV_B), BF16),
            jax.ShapeDtypeStruct((batch, 4, SEQ // 4, QKV_B), BF16),
            jax.ShapeDtypeStruct((batch, 16, SEQ // 16, QKV_B), BF16),
        ],
        scratch_shapes=[pltpu.VMEM((QKV_B // LANES, TM_IN, LANES), F32)] * 2,
        compiler_params=_cparams("parallel"),
        name="ln_in_proj",
    )(x2, ln_g, ln_b, w_in_b)


def _band_attn_kernel(*refs, m, nk, n, sub, qb, kv_of, variants, with_sink):
    if with_sink:
        sink_ref, gain_ref = refs[0], refs[1]
        refs = refs[2:]
    q_ref, k_ref, v_ref, bias_ref, o_ref = refs
    nb = n // m
    first, middle, last = variants
    starts, bias_var = [], []
    for b in range(qb):
        if nk == n:
            starts.append(0)
            bias_var.append(first)
        else:
            jj = pl.program_id(1) * qb + b
            starts.append(pl.multiple_of(jnp.clip(jj * m - (nk - m) // 2, 0, n - nk), HEAD_DIM))
            bias_var.append(jnp.where(jj == 0, first, jnp.where(jj == nb - 1, last, middle)))
    lane_row = lax.broadcasted_iota(jnp.int32, (1, LANES), 1)
    keep_lo = jnp.where(lane_row < HEAD_DIM, 1.0, 0.0).astype(BF16)
    keep_hi = jnp.where(lane_row < HEAD_DIM, 0.0, 1.0).astype(BF16)
    lane = lax.broadcasted_iota(jnp.int32, (m, LANES), 1)
    n_q = len(kv_of)
    units = [(s, b, qt) for s in range(sub) for b in range(qb) for qt in range(n_q)]

    scores = []
    for s, b, qt in units:
        t = kv_of[qt]
        k_t = k_ref[s, pl.ds(starts[b], nk), t * LANES:(t + 1) * LANES]
        q2 = q_ref[s, b * m:(b + 1) * m, qt * LANES:(qt + 1) * LANES]
        lhs = jnp.concatenate([q2 * keep_lo, q2 * keep_hi], axis=0)
        sc = lax.dot_general(lhs, k_t, (((1,), (1,)), ((), ())), preferred_element_type=F32)
        scores.append(sc + bias_ref[bias_var[b], qt])

    probs, denom, row_max = [], [], []
    for (s, b, qt), sc in zip(units, scores):
        mx = jnp.max(sc, axis=-1, keepdims=True)
        p = jnp.exp2(sc - mx)
        l = jnp.sum(p, axis=-1, keepdims=True)
        halves = []
        for h in (0, 1):
            l_h = l[h * m:(h + 1) * m]
            if with_sink:
                l_h = l_h + jnp.exp2(sink_ref[2 * qt + h] - mx[h * m:(h + 1) * m])
            halves.append(l_h)
        probs.append(p.astype(BF16))
        denom.append(halves)
        row_max.append(mx)

    for s, b in [(s, b) for s in range(sub) for b in range(qb)]:
        rows = slice(b * m, (b + 1) * m)
        pairs = []
        lse_tile = jnp.zeros((m, LANES), F32)
        for qt in range(n_q):
            u = (s * qb + b) * n_q + qt
            t = kv_of[qt]
            v_t = v_ref[s, pl.ds(starts[b], nk), t * LANES:(t + 1) * LANES]
            o = jnp.dot(probs[u], v_t, preferred_element_type=F32)
            top = o[:m] * (1.0 / denom[u][0])
            bot = o[m:] * (1.0 / denom[u][1])
            pairs.append(jnp.where(lane < HEAD_DIM, top, bot))
            if not with_sink:
                for h in (0, 1):
                    lse = (row_max[u][h * m:(h + 1) * m] + jnp.log2(denom[u][h])) * LN2
                    lse_tile = jnp.where(lane == 2 * qt + h, lse, lse_tile)
        if with_sink:
            normed = _rms_norm(jnp.concatenate(pairs, axis=1), gain_ref[...])
            pairs = [normed[:, c * LANES:(c + 1) * LANES] for c in range(n_q)]
        else:
            o_ref[s, n_q, rows, :] = lse_tile
        for c in range(n_q):
            o_ref[s, c, rows, :] = pairs[c]


def _band_window(m, nk, n):
    nb = n // m
    starts = np.clip(np.arange(nb) * m - (nk - m) // 2, 0, n - nk)
    offs = [int(o) for o in starts - np.arange(nb) * m]
    uniq = sorted(set(offs), reverse=True)
    var = [uniq.index(o) for o in offs]
    assert all(v == var[1] for v in var[1:-1])
    return uniq, var


def _band_bias(rel_bias_h, half, dil, m, nk, n):
    uniq, _ = _band_window(m, nk, n)
    n_heads = rel_bias_h.shape[1]
    span = nk + m - 1
    out = []
    for off in uniq:
        rel = off - (m - 1) + np.arange(span)
        onehot = np.zeros((span, N_BUCKETS), np.float32)
        onehot[np.arange(span), _t5_bucket(dil * rel)] = 1.0
        table = jnp.dot(jnp.asarray(onehot), rel_bias_h.astype(F32), precision=lax.Precision.HIGHEST)
        table = jnp.where((np.abs(rel) <= half)[:, None], table, NEG)
        u = jnp.concatenate([table.T, jnp.zeros((n_heads, 1), F32)], axis=1)
        flat = jnp.tile(u, (1, m + 1))[:, m - 1:m - 1 + m * span]
        b = flat.reshape(n_heads, m, span)[:, :, :nk]
        out.append(b.reshape(n_heads // 2, 2 * m, nk))
    return jnp.stack(out) * LOG2E


def _band_attention(qkv, bias, *, nk, sub, kv_of, q_width, kv_width, out_tiles, sink=None, gain=None):
    Bd, n, _ = qkv.shape
    m = TM_ATT
    nb = n // m
    qb = min(nb, ATT_BLOCKS)
    assert nb % qb == 0 and nb * m == n and Bd % sub == 0
    _, var = _band_window(m, nk, n)
    variants = (var[0], var[min(1, nb - 1)], var[-1])
    kcol = q_width // kv_width
    with_sink = sink is not None
    in_specs = [
        pl.BlockSpec((sub, qb * m, q_width), lambda b, j: (b, j, 0)),
        pl.BlockSpec((sub, n, kv_width), lambda b, j: (b, 0, kcol)),
        pl.BlockSpec((sub, n, kv_width), lambda b, j: (b, 0, kcol + 1)),
        pl.BlockSpec(bias.shape, lambda b, j: (0, 0, 0, 0)),
    ]
    args = [qkv, qkv, qkv, bias]
    if with_sink:
        in_specs = [pl.BlockSpec(memory_space=pltpu.SMEM),
                    pl.BlockSpec((1, q_width), lambda b, j: (0, 0))] + in_specs
        args = [sink, gain] + args
    return pl.pallas_call(
        functools.partial(_band_attn_kernel, m=m, nk=nk, n=n, sub=sub, qb=qb, kv_of=kv_of, variants=variants,
                          with_sink=with_sink),
        grid=(Bd // sub, nb // qb),
        in_specs=in_specs,
        out_specs=pl.BlockSpec((sub, out_tiles, qb * m, LANES), lambda b, j: (b, 0, j, 0)),
        out_shape=jax.ShapeDtypeStruct((Bd, out_tiles, n, LANES), F32),
        compiler_params=_cparams("parallel", "arbitrary"),
        name="band_attn_sink" if with_sink else f"band_attn_n{n}",
    )(*args)


def _merge_rows(ya_ref, o1_ref, o4_ref, o16_ref, x_ref, gin_ref, bin_ref, gb_ref, w_ref, g1_ref, b1_ref,
                s16_scr, x_scr):
    n4 = TM_MERGE // MERGE_STRIDE
    n16 = TM_MERGE // 16
    h0_rows = _layer_norm(x_ref[...], gin_ref[...], bin_ref[...])
    for c in range(D_MODEL // LANES):
        x_scr[c] = h0_rows[:, c * LANES:(c + 1) * LANES]
    for r16 in range(16):
        for c in range(B_OUT // LANES):
            s16_scr[c, pl.ds((r16 % 4) * n4 + r16 // 4, n16, stride=4), :] = o16_ref[0, r16, c]

    def natural(ref, c):
        return jnp.concatenate([ref[0, c, pl.ds(r, n4, stride=MERGE_STRIDE), :]
                                for r in range(MERGE_STRIDE)], axis=0)

    def stride4(c):
        return jnp.concatenate([o4_ref[0, r, c] for r in range(MERGE_STRIDE)], axis=0)

    def stride16(c):
        return s16_scr[c]

    n_val = B_WIDTH // LANES
    lses = (natural(o1_ref, n_val), stride4(n_val), stride16(n_val))
    mx = jnp.maximum(jnp.maximum(lses[0], lses[1]), lses[2])
    ex = [jnp.exp(l - mx) for l in lses]
    inv = 1.0 / (ex[0] + ex[1] + ex[2])
    head = lax.broadcasted_iota(jnp.int32, (LANES, B_WIDTH), 0)
    col = lax.broadcasted_iota(jnp.int32, (LANES, B_WIDTH), 1)
    spread = jnp.where(col // HEAD_DIM == head, 1.0, 0.0).astype(BF16)
    wide = []
    for e in ex:
        w = e * inv
        w_hi = w.astype(BF16)
        w_lo = (w - w_hi.astype(F32)).astype(BF16)
        wide.append(jnp.dot(w_hi, spread, preferred_element_type=F32)
                    + jnp.dot(w_lo, spread, preferred_element_type=F32))
    pieces = []
    for c in range(n_val):
        cs = slice(c * LANES, (c + 1) * LANES)
        outs = (natural(o1_ref, c), stride4(c), stride16(c))
        pieces.append(wide[0][:, cs] * outs[0] + wide[1][:, cs] * outs[1] + wide[2][:, cs] * outs[2])
    yb = _rms_norm(jnp.concatenate(pieces, axis=1), gb_ref[...])
    ya = jnp.concatenate([natural(ya_ref, c) for c in range(A_WIDTH // LANES)], axis=1)
    y = jnp.concatenate([ya.astype(BF16), yb.astype(BF16)], axis=1)
    mix = jnp.dot(y, w_ref[...], preferred_element_type=F32)
    h0 = jnp.concatenate(
        [jnp.concatenate([x_scr[c, pl.ds(r, n4, stride=MERGE_STRIDE), :] for r in range(MERGE_STRIDE)], axis=0)
         for c in range(D_MODEL // LANES)], axis=1)
    return _layer_norm(ALPHA * h0 + mix, g1_ref[...], b1_ref[...])


def _token_of_row(p):
    n4 = TM_MERGE // MERGE_STRIDE
    tile, rest = p // TM_MERGE, p % TM_MERGE
    return tile * TM_MERGE + (rest % n4) * MERGE_STRIDE + rest // n4


def _mem_kv_kernel(mem_ref, w_ref, k_ref, v_ref):
    kv = jnp.dot(mem_ref[0].astype(BF16), w_ref[...], preferred_element_type=F32)
    k_ref[0] = kv[:, :D_MODEL].astype(BF16)
    v_ref[0] = kv[:, D_MODEL:].astype(BF16)


def _mem_kv(mem, xkv_b):
    B = mem.shape[0]
    blk = pl.BlockSpec((1, MEM_LEN, D_MODEL), lambda b: (b, 0, 0))
    return pl.pallas_call(
        _mem_kv_kernel,
        grid=(B,),
        in_specs=[blk, pl.BlockSpec((D_MODEL, 2 * D_MODEL), lambda b: (0, 0))],
        out_specs=[blk, blk],
        out_shape=[jax.ShapeDtypeStruct((B, MEM_LEN, D_MODEL), BF16)] * 2,
        compiler_params=_cparams("parallel"),
        name="mem_kv_proj",
    )(mem, xkv_b)


def _route(logits):
    rows = logits.shape[0]
    lane = lax.broadcasted_iota(jnp.int32, (rows, LANES), 1).astype(F32)
    big = float(LANES)
    ninf = -jnp.inf
    gl = jnp.where(lane < N_GROUPS, logits, ninf)
    gmax = jnp.max(gl, axis=-1, keepdims=True)
    gidx = jnp.min(jnp.where(gl == gmax, lane, big), axis=-1, keepdims=True)
    g_p = 1.0 / jnp.sum(jnp.exp(gl - gmax), axis=-1, keepdims=True)
    lo_lane = N_GROUPS + EXPERTS_PER_GROUP * gidx
    el = jnp.where((lane >= lo_lane) & (lane < lo_lane + EXPERTS_PER_GROUP), logits, ninf)
    v1 = jnp.max(el, axis=-1, keepdims=True)
    i1 = jnp.min(jnp.where(el == v1, lane, big), axis=-1, keepdims=True)
    el2 = jnp.where(lane == i1, ninf, el)
    v2 = jnp.max(el2, axis=-1, keepdims=True)
    i2 = jnp.min(jnp.where(el2 == v2, lane, big), axis=-1, keepdims=True)
    t = jnp.exp(v2 - v1)
    w1 = g_p / (1.0 + t)
    w2 = g_p * t / (1.0 + t)
    a = jnp.minimum(i1, i2) - lo_lane
    b = jnp.maximum(i1, i2) - lo_lane
    pair = a * (2 * EXPERTS_PER_GROUP - 1 - a) * 0.5 + (b - a - 1.0)
    cls = gidx * PAIRS_PER_GROUP + pair
    w_lo = jnp.where(i1 < i2, w1, w2)
    w_hi = jnp.where(i1 < i2, w2, w1)
    return jnp.where(lane == 0, cls, jnp.where(lane == 1, w_lo, jnp.where(lane == 2, w_hi, 0.0)))


def _merge_xattn_kernel(ya_ref, o1_ref, o4_ref, o16_ref, x_ref, gin_ref, bin_ref, gb_ref, w_ref, g1_ref, b1_ref,
                        k_ref, v_ref, wq_ref, wo_ref, g2_ref, b2_ref, wr_ref, br_ref, xr_ref, s16_scr, x_scr):
    h1 = _merge_rows(ya_ref, o1_ref, o4_ref, o16_ref, x_ref, gin_ref, bin_ref, gb_ref, w_ref, g1_ref, b1_ref,
                     s16_scr, x_scr)
    q = jnp.dot(h1.astype(BF16), wq_ref[...], preferred_element_type=F32).astype(BF16)
    outs = []
    for hd in range(X_HEADS):
        sl = slice(hd * X_HEAD_DIM, (hd + 1) * X_HEAD_DIM)
        s = lax.dot_general(q[:, sl], k_ref[0, :, sl], (((1,), (1,)), ((), ())), preferred_element_type=F32)
        m = jnp.max(s, axis=-1, keepdims=True)
        p = jnp.exp2(s - m)
        l = jnp.sum(p, axis=-1, keepdims=True)
        o = jnp.dot(p.astype(BF16), v_ref[0, :, sl], preferred_element_type=F32) / l
        outs.append(o.astype(BF16))
    xa = jnp.dot(jnp.concatenate(outs, axis=1), wo_ref[...], preferred_element_type=F32)
    h2 = _layer_norm(ALPHA * h1 + xa, g2_ref[...], b2_ref[...])
    h_hi = h2.astype(BF16)
    h_lo = (h2 - h_hi.astype(F32)).astype(BF16)
    t_hi = jnp.dot(h_hi, wr_ref[...], preferred_element_type=F32)
    t_lo = jnp.dot(h_lo, wr_ref[...], preferred_element_type=F32)
    logits = (t_hi[:, :LANES] + t_hi[:, LANES:]) + (t_lo[:, :LANES] + t_lo[:, LANES:]) + br_ref[...]
    xr_ref[:, :D_MODEL] = h2
    xr_ref[:, D_MODEL:] = _route(logits)


def _merge_cross_attention_route(ya, o1, o4, o16, x2, ln_in_g, ln_in_b, gain_b, w_out_b, ln1_g, ln1_b,
                                 k, v, xq_b, xo_b, ln2_g, ln2_b, wr_split, br):
    T = x2.shape[0]
    tiles_per_seq = SEQ // TM_MERGE
    row = lambda i: (i, 0)
    const = lambda i: (0, 0)
    nat = lambda i: (i // tiles_per_seq, 0, i % tiles_per_seq, 0)
    deint = lambda i: (i // tiles_per_seq, 0, 0, i % tiles_per_seq, 0)
    vec_d = pl.BlockSpec((1, D_MODEL), const)
    mat_d = pl.BlockSpec((D_MODEL, D_MODEL), const)
    kv_blk = pl.BlockSpec((1, MEM_LEN, D_MODEL), lambda i: (i // tiles_per_seq, 0, 0))
    n_b = B_OUT // LANES
    return pl.pallas_call(
        _merge_xattn_kernel,
        grid=(T // TM_MERGE,),
        in_specs=[
            pl.BlockSpec((1, A_WIDTH // LANES, TM_MERGE, LANES), nat),
            pl.BlockSpec((1, n_b, TM_MERGE, LANES), nat),
            pl.BlockSpec((1, 4, n_b, TM_MERGE // 4, LANES), deint),
            pl.BlockSpec((1, 16, n_b, TM_MERGE // 16, LANES), deint),
            pl.BlockSpec((TM_MERGE, D_MODEL), row), vec_d, vec_d,
            pl.BlockSpec((1, B_WIDTH), const), mat_d, vec_d, vec_d,
            kv_blk, kv_blk, mat_d, mat_d, vec_d, vec_d,
            pl.BlockSpec((D_MODEL, 2 * LANES), const),
            pl.BlockSpec((1, LANES), const),
        ],
        out_specs=pl.BlockSpec((TM_MERGE, XR_WIDTH), row),
        out_shape=jax.ShapeDtypeStruct((T, XR_WIDTH), F32),
        scratch_shapes=[pltpu.VMEM((n_b, TM_MERGE, LANES), F32),
                        pltpu.VMEM((D_MODEL // LANES, TM_MERGE, LANES), F32)],
        compiler_params=_cparams("parallel"),
        name="merge_xattn_route",
    )(ya, o1, o4, o16, x2, ln_in_g, ln_in_b, gain_b, w_out_b, ln1_g, ln1_b,
      k, v, xq_b, xo_b, ln2_g, ln2_b, wr_split, br)


def _expert_kernel(src_ref, dst_ref, cnt_ref, elo_ref, ehi_ref, ntile_ref, x_hbm,
                   wg_lo, wu_lo, wd_lo, wg_hi, wu_hi, wd_hi, g3_ref, b3_ref, o_hbm,
                   xbuf, obuf, gsem, ssem):
    j = pl.program_id(0)
    n_tiles = ntile_ref[0]
    slot = j % 2

    def row_copy(t, i, s, gather):
        row = pl.ds(i, 1)
        if gather:
            return pltpu.make_async_copy(x_hbm.at[pl.ds(src_ref[t * TM_MOE + i], 1)], xbuf.at[s, row], gsem.at[s])
        return pltpu.make_async_copy(obuf.at[s, row], o_hbm.at[pl.ds(dst_ref[t * TM_MOE + i], 1)], ssem.at[s])

    def n_rows(t, gather):
        cnt = cnt_ref[t]
        return (cnt + ROW_GROUP - 1) // ROW_GROUP * ROW_GROUP if gather else cnt

    def start_rows(t, s, gather):
        cnt = cnt_ref[t]
        n_groups = (cnt + ROW_GROUP - 1) // ROW_GROUP if gather else cnt // ROW_GROUP

        def group(g, c):
            base = pl.multiple_of(g * ROW_GROUP, ROW_GROUP)
            for r in range(ROW_GROUP):
                row_copy(t, base + r, s, gather).start(priority=1 if gather else r % 2)
            return c
        lax.fori_loop(0, n_groups, group, 0)
        if not gather:
            def single(i, c):
                row_copy(t, i, s, gather).start(priority=1)
                return c
            lax.fori_loop(n_groups * ROW_GROUP, cnt, single, 0)

    def wait_rows(t, s, gather):
        cnt = n_rows(t, gather)
        for bit in range(TM_MOE.bit_length()):
            rows = pl.ds(0, 1 << bit)

            @pl.when((cnt >> bit) & 1 == 1)
            def _():
                if gather:
                    pltpu.make_async_copy(x_hbm.at[rows], xbuf.at[s, rows], gsem.at[s]).wait()
                else:
                    pltpu.make_async_copy(obuf.at[s, rows], o_hbm.at[rows], ssem.at[s]).wait()

    @pl.when(j == 0)
    def _():
        xbuf[...] = jnp.zeros_like(xbuf)
        start_rows(0, 0, True)

    @pl.when(j < n_tiles)
    def _():
        @pl.when(j + 1 < n_tiles)
        def _():
            start_rows(j + 1, 1 - slot, True)

        wait_rows(j, slot, True)
        x = xbuf[slot, :, :D_MODEL]
        xb = x.astype(BF16)
        y = jnp.zeros_like(x)
        for e, (wg, wu, wd) in enumerate(((wg_lo, wu_lo, wd_lo), (wg_hi, wu_hi, wd_hi))):
            gate = xbuf[slot, :, D_MODEL + 1 + e:D_MODEL + 2 + e]
            a = jnp.dot(xb, wg[0], preferred_element_type=F32)
            u = jnp.dot(xb, wu[0], preferred_element_type=F32)
            hid = a * jax.nn.sigmoid(a) * u
            y = y + jnp.dot((gate * hid).astype(BF16), wd[0], preferred_element_type=F32)
        out = _layer_norm(ALPHA * x + y, g3_ref[...], b3_ref[...])

        @pl.when(j >= 2)
        def _():
            wait_rows(j - 2, slot, False)

        obuf[slot] = out
        start_rows(j, slot, False)

        @pl.when(j == n_tiles - 1)
        def _():
            @pl.when(j >= 1)
            def _():
                wait_rows(j - 1, 1 - slot, False)

            wait_rows(j, slot, False)


def _expert_mlp(src_rows, dst_rows, tile_cnt, tile_elo, tile_ehi, n_tiles, xr, wg, wu, wd, ln_g, ln_b):
    n_tiles_max = tile_cnt.shape[0]
    const = lambda j, src, dst, cnt, elo, ehi, nt: (0, 0)
    lo = lambda j, src, dst, cnt, elo, ehi, nt: (elo[j], 0, 0)
    hi = lambda j, src, dst, cnt, elo, ehi, nt: (ehi[j], 0, 0)
    up = (1, D_MODEL, D_EXPERT)
    down = (1, D_EXPERT, D_MODEL)
    any_spec = pl.BlockSpec(memory_space=pl.ANY)
    return pl.pallas_call(
        _expert_kernel,
        grid_spec=pltpu.PrefetchScalarGridSpec(
            num_scalar_prefetch=6,
            grid=(n_tiles_max,),
            in_specs=[
                any_spec,
                pl.BlockSpec(up, lo), pl.BlockSpec(up, lo), pl.BlockSpec(down, lo),
                pl.BlockSpec(up, hi), pl.BlockSpec(up, hi), pl.BlockSpec(down, hi),
                pl.BlockSpec((1, D_MODEL), const),
                pl.BlockSpec((1, D_MODEL), const),
            ],
            out_specs=any_spec,
            scratch_shapes=[
                pltpu.VMEM((2, TM_MOE, XR_WIDTH), F32),
                pltpu.VMEM((2, TM_MOE, D_MODEL), F32),
                pltpu.SemaphoreType.DMA((2,)),
                pltpu.SemaphoreType.DMA((2,)),
            ],
        ),
        out_shape=jax.ShapeDtypeStruct((xr.shape[0], D_MODEL), F32),
        compiler_params=_cparams("arbitrary"),
        name="moe_experts",
    )(src_rows, dst_rows, tile_cnt, tile_elo, tile_ehi, n_tiles, xr, wg, wu, wd, wg, wu, wd, ln_g, ln_b)


def _class_experts():
    lo, hi = [], []
    for g in range(N_GROUPS):
        for a in range(EXPERTS_PER_GROUP):
            for b in range(a + 1, EXPERTS_PER_GROUP):
                lo.append(g * EXPERTS_PER_GROUP + a)
                hi.append(g * EXPERTS_PER_GROUP + b)
    return np.asarray(lo, np.int32), np.asarray(hi, np.int32)


def _plan_kernel(route_ref, pos_ref, counts_ref, tri_scr, run_scr, start_scr):
    phase, i = pl.program_id(0), pl.program_id(1)
    lane = lax.broadcasted_iota(jnp.int32, (TM_PLAN, LANES), 1)
    onehot = lane.astype(F32) == route_ref[:, 0:1]
    onehot_f = jnp.where(onehot, 1.0, 0.0)

    @pl.when((phase == 0) & (i == 0))
    def _():
        run_scr[...] = jnp.zeros_like(run_scr)
        r = lax.broadcasted_iota(jnp.int32, (TM_PLAN, TM_PLAN), 0)
        c = lax.broadcasted_iota(jnp.int32, (TM_PLAN, TM_PLAN), 1)
        tri_scr[...] = jnp.where(c < r, 1.0, 0.0).astype(BF16)

    @pl.when((phase == 1) & (i == 0))
    def _():
        counts = run_scr[...]
        counts_ref[...] = counts
        tiles = jnp.floor((counts + (TM_MOE - 1)) * (1.0 / TM_MOE))
        lane_row = lax.broadcasted_iota(jnp.int32, (1, LANES), 1)
        scan = tiles
        shift = 1
        while shift < LANES:
            scan = scan + jnp.where(lane_row >= shift, pltpu.roll(scan, shift, axis=1), 0.0)
            shift *= 2
        start_scr[...] = (scan - tiles) * TM_MOE
        run_scr[...] = jnp.zeros_like(run_scr)

    @pl.when(phase == 1)
    def _():
        before = jnp.dot(tri_scr[...], onehot_f.astype(BF16), preferred_element_type=F32)
        pos_col = jnp.sum(onehot_f * (before + run_scr[...] + start_scr[...]), axis=1, keepdims=True)
        eye = (lax.broadcasted_iota(jnp.int32, (LANES, LANES), 0)
               == lax.broadcasted_iota(jnp.int32, (LANES, LANES), 1))
        for r in range(TM_PLAN // LANES):
            row = jnp.sum(jnp.where(eye, pos_col[r * LANES:(r + 1) * LANES], 0.0), axis=0, keepdims=True)
            pos_ref[r:r + 1, :] = row.astype(jnp.int32)

    run_scr[...] = run_scr[...] + jnp.sum(onehot_f, axis=0, keepdims=True)


def _plan_positions(xr):
    T = xr.shape[0]
    n_steps = T // TM_PLAN
    pos, counts = pl.pallas_call(
        _plan_kernel,
        grid=(2, n_steps),
        in_specs=[pl.BlockSpec((TM_PLAN, LANES), lambda p, i: (i, D_MODEL // LANES))],
        out_specs=[pl.BlockSpec((TM_PLAN // LANES, LANES), lambda p, i: (i * p, 0)),
                   pl.BlockSpec((1, LANES), lambda p, i: (0, 0))],
        out_shape=[jax.ShapeDtypeStruct((T // LANES, LANES), jnp.int32),
                   jax.ShapeDtypeStruct((1, LANES), F32)],
        scratch_shapes=[pltpu.VMEM((TM_PLAN, TM_PLAN), BF16),
                        pltpu.VMEM((1, LANES), F32),
                        pltpu.VMEM((1, LANES), F32)],
        compiler_params=_cparams("arbitrary", "arbitrary"),
        name="moe_plan",
    )(xr)
    return pos.reshape(T), counts[0, :N_CLASSES].astype(jnp.int32)


def _moe_plan(xr, n_tiles_max):
    T = xr.shape[0]
    pos, counts = _plan_positions(xr)
    tiles_per_class = (counts + TM_MOE - 1) // TM_MOE
    tile_end = jnp.cumsum(tiles_per_class)
    tile_start = tile_end - tiles_per_class
    n_tiles = tile_end[-1]
    tok = jnp.arange(T, dtype=jnp.int32)
    src_rows = jnp.zeros((n_tiles_max * TM_MOE,), jnp.int32).at[pos].set(tok)
    tile_ids = jnp.arange(n_tiles_max, dtype=jnp.int32)
    used = jnp.minimum(tile_ids, n_tiles - 1)
    tile_cls = jnp.sum((tile_end[None, :] <= used[:, None]).astype(jnp.int32), axis=1)
    tile_cls = jnp.minimum(tile_cls, N_CLASSES - 1)
    cls_lo, cls_hi = _class_experts()
    tile_elo = jnp.asarray(cls_lo)[tile_cls]
    tile_ehi = jnp.asarray(cls_hi)[tile_cls]
    within = tile_ids - tile_start[tile_cls]
    tile_cnt = jnp.clip(counts[tile_cls] - within * TM_MOE, 0, TM_MOE)
    tile_cnt = jnp.where(tile_ids < n_tiles, tile_cnt, 0).astype(jnp.int32)
    return src_rows, tile_elo, tile_ehi, tile_cnt, n_tiles.reshape(1).astype(jnp.int32)


def _vec(a):
    return a.reshape(1, -1).astype(F32)


def _mixer_and_cross_attention(x, mem, ln_in_g, ln_in_b, w_in, rel_bias, sink_a, norm_a_g, norm_b_g, w_out,
                               ln1_g, ln1_b, xq, xkv, xo, ln2_g, ln2_b, w_group, b_group, w_router, b_router):
    B, S, D = x.shape
    assert S == SEQ and D == D_MODEL and mem.shape[1:] == (MEM_LEN, D_MODEL)
    T = B * S
    vec = _vec

    w = w_in[0]
    edges = np.cumsum((0, A_WIDTH, A_KV_HEADS * HEAD_DIM, A_KV_HEADS * HEAD_DIM, B_WIDTH, B_WIDTH, B_WIDTH))
    qa, ka, va, qb, kb, vb = [w[:, a:b] for a, b in zip(edges[:-1], edges[1:])]
    dup = lambda t: jnp.repeat(t.reshape(D, A_KV_HEADS, 1, HEAD_DIM), 2, axis=2).reshape(D, A_KV_TILES * LANES)
    scale = HEAD_DIM ** -0.5 * LOG2E
    w_in_b = jnp.concatenate([qa * scale, dup(ka), dup(va), qb * scale, kb, vb], axis=1).astype(BF16)
    w_out_b = w_out[0].astype(BF16)
    xq_b = (xq[0] * (X_HEAD_DIM ** -0.5 * LOG2E)).astype(BF16)
    xkv_b = xkv[0].astype(BF16)
    xo_b = xo[0].astype(BF16)
    wr = jnp.concatenate([w_group[0], w_router[0]], axis=1).astype(F32)
    wr = jnp.pad(wr, ((0, 0), (0, LANES - wr.shape[1])))
    wr_hi = wr.astype(BF16)
    wr_split = jnp.concatenate([wr_hi, (wr - wr_hi.astype(F32)).astype(BF16)], axis=1)
    br = jnp.concatenate([b_group[0], b_router[0]]).astype(F32)
    br = jnp.pad(br, (0, LANES - br.shape[0])).reshape(1, LANES)

    x2 = x.reshape(T, D)
    qkv_a, qkv_b1, qkv_b4, qkv_b16 = _input_projection(x2, vec(ln_in_g), vec(ln_in_b), w_in_b, B)

    nk_a = TM_ATT + 2 * A_HALF_WIN
    bias_a = _band_bias(rel_bias[:, :A_HEADS], A_HALF_WIN, 1, TM_ATT, nk_a, S)
    ya = _band_attention(qkv_a.reshape(B, S, QKV_A), bias_a, nk=nk_a, sub=1, kv_of=(0, 0, 1, 1),
                         q_width=A_WIDTH, kv_width=A_KV_TILES * LANES, out_tiles=A_WIDTH // LANES,
                         sink=sink_a[0].astype(F32) * LOG2E, gain=vec(norm_a_g[0]))
    branch_out = []
    for (win, dil), qkv in zip(B_BRANCHES, (qkv_b1.reshape(B, S, QKV_B),
                                             qkv_b4.reshape(B * 4, S // 4, QKV_B),
                                             qkv_b16.reshape(B * 16, S // 16, QKV_B))):
        half = (win // 2) // dil
        n = S // dil
        nk = min(TM_ATT + 2 * half, n)
        bias_b = _band_bias(rel_bias[:, A_HEADS:], half, dil, TM_ATT, nk, n)
        branch_out.append(_band_attention(qkv, bias_b, nk=nk, sub=max(1, TM_ATT * 8 // n), kv_of=(0, 1, 2, 3),
                                          q_width=B_WIDTH, kv_width=B_WIDTH, out_tiles=B_OUT // LANES))
    o1 = branch_out[0]
    o4 = branch_out[1].reshape(B, 4, B_OUT // LANES, S // 4, LANES)
    o16 = branch_out[2].reshape(B, 16, B_OUT // LANES, S // 16, LANES)

    k_mem, v_mem = _mem_kv(mem, xkv_b)
    return _merge_cross_attention_route(
        ya, o1, o4, o16, x2, vec(ln_in_g), vec(ln_in_b), vec(norm_b_g[0]), w_out_b, vec(ln1_g[0]), vec(ln1_b[0]),
        k_mem, v_mem, xq_b, xo_b, vec(ln2_g[0]), vec(ln2_b[0]), wr_split, br)


def _moe(xr, w_gate, w_up, w_down, ln3_g, ln3_b):
    T = xr.shape[0]
    wg_b = w_gate[0].reshape(N_EXPERTS, D_MODEL, D_EXPERT).astype(BF16)
    wu_b = w_up[0].reshape(N_EXPERTS, D_MODEL, D_EXPERT).astype(BF16)
    wd_b = w_down[0].reshape(N_EXPERTS, D_EXPERT, D_MODEL).astype(BF16)
    n_tiles_max = T // TM_MOE + N_CLASSES
    src_rows, tile_elo, tile_ehi, tile_cnt, n_tiles = _moe_plan(xr, n_tiles_max)
    return _expert_mlp(src_rows, _token_of_row(src_rows), tile_cnt, tile_elo, tile_ehi, n_tiles, xr, wg_b, wu_b, wd_b,
                       _vec(ln3_g[0]), _vec(ln3_b[0]))


def kernel(x, mem, ln_in_g, ln_in_b, w_in, rel_bias, sink_a, norm_a_g, norm_b_g, w_out,
           ln1_g, ln1_b, xq, xkv, xo, ln2_g, ln2_b, w_group, b_group, w_router, b_router,
           w_gate, w_up, w_down, ln3_g, ln3_b):
    xr = _mixer_and_cross_attention(
        x, mem, ln_in_g, ln_in_b, w_in, rel_bias, sink_a, norm_a_g, norm_b_g, w_out,
        ln1_g, ln1_b, xq, xkv, xo, ln2_g, ln2_b, w_group, b_group, w_router, b_router)
    return _moe(xr, w_gate, w_up, w_down, ln3_g, ln3_b).reshape(x.shape)
```

```python
import functools

import numpy as np
import jax
import jax.numpy as jnp
from jax import lax
from jax.experimental import pallas as pl
from jax.experimental.pallas import tpu as pltpu

F32 = jnp.float32
BF16 = jnp.bfloat16

D_MODEL = 1024
SEQ = 2048
MEM_LEN = 256
HEAD_DIM = 64
A_HEADS = 8
A_KV_HEADS = 2
A_HALF_WIN = 128
B_HEADS = 8
B_BRANCHES = ((128, 1), (512, 4), (2048, 16))
N_BUCKETS = 32
MAX_DISTANCE = 1024
X_HEADS = 4
X_HEAD_DIM = D_MODEL // X_HEADS
N_GROUPS = 4
EXPERTS_PER_GROUP = 8
N_EXPERTS = N_GROUPS * EXPERTS_PER_GROUP
D_EXPERT = 512
DEPTH = 1
ALPHA = (2.0 * DEPTH) ** 0.25
LN_EPS = 1e-5
NEG = -1e30
LOG2E = 1.4426950408889634

LANES = 128
XR_WIDTH = D_MODEL + LANES
A_WIDTH = A_HEADS * HEAD_DIM
B_WIDTH = B_HEADS * HEAD_DIM
A_KV_TILES = A_KV_HEADS
QKV_A = A_WIDTH + 2 * A_KV_TILES * LANES
QKV_B = 3 * B_WIDTH
B_OUT = B_WIDTH + LANES
MERGE_STRIDE = 4

PAIRS_PER_GROUP = EXPERTS_PER_GROUP * (EXPERTS_PER_GROUP - 1) // 2
N_CLASSES = N_GROUPS * PAIRS_PER_GROUP

TM_IN = 512
TM_ATT = 128
ATT_BLOCKS = 4
TM_MERGE = 512
TM_MOE = 128
ROW_GROUP = 8
TM_PLAN = 1024
VMEM_LIMIT = 56 * 1024 * 1024


def _cparams(*sem):
    return pltpu.CompilerParams(dimension_semantics=sem, vmem_limit_bytes=VMEM_LIMIT)


def _layer_norm(x, g, b):
    mu = jnp.mean(x, axis=-1, keepdims=True)
    xc = x - mu
    var = jnp.mean(xc * xc, axis=-1, keepdims=True)
    return xc * lax.rsqrt(var + LN_EPS) * g + b


def _rms_norm(x, g):
    return x * lax.rsqrt(jnp.mean(x * x, axis=-1, keepdims=True) + LN_EPS) * g


def _t5_bucket(rel):
    nb = N_BUCKETS // 2
    max_exact = nb // 2
    ret = (rel > 0).astype(np.int32) * nb
    n = np.abs(rel)
    n_safe = np.maximum(n, 1).astype(np.float64)
    large = max_exact + (np.log(n_safe / max_exact) / np.log(MAX_DISTANCE / max_exact)
                         * (nb - max_exact)).astype(np.int32)
    large = np.minimum(large, nb - 1)
    return (ret + np.where(n < max_exact, n, large)).astype(np.int32)


def _inproj_kernel(x_ref, g_ref, b_ref, w_ref, qa_ref, qb1_ref, qb4_ref, qb16_ref, pb_scr, p4_scr):
    h = _layer_norm(x_ref[...], g_ref[...], b_ref[...])
    proj = jnp.dot(h.astype(BF16), w_ref[...], preferred_element_type=F32)
    qa_ref[...] = proj[:, :QKV_A].astype(BF16)
    pb = proj[:, QKV_A:]
    qb1_ref[...] = pb.astype(BF16)
    n4 = TM_IN // 4
    for c in range(QKV_B // LANES):
        cs = slice(c * LANES, (c + 1) * LANES)
        pb_scr[c] = pb[:, cs]
        for r4 in range(4):
            rows = pb_scr[c, pl.ds(r4, n4, stride=4), :]
            qb4_ref[0, r4, :, cs] = rows.astype(BF16)
            p4_scr[c, r4 * n4:(r4 + 1) * n4, :] = rows
        for r16 in range(16):
            rows = p4_scr[c, pl.ds((r16 % 4) * n4 + r16 // 4, TM_IN // 16, stride=4), :]
            qb16_ref[0, r16, :, cs] = rows.astype(BF16)


def _input_projection(x2, ln_g, ln_b, w_in_b, batch):
    T = x2.shape[0]
    tiles_per_seq = SEQ // TM_IN
    row = lambda i: (i, 0)
    const = lambda i: (0, 0)
    deint = lambda i: (i // tiles_per_seq, 0, i % tiles_per_seq, 0)
    return pl.pallas_call(
        _inproj_kernel,
        grid=(T // TM_IN,),
        in_specs=[
            pl.BlockSpec((TM_IN, D_MODEL), row),
            pl.BlockSpec((1, D_MODEL), const),
            pl.BlockSpec((1, D_MODEL), const),
            pl.BlockSpec((D_MODEL, QKV_A + QKV_B), const),
        ],
        out_specs=[
            pl.BlockSpec((TM_IN, QKV_A), row),
            pl.BlockSpec((TM_IN, QKV_B), row),
            pl.BlockSpec((1, 4, TM_IN // 4, QKV_B), deint),
            pl.BlockSpec((1, 16, TM_IN // 16, QKV_B), deint),
        ],
        out_shape=[
            jax.ShapeDtypeStruct((T, QKV_A), BF16),
            jax.ShapeDtypeStruct((T, QKV_B), BF16),
            jax.ShapeDtypeStruct((batch, 4, SEQ // 4, QKV_B), BF16),
            jax.ShapeDtypeStruct((batch, 16, SEQ // 16, QKV_B), BF16),
        ],
        scratch_shapes=[pltpu.VMEM((QKV_B // LANES, TM_IN, LANES), F32)] * 2,
        compiler_params=_cparams("parallel"),
        name="ln_in_proj",
    )(x2, ln_g, ln_b, w_in_b)


def _band_attn_kernel(*refs, m, nk, n, sub, qb, kv_of, variants, with_sink):
    if with_sink:
        sink_ref, gain_ref = refs[0], refs[1]
        refs = refs[2:]
    q_ref, k_ref, v_ref, bias_ref, o_ref = refs
    nb = n // m
    first, middle, last = variants
    starts, bias_var = [], []
    for b in range(qb):
        if nk == n:
            starts.append(0)
            bias_var.append(first)
        else:
            jj = pl.program_id(1) * qb + b
            starts.append(pl.multiple_of(jnp.clip(jj * m - (nk - m) // 2, 0, n - nk), HEAD_DIM))
            bias_var.append(jnp.where(jj == 0, first, jnp.where(jj == nb - 1, last, middle)))
    lane_row = lax.broadcasted_iota(jnp.int32, (1, LANES), 1)
    keep_lo = jnp.where(lane_row < HEAD_DIM, 1.0, 0.0).astype(BF16)
    keep_hi = jnp.where(lane_row < HEAD_DIM, 0.0, 1.0).astype(BF16)
    lane = lax.broadcasted_iota(jnp.int32, (m, LANES), 1)
    n_q = len(kv_of)
    units = [(s, b, qt) for s in range(sub) for b in range(qb) for qt in range(n_q)]

    scores = []
    for s, b, qt in units:
        t = kv_of[qt]
        k_t = k_ref[s, pl.ds(starts[b], nk), t * LANES:(t + 1) * LANES]
        q2 = q_ref[s, b * m:(b + 1) * m, qt * LANES:(qt + 1) * LANES]
        lhs = jnp.concatenate([q2 * keep_lo, q2 * keep_hi], axis=0)
        sc = lax.dot_general(lhs, k_t, (((1,), (1,)), ((), ())), preferred_element_type=F32)
        scores.append(sc + bias_ref[bias_var[b], qt])

    probs, denom, row_max = [], [], []
    for (s, b, qt), sc in zip(units, scores):
        mx = jnp.max(sc, axis=-1, keepdims=True)
        p = jnp.exp2(sc - mx)
        l = jnp.sum(p, axis=-1, keepdims=True)
        halves = []
        for h in (0, 1):
            l_h = l[h * m:(h + 1) * m]
            if with_sink:
                l_h = l_h + jnp.exp2(sink_ref[2 * qt + h] - mx[h * m:(h + 1) * m])
            halves.append(l_h)
        probs.append(p.astype(BF16))
        denom.append(halves)
        row_max.append(mx)

    for s, b in [(s, b) for s in range(sub) for b in range(qb)]:
        rows = slice(b * m, (b + 1) * m)
        pairs = []
        stat_tile = jnp.zeros((m, LANES), F32)
        for qt in range(n_q):
            u = (s * qb + b) * n_q + qt
            t = kv_of[qt]
            v_t = v_ref[s, pl.ds(starts[b], nk), t * LANES:(t + 1) * LANES]
            o = jnp.dot(probs[u], v_t, preferred_element_type=F32)
            if with_sink:
                pairs.append(jnp.where(lane < HEAD_DIM, o[:m] * (1.0 / denom[u][0]), o[m:] * (1.0 / denom[u][1])))
            else:
                pairs.append(jnp.where(lane < HEAD_DIM, o[:m], o[m:]))
                for h in (0, 1):
                    stat_tile = jnp.where(lane == 2 * qt + h, row_max[u][h * m:(h + 1) * m], stat_tile)
                    stat_tile = jnp.where(lane == B_HEADS + 2 * qt + h, denom[u][h], stat_tile)
        if with_sink:
            normed = _rms_norm(jnp.concatenate(pairs, axis=1), gain_ref[...])
            pairs = [normed[:, c * LANES:(c + 1) * LANES] for c in range(n_q)]
        else:
            o_ref[s, n_q, rows, :] = stat_tile
        for c in range(n_q):
            o_ref[s, c, rows, :] = pairs[c]


def _band_window(m, nk, n):
    nb = n // m
    starts = np.clip(np.arange(nb) * m - (nk - m) // 2, 0, n - nk)
    offs = [int(o) for o in starts - np.arange(nb) * m]
    uniq = sorted(set(offs), reverse=True)
    var = [uniq.index(o) for o in offs]
    assert all(v == var[1] for v in var[1:-1])
    return uniq, var


def _band_bias(rel_bias_h, half, dil, m, nk, n):
    uniq, _ = _band_window(m, nk, n)
    n_heads = rel_bias_h.shape[1]
    span = nk + m - 1
    out = []
    for off in uniq:
        rel = off - (m - 1) + np.arange(span)
        onehot = np.zeros((span, N_BUCKETS), np.float32)
        onehot[np.arange(span), _t5_bucket(dil * rel)] = 1.0
        table = jnp.dot(jnp.asarray(onehot), rel_bias_h.astype(F32), precision=lax.Precision.HIGHEST)
        table = jnp.where((np.abs(rel) <= half)[:, None], table, NEG)
        u = jnp.concatenate([table.T, jnp.zeros((n_heads, 1), F32)], axis=1)
        flat = jnp.tile(u, (1, m + 1))[:, m - 1:m - 1 + m * span]
        b = flat.reshape(n_heads, m, span)[:, :, :nk]
        out.append(b.reshape(n_heads // 2, 2 * m, nk))
    return jnp.stack(out) * LOG2E


def _band_attention(qkv, bias, *, nk, sub, kv_of, q_width, kv_width, out_tiles, sink=None, gain=None):
    Bd, n, _ = qkv.shape
    m = TM_ATT
    nb = n // m
    qb = min(nb, ATT_BLOCKS)
    assert nb % qb == 0 and nb * m == n and Bd % sub == 0
    _, var = _band_window(m, nk, n)
    variants = (var[0], var[min(1, nb - 1)], var[-1])
    kcol = q_width // kv_width
    with_sink = sink is not None
    in_specs = [
        pl.BlockSpec((sub, qb * m, q_width), lambda b, j: (b, j, 0)),
        pl.BlockSpec((sub, n, kv_width), lambda b, j: (b, 0, kcol)),
        pl.BlockSpec((sub, n, kv_width), lambda b, j: (b, 0, kcol + 1)),
        pl.BlockSpec(bias.shape, lambda b, j: (0, 0, 0, 0)),
    ]
    args = [qkv, qkv, qkv, bias]
    if with_sink:
        in_specs = [pl.BlockSpec(memory_space=pltpu.SMEM),
                    pl.BlockSpec((1, q_width), lambda b, j: (0, 0))] + in_specs
        args = [sink, gain] + args
    return pl.pallas_call(
        functools.partial(_band_attn_kernel, m=m, nk=nk, n=n, sub=sub, qb=qb, kv_of=kv_of, variants=variants,
                          with_sink=with_sink),
        grid=(Bd // sub, nb // qb),
        in_specs=in_specs,
        out_specs=pl.BlockSpec((sub, out_tiles, qb * m, LANES), lambda b, j: (b, 0, j, 0)),
        out_shape=jax.ShapeDtypeStruct((Bd, out_tiles, n, LANES), F32),
        compiler_params=_cparams("parallel", "arbitrary"),
        name="band_attn_sink" if with_sink else f"band_attn_n{n}",
    )(*args)


def _merge_rows(ya_ref, o1_ref, o4_ref, o16_ref, x_ref, gin_ref, bin_ref, gb_ref, w_ref, g1_ref, b1_ref,
                s16_scr, x_scr):
    n4 = TM_MERGE // MERGE_STRIDE
    n16 = TM_MERGE // 16
    h0_rows = _layer_norm(x_ref[...], gin_ref[...], bin_ref[...])
    for c in range(D_MODEL // LANES):
        x_scr[c] = h0_rows[:, c * LANES:(c + 1) * LANES]
    for r16 in range(16):
        for c in range(B_OUT // LANES):
            s16_scr[c, pl.ds((r16 % 4) * n4 + r16 // 4, n16, stride=4), :] = o16_ref[0, r16, c]

    def natural(ref, c):
        return jnp.concatenate([ref[0, c, pl.ds(r, n4, stride=MERGE_STRIDE), :]
                                for r in range(MERGE_STRIDE)], axis=0)

    def stride4(c):
        return jnp.concatenate([o4_ref[0, r, c] for r in range(MERGE_STRIDE)], axis=0)

    def stride16(c):
        return s16_scr[c]

    n_val = B_WIDTH // LANES
    stats = (natural(o1_ref, n_val), stride4(n_val), stride16(n_val))
    dens = [pltpu.roll(s, LANES - B_HEADS, axis=1) for s in stats]
    mx = jnp.maximum(jnp.maximum(stats[0], stats[1]), stats[2])
    ex = [jnp.exp2(s - mx) for s in stats]
    inv = 1.0 / (ex[0] * dens[0] + ex[1] * dens[1] + ex[2] * dens[2])
    head_lane = lax.broadcasted_iota(jnp.int32, (TM_MERGE, LANES), 1) < B_HEADS
    head = lax.broadcasted_iota(jnp.int32, (LANES, B_WIDTH), 0)
    col = lax.broadcasted_iota(jnp.int32, (LANES, B_WIDTH), 1)
    spread = jnp.where(col // HEAD_DIM == head, 1.0, 0.0).astype(BF16)
    wide = []
    for e in ex:
        w = jnp.where(head_lane, e * inv, 0.0)
        w_hi = w.astype(BF16)
        w_lo = (w - w_hi.astype(F32)).astype(BF16)
        wide.append(jnp.dot(w_hi, spread, preferred_element_type=F32)
                    + jnp.dot(w_lo, spread, preferred_element_type=F32))
    pieces = []
    for c in range(n_val):
        cs = slice(c * LANES, (c + 1) * LANES)
        outs = (natural(o1_ref, c), stride4(c), stride16(c))
        pieces.append(wide[0][:, cs] * outs[0] + wide[1][:, cs] * outs[1] + wide[2][:, cs] * outs[2])
    yb = _rms_norm(jnp.concatenate(pieces, axis=1), gb_ref[...])
    ya = jnp.concatenate([natural(ya_ref, c) for c in range(A_WIDTH // LANES)], axis=1)
    y = jnp.concatenate([ya.astype(BF16), yb.astype(BF16)], axis=1)
    mix = jnp.dot(y, w_ref[...], preferred_element_type=F32)
    h0 = jnp.concatenate(
        [jnp.concatenate([x_scr[c, pl.ds(r, n4, stride=MERGE_STRIDE), :] for r in range(MERGE_STRIDE)], axis=0)
         for c in range(D_MODEL // LANES)], axis=1)
    return _layer_norm(ALPHA * h0 + mix, g1_ref[...], b1_ref[...])


def _token_of_row(p):
    n4 = TM_MERGE // MERGE_STRIDE
    tile, rest = p // TM_MERGE, p % TM_MERGE
    return tile * TM_MERGE + (rest % n4) * MERGE_STRIDE + rest // n4


def _mem_kv_kernel(mem_ref, w_ref, k_ref, v_ref):
    kv = jnp.dot(mem_ref[0].astype(BF16), w_ref[...], preferred_element_type=F32)
    k_ref[0] = kv[:, :D_MODEL].astype(BF16)
    v_ref[0] = kv[:, D_MODEL:].astype(BF16)


def _mem_kv(mem, xkv_b):
    B = mem.shape[0]
    blk = pl.BlockSpec((1, MEM_LEN, D_MODEL), lambda b: (b, 0, 0))
    return pl.pallas_call(
        _mem_kv_kernel,
        grid=(B,),
        in_specs=[blk, pl.BlockSpec((D_MODEL, 2 * D_MODEL), lambda b: (0, 0))],
        out_specs=[blk, blk],
        out_shape=[jax.ShapeDtypeStruct((B, MEM_LEN, D_MODEL), BF16)] * 2,
        compiler_params=_cparams("parallel"),
        name="mem_kv_proj",
    )(mem, xkv_b)


def _route(logits):
    rows = logits.shape[0]
    lane = lax.broadcasted_iota(jnp.int32, (rows, LANES), 1).astype(F32)
    big = float(LANES)
    ninf = -jnp.inf
    gl = jnp.where(lane < N_GROUPS, logits, ninf)
    gmax = jnp.max(gl, axis=-1, keepdims=True)
    gidx = jnp.min(jnp.where(gl == gmax, lane, big), axis=-1, keepdims=True)
    g_p = 1.0 / jnp.sum(jnp.exp(gl - gmax), axis=-1, keepdims=True)
    lo_lane = N_GROUPS + EXPERTS_PER_GROUP * gidx
    el = jnp.where((lane >= lo_lane) & (lane < lo_lane + EXPERTS_PER_GROUP), logits, ninf)
    v1 = jnp.max(el, axis=-1, keepdims=True)
    i1 = jnp.min(jnp.where(el == v1, lane, big), axis=-1, keepdims=True)
    el2 = jnp.where(lane == i1, ninf, el)
    v2 = jnp.max(el2, axis=-1, keepdims=True)
    i2 = jnp.min(jnp.where(el2 == v2, lane, big), axis=-1, keepdims=True)
    t = jnp.exp(v2 - v1)
    w1 = g_p / (1.0 + t)
    w2 = g_p * t / (1.0 + t)
    a = jnp.minimum(i1, i2) - lo_lane
    b = jnp.maximum(i1, i2) - lo_lane
    pair = a * (2 * EXPERTS_PER_GROUP - 1 - a) * 0.5 + (b - a - 1.0)
    cls = gidx * PAIRS_PER_GROUP + pair
    w_lo = jnp.where(i1 < i2, w1, w2)
    w_hi = jnp.where(i1 < i2, w2, w1)
    return jnp.where(lane == 0, cls, jnp.where(lane == 1, w_lo, jnp.where(lane == 2, w_hi, 0.0)))


def _merge_xattn_kernel(ya_ref, o1_ref, o4_ref, o16_ref, x_ref, gin_ref, bin_ref, gb_ref, w_ref, g1_ref, b1_ref,
                        k_ref, v_ref, wq_ref, wo_ref, g2_ref, b2_ref, wr_ref, br_ref, xr_ref, s16_scr, x_scr):
    h1 = _merge_rows(ya_ref, o1_ref, o4_ref, o16_ref, x_ref, gin_ref, bin_ref, gb_ref, w_ref, g1_ref, b1_ref,
                     s16_scr, x_scr)
    q = jnp.dot(h1.astype(BF16), wq_ref[...], preferred_element_type=F32).astype(BF16)
    outs = []
    for hd in range(X_HEADS):
        sl = slice(hd * X_HEAD_DIM, (hd + 1) * X_HEAD_DIM)
        s = lax.dot_general(q[:, sl], k_ref[0, :, sl], (((1,), (1,)), ((), ())), preferred_element_type=F32)
        m = jnp.max(s, axis=-1, keepdims=True)
        p = jnp.exp2(s - m)
        l = jnp.sum(p, axis=-1, keepdims=True)
        o = jnp.dot(p.astype(BF16), v_ref[0, :, sl], preferred_element_type=F32) / l
        outs.append(o.astype(BF16))
    xa = jnp.dot(jnp.concatenate(outs, axis=1), wo_ref[...], preferred_element_type=F32)
    h2 = _layer_norm(ALPHA * h1 + xa, g2_ref[...], b2_ref[...])
    h_hi = h2.astype(BF16)
    h_lo = (h2 - h_hi.astype(F32)).astype(BF16)
    t_hi = jnp.dot(h_hi, wr_ref[...], preferred_element_type=F32)
    t_lo = jnp.dot(h_lo, wr_ref[...], preferred_element_type=F32)
    logits = (t_hi[:, :LANES] + t_hi[:, LANES:]) + (t_lo[:, :LANES] + t_lo[:, LANES:]) + br_ref[...]
    xr_ref[:, :D_MODEL] = h2
    xr_ref[:, D_MODEL:] = _route(logits)


def _merge_cross_attention_route(ya, o1, o4, o16, x2, ln_in_g, ln_in_b, gain_b, w_out_b, ln1_g, ln1_b,
                                 k, v, xq_b, xo_b, ln2_g, ln2_b, wr_split, br):
    T = x2.shape[0]
    tiles_per_seq = SEQ // TM_MERGE
    row = lambda i: (i, 0)
    const = lambda i: (0, 0)
    nat = lambda i: (i // tiles_per_seq, 0, i % tiles_per_seq, 0)
    deint = lambda i: (i // tiles_per_seq, 0, 0, i % tiles_per_seq, 0)
    vec_d = pl.BlockSpec((1, D_MODEL), const)
    mat_d = pl.BlockSpec((D_MODEL, D_MODEL), const)
    kv_blk = pl.BlockSpec((1, MEM_LEN, D_MODEL), lambda i: (i // tiles_per_seq, 0, 0))
    n_b = B_OUT // LANES
    return pl.pallas_call(
        _merge_xattn_kernel,
        grid=(T // TM_MERGE,),
        in_specs=[
            pl.BlockSpec((1, A_WIDTH // LANES, TM_MERGE, LANES), nat),
            pl.BlockSpec((1, n_b, TM_MERGE, LANES), nat),
            pl.BlockSpec((1, 4, n_b, TM_MERGE // 4, LANES), deint),
            pl.BlockSpec((1, 16, n_b, TM_MERGE // 16, LANES), deint),
            pl.BlockSpec((TM_MERGE, D_MODEL), row), vec_d, vec_d,
            pl.BlockSpec((1, B_WIDTH), const), mat_d, vec_d, vec_d,
            kv_blk, kv_blk, mat_d, mat_d, vec_d, vec_d,
            pl.BlockSpec((D_MODEL, 2 * LANES), const),
            pl.BlockSpec((1, LANES), const),
        ],
        out_specs=pl.BlockSpec((TM_MERGE, XR_WIDTH), row),
        out_shape=jax.ShapeDtypeStruct((T, XR_WIDTH), F32),
        scratch_shapes=[pltpu.VMEM((n_b, TM_MERGE, LANES), F32),
                        pltpu.VMEM((D_MODEL // LANES, TM_MERGE, LANES), F32)],
        compiler_params=_cparams("parallel"),
        name="merge_xattn_route",
    )(ya, o1, o4, o16, x2, ln_in_g, ln_in_b, gain_b, w_out_b, ln1_g, ln1_b,
      k, v, xq_b, xo_b, ln2_g, ln2_b, wr_split, br)


def _expert_kernel(src_ref, dst_ref, cnt_ref, elo_ref, ehi_ref, ntile_ref, x_hbm,
                   wg_lo, wu_lo, wd_lo, wg_hi, wu_hi, wd_hi, g3_ref, b3_ref, o_hbm,
                   xbuf, obuf, gsem, ssem):
    j = pl.program_id(0)
    n_tiles = ntile_ref[0]
    slot = j % 2

    def row_copy(t, i, s, gather):
        row = pl.ds(i, 1)
        if gather:
            return pltpu.make_async_copy(x_hbm.at[pl.ds(src_ref[t * TM_MOE + i], 1)], xbuf.at[s, row], gsem.at[s])
        return pltpu.make_async_copy(obuf.at[s, row], o_hbm.at[pl.ds(dst_ref[t * TM_MOE + i], 1)], ssem.at[s])

    def n_rows(t, gather):
        cnt = cnt_ref[t]
        return (cnt + ROW_GROUP - 1) // ROW_GROUP * ROW_GROUP if gather else cnt

    def start_rows(t, s, gather):
        cnt = cnt_ref[t]
        n_groups = (cnt + ROW_GROUP - 1) // ROW_GROUP if gather else cnt // ROW_GROUP

        def group(g, c):
            base = pl.multiple_of(g * ROW_GROUP, ROW_GROUP)
            for r in range(ROW_GROUP):
                row_copy(t, base + r, s, gather).start(priority=0 if gather else 1)
            return c
        lax.fori_loop(0, n_groups, group, 0)
        if not gather:
            def single(i, c):
                row_copy(t, i, s, gather).start(priority=1)
                return c
            lax.fori_loop(n_groups * ROW_GROUP, cnt, single, 0)

    def wait_rows(t, s, gather):
        cnt = n_rows(t, gather)
        for bit in range(TM_MOE.bit_length()):
            rows = pl.ds(0, 1 << bit)

            @pl.when((cnt >> bit) & 1 == 1)
            def _():
                if gather:
                    pltpu.make_async_copy(x_hbm.at[rows], xbuf.at[s, rows], gsem.at[s]).wait()
                else:
                    pltpu.make_async_copy(obuf.at[s, rows], o_hbm.at[rows], ssem.at[s]).wait()

    @pl.when(j == 0)
    def _():
        xbuf[...] = jnp.zeros_like(xbuf)
        start_rows(0, 0, True)

    @pl.when(j < n_tiles)
    def _():
        @pl.when(j + 1 < n_tiles)
        def _():
            start_rows(j + 1, 1 - slot, True)

        wait_rows(j, slot, True)
        x = xbuf[slot, :, :D_MODEL]
        xb = x.astype(BF16)
        y = jnp.zeros_like(x)
        for e, (wg, wu, wd) in enumerate(((wg_lo, wu_lo, wd_lo), (wg_hi, wu_hi, wd_hi))):
            gate = xbuf[slot, :, D_MODEL + 1 + e:D_MODEL + 2 + e]
            a = jnp.dot(xb, wg[0], preferred_element_type=F32)
            u = jnp.dot(xb, wu[0], preferred_element_type=F32)
            hid = a * jax.nn.sigmoid(a) * u
            y = y + jnp.dot((gate * hid).astype(BF16), wd[0], preferred_element_type=F32)
        out = _layer_norm(ALPHA * x + y, g3_ref[...], b3_ref[...])

        @pl.when(j >= 2)
        def _():
            wait_rows(j - 2, slot, False)

        obuf[slot] = out
        start_rows(j, slot, False)

        @pl.when(j == n_tiles - 1)
        def _():
            @pl.when(j >= 1)
            def _():
                wait_rows(j - 1, 1 - slot, False)

            wait_rows(j, slot, False)


def _expert_mlp(src_rows, dst_rows, tile_cnt, tile_elo, tile_ehi, n_tiles, xr, wg, wu, wd, ln_g, ln_b):
    n_tiles_max = tile_cnt.shape[0]
    const = lambda j, src, dst, cnt, elo, ehi, nt: (0, 0)
    lo = lambda j, src, dst, cnt, elo, ehi, nt: (elo[j], 0, 0)
    hi = lambda j, src, dst, cnt, elo, ehi, nt: (ehi[j], 0, 0)
    up = (1, D_MODEL, D_EXPERT)
    down = (1, D_EXPERT, D_MODEL)
    any_spec = pl.BlockSpec(memory_space=pl.ANY)
    return pl.pallas_call(
        _expert_kernel,
        grid_spec=pltpu.PrefetchScalarGridSpec(
            num_scalar_prefetch=6,
            grid=(n_tiles_max,),
            in_specs=[
                any_spec,
                pl.BlockSpec(up, lo), pl.BlockSpec(up, lo), pl.BlockSpec(down, lo),
                pl.BlockSpec(up, hi), pl.BlockSpec(up, hi), pl.BlockSpec(down, hi),
                pl.BlockSpec((1, D_MODEL), const),
                pl.BlockSpec((1, D_MODEL), const),
            ],
            out_specs=any_spec,
            scratch_shapes=[
                pltpu.VMEM((2, TM_MOE, XR_WIDTH), F32),
                pltpu.VMEM((2, TM_MOE, D_MODEL), F32),
                pltpu.SemaphoreType.DMA((2,)),
                pltpu.SemaphoreType.DMA((2,)),
            ],
        ),
        out_shape=jax.ShapeDtypeStruct((xr.shape[0], D_MODEL), F32),
        compiler_params=_cparams("arbitrary"),
        name="moe_experts",
    )(src_rows, dst_rows, tile_cnt, tile_elo, tile_ehi, n_tiles, xr, wg, wu, wd, wg, wu, wd, ln_g, ln_b)


def _class_experts():
    lo, hi = [], []
    for g in range(N_GROUPS):
        for a in range(EXPERTS_PER_GROUP):
            for b in range(a + 1, EXPERTS_PER_GROUP):
                lo.append(g * EXPERTS_PER_GROUP + a)
                hi.append(g * EXPERTS_PER_GROUP + b)
    return np.asarray(lo, np.int32), np.asarray(hi, np.int32)


def _plan_kernel(route_ref, pos_ref, counts_ref, tri_scr, run_scr, start_scr):
    phase, i = pl.program_id(0), pl.program_id(1)
    lane = lax.broadcasted_iota(jnp.int32, (TM_PLAN, LANES), 1)
    onehot = lane.astype(F32) == route_ref[:, 0:1]
    onehot_f = jnp.where(onehot, 1.0, 0.0)

    @pl.when((phase == 0) & (i == 0))
    def _():
        run_scr[...] = jnp.zeros_like(run_scr)
        r = lax.broadcasted_iota(jnp.int32, (TM_PLAN, TM_PLAN), 0)
        c = lax.broadcasted_iota(jnp.int32, (TM_PLAN, TM_PLAN), 1)
        tri_scr[...] = jnp.where(c < r, 1.0, 0.0).astype(BF16)

    @pl.when((phase == 1) & (i == 0))
    def _():
        counts = run_scr[...]
        counts_ref[...] = counts
        tiles = jnp.floor((counts + (TM_MOE - 1)) * (1.0 / TM_MOE))
        lane_row = lax.broadcasted_iota(jnp.int32, (1, LANES), 1)
        scan = tiles
        shift = 1
        while shift < LANES:
            scan = scan + jnp.where(lane_row >= shift, pltpu.roll(scan, shift, axis=1), 0.0)
            shift *= 2
        start_scr[...] = (scan - tiles) * TM_MOE
        run_scr[...] = jnp.zeros_like(run_scr)

    @pl.when(phase == 1)
    def _():
        before = jnp.dot(tri_scr[...], onehot_f.astype(BF16), preferred_element_type=F32)
        pos_col = jnp.sum(onehot_f * (before + run_scr[...] + start_scr[...]), axis=1, keepdims=True)
        eye = (lax.broadcasted_iota(jnp.int32, (LANES, LANES), 0)
               == lax.broadcasted_iota(jnp.int32, (LANES, LANES), 1))
        for r in range(TM_PLAN // LANES):
            row = jnp.sum(jnp.where(eye, pos_col[r * LANES:(r + 1) * LANES], 0.0), axis=0, keepdims=True)
            pos_ref[r:r + 1, :] = row.astype(jnp.int32)

    run_scr[...] = run_scr[...] + jnp.sum(onehot_f, axis=0, keepdims=True)


def _plan_positions(xr):
    T = xr.shape[0]
    n_steps = T // TM_PLAN
    pos, counts = pl.pallas_call(
        _plan_kernel,
        grid=(2, n_steps),
        in_specs=[pl.BlockSpec((TM_PLAN, LANES), lambda p, i: (i, D_MODEL // LANES))],
        out_specs=[pl.BlockSpec((TM_PLAN // LANES, LANES), lambda p, i: (i * p, 0)),
                   pl.BlockSpec((1, LANES), lambda p, i: (0, 0))],
        out_shape=[jax.ShapeDtypeStruct((T // LANES, LANES), jnp.int32),
                   jax.ShapeDtypeStruct((1, LANES), F32)],
        scratch_shapes=[pltpu.VMEM((TM_PLAN, TM_PLAN), BF16),
                        pltpu.VMEM((1, LANES), F32),
                        pltpu.VMEM((1, LANES), F32)],
        compiler_params=_cparams("arbitrary", "arbitrary"),
        name="moe_plan",
    )(xr)
    return pos.reshape(T), counts[0, :N_CLASSES].astype(jnp.int32)


def _moe_plan(xr, n_tiles_max):
    T = xr.shape[0]
    pos, counts = _plan_positions(xr)
    tiles_per_class = (counts + TM_MOE - 1) // TM_MOE
    tile_end = jnp.cumsum(tiles_per_class)
    tile_start = tile_end - tiles_per_class
    n_tiles = tile_end[-1]
    tok = jnp.arange(T, dtype=jnp.int32)
    src_rows = jnp.zeros((n_tiles_max * TM_MOE,), jnp.int32).at[pos].set(tok)
    tile_ids = jnp.arange(n_tiles_max, dtype=jnp.int32)
    used = jnp.minimum(tile_ids, n_tiles - 1)
    tile_cls = jnp.sum((tile_end[None, :] <= used[:, None]).astype(jnp.int32), axis=1)
    tile_cls = jnp.minimum(tile_cls, N_CLASSES - 1)
    cls_lo, cls_hi = _class_experts()
    tile_elo = jnp.asarray(cls_lo)[tile_cls]
    tile_ehi = jnp.asarray(cls_hi)[tile_cls]
    within = tile_ids - tile_start[tile_cls]
    tile_cnt = jnp.clip(counts[tile_cls] - within * TM_MOE, 0, TM_MOE)
    tile_cnt = jnp.where(tile_ids < n_tiles, tile_cnt, 0).astype(jnp.int32)
    return src_rows, tile_elo, tile_ehi, tile_cnt, n_tiles.reshape(1).astype(jnp.int32)


def _vec(a):
    return a.reshape(1, -1).astype(F32)


def _mixer_and_cross_attention(x, mem, ln_in_g, ln_in_b, w_in, rel_bias, sink_a, norm_a_g, norm_b_g, w_out,
                               ln1_g, ln1_b, xq, xkv, xo, ln2_g, ln2_b, w_group, b_group, w_router, b_router):
    B, S, D = x.shape
    assert S == SEQ and D == D_MODEL and mem.shape[1:] == (MEM_LEN, D_MODEL)
    T = B * S
    vec = _vec

    w = w_in[0]
    edges = np.cumsum((0, A_WIDTH, A_KV_HEADS * HEAD_DIM, A_KV_HEADS * HEAD_DIM, B_WIDTH, B_WIDTH, B_WIDTH))
    qa, ka, va, qb, kb, vb = [w[:, a:b] for a, b in zip(edges[:-1], edges[1:])]
    dup = lambda t: jnp.repeat(t.reshape(D, A_KV_HEADS, 1, HEAD_DIM), 2, axis=2).reshape(D, A_KV_TILES * LANES)
    scale = HEAD_DIM ** -0.5 * LOG2E
    w_in_b = jnp.concatenate([qa * scale, dup(ka), dup(va), qb * scale, kb, vb], axis=1).astype(BF16)
    w_out_b = w_out[0].astype(BF16)
    xq_b = (xq[0] * (X_HEAD_DIM ** -0.5 * LOG2E)).astype(BF16)
    xkv_b = xkv[0].astype(BF16)
    xo_b = xo[0].astype(BF16)
    wr = jnp.concatenate([w_group[0], w_router[0]], axis=1).astype(F32)
    wr = jnp.pad(wr, ((0, 0), (0, LANES - wr.shape[1])))
    wr_hi = wr.astype(BF16)
    wr_split = jnp.concatenate([wr_hi, (wr - wr_hi.astype(F32)).astype(BF16)], axis=1)
    br = jnp.concatenate([b_group[0], b_router[0]]).astype(F32)
    br = jnp.pad(br, (0, LANES - br.shape[0])).reshape(1, LANES)

    x2 = x.reshape(T, D)
    qkv_a, qkv_b1, qkv_b4, qkv_b16 = _input_projection(x2, vec(ln_in_g), vec(ln_in_b), w_in_b, B)

    nk_a = TM_ATT + 2 * A_HALF_WIN
    bias_a = _band_bias(rel_bias[:, :A_HEADS], A_HALF_WIN, 1, TM_ATT, nk_a, S)
    ya = _band_attention(qkv_a.reshape(B, S, QKV_A), bias_a, nk=nk_a, sub=1, kv_of=(0, 0, 1, 1),
                         q_width=A_WIDTH, kv_width=A_KV_TILES * LANES, out_tiles=A_WIDTH // LANES,
                         sink=sink_a[0].astype(F32) * LOG2E, gain=vec(norm_a_g[0]))
    branch_out = []
    for (win, dil), qkv in zip(B_BRANCHES, (qkv_b1.reshape(B, S, QKV_B),
                                             qkv_b4.reshape(B * 4, S // 4, QKV_B),
                                             qkv_b16.reshape(B * 16, S // 16, QKV_B))):
        half = (win // 2) // dil
        n = S // dil
        nk = min(TM_ATT + 2 * half, n)
        bias_b = _band_bias(rel_bias[:, A_HEADS:], half, dil, TM_ATT, nk, n)
        branch_out.append(_band_attention(qkv, bias_b, nk=nk, sub=max(1, TM_ATT * 8 // n), kv_of=(0, 1, 2, 3),
                                          q_width=B_WIDTH, kv_width=B_WIDTH, out_tiles=B_OUT // LANES))
    o1 = branch_out[0]
    o4 = branch_out[1].reshape(B, 4, B_OUT // LANES, S // 4, LANES)
    o16 = branch_out[2].reshape(B, 16, B_OUT // LANES, S // 16, LANES)

    k_mem, v_mem = _mem_kv(mem, xkv_b)
    return _merge_cross_attention_route(
        ya, o1, o4, o16, x2, vec(ln_in_g), vec(ln_in_b), vec(norm_b_g[0]), w_out_b, vec(ln1_g[0]), vec(ln1_b[0]),
        k_mem, v_mem, xq_b, xo_b, vec(ln2_g[0]), vec(ln2_b[0]), wr_split, br)


def _moe(xr, w_gate, w_up, w_down, ln3_g, ln3_b):
    T = xr.shape[0]
    wg_b = w_gate[0].reshape(N_EXPERTS, D_MODEL, D_EXPERT).astype(BF16)
    wu_b = w_up[0].reshape(N_EXPERTS, D_MODEL, D_EXPERT).astype(BF16)
    wd_b = w_down[0].reshape(N_EXPERTS, D_EXPERT, D_MODEL).astype(BF16)
    n_tiles_max = T // TM_MOE + N_CLASSES
    src_rows, tile_elo, tile_ehi, tile_cnt, n_tiles = _moe_plan(xr, n_tiles_max)
    return _expert_mlp(src_rows, _token_of_row(src_rows), tile_cnt, tile_elo, tile_ehi, n_tiles, xr, wg_b, wu_b, wd_b,
                       _vec(ln3_g[0]), _vec(ln3_b[0]))


def kernel(x, mem, ln_in_g, ln_in_b, w_in, rel_bias, sink_a, norm_a_g, norm_b_g, w_out,
           ln1_g, ln1_b, xq, xkv, xo, ln2_g, ln2_b, w_group, b_group, w_router, b_router,
           w_gate, w_up, w_down, ln3_g, ln3_b):
    xr = _mixer_and_cross_attention(
        x, mem, ln_in_g, ln_in_b, w_in, rel_bias, sink_a, norm_a_g, norm_b_g, w_out,
        ln1_g, ln1_b, xq, xkv, xo, ln2_g, ln2_b, w_group, b_group, w_router, b_router)
    return _moe(xr, w_gate, w_up, w_down, ln3_g, ln3_b).reshape(x.shape)
```

```python
import functools

import numpy as np
import jax
import jax.numpy as jnp
from jax import lax
from jax.experimental import pallas as pl
from jax.experimental.pallas import tpu as pltpu

F32 = jnp.float32
BF16 = jnp.bfloat16

D_MODEL = 1024
SEQ = 2048
MEM_LEN = 256
HEAD_DIM = 64
A_HEADS = 8
A_KV_HEADS = 2
A_HALF_WIN = 128
B_HEADS = 8
B_BRANCHES = ((128, 1), (512, 4), (2048, 16))
N_BUCKETS = 32
MAX_DISTANCE = 1024
X_HEADS = 4
X_HEAD_DIM = D_MODEL // X_HEADS
N_GROUPS = 4
EXPERTS_PER_GROUP = 8
N_EXPERTS = N_GROUPS * EXPERTS_PER_GROUP
D_EXPERT = 512
DEPTH = 1
ALPHA = (2.0 * DEPTH) ** 0.25
LN_EPS = 1e-5
NEG = -1e30
LOG2E = 1.4426950408889634

LANES = 128
XR_WIDTH = D_MODEL + LANES
A_WIDTH = A_HEADS * HEAD_DIM
B_WIDTH = B_HEADS * HEAD_DIM
A_KV_TILES = A_KV_HEADS
QKV_A = A_WIDTH + 2 * A_KV_TILES * LANES
QKV_B = 3 * B_WIDTH
B_OUT = B_WIDTH + LANES
MERGE_STRIDE = 4

PAIRS_PER_GROUP = EXPERTS_PER_GROUP * (EXPERTS_PER_GROUP - 1) // 2
N_CLASSES = N_GROUPS * PAIRS_PER_GROUP

TM_IN = 512
TM_ATT = 128
ATT_BLOCKS = 4
TM_MERGE = 512
TM_MOE = 160
ROW_GROUP = 8
TM_PLAN = 1024
VMEM_LIMIT = 56 * 1024 * 1024


def _cparams(*sem):
    return pltpu.CompilerParams(dimension_semantics=sem, vmem_limit_bytes=VMEM_LIMIT)


def _layer_norm(x, g, b):
    mu = jnp.mean(x, axis=-1, keepdims=True)
    xc = x - mu
    var = jnp.mean(xc * xc, axis=-1, keepdims=True)
    return xc * lax.rsqrt(var + LN_EPS) * g + b


def _rms_norm(x, g):
    return x * lax.rsqrt(jnp.mean(x * x, axis=-1, keepdims=True) + LN_EPS) * g


def _t5_bucket(rel):
    nb = N_BUCKETS // 2
    max_exact = nb // 2
    ret = (rel > 0).astype(np.int32) * nb
    n = np.abs(rel)
    n_safe = np.maximum(n, 1).astype(np.float64)
    large = max_exact + (np.log(n_safe / max_exact) / np.log(MAX_DISTANCE / max_exact)
                         * (nb - max_exact)).astype(np.int32)
    large = np.minimum(large, nb - 1)
    return (ret + np.where(n < max_exact, n, large)).astype(np.int32)


def _inproj_kernel(x_ref, g_ref, b_ref, w_ref, qa_ref, qb1_ref, qb4_ref, qb16_ref, pb_scr, p4_scr):
    h = _layer_norm(x_ref[...], g_ref[...], b_ref[...])
    proj = jnp.dot(h.astype(BF16), w_ref[...], preferred_element_type=F32)
    qa_ref[...] = proj[:, :QKV_A].astype(BF16)
    pb = proj[:, QKV_A:]
    qb1_ref[...] = pb.astype(BF16)
    n4 = TM_IN // 4
    for c in range(QKV_B // LANES):
        cs = slice(c * LANES, (c + 1) * LANES)
        pb_scr[c] = pb[:, cs]
        for r4 in range(4):
            rows = pb_scr[c, pl.ds(r4, n4, stride=4), :]
            qb4_ref[0, r4, :, cs] = rows.astype(BF16)
            p4_scr[c, r4 * n4:(r4 + 1) * n4, :] = rows
        for r16 in range(16):
            rows = p4_scr[c, pl.ds((r16 % 4) * n4 + r16 // 4, TM_IN // 16, stride=4), :]
            qb16_ref[0, r16, :, cs] = rows.astype(BF16)


def _input_projection(x2, ln_g, ln_b, w_in_b, batch):
    T = x2.shape[0]
    tiles_per_seq = SEQ // TM_IN
    row = lambda i: (i, 0)
    const = lambda i: (0, 0)
    deint = lambda i: (i // tiles_per_seq, 0, i % tiles_per_seq, 0)
    return pl.pallas_call(
        _inproj_kernel,
        grid=(T // TM_IN,),
        in_specs=[
            pl.BlockSpec((TM_IN, D_MODEL), row),
            pl.BlockSpec((1, D_MODEL), const),
            pl.BlockSpec((1, D_MODEL), const),
            pl.BlockSpec((D_MODEL, QKV_A + QKV_B), const),
        ],
        out_specs=[
            pl.BlockSpec((TM_IN, QKV_A), row),
            pl.BlockSpec((TM_IN, QKV_B), row),
            pl.BlockSpec((1, 4, TM_IN // 4, QKV_B), deint),
            pl.BlockSpec((1, 16, TM_IN // 16, QKV_B), deint),
        ],
        out_shape=[
            jax.ShapeDtypeStruct((T, QKV_A), BF16),
            jax.ShapeDtypeStruct((T, QKV_B), BF16),
            jax.ShapeDtypeStruct((batch, 4, SEQ // 4, QKV_B), BF16),
            jax.ShapeDtypeStruct((batch, 16, SEQ // 16, QKV_B), BF16),
        ],
        scratch_shapes=[pltpu.VMEM((QKV_B // LANES, TM_IN, LANES), F32)] * 2,
        compiler_params=_cparams("parallel"),
        name="ln_in_proj",
    )(x2, ln_g, ln_b, w_in_b)


def _band_attn_kernel(*refs, m, nk, n, sub, qb, kv_of, variants, with_sink):
    if with_sink:
        sink_ref, gain_ref = refs[0], refs[1]
        refs = refs[2:]
    q_ref, k_ref, v_ref, bias_ref, o_ref = refs
    nb = n // m
    first, middle, last = variants
    starts, bias_var = [], []
    for b in range(qb):
        if nk == n:
            starts.append(0)
            bias_var.append(first)
        else:
            jj = pl.program_id(1) * qb + b
            starts.append(pl.multiple_of(jnp.clip(jj * m - (nk - m) // 2, 0, n - nk), HEAD_DIM))
            bias_var.append(jnp.where(jj == 0, first, jnp.where(jj == nb - 1, last, middle)))
    lane_row = lax.broadcasted_iota(jnp.int32, (1, LANES), 1)
    keep_lo = jnp.where(lane_row < HEAD_DIM, 1.0, 0.0).astype(BF16)
    keep_hi = jnp.where(lane_row < HEAD_DIM, 0.0, 1.0).astype(BF16)
    lane = lax.broadcasted_iota(jnp.int32, (m, LANES), 1)
    n_q = len(kv_of)
    units = [(s, b, qt) for s in range(sub) for b in range(qb) for qt in range(n_q)]

    scores = []
    for s, b, qt in units:
        t = kv_of[qt]
        k_t = k_ref[s, pl.ds(starts[b], nk), t * LANES:(t + 1) * LANES]
        q2 = q_ref[s, b * m:(b + 1) * m, qt * LANES:(qt + 1) * LANES]
        lhs = jnp.concatenate([q2 * keep_lo, q2 * keep_hi], axis=0)
        sc = lax.dot_general(lhs, k_t, (((1,), (1,)), ((), ())), preferred_element_type=F32)
        scores.append(sc + bias_ref[bias_var[b], qt])

    probs, denom, row_max = [], [], []
    for (s, b, qt), sc in zip(units, scores):
        mx = jnp.max(sc, axis=-1, keepdims=True)
        p = jnp.exp2(sc - mx)
        l = jnp.sum(p, axis=-1, keepdims=True)
        halves = []
        for h in (0, 1):
            l_h = l[h * m:(h + 1) * m]
            if with_sink:
                l_h = l_h + jnp.exp2(sink_ref[2 * qt + h] - mx[h * m:(h + 1) * m])
            halves.append(l_h)
        probs.append(p.astype(BF16))
        denom.append(halves)
        row_max.append(mx)

    for s, b in [(s, b) for s in range(sub) for b in range(qb)]:
        rows = slice(b * m, (b + 1) * m)
        pairs = []
        stat_tile = jnp.zeros((m, LANES), F32)
        for qt in range(n_q):
            u = (s * qb + b) * n_q + qt
            t = kv_of[qt]
            v_t = v_ref[s, pl.ds(starts[b], nk), t * LANES:(t + 1) * LANES]
            o = jnp.dot(probs[u], v_t, preferred_element_type=F32)
            if with_sink:
                pairs.append(jnp.where(lane < HEAD_DIM, o[:m] * (1.0 / denom[u][0]), o[m:] * (1.0 / denom[u][1])))
            else:
                pairs.append(jnp.where(lane < HEAD_DIM, o[:m], o[m:]))
                for h in (0, 1):
                    stat_tile = jnp.where(lane == 2 * qt + h, row_max[u][h * m:(h + 1) * m], stat_tile)
                    stat_tile = jnp.where(lane == B_HEADS + 2 * qt + h, denom[u][h], stat_tile)
        if with_sink:
            normed = _rms_norm(jnp.concatenate(pairs, axis=1), gain_ref[...])
            pairs = [normed[:, c * LANES:(c + 1) * LANES] for c in range(n_q)]
        else:
            o_ref[s, n_q, rows, :] = stat_tile
        for c in range(n_q):
            o_ref[s, c, rows, :] = pairs[c]


def _band_window(m, nk, n):
    nb = n // m
    starts = np.clip(np.arange(nb) * m - (nk - m) // 2, 0, n - nk)
    offs = [int(o) for o in starts - np.arange(nb) * m]
    uniq = sorted(set(offs), reverse=True)
    var = [uniq.index(o) for o in offs]
    assert all(v == var[1] for v in var[1:-1])
    return uniq, var


def _band_bias(rel_bias_h, half, dil, m, nk, n):
    uniq, _ = _band_window(m, nk, n)
    n_heads = rel_bias_h.shape[1]
    span = nk + m - 1
    out = []
    for off in uniq:
        rel = off - (m - 1) + np.arange(span)
        onehot = np.zeros((span, N_BUCKETS), np.float32)
        onehot[np.arange(span), _t5_bucket(dil * rel)] = 1.0
        table = jnp.dot(jnp.asarray(onehot), rel_bias_h.astype(F32), precision=lax.Precision.HIGHEST)
        table = jnp.where((np.abs(rel) <= half)[:, None], table, NEG)
        u = jnp.concatenate([table.T, jnp.zeros((n_heads, 1), F32)], axis=1)
        flat = jnp.tile(u, (1, m + 1))[:, m - 1:m - 1 + m * span]
        b = flat.reshape(n_heads, m, span)[:, :, :nk]
        out.append(b.reshape(n_heads // 2, 2 * m, nk))
    return jnp.stack(out) * LOG2E


def _band_attention(qkv, bias, *, nk, sub, kv_of, q_width, kv_width, out_tiles, sink=None, gain=None):
    Bd, n, _ = qkv.shape
    m = TM_ATT
    nb = n // m
    qb = min(nb, ATT_BLOCKS)
    assert nb % qb == 0 and nb * m == n and Bd % sub == 0
    _, var = _band_window(m, nk, n)
    variants = (var[0], var[min(1, nb - 1)], var[-1])
    kcol = q_width // kv_width
    with_sink = sink is not None
    in_specs = [
        pl.BlockSpec((sub, qb * m, q_width), lambda b, j: (b, j, 0)),
        pl.BlockSpec((sub, n, kv_width), lambda b, j: (b, 0, kcol)),
        pl.BlockSpec((sub, n, kv_width), lambda b, j: (b, 0, kcol + 1)),
        pl.BlockSpec(bias.shape, lambda b, j: (0, 0, 0, 0)),
    ]
    args = [qkv, qkv, qkv, bias]
    if with_sink:
        in_specs = [pl.BlockSpec(memory_space=pltpu.SMEM),
                    pl.BlockSpec((1, q_width), lambda b, j: (0, 0))] + in_specs
        args = [sink, gain] + args
    return pl.pallas_call(
        functools.partial(_band_attn_kernel, m=m, nk=nk, n=n, sub=sub, qb=qb, kv_of=kv_of, variants=variants,
                          with_sink=with_sink),
        grid=(Bd // sub, nb // qb),
        in_specs=in_specs,
        out_specs=pl.BlockSpec((sub, out_tiles, qb * m, LANES), lambda b, j: (b, 0, j, 0)),
        out_shape=jax.ShapeDtypeStruct((Bd, out_tiles, n, LANES), F32),
        compiler_params=_cparams("parallel", "arbitrary"),
        name="band_attn_sink" if with_sink else f"band_attn_n{n}",
    )(*args)


def _merge_rows(ya_ref, o1_ref, o4_ref, o16_ref, x_ref, gin_ref, bin_ref, gb_ref, w_ref, g1_ref, b1_ref,
                s16_scr, x_scr):
    n4 = TM_MERGE // MERGE_STRIDE
    n16 = TM_MERGE // 16
    h0_rows = _layer_norm(x_ref[...], gin_ref[...], bin_ref[...])
    for c in range(D_MODEL // LANES):
        x_scr[c] = h0_rows[:, c * LANES:(c + 1) * LANES]
    for r16 in range(16):
        for c in range(B_OUT // LANES):
            s16_scr[c, pl.ds((r16 % 4) * n4 + r16 // 4, n16, stride=4), :] = o16_ref[0, r16, c]

    def natural(ref, c):
        return jnp.concatenate([ref[0, c, pl.ds(r, n4, stride=MERGE_STRIDE), :]
                                for r in range(MERGE_STRIDE)], axis=0)

    def stride4(c):
        return jnp.concatenate([o4_ref[0, r, c] for r in range(MERGE_STRIDE)], axis=0)

    def stride16(c):
        return s16_scr[c]

    n_val = B_WIDTH // LANES
    stats = (natural(o1_ref, n_val), stride4(n_val), stride16(n_val))
    dens = [pltpu.roll(s, LANES - B_HEADS, axis=1) for s in stats]
    mx = jnp.maximum(jnp.maximum(stats[0], stats[1]), stats[2])
    ex = [jnp.exp2(s - mx) for s in stats]
    inv = 1.0 / (ex[0] * dens[0] + ex[1] * dens[1] + ex[2] * dens[2])
    head_lane = lax.broadcasted_iota(jnp.int32, (TM_MERGE, LANES), 1) < B_HEADS
    head = lax.broadcasted_iota(jnp.int32, (LANES, B_WIDTH), 0)
    col = lax.broadcasted_iota(jnp.int32, (LANES, B_WIDTH), 1)
    spread = jnp.where(col // HEAD_DIM == head, 1.0, 0.0).astype(BF16)
    wide = []
    for e in ex:
        w = jnp.where(head_lane, e * inv, 0.0)
        w_hi = w.astype(BF16)
        w_lo = (w - w_hi.astype(F32)).astype(BF16)
        wide.append(jnp.dot(w_hi, spread, preferred_element_type=F32)
                    + jnp.dot(w_lo, spread, preferred_element_type=F32))
    pieces = []
    for c in range(n_val):
        cs = slice(c * LANES, (c + 1) * LANES)
        outs = (natural(o1_ref, c), stride4(c), stride16(c))
        pieces.append(wide[0][:, cs] * outs[0] + wide[1][:, cs] * outs[1] + wide[2][:, cs] * outs[2])
    yb = _rms_norm(jnp.concatenate(pieces, axis=1), gb_ref[...])
    ya = jnp.concatenate([natural(ya_ref, c) for c in range(A_WIDTH // LANES)], axis=1)
    y = jnp.concatenate([ya.astype(BF16), yb.astype(BF16)], axis=1)
    mix = jnp.dot(y, w_ref[...], preferred_element_type=F32)
    h0 = jnp.concatenate(
        [jnp.concatenate([x_scr[c, pl.ds(r, n4, stride=MERGE_STRIDE), :] for r in range(MERGE_STRIDE)], axis=0)
         for c in range(D_MODEL // LANES)], axis=1)
    return _layer_norm(ALPHA * h0 + mix, g1_ref[...], b1_ref[...])


def _token_of_row(p):
    n4 = TM_MERGE // MERGE_STRIDE
    tile, rest = p // TM_MERGE, p % TM_MERGE
    return tile * TM_MERGE + (rest % n4) * MERGE_STRIDE + rest // n4


def _mem_kv_kernel(mem_ref, w_ref, k_ref, v_ref):
    kv = jnp.dot(mem_ref[0].astype(BF16), w_ref[...], preferred_element_type=F32)
    k_ref[0] = kv[:, :D_MODEL].astype(BF16)
    v_ref[0] = kv[:, D_MODEL:].astype(BF16)


def _mem_kv(mem, xkv_b):
    B = mem.shape[0]
    blk = pl.BlockSpec((1, MEM_LEN, D_MODEL), lambda b: (b, 0, 0))
    return pl.pallas_call(
        _mem_kv_kernel,
        grid=(B,),
        in_specs=[blk, pl.BlockSpec((D_MODEL, 2 * D_MODEL), lambda b: (0, 0))],
        out_specs=[blk, blk],
        out_shape=[jax.ShapeDtypeStruct((B, MEM_LEN, D_MODEL), BF16)] * 2,
        compiler_params=_cparams("parallel"),
        name="mem_kv_proj",
    )(mem, xkv_b)


def _route(logits):
    rows = logits.shape[0]
    lane = lax.broadcasted_iota(jnp.int32, (rows, LANES), 1).astype(F32)
    big = float(LANES)
    ninf = -jnp.inf
    gl = jnp.where(lane < N_GROUPS, logits, ninf)
    gmax = jnp.max(gl, axis=-1, keepdims=True)
    gidx = jnp.min(jnp.where(gl == gmax, lane, big), axis=-1, keepdims=True)
    g_p = 1.0 / jnp.sum(jnp.exp(gl - gmax), axis=-1, keepdims=True)
    lo_lane = N_GROUPS + EXPERTS_PER_GROUP * gidx
    el = jnp.where((lane >= lo_lane) & (lane < lo_lane + EXPERTS_PER_GROUP), logits, ninf)
    v1 = jnp.max(el, axis=-1, keepdims=True)
    i1 = jnp.min(jnp.where(el == v1, lane, big), axis=-1, keepdims=True)
    el2 = jnp.where(lane == i1, ninf, el)
    v2 = jnp.max(el2, axis=-1, keepdims=True)
    i2 = jnp.min(jnp.where(el2 == v2, lane, big), axis=-1, keepdims=True)
    t = jnp.exp(v2 - v1)
    w1 = g_p / (1.0 + t)
    w2 = g_p * t / (1.0 + t)
    a = jnp.minimum(i1, i2) - lo_lane
    b = jnp.maximum(i1, i2) - lo_lane
    pair = a * (2 * EXPERTS_PER_GROUP - 1 - a) * 0.5 + (b - a - 1.0)
    cls = gidx * PAIRS_PER_GROUP + pair
    w_lo = jnp.where(i1 < i2, w1, w2)
    w_hi = jnp.where(i1 < i2, w2, w1)
    return jnp.where(lane == 0, cls, jnp.where(lane == 1, w_lo, jnp.where(lane == 2, w_hi, 0.0)))


def _merge_xattn_kernel(ya_ref, o1_ref, o4_ref, o16_ref, x_ref, gin_ref, bin_ref, gb_ref, w_ref, g1_ref, b1_ref,
                        k_ref, v_ref, wq_ref, wo_ref, g2_ref, b2_ref, wr_ref, br_ref, xr_ref, s16_scr, x_scr):
    h1 = _merge_rows(ya_ref, o1_ref, o4_ref, o16_ref, x_ref, gin_ref, bin_ref, gb_ref, w_ref, g1_ref, b1_ref,
                     s16_scr, x_scr)
    q = jnp.dot(h1.astype(BF16), wq_ref[...], preferred_element_type=F32).astype(BF16)
    outs = []
    for hd in range(X_HEADS):
        sl = slice(hd * X_HEAD_DIM, (hd + 1) * X_HEAD_DIM)
        s = lax.dot_general(q[:, sl], k_ref[0, :, sl], (((1,), (1,)), ((), ())), preferred_element_type=F32)
        m = jnp.max(s, axis=-1, keepdims=True)
        p = jnp.exp2(s - m)
        l = jnp.sum(p, axis=-1, keepdims=True)
        o = jnp.dot(p.astype(BF16), v_ref[0, :, sl], preferred_element_type=F32) / l
        outs.append(o.astype(BF16))
    xa = jnp.dot(jnp.concatenate(outs, axis=1), wo_ref[...], preferred_element_type=F32)
    h2 = _layer_norm(ALPHA * h1 + xa, g2_ref[...], b2_ref[...])
    h_hi = h2.astype(BF16)
    h_lo = (h2 - h_hi.astype(F32)).astype(BF16)
    t_hi = jnp.dot(h_hi, wr_ref[...], preferred_element_type=F32)
    t_lo = jnp.dot(h_lo, wr_ref[...], preferred_element_type=F32)
    logits = (t_hi[:, :LANES] + t_hi[:, LANES:]) + (t_lo[:, :LANES] + t_lo[:, LANES:]) + br_ref[...]
    xr_ref[:, :D_MODEL] = h2
    xr_ref[:, D_MODEL:] = _route(logits)


def _merge_cross_attention_route(ya, o1, o4, o16, x2, ln_in_g, ln_in_b, gain_b, w_out_b, ln1_g, ln1_b,
                                 k, v, xq_b, xo_b, ln2_g, ln2_b, wr_split, br):
    T = x2.shape[0]
    tiles_per_seq = SEQ // TM_MERGE
    row = lambda i: (i, 0)
    const = lambda i: (0, 0)
    nat = lambda i: (i // tiles_per_seq, 0, i % tiles_per_seq, 0)
    deint = lambda i: (i // tiles_per_seq, 0, 0, i % tiles_per_seq, 0)
    vec_d = pl.BlockSpec((1, D_MODEL), const)
    mat_d = pl.BlockSpec((D_MODEL, D_MODEL), const)
    kv_blk = pl.BlockSpec((1, MEM_LEN, D_MODEL), lambda i: (i // tiles_per_seq, 0, 0))
    n_b = B_OUT // LANES
    return pl.pallas_call(
        _merge_xattn_kernel,
        grid=(T // TM_MERGE,),
        in_specs=[
            pl.BlockSpec((1, A_WIDTH // LANES, TM_MERGE, LANES), nat),
            pl.BlockSpec((1, n_b, TM_MERGE, LANES), nat),
            pl.BlockSpec((1, 4, n_b, TM_MERGE // 4, LANES), deint),
            pl.BlockSpec((1, 16, n_b, TM_MERGE // 16, LANES), deint),
            pl.BlockSpec((TM_MERGE, D_MODEL), row), vec_d, vec_d,
            pl.BlockSpec((1, B_WIDTH), const), mat_d, vec_d, vec_d,
            kv_blk, kv_blk, mat_d, mat_d, vec_d, vec_d,
            pl.BlockSpec((D_MODEL, 2 * LANES), const),
            pl.BlockSpec((1, LANES), const),
        ],
        out_specs=pl.BlockSpec((TM_MERGE, XR_WIDTH), row),
        out_shape=jax.ShapeDtypeStruct((T, XR_WIDTH), F32),
        scratch_shapes=[pltpu.VMEM((n_b, TM_MERGE, LANES), F32),
                        pltpu.VMEM((D_MODEL // LANES, TM_MERGE, LANES), F32)],
        compiler_params=_cparams("parallel"),
        name="merge_xattn_route",
    )(ya, o1, o4, o16, x2, ln_in_g, ln_in_b, gain_b, w_out_b, ln1_g, ln1_b,
      k, v, xq_b, xo_b, ln2_g, ln2_b, wr_split, br)


def _expert_kernel(src_ref, dst_ref, cnt_ref, elo_ref, ehi_ref, ntile_ref, x_hbm,
                   wg_lo, wu_lo, wd_lo, wg_hi, wu_hi, wd_hi, g3_ref, b3_ref, o_hbm,
                   xbuf, obuf, gsem, ssem):
    j = pl.program_id(0)
    n_tiles = ntile_ref[0]
    slot = j % 2

    def row_copy(t, i, s, gather):
        row = pl.ds(i, 1)
        if gather:
            return pltpu.make_async_copy(x_hbm.at[pl.ds(src_ref[t * TM_MOE + i], 1)], xbuf.at[s, row], gsem.at[s])
        return pltpu.make_async_copy(obuf.at[s, row], o_hbm.at[pl.ds(dst_ref[t * TM_MOE + i], 1)], ssem.at[s])

    def n_rows(t, gather):
        cnt = cnt_ref[t]
        return (cnt + ROW_GROUP - 1) // ROW_GROUP * ROW_GROUP if gather else cnt

    def start_rows(t, s, gather):
        cnt = cnt_ref[t]
        n_groups = (cnt + ROW_GROUP - 1) // ROW_GROUP if gather else cnt // ROW_GROUP

        def group(g, c):
            base = pl.multiple_of(g * ROW_GROUP, ROW_GROUP)
            for r in range(ROW_GROUP):
                row_copy(t, base + r, s, gather).start(priority=0 if gather else 1)
            return c
        lax.fori_loop(0, n_groups, group, 0)
        if not gather:
            def single(i, c):
                row_copy(t, i, s, gather).start(priority=1)
                return c
            lax.fori_loop(n_groups * ROW_GROUP, cnt, single, 0)

    def wait_rows(t, s, gather):
        cnt = n_rows(t, gather)
        for bit in range(TM_MOE.bit_length()):
            rows = pl.ds(0, 1 << bit)

            @pl.when((cnt >> bit) & 1 == 1)
            def _():
                if gather:
                    pltpu.make_async_copy(x_hbm.at[rows], xbuf.at[s, rows], gsem.at[s]).wait()
                else:
                    pltpu.make_async_copy(obuf.at[s, rows], o_hbm.at[rows], ssem.at[s]).wait()

    @pl.when(j == 0)
    def _():
        xbuf[...] = jnp.zeros_like(xbuf)
        start_rows(0, 0, True)

    @pl.when(j < n_tiles)
    def _():
        @pl.when(j + 1 < n_tiles)
        def _():
            start_rows(j + 1, 1 - slot, True)

        wait_rows(j, slot, True)
        x = xbuf[slot, :, :D_MODEL]
        xb = x.astype(BF16)
        y = jnp.zeros_like(x)
        for e, (wg, wu, wd) in enumerate(((wg_lo, wu_lo, wd_lo), (wg_hi, wu_hi, wd_hi))):
            gate = xbuf[slot, :, D_MODEL + 1 + e:D_MODEL + 2 + e]
            a = jnp.dot(xb, wg[0], preferred_element_type=F32)
            u = jnp.dot(xb, wu[0], preferred_element_type=F32)
            hid = a * jax.nn.sigmoid(a) * u
            y = y + jnp.dot((gate * hid).astype(BF16), wd[0], preferred_element_type=F32)
        out = _layer_norm(ALPHA * x + y, g3_ref[...], b3_ref[...])

        @pl.when(j >= 2)
        def _():
            wait_rows(j - 2, slot, False)

        obuf[slot] = out
        start_rows(j, slot, False)

        @pl.when(j == n_tiles - 1)
        def _():
            @pl.when(j >= 1)
            def _():
                wait_rows(j - 1, 1 - slot, False)

            wait_rows(j, slot, False)


def _expert_mlp(src_rows, dst_rows, tile_cnt, tile_elo, tile_ehi, n_tiles, xr, wg, wu, wd, ln_g, ln_b):
    n_tiles_max = tile_cnt.shape[0]
    const = lambda j, src, dst, cnt, elo, ehi, nt: (0, 0)
    lo = lambda j, src, dst, cnt, elo, ehi, nt: (elo[j], 0, 0)
    hi = lambda j, src, dst, cnt, elo, ehi, nt: (ehi[j], 0, 0)
    up = (1, D_MODEL, D_EXPERT)
    down = (1, D_EXPERT, D_MODEL)
    any_spec = pl.BlockSpec(memory_space=pl.ANY)
    return pl.pallas_call(
        _expert_kernel,
        grid_spec=pltpu.PrefetchScalarGridSpec(
            num_scalar_prefetch=6,
            grid=(n_tiles_max,),
            in_specs=[
                any_spec,
                pl.BlockSpec(up, lo), pl.BlockSpec(up, lo), pl.BlockSpec(down, lo),
                pl.BlockSpec(up, hi), pl.BlockSpec(up, hi), pl.BlockSpec(down, hi),
                pl.BlockSpec((1, D_MODEL), const),
                pl.BlockSpec((1, D_MODEL), const),
            ],
            out_specs=any_spec,
            scratch_shapes=[
                pltpu.VMEM((2, TM_MOE, XR_WIDTH), F32),
                pltpu.VMEM((2, TM_MOE, D_MODEL), F32),
                pltpu.SemaphoreType.DMA((2,)),
                pltpu.SemaphoreType.DMA((2,)),
            ],
        ),
        out_shape=jax.ShapeDtypeStruct((xr.shape[0], D_MODEL), F32),
        compiler_params=_cparams("arbitrary"),
        name="moe_experts",
    )(src_rows, dst_rows, tile_cnt, tile_elo, tile_ehi, n_tiles, xr, wg, wu, wd, wg, wu, wd, ln_g, ln_b)


def _class_experts():
    lo, hi = [], []
    for g in range(N_GROUPS):
        for a in range(EXPERTS_PER_GROUP):
            for b in range(a + 1, EXPERTS_PER_GROUP):
                lo.append(g * EXPERTS_PER_GROUP + a)
                hi.append(g * EXPERTS_PER_GROUP + b)
    return np.asarray(lo, np.int32), np.asarray(hi, np.int32)


def _plan_kernel(route_ref, pos_ref, counts_ref, tri_scr, run_scr, start_scr):
    phase, i = pl.program_id(0), pl.program_id(1)
    lane = lax.broadcasted_iota(jnp.int32, (TM_PLAN, LANES), 1)
    onehot = lane.astype(F32) == route_ref[:, 0:1]
    onehot_f = jnp.where(onehot, 1.0, 0.0)

    @pl.when((phase == 0) & (i == 0))
    def _():
        run_scr[...] = jnp.zeros_like(run_scr)
        r = lax.broadcasted_iota(jnp.int32, (TM_PLAN, TM_PLAN), 0)
        c = lax.broadcasted_iota(jnp.int32, (TM_PLAN, TM_PLAN), 1)
        tri_scr[...] = jnp.where(c < r, 1.0, 0.0).astype(BF16)

    @pl.when((phase == 1) & (i == 0))
    def _():
        counts = run_scr[...]
        counts_ref[...] = counts
        tiles = jnp.floor((counts + (TM_MOE - 0.5)) * (1.0 / TM_MOE))
        lane_row = lax.broadcasted_iota(jnp.int32, (1, LANES), 1)
        scan = tiles
        shift = 1
        while shift < LANES:
            scan = scan + jnp.where(lane_row >= shift, pltpu.roll(scan, shift, axis=1), 0.0)
            shift *= 2
        start_scr[...] = (scan - tiles) * TM_MOE
        run_scr[...] = jnp.zeros_like(run_scr)

    @pl.when(phase == 1)
    def _():
        before = jnp.dot(tri_scr[...], onehot_f.astype(BF16), preferred_element_type=F32)
        pos_col = jnp.sum(onehot_f * (before + run_scr[...] + start_scr[...]), axis=1, keepdims=True)
        eye = (lax.broadcasted_iota(jnp.int32, (LANES, LANES), 0)
               == lax.broadcasted_iota(jnp.int32, (LANES, LANES), 1))
        for r in range(TM_PLAN // LANES):
            row = jnp.sum(jnp.where(eye, pos_col[r * LANES:(r + 1) * LANES], 0.0), axis=0, keepdims=True)
            pos_ref[r:r + 1, :] = row.astype(jnp.int32)

    run_scr[...] = run_scr[...] + jnp.sum(onehot_f, axis=0, keepdims=True)


def _plan_positions(xr):
    T = xr.shape[0]
    n_steps = T // TM_PLAN
    pos, counts = pl.pallas_call(
        _plan_kernel,
        grid=(2, n_steps),
        in_specs=[pl.BlockSpec((TM_PLAN, LANES), lambda p, i: (i, D_MODEL // LANES))],
        out_specs=[pl.BlockSpec((TM_PLAN // LANES, LANES), lambda p, i: (i * p, 0)),
                   pl.BlockSpec((1, LANES), lambda p, i: (0, 0))],
        out_shape=[jax.ShapeDtypeStruct((T // LANES, LANES), jnp.int32),
                   jax.ShapeDtypeStruct((1, LANES), F32)],
        scratch_shapes=[pltpu.VMEM((TM_PLAN, TM_PLAN), BF16),
                        pltpu.VMEM((1, LANES), F32),
                        pltpu.VMEM((1, LANES), F32)],
        compiler_params=_cparams("arbitrary", "arbitrary"),
        name="moe_plan",
    )(xr)
    return pos.reshape(T), counts[0, :N_CLASSES].astype(jnp.int32)


def _moe_plan(xr, n_tiles_max):
    T = xr.shape[0]
    pos, counts = _plan_positions(xr)
    tiles_per_class = (counts + TM_MOE - 1) // TM_MOE
    tile_end = jnp.cumsum(tiles_per_class)
    tile_start = tile_end - tiles_per_class
    n_tiles = tile_end[-1]
    tok = jnp.arange(T, dtype=jnp.int32)
    src_rows = jnp.zeros((n_tiles_max * TM_MOE,), jnp.int32).at[pos].set(tok)
    tile_ids = jnp.arange(n_tiles_max, dtype=jnp.int32)
    used = jnp.minimum(tile_ids, n_tiles - 1)
    tile_cls = jnp.sum((tile_end[None, :] <= used[:, None]).astype(jnp.int32), axis=1)
    tile_cls = jnp.minimum(tile_cls, N_CLASSES - 1)
    cls_lo, cls_hi = _class_experts()
    tile_elo = jnp.asarray(cls_lo)[tile_cls]
    tile_ehi = jnp.asarray(cls_hi)[tile_cls]
    within = tile_ids - tile_start[tile_cls]
    tile_cnt = jnp.clip(counts[tile_cls] - within * TM_MOE, 0, TM_MOE)
    tile_cnt = jnp.where(tile_ids < n_tiles, tile_cnt, 0).astype(jnp.int32)
    return src_rows, tile_elo, tile_ehi, tile_cnt, n_tiles.reshape(1).astype(jnp.int32)


def _vec(a):
    return a.reshape(1, -1).astype(F32)


def _mixer_and_cross_attention(x, mem, ln_in_g, ln_in_b, w_in, rel_bias, sink_a, norm_a_g, norm_b_g, w_out,
                               ln1_g, ln1_b, xq, xkv, xo, ln2_g, ln2_b, w_group, b_group, w_router, b_router):
    B, S, D = x.shape
    assert S == SEQ and D == D_MODEL and mem.shape[1:] == (MEM_LEN, D_MODEL)
    T = B * S
    vec = _vec

    w = w_in[0]
    edges = np.cumsum((0, A_WIDTH, A_KV_HEADS * HEAD_DIM, A_KV_HEADS * HEAD_DIM, B_WIDTH, B_WIDTH, B_WIDTH))
    qa, ka, va, qb, kb, vb = [w[:, a:b] for a, b in zip(edges[:-1], edges[1:])]
    dup = lambda t: jnp.repeat(t.reshape(D, A_KV_HEADS, 1, HEAD_DIM), 2, axis=2).reshape(D, A_KV_TILES * LANES)
    scale = HEAD_DIM ** -0.5 * LOG2E
    w_in_b = jnp.concatenate([qa * scale, dup(ka), dup(va), qb * scale, kb, vb], axis=1).astype(BF16)
    w_out_b = w_out[0].astype(BF16)
    xq_b = (xq[0] * (X_HEAD_DIM ** -0.5 * LOG2E)).astype(BF16)
    xkv_b = xkv[0].astype(BF16)
    xo_b = xo[0].astype(BF16)
    wr = jnp.concatenate([w_group[0], w_router[0]], axis=1).astype(F32)
    wr = jnp.pad(wr, ((0, 0), (0, LANES - wr.shape[1])))
    wr_hi = wr.astype(BF16)
    wr_split = jnp.concatenate([wr_hi, (wr - wr_hi.astype(F32)).astype(BF16)], axis=1)
    br = jnp.concatenate([b_group[0], b_router[0]]).astype(F32)
    br = jnp.pad(br, (0, LANES - br.shape[0])).reshape(1, LANES)

    x2 = x.reshape(T, D)
    qkv_a, qkv_b1, qkv_b4, qkv_b16 = _input_projection(x2, vec(ln_in_g), vec(ln_in_b), w_in_b, B)

    nk_a = TM_ATT + 2 * A_HALF_WIN
    bias_a = _band_bias(rel_bias[:, :A_HEADS], A_HALF_WIN, 1, TM_ATT, nk_a, S)
    ya = _band_attention(qkv_a.reshape(B, S, QKV_A), bias_a, nk=nk_a, sub=1, kv_of=(0, 0, 1, 1),
                         q_width=A_WIDTH, kv_width=A_KV_TILES * LANES, out_tiles=A_WIDTH // LANES,
                         sink=sink_a[0].astype(F32) * LOG2E, gain=vec(norm_a_g[0]))
    branch_out = []
    for (win, dil), qkv in zip(B_BRANCHES, (qkv_b1.reshape(B, S, QKV_B),
                                             qkv_b4.reshape(B * 4, S // 4, QKV_B),
                                             qkv_b16.reshape(B * 16, S // 16, QKV_B))):
        half = (win // 2) // dil
        n = S // dil
        nk = min(TM_ATT + 2 * half, n)
        bias_b = _band_bias(rel_bias[:, A_HEADS:], half, dil, TM_ATT, nk, n)
        branch_out.append(_band_attention(qkv, bias_b, nk=nk, sub=max(1, TM_ATT * 8 // n), kv_of=(0, 1, 2, 3),
                                          q_width=B_WIDTH, kv_width=B_WIDTH, out_tiles=B_OUT // LANES))
    o1 = branch_out[0]
    o4 = branch_out[1].reshape(B, 4, B_OUT // LANES, S // 4, LANES)
    o16 = branch_out[2].reshape(B, 16, B_OUT // LANES, S // 16, LANES)

    k_mem, v_mem = _mem_kv(mem, xkv_b)
    return _merge_cross_attention_route(
        ya, o1, o4, o16, x2, vec(ln_in_g), vec(ln_in_b), vec(norm_b_g[0]), w_out_b, vec(ln1_g[0]), vec(ln1_b[0]),
        k_mem, v_mem, xq_b, xo_b, vec(ln2_g[0]), vec(ln2_b[0]), wr_split, br)


def _moe(xr, w_gate, w_up, w_down, ln3_g, ln3_b):
    T = xr.shape[0]
    wg_b = w_gate[0].reshape(N_EXPERTS, D_MODEL, D_EXPERT).astype(BF16)
    wu_b = w_up[0].reshape(N_EXPERTS, D_MODEL, D_EXPERT).astype(BF16)
    wd_b = w_down[0].reshape(N_EXPERTS, D_EXPERT, D_MODEL).astype(BF16)
    n_tiles_max = -(-T // TM_MOE) + N_CLASSES
    src_rows, tile_elo, tile_ehi, tile_cnt, n_tiles = _moe_plan(xr, n_tiles_max)
    return _expert_mlp(src_rows, _token_of_row(src_rows), tile_cnt, tile_elo, tile_ehi, n_tiles, xr, wg_b, wu_b, wd_b,
                       _vec(ln3_g[0]), _vec(ln3_b[0]))


def kernel(x, mem, ln_in_g, ln_in_b, w_in, rel_bias, sink_a, norm_a_g, norm_b_g, w_out,
           ln1_g, ln1_b, xq, xkv, xo, ln2_g, ln2_b, w_group, b_group, w_router, b_router,
           w_gate, w_up, w_down, ln3_g, ln3_b):
    xr = _mixer_and_cross_attention(
        x, mem, ln_in_g, ln_in_b, w_in, rel_bias, sink_a, norm_a_g, norm_b_g, w_out,
        ln1_g, ln1_b, xq, xkv, xo, ln2_g, ln2_b, w_group, b_group, w_router, b_router)
    return _moe(xr, w_gate, w_up, w_down, ln3_g, ln3_b).reshape(x.shape)
```

```python
import functools

import numpy as np
import jax
import jax.numpy as jnp
from jax import lax
from jax.experimental import pallas as pl
from jax.experimental.pallas import tpu as pltpu

F32 = jnp.float32
BF16 = jnp.bfloat16

D_MODEL = 1024
SEQ = 2048
MEM_LEN = 256
HEAD_DIM = 64
A_HEADS = 8
A_KV_HEADS = 2
A_HALF_WIN = 128
B_HEADS = 8
B_BRANCHES = ((128, 1), (512, 4), (2048, 16))
N_BUCKETS = 32
MAX_DISTANCE = 1024
X_HEADS = 4
X_HEAD_DIM = D_MODEL // X_HEADS
N_GROUPS = 4
EXPERTS_PER_GROUP = 8
N_EXPERTS = N_GROUPS * EXPERTS_PER_GROUP
D_EXPERT = 512
DEPTH = 1
ALPHA = (2.0 * DEPTH) ** 0.25
LN_EPS = 1e-5
NEG = -1e30
LOG2E = 1.4426950408889634

LANES = 128
XR_WIDTH = D_MODEL + LANES
A_WIDTH = A_HEADS * HEAD_DIM
B_WIDTH = B_HEADS * HEAD_DIM
A_KV_TILES = A_KV_HEADS
QKV_A = A_WIDTH + 2 * A_KV_TILES * LANES
QKV_B = 3 * B_WIDTH
B_OUT = B_WIDTH + LANES
MERGE_STRIDE = 4

PAIRS_PER_GROUP = EXPERTS_PER_GROUP * (EXPERTS_PER_GROUP - 1) // 2
N_CLASSES = N_GROUPS * PAIRS_PER_GROUP

TM_IN = 512
TM_ATT = 128
ATT_BLOCKS = 4
TM_MERGE = 512
TM_MOE = 320
ROW_GROUP = 8
TM_PLAN = 1024
VMEM_LIMIT = 56 * 1024 * 1024


def _cparams(*sem):
    return pltpu.CompilerParams(dimension_semantics=sem, vmem_limit_bytes=VMEM_LIMIT)


def _layer_norm(x, g, b):
    mu = jnp.mean(x, axis=-1, keepdims=True)
    xc = x - mu
    var = jnp.mean(xc * xc, axis=-1, keepdims=True)
    return xc * lax.rsqrt(var + LN_EPS) * g + b


def _rms_norm(x, g):
    return x * lax.rsqrt(jnp.mean(x * x, axis=-1, keepdims=True) + LN_EPS) * g


def _t5_bucket(rel):
    nb = N_BUCKETS // 2
    max_exact = nb // 2
    ret = (rel > 0).astype(np.int32) * nb
    n = np.abs(rel)
    n_safe = np.maximum(n, 1).astype(np.float64)
    large = max_exact + (np.log(n_safe / max_exact) / np.log(MAX_DISTANCE / max_exact)
                         * (nb - max_exact)).astype(np.int32)
    large = np.minimum(large, nb - 1)
    return (ret + np.where(n < max_exact, n, large)).astype(np.int32)


def _inproj_kernel(x_ref, g_ref, b_ref, w_ref, qa_ref, qb1_ref, qb4_ref, qb16_ref, pb_scr, p4_scr):
    h = _layer_norm(x_ref[...], g_ref[...], b_ref[...])
    proj = jnp.dot(h.astype(BF16), w_ref[...], preferred_element_type=F32)
    qa_ref[...] = proj[:, :QKV_A].astype(BF16)
    pb = proj[:, QKV_A:]
    qb1_ref[...] = pb.astype(BF16)
    n4 = TM_IN // 4
    for c in range(QKV_B // LANES):
        cs = slice(c * LANES, (c + 1) * LANES)
        pb_scr[c] = pb[:, cs]
        for r4 in range(4):
            rows = pb_scr[c, pl.ds(r4, n4, stride=4), :]
            qb4_ref[0, r4, :, cs] = rows.astype(BF16)
            p4_scr[c, r4 * n4:(r4 + 1) * n4, :] = rows
        for r16 in range(16):
            rows = p4_scr[c, pl.ds((r16 % 4) * n4 + r16 // 4, TM_IN // 16, stride=4), :]
            qb16_ref[0, r16, :, cs] = rows.astype(BF16)


def _input_projection(x2, ln_g, ln_b, w_in_b, batch):
    T = x2.shape[0]
    tiles_per_seq = SEQ // TM_IN
    row = lambda i: (i, 0)
    const = lambda i: (0, 0)
    deint = lambda i: (i // tiles_per_seq, 0, i % tiles_per_seq, 0)
    return pl.pallas_call(
        _inproj_kernel,
        grid=(T // TM_IN,),
        in_specs=[
            pl.BlockSpec((TM_IN, D_MODEL), row),
            pl.BlockSpec((1, D_MODEL), const),
            pl.BlockSpec((1, D_MODEL), const),
            pl.BlockSpec((D_MODEL, QKV_A + QKV_B), const),
        ],
        out_specs=[
            pl.BlockSpec((TM_IN, QKV_A), row),
            pl.BlockSpec((TM_IN, QKV_B), row),
            pl.BlockSpec((1, 4, TM_IN // 4, QKV_B), deint),
            pl.BlockSpec((1, 16, TM_IN // 16, QKV_B), deint),
        ],
        out_shape=[
            jax.ShapeDtypeStruct((T, QKV_A), BF16),
            jax.ShapeDtypeStruct((T, QKV_B), BF16),
            jax.ShapeDtypeStruct((batch, 4, SEQ // 4, QKV_B), BF16),
            jax.ShapeDtypeStruct((batch, 16, SEQ // 16, QKV_B), BF16),
        ],
        scratch_shapes=[pltpu.VMEM((QKV_B // LANES, TM_IN, LANES), F32)] * 2,
        compiler_params=_cparams("parallel"),
        name="ln_in_proj",
    )(x2, ln_g, ln_b, w_in_b)


def _band_attn_kernel(*refs, m, nk, n, sub, qb, kv_of, variants, with_sink):
    if with_sink:
        sink_ref, gain_ref = refs[0], refs[1]
        refs = refs[2:]
    q_ref, k_ref, v_ref, bias_ref, o_ref = refs
    nb = n // m
    first, middle, last = variants
    starts, bias_var = [], []
    for b in range(qb):
        if nk == n:
            starts.append(0)
            bias_var.append(first)
        else:
            jj = pl.program_id(1) * qb + b
            starts.append(pl.multiple_of(jnp.clip(jj * m - (nk - m) // 2, 0, n - nk), HEAD_DIM))
            bias_var.append(jnp.where(jj == 0, first, jnp.where(jj == nb - 1, last, middle)))
    lane_row = lax.broadcasted_iota(jnp.int32, (1, LANES), 1)
    keep_lo = jnp.where(lane_row < HEAD_DIM, 1.0, 0.0).astype(BF16)
    keep_hi = jnp.where(lane_row < HEAD_DIM, 0.0, 1.0).astype(BF16)
    lane = lax.broadcasted_iota(jnp.int32, (m, LANES), 1)
    n_q = len(kv_of)
    units = [(s, b, qt) for s in range(sub) for b in range(qb) for qt in range(n_q)]

    scores = []
    for s, b, qt in units:
        t = kv_of[qt]
        k_t = k_ref[s, pl.ds(starts[b], nk), t * LANES:(t + 1) * LANES]
        q2 = q_ref[s, b * m:(b + 1) * m, qt * LANES:(qt + 1) * LANES]
        lhs = jnp.concatenate([q2 * keep_lo, q2 * keep_hi], axis=0)
        sc = lax.dot_general(lhs, k_t, (((1,), (1,)), ((), ())), preferred_element_type=F32)
        scores.append(sc + bias_ref[bias_var[b], qt])

    probs, denom, row_max = [], [], []
    for (s, b, qt), sc in zip(units, scores):
        mx = jnp.max(sc, axis=-1, keepdims=True)
        p = jnp.exp2(sc - mx)
        l = jnp.sum(p, axis=-1, keepdims=True)
        halves = []
        for h in (0, 1):
            l_h = l[h * m:(h + 1) * m]
            if with_sink:
                l_h = l_h + jnp.exp2(sink_ref[2 * qt + h] - mx[h * m:(h + 1) * m])
            halves.append(l_h)
        probs.append(p.astype(BF16))
        denom.append(halves)
        row_max.append(mx)

    for s, b in [(s, b) for s in range(sub) for b in range(qb)]:
        rows = slice(b * m, (b + 1) * m)
        pairs = []
        stat_tile = jnp.zeros((m, LANES), F32)
        for qt in range(n_q):
            u = (s * qb + b) * n_q + qt
            t = kv_of[qt]
            v_t = v_ref[s, pl.ds(starts[b], nk), t * LANES:(t + 1) * LANES]
            o = jnp.dot(probs[u], v_t, preferred_element_type=F32)
            if with_sink:
                pairs.append(jnp.where(lane < HEAD_DIM, o[:m] * (1.0 / denom[u][0]), o[m:] * (1.0 / denom[u][1])))
            else:
                pairs.append(jnp.where(lane < HEAD_DIM, o[:m], o[m:]))
                for h in (0, 1):
                    stat_tile = jnp.where(lane == 2 * qt + h, row_max[u][h * m:(h + 1) * m], stat_tile)
                    stat_tile = jnp.where(lane == B_HEADS + 2 * qt + h, denom[u][h], stat_tile)
        if with_sink:
            normed = _rms_norm(jnp.concatenate(pairs, axis=1), gain_ref[...])
            pairs = [normed[:, c * LANES:(c + 1) * LANES] for c in range(n_q)]
        else:
            o_ref[s, n_q, rows, :] = stat_tile
        for c in range(n_q):
            o_ref[s, c, rows, :] = pairs[c]


def _band_window(m, nk, n):
    nb = n // m
    starts = np.clip(np.arange(nb) * m - (nk - m) // 2, 0, n - nk)
    offs = [int(o) for o in starts - np.arange(nb) * m]
    uniq = sorted(set(offs), reverse=True)
    var = [uniq.index(o) for o in offs]
    assert all(v == var[1] for v in var[1:-1])
    return uniq, var


def _band_bias(rel_bias_h, half, dil, m, nk, n):
    uniq, _ = _band_window(m, nk, n)
    n_heads = rel_bias_h.shape[1]
    span = nk + m - 1
    out = []
    for off in uniq:
        rel = off - (m - 1) + np.arange(span)
        onehot = np.zeros((span, N_BUCKETS), np.float32)
        onehot[np.arange(span), _t5_bucket(dil * rel)] = 1.0
        table = jnp.dot(jnp.asarray(onehot), rel_bias_h.astype(F32), precision=lax.Precision.HIGHEST)
        table = jnp.where((np.abs(rel) <= half)[:, None], table, NEG)
        u = jnp.concatenate([table.T, jnp.zeros((n_heads, 1), F32)], axis=1)
        flat = jnp.tile(u, (1, m + 1))[:, m - 1:m - 1 + m * span]
        b = flat.reshape(n_heads, m, span)[:, :, :nk]
        out.append(b.reshape(n_heads // 2, 2 * m, nk))
    return jnp.stack(out) * LOG2E


def _band_attention(qkv, bias, *, nk, sub, kv_of, q_width, kv_width, out_tiles, sink=None, gain=None):
    Bd, n, _ = qkv.shape
    m = TM_ATT
    nb = n // m
    qb = min(nb, ATT_BLOCKS)
    assert nb % qb == 0 and nb * m == n and Bd % sub == 0
    _, var = _band_window(m, nk, n)
    variants = (var[0], var[min(1, nb - 1)], var[-1])
    kcol = q_width // kv_width
    with_sink = sink is not None
    in_specs = [
        pl.BlockSpec((sub, qb * m, q_width), lambda b, j: (b, j, 0)),
        pl.BlockSpec((sub, n, kv_width), lambda b, j: (b, 0, kcol)),
        pl.BlockSpec((sub, n, kv_width), lambda b, j: (b, 0, kcol + 1)),
        pl.BlockSpec(bias.shape, lambda b, j: (0, 0, 0, 0)),
    ]
    args = [qkv, qkv, qkv, bias]
    if with_sink:
        in_specs = [pl.BlockSpec(memory_space=pltpu.SMEM),
                    pl.BlockSpec((1, q_width), lambda b, j: (0, 0))] + in_specs
        args = [sink, gain] + args
    return pl.pallas_call(
        functools.partial(_band_attn_kernel, m=m, nk=nk, n=n, sub=sub, qb=qb, kv_of=kv_of, variants=variants,
                          with_sink=with_sink),
        grid=(Bd // sub, nb // qb),
        in_specs=in_specs,
        out_specs=pl.BlockSpec((sub, out_tiles, qb * m, LANES), lambda b, j: (b, 0, j, 0)),
        out_shape=jax.ShapeDtypeStruct((Bd, out_tiles, n, LANES), F32),
        compiler_params=_cparams("parallel", "arbitrary"),
        name="band_attn_sink" if with_sink else f"band_attn_n{n}",
    )(*args)


def _merge_rows(ya_ref, o1_ref, o4_ref, o16_ref, x_ref, gin_ref, bin_ref, gb_ref, w_ref, g1_ref, b1_ref,
                s16_scr, x_scr):
    n4 = TM_MERGE // MERGE_STRIDE
    n16 = TM_MERGE // 16
    h0_rows = _layer_norm(x_ref[...], gin_ref[...], bin_ref[...])
    for c in range(D_MODEL // LANES):
        x_scr[c] = h0_rows[:, c * LANES:(c + 1) * LANES]
    for r16 in range(16):
        for c in range(B_OUT // LANES):
            s16_scr[c, pl.ds((r16 % 4) * n4 + r16 // 4, n16, stride=4), :] = o16_ref[0, r16, c]

    def natural(ref, c):
        return jnp.concatenate([ref[0, c, pl.ds(r, n4, stride=MERGE_STRIDE), :]
                                for r in range(MERGE_STRIDE)], axis=0)

    def stride4(c):
        return jnp.concatenate([o4_ref[0, r, c] for r in range(MERGE_STRIDE)], axis=0)

    def stride16(c):
        return s16_scr[c]

    n_val = B_WIDTH // LANES
    stats = (natural(o1_ref, n_val), stride4(n_val), stride16(n_val))
    dens = [pltpu.roll(s, LANES - B_HEADS, axis=1) for s in stats]
    mx = jnp.maximum(jnp.maximum(stats[0], stats[1]), stats[2])
    ex = [jnp.exp2(s - mx) for s in stats]
    inv = 1.0 / (ex[0] * dens[0] + ex[1] * dens[1] + ex[2] * dens[2])
    head_lane = lax.broadcasted_iota(jnp.int32, (TM_MERGE, LANES), 1) < B_HEADS
    head = lax.broadcasted_iota(jnp.int32, (LANES, B_WIDTH), 0)
    col = lax.broadcasted_iota(jnp.int32, (LANES, B_WIDTH), 1)
    spread = jnp.where(col // HEAD_DIM == head, 1.0, 0.0).astype(BF16)
    wide = []
    for e in ex:
        w = jnp.where(head_lane, e * inv, 0.0)
        w_hi = w.astype(BF16)
        w_lo = (w - w_hi.astype(F32)).astype(BF16)
        wide.append(jnp.dot(w_hi, spread, preferred_element_type=F32)
                    + jnp.dot(w_lo, spread, preferred_element_type=F32))
    pieces = []
    for c in range(n_val):
        cs = slice(c * LANES, (c + 1) * LANES)
        outs = (natural(o1_ref, c), stride4(c), stride16(c))
        pieces.append(wide[0][:, cs] * outs[0] + wide[1][:, cs] * outs[1] + wide[2][:, cs] * outs[2])
    yb = _rms_norm(jnp.concatenate(pieces, axis=1), gb_ref[...])
    ya = jnp.concatenate([natural(ya_ref, c) for c in range(A_WIDTH // LANES)], axis=1)
    y = jnp.concatenate([ya.astype(BF16), yb.astype(BF16)], axis=1)
    mix = jnp.dot(y, w_ref[...], preferred_element_type=F32)
    h0 = jnp.concatenate(
        [jnp.concatenate([x_scr[c, pl.ds(r, n4, stride=MERGE_STRIDE), :] for r in range(MERGE_STRIDE)], axis=0)
         for c in range(D_MODEL // LANES)], axis=1)
    return _layer_norm(ALPHA * h0 + mix, g1_ref[...], b1_ref[...])


def _token_of_row(p):
    n4 = TM_MERGE // MERGE_STRIDE
    tile, rest = p // TM_MERGE, p % TM_MERGE
    return tile * TM_MERGE + (rest % n4) * MERGE_STRIDE + rest // n4


def _mem_kv_kernel(mem_ref, w_ref, k_ref, v_ref):
    kv = jnp.dot(mem_ref[0].astype(BF16), w_ref[...], preferred_element_type=F32)
    k_ref[0] = kv[:, :D_MODEL].astype(BF16)
    v_ref[0] = kv[:, D_MODEL:].astype(BF16)


def _mem_kv(mem, xkv_b):
    B = mem.shape[0]
    blk = pl.BlockSpec((1, MEM_LEN, D_MODEL), lambda b: (b, 0, 0))
    return pl.pallas_call(
        _mem_kv_kernel,
        grid=(B,),
        in_specs=[blk, pl.BlockSpec((D_MODEL, 2 * D_MODEL), lambda b: (0, 0))],
        out_specs=[blk, blk],
        out_shape=[jax.ShapeDtypeStruct((B, MEM_LEN, D_MODEL), BF16)] * 2,
        compiler_params=_cparams("parallel"),
        name="mem_kv_proj",
    )(mem, xkv_b)


def _route(logits):
    rows = logits.shape[0]
    lane = lax.broadcasted_iota(jnp.int32, (rows, LANES), 1).astype(F32)
    big = float(LANES)
    ninf = -jnp.inf
    gl = jnp.where(lane < N_GROUPS, logits, ninf)
    gmax = jnp.max(gl, axis=-1, keepdims=True)
    gidx = jnp.min(jnp.where(gl == gmax, lane, big), axis=-1, keepdims=True)
    g_p = 1.0 / jnp.sum(jnp.exp(gl - gmax), axis=-1, keepdims=True)
    lo_lane = N_GROUPS + EXPERTS_PER_GROUP * gidx
    el = jnp.where((lane >= lo_lane) & (lane < lo_lane + EXPERTS_PER_GROUP), logits, ninf)
    v1 = jnp.max(el, axis=-1, keepdims=True)
    i1 = jnp.min(jnp.where(el == v1, lane, big), axis=-1, keepdims=True)
    el2 = jnp.where(lane == i1, ninf, el)
    v2 = jnp.max(el2, axis=-1, keepdims=True)
    i2 = jnp.min(jnp.where(el2 == v2, lane, big), axis=-1, keepdims=True)
    t = jnp.exp(v2 - v1)
    w1 = g_p / (1.0 + t)
    w2 = g_p * t / (1.0 + t)
    a = jnp.minimum(i1, i2) - lo_lane
    b = jnp.maximum(i1, i2) - lo_lane
    pair = a * (2 * EXPERTS_PER_GROUP - 1 - a) * 0.5 + (b - a - 1.0)
    cls = gidx * PAIRS_PER_GROUP + pair
    w_lo = jnp.where(i1 < i2, w1, w2)
    w_hi = jnp.where(i1 < i2, w2, w1)
    return jnp.where(lane == 0, cls, jnp.where(lane == 1, w_lo, jnp.where(lane == 2, w_hi, 0.0)))


def _merge_xattn_kernel(ya_ref, o1_ref, o4_ref, o16_ref, x_ref, gin_ref, bin_ref, gb_ref, w_ref, g1_ref, b1_ref,
                        k_ref, v_ref, wq_ref, wo_ref, g2_ref, b2_ref, wr_ref, br_ref, xr_ref, s16_scr, x_scr):
    h1 = _merge_rows(ya_ref, o1_ref, o4_ref, o16_ref, x_ref, gin_ref, bin_ref, gb_ref, w_ref, g1_ref, b1_ref,
                     s16_scr, x_scr)
    q = jnp.dot(h1.astype(BF16), wq_ref[...], preferred_element_type=F32).astype(BF16)
    outs = []
    for hd in range(X_HEADS):
        sl = slice(hd * X_HEAD_DIM, (hd + 1) * X_HEAD_DIM)
        s = lax.dot_general(q[:, sl], k_ref[0, :, sl], (((1,), (1,)), ((), ())), preferred_element_type=F32)
        m = jnp.max(s, axis=-1, keepdims=True)
        p = jnp.exp2(s - m)
        l = jnp.sum(p, axis=-1, keepdims=True)
        o = jnp.dot(p.astype(BF16), v_ref[0, :, sl], preferred_element_type=F32) / l
        outs.append(o.astype(BF16))
    xa = jnp.dot(jnp.concatenate(outs, axis=1), wo_ref[...], preferred_element_type=F32)
    h2 = _layer_norm(ALPHA * h1 + xa, g2_ref[...], b2_ref[...])
    h_hi = h2.astype(BF16)
    h_lo = (h2 - h_hi.astype(F32)).astype(BF16)
    t_hi = jnp.dot(h_hi, wr_ref[...], preferred_element_type=F32)
    t_lo = jnp.dot(h_lo, wr_ref[...], preferred_element_type=F32)
    logits = (t_hi[:, :LANES] + t_hi[:, LANES:]) + (t_lo[:, :LANES] + t_lo[:, LANES:]) + br_ref[...]
    xr_ref[:, :D_MODEL] = h2
    xr_ref[:, D_MODEL:] = _route(logits)


def _merge_cross_attention_route(ya, o1, o4, o16, x2, ln_in_g, ln_in_b, gain_b, w_out_b, ln1_g, ln1_b,
                                 k, v, xq_b, xo_b, ln2_g, ln2_b, wr_split, br):
    T = x2.shape[0]
    tiles_per_seq = SEQ // TM_MERGE
    row = lambda i: (i, 0)
    const = lambda i: (0, 0)
    nat = lambda i: (i // tiles_per_seq, 0, i % tiles_per_seq, 0)
    deint = lambda i: (i // tiles_per_seq, 0, 0, i % tiles_per_seq, 0)
    vec_d = pl.BlockSpec((1, D_MODEL), const)
    mat_d = pl.BlockSpec((D_MODEL, D_MODEL), const)
    kv_blk = pl.BlockSpec((1, MEM_LEN, D_MODEL), lambda i: (i // tiles_per_seq, 0, 0))
    n_b = B_OUT // LANES
    return pl.pallas_call(
        _merge_xattn_kernel,
        grid=(T // TM_MERGE,),
        in_specs=[
            pl.BlockSpec((1, A_WIDTH // LANES, TM_MERGE, LANES), nat),
            pl.BlockSpec((1, n_b, TM_MERGE, LANES), nat),
            pl.BlockSpec((1, 4, n_b, TM_MERGE // 4, LANES), deint),
            pl.BlockSpec((1, 16, n_b, TM_MERGE // 16, LANES), deint),
            pl.BlockSpec((TM_MERGE, D_MODEL), row), vec_d, vec_d,
            pl.BlockSpec((1, B_WIDTH), const), mat_d, vec_d, vec_d,
            kv_blk, kv_blk, mat_d, mat_d, vec_d, vec_d,
            pl.BlockSpec((D_MODEL, 2 * LANES), const),
            pl.BlockSpec((1, LANES), const),
        ],
        out_specs=pl.BlockSpec((TM_MERGE, XR_WIDTH), row),
        out_shape=jax.ShapeDtypeStruct((T, XR_WIDTH), F32),
        scratch_shapes=[pltpu.VMEM((n_b, TM_MERGE, LANES), F32),
                        pltpu.VMEM((D_MODEL // LANES, TM_MERGE, LANES), F32)],
        compiler_params=_cparams("parallel"),
        name="merge_xattn_route",
    )(ya, o1, o4, o16, x2, ln_in_g, ln_in_b, gain_b, w_out_b, ln1_g, ln1_b,
      k, v, xq_b, xo_b, ln2_g, ln2_b, wr_split, br)


def _expert_kernel(src_ref, dst_ref, cnt_ref, elo_ref, ehi_ref, ntile_ref, x_hbm,
                   wg_lo, wu_lo, wd_lo, wg_hi, wu_hi, wd_hi, g3_ref, b3_ref, o_hbm,
                   xbuf, obuf, gsem, ssem):
    j = pl.program_id(0)
    n_tiles = ntile_ref[0]
    slot = j % 2

    def row_copy(t, i, s, gather):
        row = pl.ds(i, 1)
        if gather:
            return pltpu.make_async_copy(x_hbm.at[pl.ds(src_ref[t * TM_MOE + i], 1)], xbuf.at[s, row], gsem.at[s])
        return pltpu.make_async_copy(obuf.at[s, row], o_hbm.at[pl.ds(dst_ref[t * TM_MOE + i], 1)], ssem.at[s])

    def n_rows(t, gather):
        cnt = cnt_ref[t]
        return (cnt + ROW_GROUP - 1) // ROW_GROUP * ROW_GROUP if gather else cnt

    def start_rows(t, s, gather):
        cnt = cnt_ref[t]
        n_groups = (cnt + ROW_GROUP - 1) // ROW_GROUP if gather else cnt // ROW_GROUP

        def group(g, c):
            base = pl.multiple_of(g * ROW_GROUP, ROW_GROUP)
            for r in range(ROW_GROUP):
                row_copy(t, base + r, s, gather).start(priority=0 if gather else 1)
            return c
        lax.fori_loop(0, n_groups, group, 0)
        if not gather:
            def single(i, c):
                row_copy(t, i, s, gather).start(priority=1)
                return c
            lax.fori_loop(n_groups * ROW_GROUP, cnt, single, 0)

    def wait_rows(t, s, gather):
        cnt = n_rows(t, gather)
        for bit in range(TM_MOE.bit_length()):
            rows = pl.ds(0, 1 << bit)

            @pl.when((cnt >> bit) & 1 == 1)
            def _():
                if gather:
                    pltpu.make_async_copy(x_hbm.at[rows], xbuf.at[s, rows], gsem.at[s]).wait()
                else:
                    pltpu.make_async_copy(obuf.at[s, rows], o_hbm.at[rows], ssem.at[s]).wait()

    @pl.when(j == 0)
    def _():
        xbuf[...] = jnp.zeros_like(xbuf)
        start_rows(0, 0, True)

    @pl.when(j < n_tiles)
    def _():
        @pl.when(j + 1 < n_tiles)
        def _():
            start_rows(j + 1, 1 - slot, True)

        wait_rows(j, slot, True)
        x = xbuf[slot, :, :D_MODEL]
        xb = x.astype(BF16)
        y = jnp.zeros_like(x)
        for e, (wg, wu, wd) in enumerate(((wg_lo, wu_lo, wd_lo), (wg_hi, wu_hi, wd_hi))):
            gate = xbuf[slot, :, D_MODEL + 1 + e:D_MODEL + 2 + e]
            a = jnp.dot(xb, wg[0], preferred_element_type=F32)
            u = jnp.dot(xb, wu[0], preferred_element_type=F32)
            hid = a * jax.nn.sigmoid(a) * u
            y = y + jnp.dot((gate * hid).astype(BF16), wd[0], preferred_element_type=F32)
        out = _layer_norm(ALPHA * x + y, g3_ref[...], b3_ref[...])

        @pl.when(j >= 2)
        def _():
            wait_rows(j - 2, slot, False)

        obuf[slot] = out
        start_rows(j, slot, False)

        @pl.when(j == n_tiles - 1)
        def _():
            @pl.when(j >= 1)
            def _():
                wait_rows(j - 1, 1 - slot, False)

            wait_rows(j, slot, False)


def _expert_mlp(src_rows, dst_rows, tile_cnt, tile_elo, tile_ehi, n_tiles, xr, wg, wu, wd, ln_g, ln_b):
    n_tiles_max = tile_cnt.shape[0]
    const = lambda j, src, dst, cnt, elo, ehi, nt: (0, 0)
    lo = lambda j, src, dst, cnt, elo, ehi, nt: (elo[j], 0, 0)
    hi = lambda j, src, dst, cnt, elo, ehi, nt: (ehi[j], 0, 0)
    up = (1, D_MODEL, D_EXPERT)
    down = (1, D_EXPERT, D_MODEL)
    any_spec = pl.BlockSpec(memory_space=pl.ANY)
    return pl.pallas_call(
        _expert_kernel,
        grid_spec=pltpu.PrefetchScalarGridSpec(
            num_scalar_prefetch=6,
            grid=(n_tiles_max,),
            in_specs=[
                any_spec,
                pl.BlockSpec(up, lo), pl.BlockSpec(up, lo), pl.BlockSpec(down, lo),
                pl.BlockSpec(up, hi), pl.BlockSpec(up, hi), pl.BlockSpec(down, hi),
                pl.BlockSpec((1, D_MODEL), const),
                pl.BlockSpec((1, D_MODEL), const),
            ],
            out_specs=any_spec,
            scratch_shapes=[
                pltpu.VMEM((2, TM_MOE, XR_WIDTH), F32),
                pltpu.VMEM((2, TM_MOE, D_MODEL), F32),
                pltpu.SemaphoreType.DMA((2,)),
                pltpu.SemaphoreType.DMA((2,)),
            ],
        ),
        out_shape=jax.ShapeDtypeStruct((xr.shape[0], D_MODEL), F32),
        compiler_params=_cparams("arbitrary"),
        name="moe_experts",
    )(src_rows, dst_rows, tile_cnt, tile_elo, tile_ehi, n_tiles, xr, wg, wu, wd, wg, wu, wd, ln_g, ln_b)


def _class_experts():
    lo, hi = [], []
    for g in range(N_GROUPS):
        for a in range(EXPERTS_PER_GROUP):
            for b in range(a + 1, EXPERTS_PER_GROUP):
                lo.append(g * EXPERTS_PER_GROUP + a)
                hi.append(g * EXPERTS_PER_GROUP + b)
    return np.asarray(lo, np.int32), np.asarray(hi, np.int32)


def _plan_kernel(route_ref, pos_ref, counts_ref, tri_scr, run_scr, start_scr):
    phase, i = pl.program_id(0), pl.program_id(1)
    lane = lax.broadcasted_iota(jnp.int32, (TM_PLAN, LANES), 1)
    onehot = lane.astype(F32) == route_ref[:, 0:1]
    onehot_f = jnp.where(onehot, 1.0, 0.0)

    @pl.when((phase == 0) & (i == 0))
    def _():
        run_scr[...] = jnp.zeros_like(run_scr)
        r = lax.broadcasted_iota(jnp.int32, (TM_PLAN, TM_PLAN), 0)
        c = lax.broadcasted_iota(jnp.int32, (TM_PLAN, TM_PLAN), 1)
        tri_scr[...] = jnp.where(c < r, 1.0, 0.0).astype(BF16)

    @pl.when((phase == 1) & (i == 0))
    def _():
        counts = run_scr[...]
        counts_ref[...] = counts
        tiles = jnp.floor((counts + (TM_MOE - 0.5)) * (1.0 / TM_MOE))
        lane_row = lax.broadcasted_iota(jnp.int32, (1, LANES), 1)
        scan = tiles
        shift = 1
        while shift < LANES:
            scan = scan + jnp.where(lane_row >= shift, pltpu.roll(scan, shift, axis=1), 0.0)
            shift *= 2
        start_scr[...] = (scan - tiles) * TM_MOE
        run_scr[...] = jnp.zeros_like(run_scr)

    @pl.when(phase == 1)
    def _():
        before = jnp.dot(tri_scr[...], onehot_f.astype(BF16), preferred_element_type=F32)
        pos_col = jnp.sum(onehot_f * (before + run_scr[...] + start_scr[...]), axis=1, keepdims=True)
        eye = (lax.broadcasted_iota(jnp.int32, (LANES, LANES), 0)
               == lax.broadcasted_iota(jnp.int32, (LANES, LANES), 1))
        for r in range(TM_PLAN // LANES):
            row = jnp.sum(jnp.where(eye, pos_col[r * LANES:(r + 1) * LANES], 0.0), axis=0, keepdims=True)
            pos_ref[r:r + 1, :] = row.astype(jnp.int32)

    run_scr[...] = run_scr[...] + jnp.sum(onehot_f, axis=0, keepdims=True)


def _plan_positions(xr):
    T = xr.shape[0]
    n_steps = T // TM_PLAN
    pos, counts = pl.pallas_call(
        _plan_kernel,
        grid=(2, n_steps),
        in_specs=[pl.BlockSpec((TM_PLAN, LANES), lambda p, i: (i, D_MODEL // LANES))],
        out_specs=[pl.BlockSpec((TM_PLAN // LANES, LANES), lambda p, i: (i * p, 0)),
                   pl.BlockSpec((1, LANES), lambda p, i: (0, 0))],
        out_shape=[jax.ShapeDtypeStruct((T // LANES, LANES), jnp.int32),
                   jax.ShapeDtypeStruct((1, LANES), F32)],
        scratch_shapes=[pltpu.VMEM((TM_PLAN, TM_PLAN), BF16),
                        pltpu.VMEM((1, LANES), F32),
                        pltpu.VMEM((1, LANES), F32)],
        compiler_params=_cparams("arbitrary", "arbitrary"),
        name="moe_plan",
    )(xr)
    return pos.reshape(T), counts[0, :N_CLASSES].astype(jnp.int32)


def _moe_plan(xr, n_tiles_max):
    T = xr.shape[0]
    pos, counts = _plan_positions(xr)
    tiles_per_class = (counts + TM_MOE - 1) // TM_MOE
    tile_end = jnp.cumsum(tiles_per_class)
    tile_start = tile_end - tiles_per_class
    n_tiles = tile_end[-1]
    tok = jnp.arange(T, dtype=jnp.int32)
    src_rows = jnp.zeros((n_tiles_max * TM_MOE,), jnp.int32).at[pos].set(tok)
    tile_ids = jnp.arange(n_tiles_max, dtype=jnp.int32)
    used = jnp.minimum(tile_ids, n_tiles - 1)
    tile_cls = jnp.sum((tile_end[None, :] <= used[:, None]).astype(jnp.int32), axis=1)
    tile_cls = jnp.minimum(tile_cls, N_CLASSES - 1)
    cls_lo, cls_hi = _class_experts()
    tile_elo = jnp.asarray(cls_lo)[tile_cls]
    tile_ehi = jnp.asarray(cls_hi)[tile_cls]
    within = tile_ids - tile_start[tile_cls]
    tile_cnt = jnp.clip(counts[tile_cls] - within * TM_MOE, 0, TM_MOE)
    tile_cnt = jnp.where(tile_ids < n_tiles, tile_cnt, 0).astype(jnp.int32)
    return src_rows, tile_elo, tile_ehi, tile_cnt, n_tiles.reshape(1).astype(jnp.int32)


def _vec(a):
    return a.reshape(1, -1).astype(F32)


def _mixer_and_cross_attention(x, mem, ln_in_g, ln_in_b, w_in, rel_bias, sink_a, norm_a_g, norm_b_g, w_out,
                               ln1_g, ln1_b, xq, xkv, xo, ln2_g, ln2_b, w_group, b_group, w_router, b_router):
    B, S, D = x.shape
    assert S == SEQ and D == D_MODEL and mem.shape[1:] == (MEM_LEN, D_MODEL)
    T = B * S
    vec = _vec

    w = w_in[0]
    edges = np.cumsum((0, A_WIDTH, A_KV_HEADS * HEAD_DIM, A_KV_HEADS * HEAD_DIM, B_WIDTH, B_WIDTH, B_WIDTH))
    qa, ka, va, qb, kb, vb = [w[:, a:b] for a, b in zip(edges[:-1], edges[1:])]
    dup = lambda t: jnp.repeat(t.reshape(D, A_KV_HEADS, 1, HEAD_DIM), 2, axis=2).reshape(D, A_KV_TILES * LANES)
    scale = HEAD_DIM ** -0.5 * LOG2E
    w_in_b = jnp.concatenate([qa * scale, dup(ka), dup(va), qb * scale, kb, vb], axis=1).astype(BF16)
    w_out_b = w_out[0].astype(BF16)
    xq_b = (xq[0] * (X_HEAD_DIM ** -0.5 * LOG2E)).astype(BF16)
    xkv_b = xkv[0].astype(BF16)
    xo_b = xo[0].astype(BF16)
    wr = jnp.concatenate([w_group[0], w_router[0]], axis=1).astype(F32)
    wr = jnp.pad(wr, ((0, 0), (0, LANES - wr.shape[1])))
    wr_hi = wr.astype(BF16)
    wr_split = jnp.concatenate([wr_hi, (wr - wr_hi.astype(F32)).astype(BF16)], axis=1)
    br = jnp.concatenate([b_group[0], b_router[0]]).astype(F32)
    br = jnp.pad(br, (0, LANES - br.shape[0])).reshape(1, LANES)

    x2 = x.reshape(T, D)
    qkv_a, qkv_b1, qkv_b4, qkv_b16 = _input_projection(x2, vec(ln_in_g), vec(ln_in_b), w_in_b, B)

    nk_a = TM_ATT + 2 * A_HALF_WIN
    bias_a = _band_bias(rel_bias[:, :A_HEADS], A_HALF_WIN, 1, TM_ATT, nk_a, S)
    ya = _band_attention(qkv_a.reshape(B, S, QKV_A), bias_a, nk=nk_a, sub=1, kv_of=(0, 0, 1, 1),
                         q_width=A_WIDTH, kv_width=A_KV_TILES * LANES, out_tiles=A_WIDTH // LANES,
                         sink=sink_a[0].astype(F32) * LOG2E, gain=vec(norm_a_g[0]))
    branch_out = []
    for (win, dil), qkv in zip(B_BRANCHES, (qkv_b1.reshape(B, S, QKV_B),
                                             qkv_b4.reshape(B * 4, S // 4, QKV_B),
                                             qkv_b16.reshape(B * 16, S // 16, QKV_B))):
        half = (win // 2) // dil
        n = S // dil
        nk = min(TM_ATT + 2 * half, n)
        bias_b = _band_bias(rel_bias[:, A_HEADS:], half, dil, TM_ATT, nk, n)
        branch_out.append(_band_attention(qkv, bias_b, nk=nk, sub=max(1, TM_ATT * 8 // n), kv_of=(0, 1, 2, 3),
                                          q_width=B_WIDTH, kv_width=B_WIDTH, out_tiles=B_OUT // LANES))
    o1 = branch_out[0]
    o4 = branch_out[1].reshape(B, 4, B_OUT // LANES, S // 4, LANES)
    o16 = branch_out[2].reshape(B, 16, B_OUT // LANES, S // 16, LANES)

    k_mem, v_mem = _mem_kv(mem, xkv_b)
    return _merge_cross_attention_route(
        ya, o1, o4, o16, x2, vec(ln_in_g), vec(ln_in_b), vec(norm_b_g[0]), w_out_b, vec(ln1_g[0]), vec(ln1_b[0]),
        k_mem, v_mem, xq_b, xo_b, vec(ln2_g[0]), vec(ln2_b[0]), wr_split, br)


def _moe(xr, w_gate, w_up, w_down, ln3_g, ln3_b):
    T = xr.shape[0]
    wg_b = w_gate[0].reshape(N_EXPERTS, D_MODEL, D_EXPERT).astype(BF16)
    wu_b = w_up[0].reshape(N_EXPERTS, D_MODEL, D_EXPERT).astype(BF16)
    wd_b = w_down[0].reshape(N_EXPERTS, D_EXPERT, D_MODEL).astype(BF16)
    n_tiles_max = -(-T // TM_MOE) + N_CLASSES
    src_rows, tile_elo, tile_ehi, tile_cnt, n_tiles = _moe_plan(xr, n_tiles_max)
    return _expert_mlp(src_rows, _token_of_row(src_rows), tile_cnt, tile_elo, tile_ehi, n_tiles, xr, wg_b, wu_b, wd_b,
                       _vec(ln3_g[0]), _vec(ln3_b[0]))


def kernel(x, mem, ln_in_g, ln_in_b, w_in, rel_bias, sink_a, norm_a_g, norm_b_g, w_out,
           ln1_g, ln1_b, xq, xkv, xo, ln2_g, ln2_b, w_group, b_group, w_router, b_router,
           w_gate, w_up, w_down, ln3_g, ln3_b):
    xr = _mixer_and_cross_attention(
        x, mem, ln_in_g, ln_in_b, w_in, rel_bias, sink_a, norm_a_g, norm_b_g, w_out,
        ln1_g, ln1_b, xq, xkv, xo, ln2_g, ln2_b, w_group, b_group, w_router, b_router)
    return _moe(xr, w_gate, w_up, w_down, ln3_g, ln3_b).reshape(x.shape)
```

```python
import functools

import numpy as np
import jax
import jax.numpy as jnp
from jax import lax
from jax.experimental import pallas as pl
from jax.experimental.pallas import tpu as pltpu

F32 = jnp.float32
BF16 = jnp.bfloat16

D_MODEL = 1024
SEQ = 2048
MEM_LEN = 256
HEAD_DIM = 64
A_HEADS = 8
A_KV_HEADS = 2
A_HALF_WIN = 128
B_HEADS = 8
B_BRANCHES = ((128, 1), (512, 4), (2048, 16))
N_BUCKETS = 32
MAX_DISTANCE = 1024
X_HEADS = 4
X_HEAD_DIM = D_MODEL // X_HEADS
N_GROUPS = 4
EXPERTS_PER_GROUP = 8
N_EXPERTS = N_GROUPS * EXPERTS_PER_GROUP
D_EXPERT = 512
DEPTH = 1
ALPHA = (2.0 * DEPTH) ** 0.25
LN_EPS = 1e-5
NEG = -1e30
LOG2E = 1.4426950408889634

LANES = 128
XR_WIDTH = D_MODEL + LANES
A_WIDTH = A_HEADS * HEAD_DIM
B_WIDTH = B_HEADS * HEAD_DIM
A_KV_TILES = A_KV_HEADS
QKV_A = A_WIDTH + 2 * A_KV_TILES * LANES
QKV_B = 3 * B_WIDTH
B_OUT = B_WIDTH + LANES
MERGE_STRIDE = 4

PAIRS_PER_GROUP = EXPERTS_PER_GROUP * (EXPERTS_PER_GROUP - 1) // 2
N_CLASSES = N_GROUPS * PAIRS_PER_GROUP

TM_IN = 512
TM_ATT = 128
ATT_BLOCKS = 4
TM_MERGE = 512
TM_MOE = 320
ROW_GROUP = 8
TM_PLAN = 1024
VMEM_LIMIT = 56 * 1024 * 1024


def _cparams(*sem):
    return pltpu.CompilerParams(dimension_semantics=sem, vmem_limit_bytes=VMEM_LIMIT)


def _layer_norm(x, g, b):
    mu = jnp.mean(x, axis=-1, keepdims=True)
    xc = x - mu
    var = jnp.mean(xc * xc, axis=-1, keepdims=True)
    return xc * lax.rsqrt(var + LN_EPS) * g + b


def _rms_norm(x, g):
    return x * lax.rsqrt(jnp.mean(x * x, axis=-1, keepdims=True) + LN_EPS) * g


def _t5_bucket(rel):
    nb = N_BUCKETS // 2
    max_exact = nb // 2
    ret = (rel > 0).astype(np.int32) * nb
    n = np.abs(rel)
    n_safe = np.maximum(n, 1).astype(np.float64)
    large = max_exact + (np.log(n_safe / max_exact) / np.log(MAX_DISTANCE / max_exact)
                         * (nb - max_exact)).astype(np.int32)
    large = np.minimum(large, nb - 1)
    return (ret + np.where(n < max_exact, n, large)).astype(np.int32)


def _inproj_kernel(x_ref, g_ref, b_ref, w_ref, qa_ref, qb1_ref, qb4_ref, qb16_ref, pb_scr, p4_scr):
    h = _layer_norm(x_ref[...], g_ref[...], b_ref[...])
    proj = jnp.dot(h.astype(BF16), w_ref[...], preferred_element_type=F32)
    qa_ref[...] = proj[:, :QKV_A].astype(BF16)
    pb = proj[:, QKV_A:]
    qb1_ref[...] = pb.astype(BF16)
    n4 = TM_IN // 4
    for c in range(QKV_B // LANES):
        cs = slice(c * LANES, (c + 1) * LANES)
        pb_scr[c] = pb[:, cs]
        for r4 in range(4):
            rows = pb_scr[c, pl.ds(r4, n4, stride=4), :]
            qb4_ref[0, r4, :, cs] = rows.astype(BF16)
            p4_scr[c, r4 * n4:(r4 + 1) * n4, :] = rows
        for r16 in range(16):
            rows = p4_scr[c, pl.ds((r16 % 4) * n4 + r16 // 4, TM_IN // 16, stride=4), :]
            qb16_ref[0, r16, :, cs] = rows.astype(BF16)


def _input_projection(x2, ln_g, ln_b, w_in_b, batch):
    T = x2.shape[0]
    tiles_per_seq = SEQ // TM_IN
    row = lambda i: (i, 0)
    const = lambda i: (0, 0)
    deint = lambda i: (i // tiles_per_seq, 0, i % tiles_per_seq, 0)
    return pl.pallas_call(
        _inproj_kernel,
        grid=(T // TM_IN,),
        in_specs=[
            pl.BlockSpec((TM_IN, D_MODEL), row),
            pl.BlockSpec((1, D_MODEL), const),
            pl.BlockSpec((1, D_MODEL), const),
            pl.BlockSpec((D_MODEL, QKV_A + QKV_B), const),
        ],
        out_specs=[
            pl.BlockSpec((TM_IN, QKV_A), row),
            pl.BlockSpec((TM_IN, QKV_B), row),
            pl.BlockSpec((1, 4, TM_IN // 4, QKV_B), deint),
            pl.BlockSpec((1, 16, TM_IN // 16, QKV_B), deint),
        ],
        out_shape=[
            jax.ShapeDtypeStruct((T, QKV_A), BF16),
            jax.ShapeDtypeStruct((T, QKV_B), BF16),
            jax.ShapeDtypeStruct((batch, 4, SEQ // 4, QKV_B), BF16),
            jax.ShapeDtypeStruct((batch, 16, SEQ // 16, QKV_B), BF16),
        ],
        scratch_shapes=[pltpu.VMEM((QKV_B // LANES, TM_IN, LANES), F32)] * 2,
        compiler_params=_cparams("parallel"),
        name="ln_in_proj",
    )(x2, ln_g, ln_b, w_in_b)


def _band_attn_kernel(*refs, m, nk, n, sub, qb, kv_of, variants, with_sink):
    if with_sink:
        sink_ref, gain_ref = refs[0], refs[1]
        refs = refs[2:]
    q_ref, k_ref, v_ref, bias_ref, o_ref = refs
    nb = n // m
    first, middle, last = variants
    starts, bias_var = [], []
    for b in range(qb):
        if nk == n:
            starts.append(0)
            bias_var.append(first)
        else:
            jj = pl.program_id(1) * qb + b
            starts.append(pl.multiple_of(jnp.clip(jj * m - (nk - m) // 2, 0, n - nk), HEAD_DIM))
            bias_var.append(jnp.where(jj == 0, first, jnp.where(jj == nb - 1, last, middle)))
    lane_row = lax.broadcasted_iota(jnp.int32, (1, LANES), 1)
    keep_lo = jnp.where(lane_row < HEAD_DIM, 1.0, 0.0).astype(BF16)
    keep_hi = jnp.where(lane_row < HEAD_DIM, 0.0, 1.0).astype(BF16)
    lane = lax.broadcasted_iota(jnp.int32, (m, LANES), 1)
    n_q = len(kv_of)
    units = [(s, b, qt) for s in range(sub) for b in range(qb) for qt in range(n_q)]

    scores = []
    for s, b, qt in units:
        t = kv_of[qt]
        k_t = k_ref[s, pl.ds(starts[b], nk), t * LANES:(t + 1) * LANES]
        q2 = q_ref[s, b * m:(b + 1) * m, qt * LANES:(qt + 1) * LANES]
        lhs = jnp.concatenate([q2 * keep_lo, q2 * keep_hi], axis=0)
        sc = lax.dot_general(lhs, k_t, (((1,), (1,)), ((), ())), preferred_element_type=F32)
        scores.append(sc + bias_ref[bias_var[b], qt])

    probs, denom, row_max = [], [], []
    for (s, b, qt), sc in zip(units, scores):
        mx = jnp.max(sc, axis=-1, keepdims=True)
        p = jnp.exp2(sc - mx)
        l = jnp.sum(p, axis=-1, keepdims=True)
        halves = []
        for h in (0, 1):
            l_h = l[h * m:(h + 1) * m]
            if with_sink:
                l_h = l_h + jnp.exp2(sink_ref[2 * qt + h] - mx[h * m:(h + 1) * m])
            halves.append(l_h)
        probs.append(p.astype(BF16))
        denom.append(halves)
        row_max.append(mx)

    for s, b in [(s, b) for s in range(sub) for b in range(qb)]:
        rows = slice(b * m, (b + 1) * m)
        pairs = []
        stat_tile = jnp.zeros((m, LANES), F32)
        for qt in range(n_q):
            u = (s * qb + b) * n_q + qt
            t = kv_of[qt]
            v_t = v_ref[s, pl.ds(starts[b], nk), t * LANES:(t + 1) * LANES]
            o = jnp.dot(probs[u], v_t, preferred_element_type=F32)
            if with_sink:
                pairs.append(jnp.where(lane < HEAD_DIM, o[:m] * (1.0 / denom[u][0]), o[m:] * (1.0 / denom[u][1])))
            else:
                pairs.append(jnp.where(lane < HEAD_DIM, o[:m], o[m:]))
                for h in (0, 1):
                    stat_tile = jnp.where(lane == 2 * qt + h, row_max[u][h * m:(h + 1) * m], stat_tile)
                    stat_tile = jnp.where(lane == B_HEADS + 2 * qt + h, denom[u][h], stat_tile)
        if with_sink:
            normed = _rms_norm(jnp.concatenate(pairs, axis=1), gain_ref[...])
            pairs = [normed[:, c * LANES:(c + 1) * LANES] for c in range(n_q)]
        else:
            o_ref[s, n_q, rows, :] = stat_tile
        for c in range(n_q):
            o_ref[s, c, rows, :] = pairs[c]


def _band_window(m, nk, n):
    nb = n // m
    starts = np.clip(np.arange(nb) * m - (nk - m) // 2, 0, n - nk)
    offs = [int(o) for o in starts - np.arange(nb) * m]
    uniq = sorted(set(offs), reverse=True)
    var = [uniq.index(o) for o in offs]
    assert all(v == var[1] for v in var[1:-1])
    return uniq, var


def _band_bias(rel_bias_h, half, dil, m, nk, n):
    uniq, _ = _band_window(m, nk, n)
    n_heads = rel_bias_h.shape[1]
    span = nk + m - 1
    out = []
    for off in uniq:
        rel = off - (m - 1) + np.arange(span)
        onehot = np.zeros((span, N_BUCKETS), np.float32)
        onehot[np.arange(span), _t5_bucket(dil * rel)] = 1.0
        table = jnp.dot(jnp.asarray(onehot), rel_bias_h.astype(F32), precision=lax.Precision.HIGHEST)
        table = jnp.where((np.abs(rel) <= half)[:, None], table, NEG)
        u = jnp.concatenate([table.T, jnp.zeros((n_heads, 1), F32)], axis=1)
        flat = jnp.tile(u, (1, m + 1))[:, m - 1:m - 1 + m * span]
        b = flat.reshape(n_heads, m, span)[:, :, :nk]
        out.append(b.reshape(n_heads // 2, 2 * m, nk))
    return jnp.stack(out) * LOG2E


def _band_attention(qkv, bias, *, nk, sub, kv_of, q_width, kv_width, out_tiles, sink=None, gain=None):
    Bd, n, _ = qkv.shape
    m = TM_ATT
    nb = n // m
    qb = min(nb, ATT_BLOCKS)
    assert nb % qb == 0 and nb * m == n and Bd % sub == 0
    _, var = _band_window(m, nk, n)
    variants = (var[0], var[min(1, nb - 1)], var[-1])
    kcol = q_width // kv_width
    with_sink = sink is not None
    in_specs = [
        pl.BlockSpec((sub, qb * m, q_width), lambda b, j: (b, j, 0)),
        pl.BlockSpec((sub, n, kv_width), lambda b, j: (b, 0, kcol)),
        pl.BlockSpec((sub, n, kv_width), lambda b, j: (b, 0, kcol + 1)),
        pl.BlockSpec(bias.shape, lambda b, j: (0, 0, 0, 0)),
    ]
    args = [qkv, qkv, qkv, bias]
    if with_sink:
        in_specs = [pl.BlockSpec(memory_space=pltpu.SMEM),
                    pl.BlockSpec((1, q_width), lambda b, j: (0, 0))] + in_specs
        args = [sink, gain] + args
    return pl.pallas_call(
        functools.partial(_band_attn_kernel, m=m, nk=nk, n=n, sub=sub, qb=qb, kv_of=kv_of, variants=variants,
                          with_sink=with_sink),
        grid=(Bd // sub, nb // qb),
        in_specs=in_specs,
        out_specs=pl.BlockSpec((sub, out_tiles, qb * m, LANES), lambda b, j: (b, 0, j, 0)),
        out_shape=jax.ShapeDtypeStruct((Bd, out_tiles, n, LANES), F32),
        compiler_params=_cparams("parallel", "arbitrary"),
        name="band_attn_sink" if with_sink else f"band_attn_n{n}",
    )(*args)


def _merge_rows(ya_ref, o1_ref, o4_ref, o16_ref, x_ref, gin_ref, bin_ref, gb_ref, w_ref, g1_ref, b1_ref,
                s16_scr, x_scr):
    n4 = TM_MERGE // MERGE_STRIDE
    n16 = TM_MERGE // 16
    h0_rows = _layer_norm(x_ref[...], gin_ref[...], bin_ref[...])
    for c in range(D_MODEL // LANES):
        x_scr[c] = h0_rows[:, c * LANES:(c + 1) * LANES]
    for r16 in range(16):
        for c in range(B_OUT // LANES):
            s16_scr[c, pl.ds((r16 % 4) * n4 + r16 // 4, n16, stride=4), :] = o16_ref[0, r16, c]

    def natural(ref, c):
        return jnp.concatenate([ref[0, c, pl.ds(r, n4, stride=MERGE_STRIDE), :]
                                for r in range(MERGE_STRIDE)], axis=0)

    def stride4(c):
        return jnp.concatenate([o4_ref[0, r, c] for r in range(MERGE_STRIDE)], axis=0)

    def stride16(c):
        return s16_scr[c]

    n_val = B_WIDTH // LANES
    stats = (natural(o1_ref, n_val), stride4(n_val), stride16(n_val))
    dens = [pltpu.roll(s, LANES - B_HEADS, axis=1) for s in stats]
    mx = jnp.maximum(jnp.maximum(stats[0], stats[1]), stats[2])
    ex = [jnp.exp2(s - mx) for s in stats]
    inv = 1.0 / (ex[0] * dens[0] + ex[1] * dens[1] + ex[2] * dens[2])
    head_lane = lax.broadcasted_iota(jnp.int32, (TM_MERGE, LANES), 1) < B_HEADS
    head = lax.broadcasted_iota(jnp.int32, (LANES, B_WIDTH), 0)
    col = lax.broadcasted_iota(jnp.int32, (LANES, B_WIDTH), 1)
    spread = jnp.where(col // HEAD_DIM == head, 1.0, 0.0).astype(BF16)
    wide = []
    for e in ex:
        w = jnp.where(head_lane, e * inv, 0.0)
        w_hi = w.astype(BF16)
        w_lo = (w - w_hi.astype(F32)).astype(BF16)
        wide.append(jnp.dot(w_hi, spread, preferred_element_type=F32)
                    + jnp.dot(w_lo, spread, preferred_element_type=F32))
    pieces = []
    for c in range(n_val):
        cs = slice(c * LANES, (c + 1) * LANES)
        outs = (natural(o1_ref, c), stride4(c), stride16(c))
        pieces.append(wide[0][:, cs] * outs[0] + wide[1][:, cs] * outs[1] + wide[2][:, cs] * outs[2])
    yb = _rms_norm(jnp.concatenate(pieces, axis=1), gb_ref[...])
    ya = jnp.concatenate([natural(ya_ref, c) for c in range(A_WIDTH // LANES)], axis=1)
    y = jnp.concatenate([ya.astype(BF16), yb.astype(BF16)], axis=1)
    mix = jnp.dot(y, w_ref[...], preferred_element_type=F32)
    h0 = jnp.concatenate(
        [jnp.concatenate([x_scr[c, pl.ds(r, n4, stride=MERGE_STRIDE), :] for r in range(MERGE_STRIDE)], axis=0)
         for c in range(D_MODEL // LANES)], axis=1)
    return _layer_norm(ALPHA * h0 + mix, g1_ref[...], b1_ref[...])


def _token_of_row(p):
    n4 = TM_MERGE // MERGE_STRIDE
    tile, rest = p // TM_MERGE, p % TM_MERGE
    return tile * TM_MERGE + (rest % n4) * MERGE_STRIDE + rest // n4


def _mem_kv_kernel(mem_ref, w_ref, k_ref, v_ref):
    kv = jnp.dot(mem_ref[0].astype(BF16), w_ref[...], preferred_element_type=F32)
    k_ref[0] = kv[:, :D_MODEL].astype(BF16)
    v_ref[0] = kv[:, D_MODEL:].astype(BF16)


def _mem_kv(mem, xkv_b):
    B = mem.shape[0]
    blk = pl.BlockSpec((1, MEM_LEN, D_MODEL), lambda b: (b, 0, 0))
    return pl.pallas_call(
        _mem_kv_kernel,
        grid=(B,),
        in_specs=[blk, pl.BlockSpec((D_MODEL, 2 * D_MODEL), lambda b: (0, 0))],
        out_specs=[blk, blk],
        out_shape=[jax.ShapeDtypeStruct((B, MEM_LEN, D_MODEL), BF16)] * 2,
        compiler_params=_cparams("parallel"),
        name="mem_kv_proj",
    )(mem, xkv_b)


def _route(logits):
    rows = logits.shape[0]
    lane = lax.broadcasted_iota(jnp.int32, (rows, LANES), 1).astype(F32)
    big = float(LANES)
    ninf = -jnp.inf
    gl = jnp.where(lane < N_GROUPS, logits, ninf)
    gmax = jnp.max(gl, axis=-1, keepdims=True)
    gidx = jnp.min(jnp.where(gl == gmax, lane, big), axis=-1, keepdims=True)
    g_p = 1.0 / jnp.sum(jnp.exp(gl - gmax), axis=-1, keepdims=True)
    lo_lane = N_GROUPS + EXPERTS_PER_GROUP * gidx
    el = jnp.where((lane >= lo_lane) & (lane < lo_lane + EXPERTS_PER_GROUP), logits, ninf)
    v1 = jnp.max(el, axis=-1, keepdims=True)
    i1 = jnp.min(jnp.where(el == v1, lane, big), axis=-1, keepdims=True)
    el2 = jnp.where(lane == i1, ninf, el)
    v2 = jnp.max(el2, axis=-1, keepdims=True)
    i2 = jnp.min(jnp.where(el2 == v2, lane, big), axis=-1, keepdims=True)
    t = jnp.exp(v2 - v1)
    w1 = g_p / (1.0 + t)
    w2 = g_p * t / (1.0 + t)
    a = jnp.minimum(i1, i2) - lo_lane
    b = jnp.maximum(i1, i2) - lo_lane
    pair = a * (2 * EXPERTS_PER_GROUP - 1 - a) * 0.5 + (b - a - 1.0)
    cls = gidx * PAIRS_PER_GROUP + pair
    w_lo = jnp.where(i1 < i2, w1, w2)
    w_hi = jnp.where(i1 < i2, w2, w1)
    return jnp.where(lane == 0, cls, jnp.where(lane == 1, w_lo, jnp.where(lane == 2, w_hi, 0.0)))


def _merge_xattn_kernel(ya_ref, o1_ref, o4_ref, o16_ref, x_ref, gin_ref, bin_ref, gb_ref, w_ref, g1_ref, b1_ref,
                        k_ref, v_ref, wq_ref, wo_ref, g2_ref, b2_ref, wr_ref, br_ref, xr_ref, counts_ref,
                        s16_scr, x_scr):
    h1 = _merge_rows(ya_ref, o1_ref, o4_ref, o16_ref, x_ref, gin_ref, bin_ref, gb_ref, w_ref, g1_ref, b1_ref,
                     s16_scr, x_scr)
    q = jnp.dot(h1.astype(BF16), wq_ref[...], preferred_element_type=F32).astype(BF16)
    outs = []
    for hd in range(X_HEADS):
        sl = slice(hd * X_HEAD_DIM, (hd + 1) * X_HEAD_DIM)
        s = lax.dot_general(q[:, sl], k_ref[0, :, sl], (((1,), (1,)), ((), ())), preferred_element_type=F32)
        m = jnp.max(s, axis=-1, keepdims=True)
        p = jnp.exp2(s - m)
        l = jnp.sum(p, axis=-1, keepdims=True)
        o = jnp.dot(p.astype(BF16), v_ref[0, :, sl], preferred_element_type=F32) / l
        outs.append(o.astype(BF16))
    xa = jnp.dot(jnp.concatenate(outs, axis=1), wo_ref[...], preferred_element_type=F32)
    h2 = _layer_norm(ALPHA * h1 + xa, g2_ref[...], b2_ref[...])
    h_hi = h2.astype(BF16)
    h_lo = (h2 - h_hi.astype(F32)).astype(BF16)
    t_hi = jnp.dot(h_hi, wr_ref[...], preferred_element_type=F32)
    t_lo = jnp.dot(h_lo, wr_ref[...], preferred_element_type=F32)
    logits = (t_hi[:, :LANES] + t_hi[:, LANES:]) + (t_lo[:, :LANES] + t_lo[:, LANES:]) + br_ref[...]
    route = _route(logits)
    xr_ref[:, :D_MODEL] = h2
    xr_ref[:, D_MODEL:] = route

    @pl.when(pl.program_id(0) == 0)
    def _():
        counts_ref[...] = jnp.zeros_like(counts_ref)

    lane = lax.broadcasted_iota(jnp.int32, (TM_MERGE, LANES), 1).astype(F32)
    counts_ref[...] += jnp.sum(jnp.where(lane == route[:, 0:1], 1.0, 0.0), axis=0, keepdims=True)


def _merge_cross_attention_route(ya, o1, o4, o16, x2, ln_in_g, ln_in_b, gain_b, w_out_b, ln1_g, ln1_b,
                                 k, v, xq_b, xo_b, ln2_g, ln2_b, wr_split, br):
    T = x2.shape[0]
    tiles_per_seq = SEQ // TM_MERGE
    row = lambda i: (i, 0)
    const = lambda i: (0, 0)
    nat = lambda i: (i // tiles_per_seq, 0, i % tiles_per_seq, 0)
    deint = lambda i: (i // tiles_per_seq, 0, 0, i % tiles_per_seq, 0)
    vec_d = pl.BlockSpec((1, D_MODEL), const)
    mat_d = pl.BlockSpec((D_MODEL, D_MODEL), const)
    kv_blk = pl.BlockSpec((1, MEM_LEN, D_MODEL), lambda i: (i // tiles_per_seq, 0, 0))
    n_b = B_OUT // LANES
    return pl.pallas_call(
        _merge_xattn_kernel,
        grid=(T // TM_MERGE,),
        in_specs=[
            pl.BlockSpec((1, A_WIDTH // LANES, TM_MERGE, LANES), nat),
            pl.BlockSpec((1, n_b, TM_MERGE, LANES), nat),
            pl.BlockSpec((1, 4, n_b, TM_MERGE // 4, LANES), deint),
            pl.BlockSpec((1, 16, n_b, TM_MERGE // 16, LANES), deint),
            pl.BlockSpec((TM_MERGE, D_MODEL), row), vec_d, vec_d,
            pl.BlockSpec((1, B_WIDTH), const), mat_d, vec_d, vec_d,
            kv_blk, kv_blk, mat_d, mat_d, vec_d, vec_d,
            pl.BlockSpec((D_MODEL, 2 * LANES), const),
            pl.BlockSpec((1, LANES), const),
        ],
        out_specs=[pl.BlockSpec((TM_MERGE, XR_WIDTH), row), pl.BlockSpec((1, LANES), const)],
        out_shape=[jax.ShapeDtypeStruct((T, XR_WIDTH), F32), jax.ShapeDtypeStruct((1, LANES), F32)],
        scratch_shapes=[pltpu.VMEM((n_b, TM_MERGE, LANES), F32),
                        pltpu.VMEM((D_MODEL // LANES, TM_MERGE, LANES), F32)],
        compiler_params=_cparams("arbitrary"),
        name="merge_xattn_route",
    )(ya, o1, o4, o16, x2, ln_in_g, ln_in_b, gain_b, w_out_b, ln1_g, ln1_b,
      k, v, xq_b, xo_b, ln2_g, ln2_b, wr_split, br)


def _expert_kernel(src_ref, dst_ref, cnt_ref, elo_ref, ehi_ref, ntile_ref, x_hbm,
                   wg_lo, wu_lo, wd_lo, wg_hi, wu_hi, wd_hi, g3_ref, b3_ref, o_hbm,
                   xbuf, obuf, gsem, ssem):
    j = pl.program_id(0)
    n_tiles = ntile_ref[0]
    slot = j % 2

    def row_copy(t, i, s, gather):
        row = pl.ds(i, 1)
        if gather:
            return pltpu.make_async_copy(x_hbm.at[pl.ds(src_ref[t * TM_MOE + i], 1)], xbuf.at[s, row], gsem.at[s])
        return pltpu.make_async_copy(obuf.at[s, row], o_hbm.at[pl.ds(dst_ref[t * TM_MOE + i], 1)], ssem.at[s])

    def n_rows(t, gather):
        cnt = cnt_ref[t]
        return (cnt + ROW_GROUP - 1) // ROW_GROUP * ROW_GROUP if gather else cnt

    def start_rows(t, s, gather):
        cnt = cnt_ref[t]
        n_groups = (cnt + ROW_GROUP - 1) // ROW_GROUP if gather else cnt // ROW_GROUP

        def group(g, c):
            base = pl.multiple_of(g * ROW_GROUP, ROW_GROUP)
            for r in range(ROW_GROUP):
                row_copy(t, base + r, s, gather).start(priority=0 if gather else 1)
            return c
        lax.fori_loop(0, n_groups, group, 0)
        if not gather:
            def single(i, c):
                row_copy(t, i, s, gather).start(priority=1)
                return c
            lax.fori_loop(n_groups * ROW_GROUP, cnt, single, 0)

    def wait_rows(t, s, gather):
        cnt = n_rows(t, gather)
        for bit in range(TM_MOE.bit_length()):
            rows = pl.ds(0, 1 << bit)

            @pl.when((cnt >> bit) & 1 == 1)
            def _():
                if gather:
                    pltpu.make_async_copy(x_hbm.at[rows], xbuf.at[s, rows], gsem.at[s]).wait()
                else:
                    pltpu.make_async_copy(obuf.at[s, rows], o_hbm.at[rows], ssem.at[s]).wait()

    @pl.when(j == 0)
    def _():
        xbuf[...] = jnp.zeros_like(xbuf)
        start_rows(0, 0, True)

    @pl.when(j < n_tiles)
    def _():
        @pl.when(j + 1 < n_tiles)
        def _():
            start_rows(j + 1, 1 - slot, True)

        wait_rows(j, slot, True)
        x = xbuf[slot, :, :D_MODEL]
        xb = x.astype(BF16)
        y = jnp.zeros_like(x)
        for e, (wg, wu, wd) in enumerate(((wg_lo, wu_lo, wd_lo), (wg_hi, wu_hi, wd_hi))):
            gate = xbuf[slot, :, D_MODEL + 1 + e:D_MODEL + 2 + e]
            a = jnp.dot(xb, wg[0], preferred_element_type=F32)
            u = jnp.dot(xb, wu[0], preferred_element_type=F32)
            hid = a * jax.nn.sigmoid(a) * u
            y = y + jnp.dot((gate * hid).astype(BF16), wd[0], preferred_element_type=F32)
        out = _layer_norm(ALPHA * x + y, g3_ref[...], b3_ref[...])

        @pl.when(j >= 2)
        def _():
            wait_rows(j - 2, slot, False)

        obuf[slot] = out
        start_rows(j, slot, False)

        @pl.when(j == n_tiles - 1)
        def _():
            @pl.when(j >= 1)
            def _():
                wait_rows(j - 1, 1 - slot, False)

            wait_rows(j, slot, False)


def _expert_mlp(src_rows, dst_rows, tile_cnt, tile_elo, tile_ehi, n_tiles, xr, wg, wu, wd, ln_g, ln_b):
    n_tiles_max = tile_cnt.shape[0]
    const = lambda j, src, dst, cnt, elo, ehi, nt: (0, 0)
    lo = lambda j, src, dst, cnt, elo, ehi, nt: (elo[j], 0, 0)
    hi = lambda j, src, dst, cnt, elo, ehi, nt: (ehi[j], 0, 0)
    up = (1, D_MODEL, D_EXPERT)
    down = (1, D_EXPERT, D_MODEL)
    any_spec = pl.BlockSpec(memory_space=pl.ANY)
    return pl.pallas_call(
        _expert_kernel,
        grid_spec=pltpu.PrefetchScalarGridSpec(
            num_scalar_prefetch=6,
            grid=(n_tiles_max,),
            in_specs=[
                any_spec,
                pl.BlockSpec(up, lo), pl.BlockSpec(up, lo), pl.BlockSpec(down, lo),
                pl.BlockSpec(up, hi), pl.BlockSpec(up, hi), pl.BlockSpec(down, hi),
                pl.BlockSpec((1, D_MODEL), const),
                pl.BlockSpec((1, D_MODEL), const),
            ],
            out_specs=any_spec,
            scratch_shapes=[
                pltpu.VMEM((2, TM_MOE, XR_WIDTH), F32),
                pltpu.VMEM((2, TM_MOE, D_MODEL), F32),
                pltpu.SemaphoreType.DMA((2,)),
                pltpu.SemaphoreType.DMA((2,)),
            ],
        ),
        out_shape=jax.ShapeDtypeStruct((xr.shape[0], D_MODEL), F32),
        compiler_params=_cparams("arbitrary"),
        name="moe_experts",
    )(src_rows, dst_rows, tile_cnt, tile_elo, tile_ehi, n_tiles, xr, wg, wu, wd, wg, wu, wd, ln_g, ln_b)


def _class_experts():
    lo, hi = [], []
    for g in range(N_GROUPS):
        for a in range(EXPERTS_PER_GROUP):
            for b in range(a + 1, EXPERTS_PER_GROUP):
                lo.append(g * EXPERTS_PER_GROUP + a)
                hi.append(g * EXPERTS_PER_GROUP + b)
    return np.asarray(lo, np.int32), np.asarray(hi, np.int32)


def _plan_kernel(route_ref, counts_ref, pos_ref, tri_scr, run_scr, start_scr):
    i = pl.program_id(0)
    lane = lax.broadcasted_iota(jnp.int32, (TM_PLAN, LANES), 1)
    onehot = lane.astype(F32) == route_ref[:, 0:1]
    onehot_f = jnp.where(onehot, 1.0, 0.0)

    @pl.when(i == 0)
    def _():
        r = lax.broadcasted_iota(jnp.int32, (TM_PLAN, TM_PLAN), 0)
        c = lax.broadcasted_iota(jnp.int32, (TM_PLAN, TM_PLAN), 1)
        tri_scr[...] = jnp.where(c < r, 1.0, 0.0).astype(BF16)
        counts = counts_ref[...]
        tiles = jnp.floor((counts + (TM_MOE - 0.5)) * (1.0 / TM_MOE))
        lane_row = lax.broadcasted_iota(jnp.int32, (1, LANES), 1)
        scan = tiles
        shift = 1
        while shift < LANES:
            scan = scan + jnp.where(lane_row >= shift, pltpu.roll(scan, shift, axis=1), 0.0)
            shift *= 2
        start_scr[...] = (scan - tiles) * TM_MOE
        run_scr[...] = jnp.zeros_like(run_scr)

    before = jnp.dot(tri_scr[...], onehot_f.astype(BF16), preferred_element_type=F32)
    pos_col = jnp.sum(onehot_f * (before + run_scr[...] + start_scr[...]), axis=1, keepdims=True)
    eye = (lax.broadcasted_iota(jnp.int32, (LANES, LANES), 0)
           == lax.broadcasted_iota(jnp.int32, (LANES, LANES), 1))
    for r in range(TM_PLAN // LANES):
        row = jnp.sum(jnp.where(eye, pos_col[r * LANES:(r + 1) * LANES], 0.0), axis=0, keepdims=True)
        pos_ref[r:r + 1, :] = row.astype(jnp.int32)
    run_scr[...] = run_scr[...] + jnp.sum(onehot_f, axis=0, keepdims=True)


def _plan_positions(xr, counts):
    T = xr.shape[0]
    pos = pl.pallas_call(
        _plan_kernel,
        grid=(T // TM_PLAN,),
        in_specs=[pl.BlockSpec((TM_PLAN, LANES), lambda i: (i, D_MODEL // LANES)),
                  pl.BlockSpec((1, LANES), lambda i: (0, 0))],
        out_specs=pl.BlockSpec((TM_PLAN // LANES, LANES), lambda i: (i, 0)),
        out_shape=jax.ShapeDtypeStruct((T // LANES, LANES), jnp.int32),
        scratch_shapes=[pltpu.VMEM((TM_PLAN, TM_PLAN), BF16),
                        pltpu.VMEM((1, LANES), F32),
                        pltpu.VMEM((1, LANES), F32)],
        compiler_params=_cparams("arbitrary"),
        name="moe_plan",
    )(xr, counts)
    return pos.reshape(T)


def _moe_plan(xr, class_counts, n_tiles_max):
    T = xr.shape[0]
    pos = _plan_positions(xr, class_counts)
    counts = class_counts[0, :N_CLASSES].astype(jnp.int32)
    tiles_per_class = (counts + TM_MOE - 1) // TM_MOE
    tile_end = jnp.cumsum(tiles_per_class)
    tile_start = tile_end - tiles_per_class
    n_tiles = tile_end[-1]
    tok = jnp.arange(T, dtype=jnp.int32)
    src_rows = jnp.zeros((n_tiles_max * TM_MOE,), jnp.int32).at[pos].set(tok)
    tile_ids = jnp.arange(n_tiles_max, dtype=jnp.int32)
    used = jnp.minimum(tile_ids, n_tiles - 1)
    tile_cls = jnp.sum((tile_end[None, :] <= used[:, None]).astype(jnp.int32), axis=1)
    tile_cls = jnp.minimum(tile_cls, N_CLASSES - 1)
    cls_lo, cls_hi = _class_experts()
    tile_elo = jnp.asarray(cls_lo)[tile_cls]
    tile_ehi = jnp.asarray(cls_hi)[tile_cls]
    within = tile_ids - tile_start[tile_cls]
    tile_cnt = jnp.clip(counts[tile_cls] - within * TM_MOE, 0, TM_MOE)
    tile_cnt = jnp.where(tile_ids < n_tiles, tile_cnt, 0).astype(jnp.int32)
    return src_rows, tile_elo, tile_ehi, tile_cnt, n_tiles.reshape(1).astype(jnp.int32)


def _vec(a):
    return a.reshape(1, -1).astype(F32)


def _mixer_and_cross_attention(x, mem, ln_in_g, ln_in_b, w_in, rel_bias, sink_a, norm_a_g, norm_b_g, w_out,
                               ln1_g, ln1_b, xq, xkv, xo, ln2_g, ln2_b, w_group, b_group, w_router, b_router):
    B, S, D = x.shape
    assert S == SEQ and D == D_MODEL and mem.shape[1:] == (MEM_LEN, D_MODEL)
    T = B * S
    vec = _vec

    w = w_in[0]
    edges = np.cumsum((0, A_WIDTH, A_KV_HEADS * HEAD_DIM, A_KV_HEADS * HEAD_DIM, B_WIDTH, B_WIDTH, B_WIDTH))
    qa, ka, va, qb, kb, vb = [w[:, a:b] for a, b in zip(edges[:-1], edges[1:])]
    dup = lambda t: jnp.repeat(t.reshape(D, A_KV_HEADS, 1, HEAD_DIM), 2, axis=2).reshape(D, A_KV_TILES * LANES)
    scale = HEAD_DIM ** -0.5 * LOG2E
    w_in_b = jnp.concatenate([qa * scale, dup(ka), dup(va), qb * scale, kb, vb], axis=1).astype(BF16)
    w_out_b = w_out[0].astype(BF16)
    xq_b = (xq[0] * (X_HEAD_DIM ** -0.5 * LOG2E)).astype(BF16)
    xkv_b = xkv[0].astype(BF16)
    xo_b = xo[0].astype(BF16)
    wr = jnp.concatenate([w_group[0], w_router[0]], axis=1).astype(F32)
    wr = jnp.pad(wr, ((0, 0), (0, LANES - wr.shape[1])))
    wr_hi = wr.astype(BF16)
    wr_split = jnp.concatenate([wr_hi, (wr - wr_hi.astype(F32)).astype(BF16)], axis=1)
    br = jnp.concatenate([b_group[0], b_router[0]]).astype(F32)
    br = jnp.pad(br, (0, LANES - br.shape[0])).reshape(1, LANES)

    x2 = x.reshape(T, D)
    qkv_a, qkv_b1, qkv_b4, qkv_b16 = _input_projection(x2, vec(ln_in_g), vec(ln_in_b), w_in_b, B)

    nk_a = TM_ATT + 2 * A_HALF_WIN
    bias_a = _band_bias(rel_bias[:, :A_HEADS], A_HALF_WIN, 1, TM_ATT, nk_a, S)
    ya = _band_attention(qkv_a.reshape(B, S, QKV_A), bias_a, nk=nk_a, sub=1, kv_of=(0, 0, 1, 1),
                         q_width=A_WIDTH, kv_width=A_KV_TILES * LANES, out_tiles=A_WIDTH // LANES,
                         sink=sink_a[0].astype(F32) * LOG2E, gain=vec(norm_a_g[0]))
    branch_out = []
    for (win, dil), qkv in zip(B_BRANCHES, (qkv_b1.reshape(B, S, QKV_B),
                                             qkv_b4.reshape(B * 4, S // 4, QKV_B),
                                             qkv_b16.reshape(B * 16, S // 16, QKV_B))):
        half = (win // 2) // dil
        n = S // dil
        nk = min(TM_ATT + 2 * half, n)
        bias_b = _band_bias(rel_bias[:, A_HEADS:], half, dil, TM_ATT, nk, n)
        branch_out.append(_band_attention(qkv, bias_b, nk=nk, sub=max(1, TM_ATT * 8 // n), kv_of=(0, 1, 2, 3),
                                          q_width=B_WIDTH, kv_width=B_WIDTH, out_tiles=B_OUT // LANES))
    o1 = branch_out[0]
    o4 = branch_out[1].reshape(B, 4, B_OUT // LANES, S // 4, LANES)
    o16 = branch_out[2].reshape(B, 16, B_OUT // LANES, S // 16, LANES)

    k_mem, v_mem = _mem_kv(mem, xkv_b)
    return _merge_cross_attention_route(
        ya, o1, o4, o16, x2, vec(ln_in_g), vec(ln_in_b), vec(norm_b_g[0]), w_out_b, vec(ln1_g[0]), vec(ln1_b[0]),
        k_mem, v_mem, xq_b, xo_b, vec(ln2_g[0]), vec(ln2_b[0]), wr_split, br)


def _moe(xr, class_counts, w_gate, w_up, w_down, ln3_g, ln3_b):
    T = xr.shape[0]
    wg_b = w_gate[0].reshape(N_EXPERTS, D_MODEL, D_EXPERT).astype(BF16)
    wu_b = w_up[0].reshape(N_EXPERTS, D_MODEL, D_EXPERT).astype(BF16)
    wd_b = w_down[0].reshape(N_EXPERTS, D_EXPERT, D_MODEL).astype(BF16)
    n_tiles_max = -(-T // TM_MOE) + N_CLASSES
    src_rows, tile_elo, tile_ehi, tile_cnt, n_tiles = _moe_plan(xr, class_counts, n_tiles_max)
    return _expert_mlp(src_rows, _token_of_row(src_rows), tile_cnt, tile_elo, tile_ehi, n_tiles, xr, wg_b, wu_b, wd_b,
                       _vec(ln3_g[0]), _vec(ln3_b[0]))


def kernel(x, mem, ln_in_g, ln_in_b, w_in, rel_bias, sink_a, norm_a_g, norm_b_g, w_out,
           ln1_g, ln1_b, xq, xkv, xo, ln2_g, ln2_b, w_group, b_group, w_router, b_router,
           w_gate, w_up, w_down, ln3_g, ln3_b):
    xr, class_counts = _mixer_and_cross_attention(
        x, mem, ln_in_g, ln_in_b, w_in, rel_bias, sink_a, norm_a_g, norm_b_g, w_out,
        ln1_g, ln1_b, xq, xkv, xo, ln2_g, ln2_b, w_group, b_group, w_router, b_router)
    return _moe(xr, class_counts, w_gate, w_up, w_down, ln3_g, ln3_b).reshape(x.shape)
```

```python
import functools

import numpy as np
import jax
import jax.numpy as jnp
from jax import lax
from jax.experimental import pallas as pl
from jax.experimental.pallas import tpu as pltpu

F32 = jnp.float32
BF16 = jnp.bfloat16

D_MODEL = 1024
SEQ = 2048
MEM_LEN = 256
HEAD_DIM = 64
A_HEADS = 8
A_KV_HEADS = 2
A_HALF_WIN = 128
B_HEADS = 8
B_BRANCHES = ((128, 1), (512, 4), (2048, 16))
N_BUCKETS = 32
MAX_DISTANCE = 1024
X_HEADS = 4
X_HEAD_DIM = D_MODEL // X_HEADS
N_GROUPS = 4
EXPERTS_PER_GROUP = 8
N_EXPERTS = N_GROUPS * EXPERTS_PER_GROUP
D_EXPERT = 512
DEPTH = 1
ALPHA = (2.0 * DEPTH) ** 0.25
LN_EPS = 1e-5
NEG = -1e30
LOG2E = 1.4426950408889634

LANES = 128
XR_WIDTH = D_MODEL + LANES
A_WIDTH = A_HEADS * HEAD_DIM
B_WIDTH = B_HEADS * HEAD_DIM
A_KV_TILES = A_KV_HEADS
QKV_A = A_WIDTH + 2 * A_KV_TILES * LANES
QKV_B = 3 * B_WIDTH
B_OUT = B_WIDTH + LANES
MERGE_STRIDE = 4

PAIRS_PER_GROUP = EXPERTS_PER_GROUP * (EXPERTS_PER_GROUP - 1) // 2
N_CLASSES = N_GROUPS * PAIRS_PER_GROUP

TM_IN = 512
TM_ATT = 128
ATT_BLOCKS = 4
TM_MERGE = 512
TM_MOE = 320
ROW_GROUP = 16
TM_PLAN = 1024
VMEM_LIMIT = 56 * 1024 * 1024


def _cparams(*sem):
    return pltpu.CompilerParams(dimension_semantics=sem, vmem_limit_bytes=VMEM_LIMIT)


def _layer_norm(x, g, b):
    mu = jnp.mean(x, axis=-1, keepdims=True)
    xc = x - mu
    var = jnp.mean(xc * xc, axis=-1, keepdims=True)
    return xc * lax.rsqrt(var + LN_EPS) * g + b


def _rms_norm(x, g):
    return x * lax.rsqrt(jnp.mean(x * x, axis=-1, keepdims=True) + LN_EPS) * g


def _t5_bucket(rel):
    nb = N_BUCKETS // 2
    max_exact = nb // 2
    ret = (rel > 0).astype(np.int32) * nb
    n = np.abs(rel)
    n_safe = np.maximum(n, 1).astype(np.float64)
    large = max_exact + (np.log(n_safe / max_exact) / np.log(MAX_DISTANCE / max_exact)
                         * (nb - max_exact)).astype(np.int32)
    large = np.minimum(large, nb - 1)
    return (ret + np.where(n < max_exact, n, large)).astype(np.int32)


def _inproj_kernel(x_ref, g_ref, b_ref, w_ref, qa_ref, qb1_ref, qb4_ref, qb16_ref, pb_scr, p4_scr):
    h = _layer_norm(x_ref[...], g_ref[...], b_ref[...])
    proj = jnp.dot(h.astype(BF16), w_ref[...], preferred_element_type=F32)
    qa_ref[...] = proj[:, :QKV_A].astype(BF16)
    pb = proj[:, QKV_A:]
    qb1_ref[...] = pb.astype(BF16)
    n4 = TM_IN // 4
    for c in range(QKV_B // LANES):
        cs = slice(c * LANES, (c + 1) * LANES)
        pb_scr[c] = pb[:, cs]
        for r4 in range(4):
            rows = pb_scr[c, pl.ds(r4, n4, stride=4), :]
            qb4_ref[0, r4, :, cs] = rows.astype(BF16)
            p4_scr[c, r4 * n4:(r4 + 1) * n4, :] = rows
        for r16 in range(16):
            rows = p4_scr[c, pl.ds((r16 % 4) * n4 + r16 // 4, TM_IN // 16, stride=4), :]
            qb16_ref[0, r16, :, cs] = rows.astype(BF16)


def _input_projection(x2, ln_g, ln_b, w_in_b, batch):
    T = x2.shape[0]
    tiles_per_seq = SEQ // TM_IN
    row = lambda i: (i, 0)
    const = lambda i: (0, 0)
    deint = lambda i: (i // tiles_per_seq, 0, i % tiles_per_seq, 0)
    return pl.pallas_call(
        _inproj_kernel,
        grid=(T // TM_IN,),
        in_specs=[
            pl.BlockSpec((TM_IN, D_MODEL), row),
            pl.BlockSpec((1, D_MODEL), const),
            pl.BlockSpec((1, D_MODEL), const),
            pl.BlockSpec((D_MODEL, QKV_A + QKV_B), const),
        ],
        out_specs=[
            pl.BlockSpec((TM_IN, QKV_A), row),
            pl.BlockSpec((TM_IN, QKV_B), row),
            pl.BlockSpec((1, 4, TM_IN // 4, QKV_B), deint),
            pl.BlockSpec((1, 16, TM_IN // 16, QKV_B), deint),
        ],
        out_shape=[
            jax.ShapeDtypeStruct((T, QKV_A), BF16),
            jax.ShapeDtypeStruct((T, QKV_B), BF16),
            jax.ShapeDtypeStruct((batch, 4, SEQ // 4, QKV_B), BF16),
            jax.ShapeDtypeStruct((batch, 16, SEQ // 16, QKV_B), BF16),
        ],
        scratch_shapes=[pltpu.VMEM((QKV_B // LANES, TM_IN, LANES), F32)] * 2,
        compiler_params=_cparams("parallel"),
        name="ln_in_proj",
    )(x2, ln_g, ln_b, w_in_b)


def _band_attn_kernel(*refs, m, nk, n, sub, qb, kv_of, variants, with_sink):
    if with_sink:
        sink_ref, gain_ref = refs[0], refs[1]
        refs = refs[2:]
    q_ref, k_ref, v_ref, bias_ref, o_ref = refs
    nb = n // m
    first, middle, last = variants
    starts, bias_var = [], []
    for b in range(qb):
        if nk == n:
            starts.append(0)
            bias_var.append(first)
        else:
            jj = pl.program_id(1) * qb + b
            starts.append(pl.multiple_of(jnp.clip(jj * m - (nk - m) // 2, 0, n - nk), HEAD_DIM))
            bias_var.append(jnp.where(jj == 0, first, jnp.where(jj == nb - 1, last, middle)))
    lane_row = lax.broadcasted_iota(jnp.int32, (1, LANES), 1)
    keep_lo = jnp.where(lane_row < HEAD_DIM, 1.0, 0.0).astype(BF16)
    keep_hi = jnp.where(lane_row < HEAD_DIM, 0.0, 1.0).astype(BF16)
    lane = lax.broadcasted_iota(jnp.int32, (m, LANES), 1)
    n_q = len(kv_of)
    units = [(s, b, qt) for s in range(sub) for b in range(qb) for qt in range(n_q)]

    scores = []
    for s, b, qt in units:
        t = kv_of[qt]
        k_t = k_ref[s, pl.ds(starts[b], nk), t * LANES:(t + 1) * LANES]
        q2 = q_ref[s, b * m:(b + 1) * m, qt * LANES:(qt + 1) * LANES]
        lhs = jnp.concatenate([q2 * keep_lo, q2 * keep_hi], axis=0)
        sc = lax.dot_general(lhs, k_t, (((1,), (1,)), ((), ())), preferred_element_type=F32)
        scores.append(sc + bias_ref[bias_var[b], qt])

    probs, denom, row_max = [], [], []
    for (s, b, qt), sc in zip(units, scores):
        mx = jnp.max(sc, axis=-1, keepdims=True)
        p = jnp.exp2(sc - mx)
        l = jnp.sum(p, axis=-1, keepdims=True)
        halves = []
        for h in (0, 1):
            l_h = l[h * m:(h + 1) * m]
            if with_sink:
                l_h = l_h + jnp.exp2(sink_ref[2 * qt + h] - mx[h * m:(h + 1) * m])
            halves.append(l_h)
        probs.append(p.astype(BF16))
        denom.append(halves)
        row_max.append(mx)

    for s, b in [(s, b) for s in range(sub) for b in range(qb)]:
        rows = slice(b * m, (b + 1) * m)
        pairs = []
        stat_tile = jnp.zeros((m, LANES), F32)
        for qt in range(n_q):
            u = (s * qb + b) * n_q + qt
            t = kv_of[qt]
            v_t = v_ref[s, pl.ds(starts[b], nk), t * LANES:(t + 1) * LANES]
            o = jnp.dot(probs[u], v_t, preferred_element_type=F32)
            if with_sink:
                pairs.append(jnp.where(lane < HEAD_DIM, o[:m] * (1.0 / denom[u][0]), o[m:] * (1.0 / denom[u][1])))
            else:
                pairs.append(jnp.where(lane < HEAD_DIM, o[:m], o[m:]))
                for h in (0, 1):
                    stat_tile = jnp.where(lane == 2 * qt + h, row_max[u][h * m:(h + 1) * m], stat_tile)
                    stat_tile = jnp.where(lane == B_HEADS + 2 * qt + h, denom[u][h], stat_tile)
        if with_sink:
            normed = _rms_norm(jnp.concatenate(pairs, axis=1), gain_ref[...])
            pairs = [normed[:, c * LANES:(c + 1) * LANES] for c in range(n_q)]
        else:
            o_ref[s, n_q, rows, :] = stat_tile
        for c in range(n_q):
            o_ref[s, c, rows, :] = pairs[c]


def _band_window(m, nk, n):
    nb = n // m
    starts = np.clip(np.arange(nb) * m - (nk - m) // 2, 0, n - nk)
    offs = [int(o) for o in starts - np.arange(nb) * m]
    uniq = sorted(set(offs), reverse=True)
    var = [uniq.index(o) for o in offs]
    assert all(v == var[1] for v in var[1:-1])
    return uniq, var


def _band_bias(rel_bias_h, half, dil, m, nk, n):
    uniq, _ = _band_window(m, nk, n)
    n_heads = rel_bias_h.shape[1]
    span = nk + m - 1
    out = []
    for off in uniq:
        rel = off - (m - 1) + np.arange(span)
        onehot = np.zeros((span, N_BUCKETS), np.float32)
        onehot[np.arange(span), _t5_bucket(dil * rel)] = 1.0
        table = jnp.dot(jnp.asarray(onehot), rel_bias_h.astype(F32), precision=lax.Precision.HIGHEST)
        table = jnp.where((np.abs(rel) <= half)[:, None], table, NEG)
        u = jnp.concatenate([table.T, jnp.zeros((n_heads, 1), F32)], axis=1)
        flat = jnp.tile(u, (1, m + 1))[:, m - 1:m - 1 + m * span]
        b = flat.reshape(n_heads, m, span)[:, :, :nk]
        out.append(b.reshape(n_heads // 2, 2 * m, nk))
    return jnp.stack(out) * LOG2E


def _band_attention(qkv, bias, *, nk, sub, kv_of, q_width, kv_width, out_tiles, sink=None, gain=None):
    Bd, n, _ = qkv.shape
    m = TM_ATT
    nb = n // m
    qb = min(nb, ATT_BLOCKS)
    assert nb % qb == 0 and nb * m == n and Bd % sub == 0
    _, var = _band_window(m, nk, n)
    variants = (var[0], var[min(1, nb - 1)], var[-1])
    kcol = q_width // kv_width
    with_sink = sink is not None
    in_specs = [
        pl.BlockSpec((sub, qb * m, q_width), lambda b, j: (b, j, 0)),
        pl.BlockSpec((sub, n, kv_width), lambda b, j: (b, 0, kcol)),
        pl.BlockSpec((sub, n, kv_width), lambda b, j: (b, 0, kcol + 1)),
        pl.BlockSpec(bias.shape, lambda b, j: (0, 0, 0, 0)),
    ]
    args = [qkv, qkv, qkv, bias]
    if with_sink:
        in_specs = [pl.BlockSpec(memory_space=pltpu.SMEM),
                    pl.BlockSpec((1, q_width), lambda b, j: (0, 0))] + in_specs
        args = [sink, gain] + args
    return pl.pallas_call(
        functools.partial(_band_attn_kernel, m=m, nk=nk, n=n, sub=sub, qb=qb, kv_of=kv_of, variants=variants,
                          with_sink=with_sink),
        grid=(Bd // sub, nb // qb),
        in_specs=in_specs,
        out_specs=pl.BlockSpec((sub, out_tiles, qb * m, LANES), lambda b, j: (b, 0, j, 0)),
        out_shape=jax.ShapeDtypeStruct((Bd, out_tiles, n, LANES), F32),
        compiler_params=_cparams("parallel", "arbitrary"),
        name="band_attn_sink" if with_sink else f"band_attn_n{n}",
    )(*args)


def _merge_rows(ya_ref, o1_ref, o4_ref, o16_ref, x_ref, gin_ref, bin_ref, gb_ref, w_ref, g1_ref, b1_ref,
                s16_scr, x_scr):
    n4 = TM_MERGE // MERGE_STRIDE
    n16 = TM_MERGE // 16
    h0_rows = _layer_norm(x_ref[...], gin_ref[...], bin_ref[...])
    for c in range(D_MODEL // LANES):
        x_scr[c] = h0_rows[:, c * LANES:(c + 1) * LANES]
    for r16 in range(16):
        for c in range(B_OUT // LANES):
            s16_scr[c, pl.ds((r16 % 4) * n4 + r16 // 4, n16, stride=4), :] = o16_ref[0, r16, c]

    def natural(ref, c):
        return jnp.concatenate([ref[0, c, pl.ds(r, n4, stride=MERGE_STRIDE), :]
                                for r in range(MERGE_STRIDE)], axis=0)

    def stride4(c):
        return jnp.concatenate([o4_ref[0, r, c] for r in range(MERGE_STRIDE)], axis=0)

    def stride16(c):
        return s16_scr[c]

    n_val = B_WIDTH // LANES
    stats = (natural(o1_ref, n_val), stride4(n_val), stride16(n_val))
    dens = [pltpu.roll(s, LANES - B_HEADS, axis=1) for s in stats]
    mx = jnp.maximum(jnp.maximum(stats[0], stats[1]), stats[2])
    ex = [jnp.exp2(s - mx) for s in stats]
    inv = 1.0 / (ex[0] * dens[0] + ex[1] * dens[1] + ex[2] * dens[2])
    head_lane = lax.broadcasted_iota(jnp.int32, (TM_MERGE, LANES), 1) < B_HEADS
    head = lax.broadcasted_iota(jnp.int32, (LANES, B_WIDTH), 0)
    col = lax.broadcasted_iota(jnp.int32, (LANES, B_WIDTH), 1)
    spread = jnp.where(col // HEAD_DIM == head, 1.0, 0.0).astype(BF16)
    wide = []
    for e in ex:
        w = jnp.where(head_lane, e * inv, 0.0)
        w_hi = w.astype(BF16)
        w_lo = (w - w_hi.astype(F32)).astype(BF16)
        wide.append(jnp.dot(w_hi, spread, preferred_element_type=F32)
                    + jnp.dot(w_lo, spread, preferred_element_type=F32))
    pieces = []
    for c in range(n_val):
        cs = slice(c * LANES, (c + 1) * LANES)
        outs = (natural(o1_ref, c), stride4(c), stride16(c))
        pieces.append(wide[0][:, cs] * outs[0] + wide[1][:, cs] * outs[1] + wide[2][:, cs] * outs[2])
    yb = _rms_norm(jnp.concatenate(pieces, axis=1), gb_ref[...])
    ya = jnp.concatenate([natural(ya_ref, c) for c in range(A_WIDTH // LANES)], axis=1)
    y = jnp.concatenate([ya.astype(BF16), yb.astype(BF16)], axis=1)
    mix = jnp.dot(y, w_ref[...], preferred_element_type=F32)
    h0 = jnp.concatenate(
        [jnp.concatenate([x_scr[c, pl.ds(r, n4, stride=MERGE_STRIDE), :] for r in range(MERGE_STRIDE)], axis=0)
         for c in range(D_MODEL // LANES)], axis=1)
    return _layer_norm(ALPHA * h0 + mix, g1_ref[...], b1_ref[...])


def _token_of_row(p):
    n4 = TM_MERGE // MERGE_STRIDE
    tile, rest = p // TM_MERGE, p % TM_MERGE
    return tile * TM_MERGE + (rest % n4) * MERGE_STRIDE + rest // n4


def _mem_kv_kernel(mem_ref, w_ref, k_ref, v_ref):
    kv = jnp.dot(mem_ref[0].astype(BF16), w_ref[...], preferred_element_type=F32)
    k_ref[0] = kv[:, :D_MODEL].astype(BF16)
    v_ref[0] = kv[:, D_MODEL:].astype(BF16)


def _mem_kv(mem, xkv_b):
    B = mem.shape[0]
    blk = pl.BlockSpec((1, MEM_LEN, D_MODEL), lambda b: (b, 0, 0))
    return pl.pallas_call(
        _mem_kv_kernel,
        grid=(B,),
        in_specs=[blk, pl.BlockSpec((D_MODEL, 2 * D_MODEL), lambda b: (0, 0))],
        out_specs=[blk, blk],
        out_shape=[jax.ShapeDtypeStruct((B, MEM_LEN, D_MODEL), BF16)] * 2,
        compiler_params=_cparams("parallel"),
        name="mem_kv_proj",
    )(mem, xkv_b)


def _route(logits):
    rows = logits.shape[0]
    lane = lax.broadcasted_iota(jnp.int32, (rows, LANES), 1).astype(F32)
    big = float(LANES)
    ninf = -jnp.inf
    gl = jnp.where(lane < N_GROUPS, logits, ninf)
    gmax = jnp.max(gl, axis=-1, keepdims=True)
    gidx = jnp.min(jnp.where(gl == gmax, lane, big), axis=-1, keepdims=True)
    g_p = 1.0 / jnp.sum(jnp.exp(gl - gmax), axis=-1, keepdims=True)
    lo_lane = N_GROUPS + EXPERTS_PER_GROUP * gidx
    el = jnp.where((lane >= lo_lane) & (lane < lo_lane + EXPERTS_PER_GROUP), logits, ninf)
    v1 = jnp.max(el, axis=-1, keepdims=True)
    i1 = jnp.min(jnp.where(el == v1, lane, big), axis=-1, keepdims=True)
    el2 = jnp.where(lane == i1, ninf, el)
    v2 = jnp.max(el2, axis=-1, keepdims=True)
    i2 = jnp.min(jnp.where(el2 == v2, lane, big), axis=-1, keepdims=True)
    t = jnp.exp(v2 - v1)
    w1 = g_p / (1.0 + t)
    w2 = g_p * t / (1.0 + t)
    a = jnp.minimum(i1, i2) - lo_lane
    b = jnp.maximum(i1, i2) - lo_lane
    pair = a * (2 * EXPERTS_PER_GROUP - 1 - a) * 0.5 + (b - a - 1.0)
    cls = gidx * PAIRS_PER_GROUP + pair
    w_lo = jnp.where(i1 < i2, w1, w2)
    w_hi = jnp.where(i1 < i2, w2, w1)
    return jnp.where(lane == 0, cls, jnp.where(lane == 1, w_lo, jnp.where(lane == 2, w_hi, 0.0)))


def _merge_xattn_kernel(ya_ref, o1_ref, o4_ref, o16_ref, x_ref, gin_ref, bin_ref, gb_ref, w_ref, g1_ref, b1_ref,
                        k_ref, v_ref, wq_ref, wo_ref, g2_ref, b2_ref, wr_ref, br_ref, xr_ref, counts_ref,
                        s16_scr, x_scr):
    h1 = _merge_rows(ya_ref, o1_ref, o4_ref, o16_ref, x_ref, gin_ref, bin_ref, gb_ref, w_ref, g1_ref, b1_ref,
                     s16_scr, x_scr)
    q = jnp.dot(h1.astype(BF16), wq_ref[...], preferred_element_type=F32).astype(BF16)
    outs = []
    for hd in range(X_HEADS):
        sl = slice(hd * X_HEAD_DIM, (hd + 1) * X_HEAD_DIM)
        s = lax.dot_general(q[:, sl], k_ref[0, :, sl], (((1,), (1,)), ((), ())), preferred_element_type=F32)
        m = jnp.max(s, axis=-1, keepdims=True)
        p = jnp.exp2(s - m)
        l = jnp.sum(p, axis=-1, keepdims=True)
        o = jnp.dot(p.astype(BF16), v_ref[0, :, sl], preferred_element_type=F32) / l
        outs.append(o.astype(BF16))
    xa = jnp.dot(jnp.concatenate(outs, axis=1), wo_ref[...], preferred_element_type=F32)
    h2 = _layer_norm(ALPHA * h1 + xa, g2_ref[...], b2_ref[...])
    h_hi = h2.astype(BF16)
    h_lo = (h2 - h_hi.astype(F32)).astype(BF16)
    t_hi = jnp.dot(h_hi, wr_ref[...], preferred_element_type=F32)
    t_lo = jnp.dot(h_lo, wr_ref[...], preferred_element_type=F32)
    logits = (t_hi[:, :LANES] + t_hi[:, LANES:]) + (t_lo[:, :LANES] + t_lo[:, LANES:]) + br_ref[...]
    route = _route(logits)
    xr_ref[:, :D_MODEL] = h2
    xr_ref[:, D_MODEL:] = route

    @pl.when(pl.program_id(0) == 0)
    def _():
        counts_ref[...] = jnp.zeros_like(counts_ref)

    lane = lax.broadcasted_iota(jnp.int32, (TM_MERGE, LANES), 1).astype(F32)
    counts_ref[...] += jnp.sum(jnp.where(lane == route[:, 0:1], 1.0, 0.0), axis=0, keepdims=True)


def _merge_cross_attention_route(ya, o1, o4, o16, x2, ln_in_g, ln_in_b, gain_b, w_out_b, ln1_g, ln1_b,
                                 k, v, xq_b, xo_b, ln2_g, ln2_b, wr_split, br):
    T = x2.shape[0]
    tiles_per_seq = SEQ // TM_MERGE
    row = lambda i: (i, 0)
    const = lambda i: (0, 0)
    nat = lambda i: (i // tiles_per_seq, 0, i % tiles_per_seq, 0)
    deint = lambda i: (i // tiles_per_seq, 0, 0, i % tiles_per_seq, 0)
    vec_d = pl.BlockSpec((1, D_MODEL), const)
    mat_d = pl.BlockSpec((D_MODEL, D_MODEL), const)
    kv_blk = pl.BlockSpec((1, MEM_LEN, D_MODEL), lambda i: (i // tiles_per_seq, 0, 0))
    n_b = B_OUT // LANES
    return pl.pallas_call(
        _merge_xattn_kernel,
        grid=(T // TM_MERGE,),
        in_specs=[
            pl.BlockSpec((1, A_WIDTH // LANES, TM_MERGE, LANES), nat),
            pl.BlockSpec((1, n_b, TM_MERGE, LANES), nat),
            pl.BlockSpec((1, 4, n_b, TM_MERGE // 4, LANES), deint),
            pl.BlockSpec((1, 16, n_b, TM_MERGE // 16, LANES), deint),
            pl.BlockSpec((TM_MERGE, D_MODEL), row), vec_d, vec_d,
            pl.BlockSpec((1, B_WIDTH), const), mat_d, vec_d, vec_d,
            kv_blk, kv_blk, mat_d, mat_d, vec_d, vec_d,
            pl.BlockSpec((D_MODEL, 2 * LANES), const),
            pl.BlockSpec((1, LANES), const),
        ],
        out_specs=[pl.BlockSpec((TM_MERGE, XR_WIDTH), row), pl.BlockSpec((1, LANES), const)],
        out_shape=[jax.ShapeDtypeStruct((T, XR_WIDTH), F32), jax.ShapeDtypeStruct((1, LANES), F32)],
        scratch_shapes=[pltpu.VMEM((n_b, TM_MERGE, LANES), F32),
                        pltpu.VMEM((D_MODEL // LANES, TM_MERGE, LANES), F32)],
        compiler_params=_cparams("arbitrary"),
        name="merge_xattn_route",
    )(ya, o1, o4, o16, x2, ln_in_g, ln_in_b, gain_b, w_out_b, ln1_g, ln1_b,
      k, v, xq_b, xo_b, ln2_g, ln2_b, wr_split, br)


def _expert_kernel(src_ref, dst_ref, cnt_ref, elo_ref, ehi_ref, ntile_ref, x_hbm,
                   wg_lo, wu_lo, wd_lo, wg_hi, wu_hi, wd_hi, g3_ref, b3_ref, o_hbm,
                   xbuf, obuf, gsem, ssem):
    j = pl.program_id(0)
    n_tiles = ntile_ref[0]
    slot = j % 2

    def row_copy(t, i, s, gather):
        row = pl.ds(i, 1)
        if gather:
            return pltpu.make_async_copy(x_hbm.at[pl.ds(src_ref[t * TM_MOE + i], 1)], xbuf.at[s, row], gsem.at[s])
        return pltpu.make_async_copy(obuf.at[s, row], o_hbm.at[pl.ds(dst_ref[t * TM_MOE + i], 1)], ssem.at[s])

    def n_rows(t, gather):
        cnt = cnt_ref[t]
        return (cnt + ROW_GROUP - 1) // ROW_GROUP * ROW_GROUP if gather else cnt

    def start_rows(t, s, gather):
        cnt = cnt_ref[t]
        n_groups = (cnt + ROW_GROUP - 1) // ROW_GROUP if gather else cnt // ROW_GROUP

        def group(g, c):
            base = pl.multiple_of(g * ROW_GROUP, ROW_GROUP)
            for r in range(ROW_GROUP):
                row_copy(t, base + r, s, gather).start(priority=0 if gather else 1)
            return c
        lax.fori_loop(0, n_groups, group, 0)
        if not gather:
            def single(i, c):
                row_copy(t, i, s, gather).start(priority=1)
                return c
            lax.fori_loop(n_groups * ROW_GROUP, cnt, single, 0)

    def wait_rows(t, s, gather):
        cnt = n_rows(t, gather)
        for bit in range(TM_MOE.bit_length()):
            rows = pl.ds(0, 1 << bit)

            @pl.when((cnt >> bit) & 1 == 1)
            def _():
                if gather:
                    pltpu.make_async_copy(x_hbm.at[rows], xbuf.at[s, rows], gsem.at[s]).wait()
                else:
                    pltpu.make_async_copy(obuf.at[s, rows], o_hbm.at[rows], ssem.at[s]).wait()

    @pl.when(j == 0)
    def _():
        xbuf[...] = jnp.zeros_like(xbuf)
        start_rows(0, 0, True)

    @pl.when(j < n_tiles)
    def _():
        @pl.when(j + 1 < n_tiles)
        def _():
            start_rows(j + 1, 1 - slot, True)

        wait_rows(j, slot, True)
        x = xbuf[slot, :, :D_MODEL]
        xb = x.astype(BF16)
        y = jnp.zeros_like(x)
        for e, (wg, wu, wd) in enumerate(((wg_lo, wu_lo, wd_lo), (wg_hi, wu_hi, wd_hi))):
            gate = xbuf[slot, :, D_MODEL + 1 + e:D_MODEL + 2 + e]
            a = jnp.dot(xb, wg[0], preferred_element_type=F32)
            u = jnp.dot(xb, wu[0], preferred_element_type=F32)
            hid = a * jax.nn.sigmoid(a) * u
            y = y + jnp.dot((gate * hid).astype(BF16), wd[0], preferred_element_type=F32)
        out = _layer_norm(ALPHA * x + y, g3_ref[...], b3_ref[...])

        @pl.when(j >= 2)
        def _():
            wait_rows(j - 2, slot, False)

        obuf[slot] = out
        start_rows(j, slot, False)

        @pl.when(j == n_tiles - 1)
        def _():
            @pl.when(j >= 1)
            def _():
                wait_rows(j - 1, 1 - slot, False)

            wait_rows(j, slot, False)


def _expert_mlp(src_rows, dst_rows, tile_cnt, tile_elo, tile_ehi, n_tiles, xr, wg, wu, wd, ln_g, ln_b):
    n_tiles_max = tile_cnt.shape[0]
    const = lambda j, src, dst, cnt, elo, ehi, nt: (0, 0)
    lo = lambda j, src, dst, cnt, elo, ehi, nt: (elo[j], 0, 0)
    hi = lambda j, src, dst, cnt, elo, ehi, nt: (ehi[j], 0, 0)
    up = (1, D_MODEL, D_EXPERT)
    down = (1, D_EXPERT, D_MODEL)
    any_spec = pl.BlockSpec(memory_space=pl.ANY)
    return pl.pallas_call(
        _expert_kernel,
        grid_spec=pltpu.PrefetchScalarGridSpec(
            num_scalar_prefetch=6,
            grid=(n_tiles_max,),
            in_specs=[
                any_spec,
                pl.BlockSpec(up, lo), pl.BlockSpec(up, lo), pl.BlockSpec(down, lo),
                pl.BlockSpec(up, hi), pl.BlockSpec(up, hi), pl.BlockSpec(down, hi),
                pl.BlockSpec((1, D_MODEL), const),
                pl.BlockSpec((1, D_MODEL), const),
            ],
            out_specs=any_spec,
            scratch_shapes=[
                pltpu.VMEM((2, TM_MOE, XR_WIDTH), F32),
                pltpu.VMEM((2, TM_MOE, D_MODEL), F32),
                pltpu.SemaphoreType.DMA((2,)),
                pltpu.SemaphoreType.DMA((2,)),
            ],
        ),
        out_shape=jax.ShapeDtypeStruct((xr.shape[0], D_MODEL), F32),
        compiler_params=_cparams("arbitrary"),
        name="moe_experts",
    )(src_rows, dst_rows, tile_cnt, tile_elo, tile_ehi, n_tiles, xr, wg, wu, wd, wg, wu, wd, ln_g, ln_b)


def _class_experts():
    lo, hi = [], []
    for g in range(N_GROUPS):
        for a in range(EXPERTS_PER_GROUP):
            for b in range(a + 1, EXPERTS_PER_GROUP):
                lo.append(g * EXPERTS_PER_GROUP + a)
                hi.append(g * EXPERTS_PER_GROUP + b)
    return np.asarray(lo, np.int32), np.asarray(hi, np.int32)


def _plan_kernel(route_ref, counts_ref, pos_ref, tri_scr, run_scr, start_scr):
    i = pl.program_id(0)
    lane = lax.broadcasted_iota(jnp.int32, (TM_PLAN, LANES), 1)
    onehot = lane.astype(F32) == route_ref[:, 0:1]
    onehot_f = jnp.where(onehot, 1.0, 0.0)

    @pl.when(i == 0)
    def _():
        r = lax.broadcasted_iota(jnp.int32, (TM_PLAN, TM_PLAN), 0)
        c = lax.broadcasted_iota(jnp.int32, (TM_PLAN, TM_PLAN), 1)
        tri_scr[...] = jnp.where(c < r, 1.0, 0.0).astype(BF16)
        counts = counts_ref[...]
        tiles = jnp.floor((counts + (TM_MOE - 0.5)) * (1.0 / TM_MOE))
        lane_row = lax.broadcasted_iota(jnp.int32, (1, LANES), 1)
        scan = tiles
        shift = 1
        while shift < LANES:
            scan = scan + jnp.where(lane_row >= shift, pltpu.roll(scan, shift, axis=1), 0.0)
            shift *= 2
        start_scr[...] = (scan - tiles) * TM_MOE
        run_scr[...] = jnp.zeros_like(run_scr)

    before = jnp.dot(tri_scr[...], onehot_f.astype(BF16), preferred_element_type=F32)
    pos_col = jnp.sum(onehot_f * (before + run_scr[...] + start_scr[...]), axis=1, keepdims=True)
    eye = (lax.broadcasted_iota(jnp.int32, (LANES, LANES), 0)
           == lax.broadcasted_iota(jnp.int32, (LANES, LANES), 1))
    for r in range(TM_PLAN // LANES):
        row = jnp.sum(jnp.where(eye, pos_col[r * LANES:(r + 1) * LANES], 0.0), axis=0, keepdims=True)
        pos_ref[r:r + 1, :] = row.astype(jnp.int32)
    run_scr[...] = run_scr[...] + jnp.sum(onehot_f, axis=0, keepdims=True)


def _plan_positions(xr, counts):
    T = xr.shape[0]
    pos = pl.pallas_call(
        _plan_kernel,
        grid=(T // TM_PLAN,),
        in_specs=[pl.BlockSpec((TM_PLAN, LANES), lambda i: (i, D_MODEL // LANES)),
                  pl.BlockSpec((1, LANES), lambda i: (0, 0))],
        out_specs=pl.BlockSpec((TM_PLAN // LANES, LANES), lambda i: (i, 0)),
        out_shape=jax.ShapeDtypeStruct((T // LANES, LANES), jnp.int32),
        scratch_shapes=[pltpu.VMEM((TM_PLAN, TM_PLAN), BF16),
                        pltpu.VMEM((1, LANES), F32),
                        pltpu.VMEM((1, LANES), F32)],
        compiler_params=_cparams("arbitrary"),
        name="moe_plan",
    )(xr, counts)
    return pos.reshape(T)


def _moe_plan(xr, class_counts, n_tiles_max):
    T = xr.shape[0]
    pos = _plan_positions(xr, class_counts)
    counts = class_counts[0, :N_CLASSES].astype(jnp.int32)
    tiles_per_class = (counts + TM_MOE - 1) // TM_MOE
    tile_end = jnp.cumsum(tiles_per_class)
    tile_start = tile_end - tiles_per_class
    n_tiles = tile_end[-1]
    tok = jnp.arange(T, dtype=jnp.int32)
    src_rows = jnp.zeros((n_tiles_max * TM_MOE,), jnp.int32).at[pos].set(tok)
    tile_ids = jnp.arange(n_tiles_max, dtype=jnp.int32)
    used = jnp.minimum(tile_ids, n_tiles - 1)
    tile_cls = jnp.sum((tile_end[None, :] <= used[:, None]).astype(jnp.int32), axis=1)
    tile_cls = jnp.minimum(tile_cls, N_CLASSES - 1)
    cls_lo, cls_hi = _class_experts()
    tile_elo = jnp.asarray(cls_lo)[tile_cls]
    tile_ehi = jnp.asarray(cls_hi)[tile_cls]
    within = tile_ids - tile_start[tile_cls]
    tile_cnt = jnp.clip(counts[tile_cls] - within * TM_MOE, 0, TM_MOE)
    tile_cnt = jnp.where(tile_ids < n_tiles, tile_cnt, 0).astype(jnp.int32)
    return src_rows, tile_elo, tile_ehi, tile_cnt, n_tiles.reshape(1).astype(jnp.int32)


def _vec(a):
    return a.reshape(1, -1).astype(F32)


def _mixer_and_cross_attention(x, mem, ln_in_g, ln_in_b, w_in, rel_bias, sink_a, norm_a_g, norm_b_g, w_out,
                               ln1_g, ln1_b, xq, xkv, xo, ln2_g, ln2_b, w_group, b_group, w_router, b_router):
    B, S, D = x.shape
    assert S == SEQ and D == D_MODEL and mem.shape[1:] == (MEM_LEN, D_MODEL)
    T = B * S
    vec = _vec

    w = w_in[0]
    edges = np.cumsum((0, A_WIDTH, A_KV_HEADS * HEAD_DIM, A_KV_HEADS * HEAD_DIM, B_WIDTH, B_WIDTH, B_WIDTH))
    qa, ka, va, qb, kb, vb = [w[:, a:b] for a, b in zip(edges[:-1], edges[1:])]
    dup = lambda t: jnp.repeat(t.reshape(D, A_KV_HEADS, 1, HEAD_DIM), 2, axis=2).reshape(D, A_KV_TILES * LANES)
    scale = HEAD_DIM ** -0.5 * LOG2E
    w_in_b = jnp.concatenate([qa * scale, dup(ka), dup(va), qb * scale, kb, vb], axis=1).astype(BF16)
    w_out_b = w_out[0].astype(BF16)
    xq_b = (xq[0] * (X_HEAD_DIM ** -0.5 * LOG2E)).astype(BF16)
    xkv_b = xkv[0].astype(BF16)
    xo_b = xo[0].astype(BF16)
    wr = jnp.concatenate([w_group[0], w_router[0]], axis=1).astype(F32)
    wr = jnp.pad(wr, ((0, 0), (0, LANES - wr.shape[1])))
    wr_hi = wr.astype(BF16)
    wr_split = jnp.concatenate([wr_hi, (wr - wr_hi.astype(F32)).astype(BF16)], axis=1)
    br = jnp.concatenate([b_group[0], b_router[0]]).astype(F32)
    br = jnp.pad(br, (0, LANES - br.shape[0])).reshape(1, LANES)

    x2 = x.reshape(T, D)
    qkv_a, qkv_b1, qkv_b4, qkv_b16 = _input_projection(x2, vec(ln_in_g), vec(ln_in_b), w_in_b, B)

    nk_a = TM_ATT + 2 * A_HALF_WIN
    bias_a = _band_bias(rel_bias[:, :A_HEADS], A_HALF_WIN, 1, TM_ATT, nk_a, S)
    ya = _band_attention(qkv_a.reshape(B, S, QKV_A), bias_a, nk=nk_a, sub=1, kv_of=(0, 0, 1, 1),
                         q_width=A_WIDTH, kv_width=A_KV_TILES * LANES, out_tiles=A_WIDTH // LANES,
                         sink=sink_a[0].astype(F32) * LOG2E, gain=vec(norm_a_g[0]))
    branch_out = []
    for (win, dil), qkv in zip(B_BRANCHES, (qkv_b1.reshape(B, S, QKV_B),
                                             qkv_b4.reshape(B * 4, S // 4, QKV_B),
                                             qkv_b16.reshape(B * 16, S // 16, QKV_B))):
        half = (win // 2) // dil
        n = S // dil
        nk = min(TM_ATT + 2 * half, n)
        bias_b = _band_bias(rel_bias[:, A_HEADS:], half, dil, TM_ATT, nk, n)
        branch_out.append(_band_attention(qkv, bias_b, nk=nk, sub=max(1, TM_ATT * 8 // n), kv_of=(0, 1, 2, 3),
                                          q_width=B_WIDTH, kv_width=B_WIDTH, out_tiles=B_OUT // LANES))
    o1 = branch_out[0]
    o4 = branch_out[1].reshape(B, 4, B_OUT // LANES, S // 4, LANES)
    o16 = branch_out[2].reshape(B, 16, B_OUT // LANES, S // 16, LANES)

    k_mem, v_mem = _mem_kv(mem, xkv_b)
    return _merge_cross_attention_route(
        ya, o1, o4, o16, x2, vec(ln_in_g), vec(ln_in_b), vec(norm_b_g[0]), w_out_b, vec(ln1_g[0]), vec(ln1_b[0]),
        k_mem, v_mem, xq_b, xo_b, vec(ln2_g[0]), vec(ln2_b[0]), wr_split, br)


def _moe(xr, class_counts, w_gate, w_up, w_down, ln3_g, ln3_b):
    T = xr.shape[0]
    wg_b = w_gate[0].reshape(N_EXPERTS, D_MODEL, D_EXPERT).astype(BF16)
    wu_b = w_up[0].reshape(N_EXPERTS, D_MODEL, D_EXPERT).astype(BF16)
    wd_b = w_down[0].reshape(N_EXPERTS, D_EXPERT, D_MODEL).astype(BF16)
    n_tiles_max = -(-T // TM_MOE) + N_CLASSES
    src_rows, tile_elo, tile_ehi, tile_cnt, n_tiles = _moe_plan(xr, class_counts, n_tiles_max)
    return _expert_mlp(src_rows, _token_of_row(src_rows), tile_cnt, tile_elo, tile_ehi, n_tiles, xr, wg_b, wu_b, wd_b,
                       _vec(ln3_g[0]), _vec(ln3_b[0]))


def kernel(x, mem, ln_in_g, ln_in_b, w_in, rel_bias, sink_a, norm_a_g, norm_b_g, w_out,
           ln1_g, ln1_b, xq, xkv, xo, ln2_g, ln2_b, w_group, b_group, w_router, b_router,
           w_gate, w_up, w_down, ln3_g, ln3_b):
    xr, class_counts = _mixer_and_cross_attention(
        x, mem, ln_in_g, ln_in_b, w_in, rel_bias, sink_a, norm_a_g, norm_b_g, w_out,
        ln1_g, ln1_b, xq, xkv, xo, ln2_g, ln2_b, w_group, b_group, w_router, b_router)
    return _moe(xr, class_counts, w_gate, w_up, w_down, ln3_g, ln3_b).reshape(x.shape)
```
